```python
import math
import jax, jax.numpy as jnp
from jax import lax
import numpy as np

D_MODEL = 1024
BATCH = 8
SEQ = 4096
DEPTH = 1
DEC_BATCH = 16
DEC_SEQ = 64
PAST_LEN = 4096

CHUNK = 64
N_PREV_CHUNKS = 8
BAND = N_PREV_CHUNKS * CHUNK
HEAD_DIM = 64
D_SSM = D_MODEL // 2
D_ATT = D_MODEL // 4
D_MEM = D_MODEL // 4
D_IN = D_SSM + 3 * D_ATT + D_MEM
N_ATT_HEADS = D_ATT // HEAD_DIM
N_MEM_HEADS = D_MEM // HEAD_DIM
SSM_GROUP = 16
N_SSM_GROUPS = D_SSM // SSM_GROUP
SSM_STATE = 64
REL_CLIP = 128
N_MEM = 256
N_EXPERTS = 32
TOP_K = 4
D_FF = D_MODEL
SWIGLU_LIMIT = 7.0
SWIGLU_ALPHA = 1.702
MOE_BLOCK = 128
LN_EPS = 1e-5
NEG_INF = -1e30
ATT_SCALE = HEAD_DIM ** -0.5
DEEPNORM_ALPHA = (2 * DEPTH) ** 0.25
DEEPNORM_BETA = (8 * DEPTH) ** -0.25

kernel_name = 'hymba_s5_chunkband_mem_moe_step'


def layer_norm(x, g, b):
    xf = x.astype(jnp.float32)
    mu = jnp.mean(xf, axis=-1, keepdims=True)
    var = jnp.mean(jnp.square(xf - mu), axis=-1, keepdims=True)
    y = (xf - mu) * lax.rsqrt(var + LN_EPS) * g.astype(jnp.float32) + b.astype(jnp.float32)
    return y.astype(x.dtype)


def rms_norm(x, g):
    xf = x.astype(jnp.float32)
    y = xf * lax.rsqrt(jnp.mean(jnp.square(xf), axis=-1, keepdims=True) + LN_EPS) * g.astype(jnp.float32)
    return y.astype(x.dtype)


def split_heads(t):
    return t.reshape(t.shape[:-1] + (t.shape[-1] // HEAD_DIM, HEAD_DIM))


def in_projection(x, w_in):
    z = jnp.einsum('bsd,de->bse', x, w_in)
    u, q, k, v, qm = jnp.split(z, [D_SSM, D_SSM + D_ATT, D_SSM + 2 * D_ATT, D_SSM + 3 * D_ATT], axis=-1)
    return u, split_heads(q), split_heads(k), split_heads(v), split_heads(qm)


def _complex_linear_combine(earlier, later):
    a1r, a1i, b1r, b1i = earlier
    a2r, a2i, b2r, b2i = later
    return (a2r * a1r - a2i * a1i,
            a2r * a1i + a2i * a1r,
            a2r * b1r - a2i * b1i + b2r,
            a2r * b1i + a2i * b1r + b2i)


def ssm_branch(u, h0_re, h0_im, lp):
    f32 = jnp.float32
    b, s, _ = u.shape
    uf = u.astype(f32)
    ug = uf.reshape(b, s, N_SSM_GROUPS, SSM_GROUP)
    lr = lp['lam_re'].astype(f32)
    li = lp['lam_im'].astype(f32)
    dt = jnp.exp(lp['log_dt'].astype(f32))[:, None]
    mag = jnp.exp(lr * dt)
    ab_re = mag * jnp.cos(li * dt)
    ab_im = mag * jnp.sin(li * dt)
    den = lr * lr + li * li
    f_re = ((ab_re - 1.0) * lr + ab_im * li) / den
    f_im = (ab_im * lr - (ab_re - 1.0) * li) / den
    br = lp['ssm_b_re'].astype(f32)
    bi = lp['ssm_b_im'].astype(f32)
    bb_re = f_re[..., None] * br - f_im[..., None] * bi
    bb_im = f_re[..., None] * bi + f_im[..., None] * br
    bu_re = jnp.einsum('bsgh,gph->bsgp', ug, bb_re)
    bu_im = jnp.einsum('bsgh,gph->bsgp', ug, bb_im)
    h0r = h0_re.astype(f32)
    h0i = h0_im.astype(f32)
    bu_re = bu_re.at[:, 0].add(ab_re * h0r - ab_im * h0i)
    bu_im = bu_im.at[:, 0].add(ab_re * h0i + ab_im * h0r)
    a_re = jnp.broadcast_to(ab_re, bu_re.shape)
    a_im = jnp.broadcast_to(ab_im, bu_im.shape)
    _, _, xr, xi = lax.associative_scan(_complex_linear_combine, (a_re, a_im, bu_re, bu_im), axis=1)
    y = (jnp.einsum('bsgp,ghp->bsgh', xr, lp['ssm_c_re'].astype(f32))
         - jnp.einsum('bsgp,ghp->bsgh', xi, lp['ssm_c_im'].astype(f32)))
    y = y.reshape(b, s, D_SSM) + lp['ssm_d'].astype(f32) * uf
    y = jax.nn.gelu(y)
    z = y @ lp['w_glu'].astype(f32) + lp['b_glu'].astype(f32)
    val, gate = jnp.split(z, 2, axis=-1)
    return (val * jax.nn.sigmoid(gate)).astype(u.dtype), xr[:, -1], xi[:, -1]


def rel_position_bias(table, n_q, n_k, n_past):
    dist = jnp.arange(n_q)[:, None] + n_past - jnp.arange(n_k)[None, :]
    idx = jnp.clip(dist, -REL_CLIP, REL_CLIP) + REL_CLIP
    return table.astype(jnp.float32)[:, idx]


def chunk_band_attention_prompt(q, k, v, table):
    b, s, h, dh = q.shape
    nc = s // CHUNK
    nb = N_PREV_CHUNKS + 1
    qc = q.reshape(b, nc, CHUNK, h, dh)
    pad = ((0, 0), (N_PREV_CHUNKS, 0), (0, 0), (0, 0), (0, 0))
    kp = jnp.pad(k.reshape(b, nc, CHUNK, h, dh), pad)
    vp = jnp.pad(v.reshape(b, nc, CHUNK, h, dh), pad)
    kb = jnp.concatenate([kp[:, o:o + nc] for o in range(nb)], axis=2)
    vb = jnp.concatenate([vp[:, o:o + nc] for o in range(nb)], axis=2)
    scores = jnp.einsum('bnqhd,bnkhd->bnhqk', qc, kb, preferred_element_type=jnp.float32) * ATT_SCALE
    scores = scores + rel_position_bias(table, CHUNK, nb * CHUNK, BAND)[None, None]
    chunk_ok = (jnp.arange(nc)[:, None] + jnp.arange(nb)[None, :]) >= N_PREV_CHUNKS
    key_ok = jnp.repeat(chunk_ok, CHUNK, axis=1)
    scores = jnp.where(key_ok[None, :, None, None, :], scores, NEG_INF)
    p = jax.nn.softmax(scores, axis=-1).astype(v.dtype)
    out = jnp.einsum('bnhqk,bnkhd->bnqhd', p, vb)
    return out.reshape(b, s, h * dh)


def chunk_band_attention_sample(q, k, v, cache_k, cache_v, table):
    b, n, h, dh = q.shape
    w = cache_k.shape[1]
    kk = jnp.concatenate([cache_k.astype(k.dtype), k], axis=1)
    vv = jnp.concatenate([cache_v.astype(v.dtype), v], axis=1)
    scores = jnp.einsum('bqhd,bkhd->bhqk', q, kk, preferred_element_type=jnp.float32) * ATT_SCALE
    scores = scores + rel_position_bias(table, n, w + n, w)[None]
    p = jax.nn.softmax(scores, axis=-1).astype(v.dtype)
    out = jnp.einsum('bhqk,bkhd->bqhd', p, vv)
    return out.reshape(b, n, h * dh), kk[:, n:], vv[:, n:]


def memory_kv(mem, w_mem_kv):
    kv = jnp.einsum('bmd,de->bme', mem, w_mem_kv)
    mk, mv = jnp.split(kv, 2, axis=-1)
    return split_heads(mk), split_heads(mv)


def memory_attention(q, mk, mv):
    b, s, h, dh = q.shape
    scores = jnp.einsum('bshd,bmhd->bhsm', q, mk, preferred_element_type=jnp.float32) * ATT_SCALE
    p = jax.nn.softmax(scores, axis=-1).astype(mv.dtype)
    return jnp.einsum('bhsm,bmhd->bshd', p, mv).reshape(b, s, h * dh)


def clamped_swiglu(gu):
    gate, lin = jnp.split(gu, 2, axis=-1)
    gate = jnp.minimum(gate, SWIGLU_LIMIT)
    lin = jnp.clip(lin, -SWIGLU_LIMIT, SWIGLU_LIMIT)
    return gate * jax.nn.sigmoid(SWIGLU_ALPHA * gate) * (lin + 1.0)


def moe(x, w_router, b_router, w_gu, b_gu, w_down, b_down):
    t, d = x.shape
    logits = jnp.dot(x, w_router, preferred_element_type=jnp.float32) + b_router.astype(jnp.float32)
    top_logit, top_idx = lax.top_k(logits, TOP_K)
    gates = jax.nn.softmax(top_logit, axis=-1)
    n_assign = t * TOP_K
    e_flat = top_idx.reshape(-1)
    order = jnp.argsort(e_flat)
    e_sorted = e_flat[order]
    tok_sorted = (order // TOP_K).astype(jnp.int32)
    gate_sorted = gates.reshape(-1)[order]
    counts = jnp.bincount(e_flat, length=N_EXPERTS)
    starts = jnp.cumsum(counts) - counts
    padded = (counts + MOE_BLOCK - 1) // MOE_BLOCK * MOE_BLOCK
    pad_ends = jnp.cumsum(padded)
    pad_starts = pad_ends - padded
    n_blocks = (n_assign + MOE_BLOCK - 1) // MOE_BLOCK + N_EXPERTS
    cap = n_blocks * MOE_BLOCK
    dest = pad_starts[e_sorted] + jnp.arange(n_assign) - starts[e_sorted]
    slot_tok = jnp.zeros((cap,), jnp.int32).at[dest].set(tok_sorted)
    slot_gate = jnp.zeros((cap,), jnp.float32).at[dest].set(gate_sorted)
    block_expert = jnp.minimum(
        jnp.searchsorted(pad_ends, jnp.arange(n_blocks) * MOE_BLOCK, side='right'), N_EXPERTS - 1)

    def expert_block(args):
        tok, e = args
        hb = clamped_swiglu(x[tok] @ w_gu[e] + b_gu[e])
        return hb @ w_down[e] + b_down[e]

    outs = lax.map(expert_block, (slot_tok.reshape(n_blocks, MOE_BLOCK), block_expert))
    contrib = outs.reshape(cap, d).astype(jnp.float32) * slot_gate[:, None]
    return jax.ops.segment_sum(contrib, slot_tok, num_segments=t).astype(x.dtype)


def merge_and_channel_mix(x, y_ssm, y_att, y_mem, lp):
    merged = jnp.concatenate([rms_norm(y_ssm, lp['g_ssm']), rms_norm(y_att, lp['g_att']),
                              rms_norm(y_mem, lp['g_mem'])], axis=-1)
    h = layer_norm(DEEPNORM_ALPHA * x + merged @ lp['w_out'], lp['ln1_g'], lp['ln1_b'])
    b, s, d = h.shape
    f = moe(h.reshape(b * s, d), lp['w_router'], lp['b_router'], lp['w_gu'], lp['b_gu'],
            lp['w_down'], lp['b_down']).reshape(b, s, d)
    return layer_norm(DEEPNORM_ALPHA * h + f, lp['ln2_g'], lp['ln2_b'])


def prompt_layer(x, mem, lp):
    b, s, _ = x.shape
    u, q, k, v, qm = in_projection(x, lp['w_in'])
    h0 = jnp.zeros((b, N_SSM_GROUPS, SSM_STATE), jnp.float32)
    y_ssm, h_re, h_im = ssm_branch(u, h0, h0, lp)
    y_att = chunk_band_attention_prompt(q, k, v, lp['rel_bias'])
    mk, mv = memory_kv(mem, lp['w_mem_kv'])
    y_mem = memory_attention(qm, mk, mv)
    y = merge_and_channel_mix(x, y_ssm, y_att, y_mem, lp)
    w = min(BAND, s)
    return y, k[:, s - w:], v[:, s - w:], mk, mv, h_re, h_im


def sample_layer(x, cache_k, cache_v, mk, mv, h_re, h_im, lp):
    u, q, k, v, qm = in_projection(x, lp['w_in'])
    y_ssm, n_re, n_im = ssm_branch(u, h_re, h_im, lp)
    y_att, nk, nv = chunk_band_attention_sample(q, k, v, cache_k, cache_v, lp['rel_bias'])
    y_mem = memory_attention(qm, mk, mv)
    y = merge_and_channel_mix(x, y_ssm, y_att, y_mem, lp)
    return y, nk, nv, n_re, n_im


def setup_inputs(seed: int = 0) -> dict:
    key = jax.random.key(seed)
    ks = jax.random.split(key, 40)
    f32 = jnp.float32
    L = DEPTH
    G = N_SSM_GROUPS
    P = SSM_STATE
    w_band = min(BAND, PAST_LEN)
    beta = DEEPNORM_BETA

    def nrm(k, shape, scale=1.0):
        return scale * jax.random.normal(k, shape, f32)

    in_col_scale = jnp.concatenate([jnp.ones((D_SSM + 2 * D_ATT,), f32), jnp.full((D_ATT,), beta, f32),
                                    jnp.ones((D_MEM,), f32)])
    kv_col_scale = jnp.concatenate([jnp.ones((D_MEM,), f32), jnp.full((D_MEM,), beta, f32)])
    n_idx = jnp.arange(P, dtype=f32)
    return {
        'x_prompt': nrm(ks[0], (BATCH, SEQ, D_MODEL)),
        'x_sample': nrm(ks[1], (DEC_BATCH, DEC_SEQ, D_MODEL)),
        'cache_attn_k': nrm(ks[2], (L, DEC_BATCH, w_band, N_ATT_HEADS, HEAD_DIM)),
        'cache_attn_v': nrm(ks[3], (L, DEC_BATCH, w_band, N_ATT_HEADS, HEAD_DIM), beta),
        'cache_mem_k': nrm(ks[4], (L, DEC_BATCH, N_MEM, N_MEM_HEADS, HEAD_DIM)),
        'cache_mem_v': nrm(ks[5], (L, DEC_BATCH, N_MEM, N_MEM_HEADS, HEAD_DIM), beta),
        'state_ssm_re': nrm(ks[6], (L, DEC_BATCH, G, P), 0.1),
        'state_ssm_im': nrm(ks[7], (L, DEC_BATCH, G, P), 0.1),
        'mem_prompt': nrm(ks[8], (BATCH, N_MEM, D_MODEL)),
        'w_in': nrm(ks[9], (L, D_MODEL, D_IN), D_MODEL ** -0.5) * in_col_scale,
        'lam_re': -0.5 + nrm(ks[10], (L, G, P), 0.01),
        'lam_im': math.pi * n_idx + nrm(ks[11], (L, G, P), 0.01),
        'log_dt': jax.random.uniform(ks[12], (L, G), f32, math.log(1e-3), math.log(1e-1)),
        'ssm_b_re': nrm(ks[13], (L, G, P, SSM_GROUP), (2 * SSM_GROUP) ** -0.5),
        'ssm_b_im': nrm(ks[14], (L, G, P, SSM_GROUP), (2 * SSM_GROUP) ** -0.5),
        'ssm_c_re': nrm(ks[15], (L, G, SSM_GROUP, P), P ** -0.5),
        'ssm_c_im': nrm(ks[16], (L, G, SSM_GROUP, P), P ** -0.5),
        'ssm_d': nrm(ks[17], (L, D_SSM), 0.5),
        'w_glu': nrm(ks[18], (L, D_SSM, 2 * D_SSM), D_SSM ** -0.5),
        'b_glu': nrm(ks[19], (L, 2 * D_SSM), 0.02),
        'rel_bias': nrm(ks[20], (L, N_ATT_HEADS, 2 * REL_CLIP + 1), 0.5),
        'w_mem_kv': nrm(ks[21], (L, D_MODEL, 2 * D_MEM), D_MODEL ** -0.5) * kv_col_scale,
        'g_ssm': 1.0 + nrm(ks[22], (L, D_SSM), 0.05),
        'g_att': 1.0 + nrm(ks[23], (L, D_ATT), 0.05),
        'g_mem': 1.0 + nrm(ks[24], (L, D_MEM), 0.05),
        'w_out': nrm(ks[25], (L, D_MODEL, D_MODEL), beta * D_MODEL ** -0.5),
        'ln1_g': 1.0 + nrm(ks[26], (L, D_MODEL), 0.05),
        'ln1_b': nrm(ks[27], (L, D_MODEL), 0.02),
        'w_router': nrm(ks[28], (L, D_MODEL, N_EXPERTS), D_MODEL ** -0.5),
        'b_router': nrm(ks[29], (L, N_EXPERTS), 0.01),
        'w_gu': nrm(ks[30], (L, N_EXPERTS, D_MODEL, 2 * D_FF), D_MODEL ** -0.5),
        'b_gu': nrm(ks[31], (L, N_EXPERTS, 2 * D_FF), 0.02),
        'w_down': nrm(ks[32], (L, N_EXPERTS, D_FF, D_MODEL), beta * D_FF ** -0.5),
        'b_down': nrm(ks[33], (L, N_EXPERTS, D_MODEL), 0.02),
        'ln2_g': 1.0 + nrm(ks[34], (L, D_MODEL), 0.05),
        'ln2_b': nrm(ks[35], (L, D_MODEL), 0.02),
    }


def reference(x_prompt, x_sample, cache_attn_k, cache_attn_v, cache_mem_k, cache_mem_v,
              state_ssm_re, state_ssm_im, mem_prompt, w_in, lam_re, lam_im, log_dt,
              ssm_b_re, ssm_b_im, ssm_c_re, ssm_c_im, ssm_d, w_glu, b_glu, rel_bias, w_mem_kv,
              g_ssm, g_att, g_mem, w_out, ln1_g, ln1_b, w_router, b_router, w_gu, b_gu,
              w_down, b_down, ln2_g, ln2_b):
    xp = x_prompt
    xs = x_sample
    ak_p, av_p, mk_p, mv_p, sr_p, si_p = [], [], [], [], [], []
    ak_s, av_s, sr_s, si_s = [], [], [], []
    for l in range(DEPTH):
        lp = {
            'w_in': w_in[l], 'lam_re': lam_re[l], 'lam_im': lam_im[l], 'log_dt': log_dt[l],
            'ssm_b_re': ssm_b_re[l], 'ssm_b_im': ssm_b_im[l], 'ssm_c_re': ssm_c_re[l],
            'ssm_c_im': ssm_c_im[l], 'ssm_d': ssm_d[l], 'w_glu': w_glu[l], 'b_glu': b_glu[l],
            'rel_bias': rel_bias[l], 'w_mem_kv': w_mem_kv[l], 'g_ssm': g_ssm[l], 'g_att': g_att[l],
            'g_mem': g_mem[l], 'w_out': w_out[l], 'ln1_g': ln1_g[l], 'ln1_b': ln1_b[l],
            'w_router': w_router[l], 'b_router': b_router[l], 'w_gu': w_gu[l], 'b_gu': b_gu[l],
            'w_down': w_down[l], 'b_down': b_down[l], 'ln2_g': ln2_g[l], 'ln2_b': ln2_b[l],
        }
        xp, k_p, v_p, mk, mv, r_p, i_p = prompt_layer(xp, mem_prompt, lp)
        ak_p.append(k_p)
        av_p.append(v_p)
        mk_p.append(mk)
        mv_p.append(mv)
        sr_p.append(r_p)
        si_p.append(i_p)
        xs, k_s, v_s, r_s, i_s = sample_layer(xs, cache_attn_k[l], cache_attn_v[l], cache_mem_k[l],
                                              cache_mem_v[l], state_ssm_re[l], state_ssm_im[l], lp)
        ak_s.append(k_s)
        av_s.append(v_s)
        sr_s.append(r_s)
        si_s.append(i_s)
    return (xp, xs,
            jnp.stack(ak_p), jnp.stack(av_p), jnp.stack(mk_p), jnp.stack(mv_p),
            jnp.stack(sr_p), jnp.stack(si_p),
            jnp.stack(ak_s), jnp.stack(av_s), jnp.stack(sr_s), jnp.stack(si_s))
```

```python
import functools

import jax
import jax.numpy as jnp
from jax import lax
from jax.experimental import pallas as pl
from jax.experimental.pallas import tpu as pltpu

F32 = jnp.float32
BF16 = jnp.bfloat16
I32 = jnp.int32

D_MODEL = 1024
D_SSM = 512
D_ATT = 256
D_MEM = 256
D_IN = D_SSM + 3 * D_ATT + D_MEM
D_REST = D_IN - D_SSM
HEAD_DIM = 64
N_HEADS = 4
N_GROUPS = 32
SSM_GROUP = 16
SSM_STATE = 64
D_STATE = N_GROUPS * SSM_STATE
CHUNK = 64
N_PREV_CHUNKS = 8
BAND = N_PREV_CHUNKS * CHUNK
REL_CLIP = 128
N_MEM = 256
N_EXPERTS = 32
TOP_K = 4
D_FF = D_MODEL
SWIGLU_LIMIT = 7.0
SWIGLU_ALPHA = 1.702
LN_EPS = 1e-5
NEG_INF = -1e30
ATT_SCALE = HEAD_DIM ** -0.5
DEEPNORM_ALPHA = 2.0 ** 0.25

V7X_VMEM_LIMIT = 56 * 1024 * 1024
ATT_TQ = 4 * CHUNK
SCAN_LANES = 512
SCAN_ROWS = 1024
EXPERT_TM = 256
TOKEN_TM = 256

_NT = (((1,), (1,)), ((), ()))


def _params(n_axes, vmem=V7X_VMEM_LIMIT):
    return pltpu.CompilerParams(dimension_semantics=("arbitrary",) * n_axes,
                                vmem_limit_bytes=vmem)


def _in_proj_kernel(x_ref, w_ref, u_ref, z_ref, wb_ref):
    @pl.when((pl.program_id(0) == 0) & (pl.program_id(1) == 0))
    def _():
        wb_ref[...] = w_ref[...].astype(BF16)

    z = jnp.dot(x_ref[0].astype(BF16), wb_ref[...], preferred_element_type=F32)
    u_ref[...] = z[:, :D_SSM]
    z_ref[0] = z[:, D_SSM:]


def _in_proj(x, w_in, ts):
    b, s, _ = x.shape
    return pl.pallas_call(
        _in_proj_kernel,
        grid=(b, s // ts),
        in_specs=[pl.BlockSpec((1, ts, D_MODEL), lambda i, j: (i, j, 0)),
                  pl.BlockSpec((D_MODEL, D_IN), lambda i, j: (0, 0))],
        out_specs=[pl.BlockSpec((ts, D_SSM), lambda i, j: (j, i)),
                   pl.BlockSpec((1, ts, D_REST), lambda i, j: (i, j, 0))],
        out_shape=[jax.ShapeDtypeStruct((s, b * D_SSM), F32),
                   jax.ShapeDtypeStruct((b, s, D_REST), F32)],
        scratch_shapes=[pltpu.VMEM((D_MODEL, D_IN), BF16)],
        compiler_params=_params(2),
        name="in_proj",
    )(x, w_in)


def _mem_kv_kernel(m_ref, w_ref, mk_ref, mv_ref):
    kv = jnp.dot(m_ref[0].astype(BF16), w_ref[...].astype(BF16), preferred_element_type=F32)
    mk_ref[0] = kv[:, :D_MEM]
    mv_ref[0] = kv[:, D_MEM:]


def _mem_kv(mem, w_mem_kv):
    b = mem.shape[0]
    return pl.pallas_call(
        _mem_kv_kernel,
        grid=(b,),
        in_specs=[pl.BlockSpec((1, N_MEM, D_MODEL), lambda i: (i, 0, 0)),
                  pl.BlockSpec((D_MODEL, 2 * D_MEM), lambda i: (0, 0))],
        out_specs=[pl.BlockSpec((1, N_MEM, D_MEM), lambda i: (i, 0, 0)),
                   pl.BlockSpec((1, N_MEM, D_MEM), lambda i: (i, 0, 0))],
        out_shape=[jax.ShapeDtypeStruct((b, N_MEM, D_MEM), F32)] * 2,
        compiler_params=_params(1),
        name="mem_kv",
    )(mem, w_mem_kv)


def _ssm_kernel(u_ref, h0r_ref, h0i_ref, lr_ref, li_ref, ldt_ref, bre_ref, bim_ref,
                cre_ref, cim_ref, d_ref, wg_ref, bg_ref,
                y_ref, sr_ref, si_ref,
                a_sc, bbr_sc, bbi_sc, cr_sc, ci_sc, wg_sc, str_sc, sti_sc, xr_sc, xi_sc,
                *, n_batch):
    n_rows = u_ref.shape[0]
    n_steps = n_rows // n_batch

    @pl.when(pl.program_id(0) == 0)
    def _():
        lr = lr_ref[...]
        li = li_ref[...]
        dt = jnp.exp(ldt_ref[...])
        mag = jnp.exp(lr * dt)
        ar = mag * jnp.cos(li * dt)
        ai = mag * jnp.sin(li * dt)
        den = lr * lr + li * li
        fr = ((ar - 1.0) * lr + ai * li) / den
        fi = (ai * lr - (ar - 1.0) * li) / den
        a_sc[0:1, :] = ar
        a_sc[1:2, :] = ai
        for j in range(4):
            frj = fr[:, 512 * j:512 * (j + 1)]
            fij = fi[:, 512 * j:512 * (j + 1)]
            bbr_sc[j] = (frj * bre_ref[j] - fij * bim_ref[j]).astype(BF16)
            bbi_sc[j] = (frj * bim_ref[j] + fij * bre_ref[j]).astype(BF16)
            cr_sc[j] = cre_ref[j].astype(BF16)
            ci_sc[j] = cim_ref[j].astype(BF16)
        wg_sc[...] = wg_ref[...].astype(BF16)
        str_sc[...] = h0r_ref[...]
        sti_sc[...] = h0i_ref[...]

    u = u_ref[...]
    ub = u.astype(BF16)
    for j in range(4):
        uc = ub[:, 128 * j:128 * (j + 1)]
        xr_sc[:, 512 * j:512 * (j + 1)] = jnp.dot(uc, bbr_sc[j], preferred_element_type=F32)
        xi_sc[:, 512 * j:512 * (j + 1)] = jnp.dot(uc, bbi_sc[j], preferred_element_type=F32)

    for c in range(D_STATE // SCAN_LANES):
        lo = c * SCAN_LANES
        ar = jnp.broadcast_to(a_sc[0:1, lo:lo + SCAN_LANES], (n_batch, SCAN_LANES))
        ai = jnp.broadcast_to(a_sc[1:2, lo:lo + SCAN_LANES], (n_batch, SCAN_LANES))

        def step(t, carry, lo=lo, ar=ar, ai=ai):
            sr, si = carry
            r0 = pl.multiple_of(t * n_batch, n_batch)
            nr = ar * sr - ai * si + xr_sc[pl.ds(r0, n_batch), lo:lo + SCAN_LANES]
            ni = ar * si + ai * sr + xi_sc[pl.ds(r0, n_batch), lo:lo + SCAN_LANES]
            xr_sc[pl.ds(r0, n_batch), lo:lo + SCAN_LANES] = nr
            xi_sc[pl.ds(r0, n_batch), lo:lo + SCAN_LANES] = ni
            return nr, ni

        sr, si = lax.fori_loop(0, n_steps, step,
                               (str_sc[:, lo:lo + SCAN_LANES], sti_sc[:, lo:lo + SCAN_LANES]),
                               unroll=4)
        str_sc[:, lo:lo + SCAN_LANES] = sr
        sti_sc[:, lo:lo + SCAN_LANES] = si

    pieces = []
    for j in range(4):
        xr = xr_sc[:, 512 * j:512 * (j + 1)].astype(BF16)
        xi = xi_sc[:, 512 * j:512 * (j + 1)].astype(BF16)
        pieces.append(jnp.dot(xr, cr_sc[j], preferred_element_type=F32)
                      - jnp.dot(xi, ci_sc[j], preferred_element_type=F32))
    y = jnp.concatenate(pieces, axis=1) + d_ref[...] * u
    y = jax.nn.gelu(y)
    z = jnp.dot(y.astype(BF16), wg_sc[...], preferred_element_type=F32) + bg_ref[...]
    y_ref[...] = z[:, :D_SSM] * jax.nn.sigmoid(z[:, D_SSM:])
    sr_ref[...] = str_sc[...]
    si_ref[...] = sti_sc[...]


def _block_diag_b(b):
    bt = b.transpose(0, 2, 1).reshape(4, 8, SSM_GROUP, SSM_STATE)
    same = jnp.eye(8, dtype=bool)[None, :, None, :, None]
    t = jnp.where(same, bt[:, :, :, None, :], 0.0)
    return t.reshape(4, 8 * SSM_GROUP, 8 * SSM_STATE)


def _block_diag_c(c):
    ct = c.transpose(0, 2, 1).reshape(4, 8, SSM_STATE, SSM_GROUP)
    same = jnp.eye(8, dtype=bool)[None, :, None, :, None]
    t = jnp.where(same, ct[:, :, :, None, :], 0.0)
    return t.reshape(4, 8 * SSM_STATE, 8 * SSM_GROUP)


def _ssm(u_tm, h0_re, h0_im, lp, n_batch):
    s = u_tm.shape[0]
    rows = s * n_batch
    tr = min(SCAN_ROWS, rows)
    u_rows = u_tm.reshape(rows, D_SSM)
    flat = lambda a: a.reshape(1, D_STATE)
    ldt = jnp.repeat(lp['log_dt'], SSM_STATE).reshape(1, D_STATE)
    const2 = lambda i: (0, 0)
    const3 = lambda i: (0, 0, 0)
    y, sr, si = pl.pallas_call(
        functools.partial(_ssm_kernel, n_batch=n_batch),
        grid=(rows // tr,),
        in_specs=[pl.BlockSpec((tr, D_SSM), lambda i: (i, 0)),
                  pl.BlockSpec((n_batch, D_STATE), const2),
                  pl.BlockSpec((n_batch, D_STATE), const2),
                  pl.BlockSpec((1, D_STATE), const2),
                  pl.BlockSpec((1, D_STATE), const2),
                  pl.BlockSpec((1, D_STATE), const2),
                  pl.BlockSpec((4, 128, 512), const3),
                  pl.BlockSpec((4, 128, 512), const3),
                  pl.BlockSpec((4, 512, 128), const3),
                  pl.BlockSpec((4, 512, 128), const3),
                  pl.BlockSpec((1, D_SSM), const2),
                  pl.BlockSpec((D_SSM, 2 * D_SSM), const2),
                  pl.BlockSpec((1, 2 * D_SSM), const2)],
        out_specs=[pl.BlockSpec((tr, D_SSM), lambda i: (i, 0)),
                   pl.BlockSpec((n_batch, D_STATE), const2),
                   pl.BlockSpec((n_batch, D_STATE), const2)],
        out_shape=[jax.ShapeDtypeStruct((rows, D_SSM), F32),
                   jax.ShapeDtypeStruct((n_batch, D_STATE), F32),
                   jax.ShapeDtypeStruct((n_batch, D_STATE), F32)],
        scratch_shapes=[pltpu.VMEM((2, D_STATE), F32),
                        pltpu.VMEM((4, 128, 512), BF16), pltpu.VMEM((4, 128, 512), BF16),
                        pltpu.VMEM((4, 512, 128), BF16), pltpu.VMEM((4, 512, 128), BF16),
                        pltpu.VMEM((D_SSM, 2 * D_SSM), BF16),
                        pltpu.VMEM((n_batch, D_STATE), F32), pltpu.VMEM((n_batch, D_STATE), F32),
                        pltpu.VMEM((tr, D_STATE), F32), pltpu.VMEM((tr, D_STATE), F32)],
        compiler_params=_params(1),
        name="ssm",
    )(u_rows, h0_re.reshape(n_batch, D_STATE), h0_im.reshape(n_batch, D_STATE),
      flat(lp['lam_re']), flat(lp['lam_im']), ldt,
      _block_diag_b(lp['ssm_b_re']), _block_diag_b(lp['ssm_b_im']),
      _block_diag_c(lp['ssm_c_re']), _block_diag_c(lp['ssm_c_im']),
      lp['ssm_d'].reshape(1, D_SSM), lp['w_glu'], lp['b_glu'].reshape(1, 2 * D_SSM))
    return y.reshape(s, n_batch * D_SSM), sr, si


def _softmax_pv(s, v):
    m = jnp.max(s, axis=-1, keepdims=True)
    p = jnp.exp(s - m)
    l = jnp.sum(p, axis=-1, keepdims=True)
    return jnp.dot(p.astype(BF16), v, preferred_element_type=F32) / l


def _attend(q, k, v, out_ref, bias_ref=None, valid=None):
    qb = (q * ATT_SCALE).astype(BF16)
    for h in range(N_HEADS):
        sl = slice(HEAD_DIM * h, HEAD_DIM * (h + 1))
        s = lax.dot_general(qb[:, sl], k[:, sl], _NT, preferred_element_type=F32)
        if bias_ref is not None:
            s = s + bias_ref[h]
        if valid is not None:
            s = jnp.where(valid, s, NEG_INF)
        out_ref[0, :, sl] = _softmax_pv(s, v[:, sl])


def _attn_prompt_kernel(q_ref, k0_ref, k1_ref, k2_ref, v0_ref, v1_ref, v2_ref, qm_ref,
                        mk_ref, mv_ref, bias_ref, ya_ref, ym_ref):
    tq = q_ref.shape[1]
    k = jnp.concatenate([k0_ref[0], k1_ref[0], k2_ref[0]], axis=0).astype(BF16)
    v = jnp.concatenate([v0_ref[0], v1_ref[0], v2_ref[0]], axis=0).astype(BF16)
    kpos = (pl.program_id(1) - 2) * tq + lax.broadcasted_iota(I32, (1, 3 * tq), 1)
    _attend(q_ref[0], k, v, ya_ref, bias_ref, kpos >= 0)
    _attend(qm_ref[0], mk_ref[0].astype(BF16), mv_ref[0].astype(BF16), ym_ref)


def _attn_sample_kernel(q_ref, kn_ref, vn_ref, qm_ref, ck_ref, cv_ref, mk_ref, mv_ref, bias_ref,
                        ya_ref, ym_ref, nk_ref, nv_ref):
    n = kn_ref.shape[1]
    kk = jnp.concatenate([ck_ref[0], kn_ref[0]], axis=0)
    vv = jnp.concatenate([cv_ref[0], vn_ref[0]], axis=0)
    nk_ref[0] = kk[n:]
    nv_ref[0] = vv[n:]
    _attend(q_ref[0], kk.astype(BF16), vv.astype(BF16), ya_ref, bias_ref)
    _attend(qm_ref[0], mk_ref[0].astype(BF16), mv_ref[0].astype(BF16), ym_ref)


def _rel_bias(table, n_q, n_k, band_mask):
    qi = jnp.arange(n_q)[:, None]
    kj = jnp.arange(n_k)[None, :]
    idx = jnp.clip(qi + BAND - kj, -REL_CLIP, REL_CLIP) + REL_CLIP
    bias = table.astype(F32)[:, idx]
    if band_mask:
        off = kj // CHUNK - qi // CHUNK
        ok = (off >= 0) & (off <= N_PREV_CHUNKS)
        bias = jnp.where(ok[None], bias, NEG_INF)
    return bias


def _attn_prompt(zr, mk, mv, table):
    b, s, _ = zr.shape
    tq = ATT_TQ
    bias = _rel_bias(table, tq, 3 * tq, True)
    col = lambda c: (lambda i, j: (i, j, c))
    prev = lambda c, d: (lambda i, j: (i, jnp.maximum(j - d, 0), c))
    blk = lambda: (1, tq, D_ATT)
    return pl.pallas_call(
        _attn_prompt_kernel,
        grid=(b, s // tq),
        in_specs=[pl.BlockSpec(blk(), col(0)),
                  pl.BlockSpec(blk(), prev(1, 2)), pl.BlockSpec(blk(), prev(1, 1)),
                  pl.BlockSpec(blk(), col(1)),
                  pl.BlockSpec(blk(), prev(2, 2)), pl.BlockSpec(blk(), prev(2, 1)),
                  pl.BlockSpec(blk(), col(2)),
                  pl.BlockSpec(blk(), col(3)),
                  pl.BlockSpec((1, N_MEM, D_MEM), lambda i, j: (i, 0, 0)),
                  pl.BlockSpec((1, N_MEM, D_MEM), lambda i, j: (i, 0, 0)),
                  pl.BlockSpec((N_HEADS, tq, 3 * tq), lambda i, j: (0, 0, 0))],
        out_specs=[pl.BlockSpec(blk(), col(0)), pl.BlockSpec(blk(), col(0))],
        out_shape=[jax.ShapeDtypeStruct((b, s, D_ATT), F32),
                   jax.ShapeDtypeStruct((b, s, D_MEM), F32)],
        compiler_params=_params(2),
        name="attn_prompt",
    )(zr, zr, zr, zr, zr, zr, zr, zr, mk, mv, bias)


def _attn_sample(zr, cache_k, cache_v, mk, mv, table):
    b, n, _ = zr.shape
    w = cache_k.shape[1]
    bias = _rel_bias(table, n, w + n, False)
    col = lambda c: (lambda i: (i, 0, c))
    blk = (1, n, D_ATT)
    cblk = (1, w, D_ATT)
    mblk = (1, N_MEM, D_MEM)
    row = lambda i: (i, 0, 0)
    return pl.pallas_call(
        _attn_sample_kernel,
        grid=(b,),
        in_specs=[pl.BlockSpec(blk, col(0)), pl.BlockSpec(blk, col(1)), pl.BlockSpec(blk, col(2)),
                  pl.BlockSpec(blk, col(3)),
                  pl.BlockSpec(cblk, row), pl.BlockSpec(cblk, row),
                  pl.BlockSpec(mblk, row), pl.BlockSpec(mblk, row),
                  pl.BlockSpec((N_HEADS, n, w + n), lambda i: (0, 0, 0))],
        out_specs=[pl.BlockSpec(blk, row), pl.BlockSpec(blk, row),
                   pl.BlockSpec(cblk, row), pl.BlockSpec(cblk, row)],
        out_shape=[jax.ShapeDtypeStruct((b, n, D_ATT), F32),
                   jax.ShapeDtypeStruct((b, n, D_MEM), F32),
                   jax.ShapeDtypeStruct((b, w, D_ATT), F32),
                   jax.ShapeDtypeStruct((b, w, D_ATT), F32)],
        compiler_params=_params(1),
        name="attn_sample",
    )(zr, zr, zr, zr, cache_k, cache_v, mk, mv, bias)


def _rms(x, g):
    return x * lax.rsqrt(jnp.mean(jnp.square(x), axis=-1, keepdims=True) + LN_EPS) * g


def _layer_norm(x, g, b):
    mu = jnp.mean(x, axis=-1, keepdims=True)
    xc = x - mu
    var = jnp.mean(jnp.square(xc), axis=-1, keepdims=True)
    return xc * lax.rsqrt(var + LN_EPS) * g + b


def _split_bf16(a):
    hi = a.astype(BF16)
    lo = (a - hi.astype(F32)).astype(BF16)
    return hi, lo


def _merge_kernel(x_ref, ys_ref, ya_ref, ym_ref, gs_ref, ga_ref, gm_ref, wo_ref, l1g_ref, l1b_ref,
                  wrt_ref, brt_ref,
                  h_ref, idx_ref, gate_ref, rank_ref, cnt_ref,
                  wo_sc, cnt_sc, *, nb):
    st = x_ref.shape[1]
    tm = nb * st

    @pl.when((pl.program_id(0) == 0) & (pl.program_id(1) == 0))
    def _():
        wo_sc[...] = wo_ref[...].astype(BF16)
        cnt_sc[...] = jnp.zeros_like(cnt_sc)

    x = x_ref[...].reshape(tm, D_MODEL)
    ysb = ys_ref[...]
    ys = jnp.concatenate([ysb[:, D_SSM * i:D_SSM * (i + 1)] for i in range(nb)], axis=0)
    ya = ya_ref[...].reshape(tm, D_ATT)
    ym = ym_ref[...].reshape(tm, D_MEM)
    a = _rms(ys, gs_ref[...]).astype(BF16)
    b = _rms(ya, ga_ref[...]).astype(BF16)
    c = _rms(ym, gm_ref[...]).astype(BF16)
    mix = (jnp.dot(a, wo_sc[0:D_SSM, :], preferred_element_type=F32)
           + jnp.dot(b, wo_sc[D_SSM:D_SSM + D_ATT, :], preferred_element_type=F32)
           + jnp.dot(c, wo_sc[D_SSM + D_ATT:, :], preferred_element_type=F32))
    h = _layer_norm(DEEPNORM_ALPHA * x + mix, l1g_ref[...], l1b_ref[...])
    h_ref[...] = h

    h_hi, h_lo = _split_bf16(h)
    w_hi, w_lo = _split_bf16(wrt_ref[...])
    logits = (lax.dot_general(w_hi, h_hi, _NT, preferred_element_type=F32)
              + lax.dot_general(w_hi, h_lo, _NT, preferred_element_type=F32)
              + lax.dot_general(w_lo, h_hi, _NT, preferred_element_type=F32)
              + brt_ref[...])
    erow = lax.broadcasted_iota(I32, (N_EXPERTS, tm), 0).astype(F32)
    tops, onehots = [], []
    l = logits
    for k in range(TOP_K):
        m = jnp.max(l, axis=0, keepdims=True)
        e = jnp.min(jnp.where(l == m, erow, float(N_EXPERTS)), axis=0, keepdims=True)
        pick = erow == e
        tops.append(m)
        onehots.append(jnp.where(pick, 1.0, 0.0))
        idx_ref[k:k + 1, :] = e.astype(I32)
        l = jnp.where(pick, -jnp.inf, l)
    ex = [jnp.exp(t - tops[0]) for t in tops]
    den = ex[0] + ex[1] + ex[2] + ex[3]
    for k in range(TOP_K):
        gate_ref[k:k + 1, :] = ex[k] / den

    oh_all = onehots[0] + onehots[1] + onehots[2] + onehots[3]
    before = (lax.broadcasted_iota(I32, (tm, tm), 0) < lax.broadcasted_iota(I32, (tm, tm), 1))
    upper = jnp.where(before, 1.0, 0.0).astype(BF16)
    cum = jnp.dot(oh_all.astype(BF16), upper, preferred_element_type=F32) + cnt_sc[...]
    for k in range(TOP_K):
        rank_ref[k:k + 1, :] = jnp.sum(onehots[k] * cum, axis=0, keepdims=True).astype(I32)
    cnt_sc[...] = cnt_sc[...] + jnp.sum(oh_all, axis=1, keepdims=True)
    cnt_ref[...] = cnt_sc[...]


def _merge(x, ys_tm, ya, ym, lp, nb, st):
    b, s, _ = x.shape
    tm = nb * st
    n_s = s // st
    t = b * s
    tile = lambda i, j: (i * n_s + j)
    c2 = lambda i, j: (0, 0)
    row3 = lambda i, j: (i, j, 0)
    vec = lambda a: a.reshape(1, -1)
    return pl.pallas_call(
        functools.partial(_merge_kernel, nb=nb),
        grid=(b // nb, n_s),
        in_specs=[pl.BlockSpec((nb, st, D_MODEL), row3),
                  pl.BlockSpec((st, nb * D_SSM), lambda i, j: (j, i)),
                  pl.BlockSpec((nb, st, D_ATT), row3),
                  pl.BlockSpec((nb, st, D_MEM), row3),
                  pl.BlockSpec((1, D_SSM), c2), pl.BlockSpec((1, D_ATT), c2),
                  pl.BlockSpec((1, D_MEM), c2),
                  pl.BlockSpec((D_MODEL, D_MODEL), c2),
                  pl.BlockSpec((1, D_MODEL), c2), pl.BlockSpec((1, D_MODEL), c2),
                  pl.BlockSpec((N_EXPERTS, D_MODEL), c2), pl.BlockSpec((N_EXPERTS, 1), c2)],
        out_specs=[pl.BlockSpec((tm, D_MODEL), lambda i, j: (tile(i, j), 0)),
                   pl.BlockSpec((TOP_K, tm), lambda i, j: (0, tile(i, j))),
                   pl.BlockSpec((TOP_K, tm), lambda i, j: (0, tile(i, j))),
                   pl.BlockSpec((TOP_K, tm), lambda i, j: (0, tile(i, j))),
                   pl.BlockSpec((N_EXPERTS, 1), c2)],
        out_shape=[jax.ShapeDtypeStruct((t, D_MODEL), F32),
                   jax.ShapeDtypeStruct((TOP_K, t), I32),
                   jax.ShapeDtypeStruct((TOP_K, t), F32),
                   jax.ShapeDtypeStruct((TOP_K, t), I32),
                   jax.ShapeDtypeStruct((N_EXPERTS, 1), F32)],
        scratch_shapes=[pltpu.VMEM((D_MODEL, D_MODEL), BF16), pltpu.VMEM((N_EXPERTS, 1), F32)],
        compiler_params=_params(2),
        name="merge_router",
    )(x, ys_tm, ya, ym, vec(lp['g_ssm']), vec(lp['g_att']), vec(lp['g_mem']), lp['w_out'],
      vec(lp['ln1_g']), vec(lp['ln1_b']), lp['w_router'].T, lp['b_router'].reshape(N_EXPERTS, 1))


def _fetch_tile_indices(idx_hbm, idx_smem, isem):
    i = pl.program_id(0)
    n = pl.num_programs(0)
    slot = lax.rem(i, 2)

    def copy(tile, sl):
        return pltpu.make_async_copy(idx_hbm.at[tile], idx_smem.at[sl], isem.at[sl])

    @pl.when(i == 0)
    def _():
        copy(0, 0).start()

    @pl.when(i + 1 < n)
    def _():
        copy(i + 1, 1 - slot).start()

    copy(i, slot).wait()
    return slot


def _dispatch_kernel(dest_hbm, h_ref, xs_hbm, idx_smem, isem, dsem):
    tm = h_ref.shape[0]
    slot = _fetch_tile_indices(dest_hbm, idx_smem, isem)

    def row_copy(r, d):
        return pltpu.make_async_copy(h_ref.at[pl.ds(r, 1)], xs_hbm.at[pl.ds(d, 1)], dsem)

    def issue(r, carry):
        for k in range(TOP_K):
            row_copy(r, idx_smem[slot, k * tm + r]).start()
        return carry

    lax.fori_loop(0, tm, issue, 0, unroll=8)

    def drain(r, carry):
        for k in range(TOP_K):
            row_copy(0, 0).wait()
        return carry

    lax.fori_loop(0, tm, drain, 0, unroll=8)


def _dispatch(dest_tiles, h, cap):
    t = h.shape[0]
    tm = TOKEN_TM
    return pl.pallas_call(
        _dispatch_kernel,
        grid=(t // tm,),
        in_specs=[pl.BlockSpec(memory_space=pl.ANY),
                  pl.BlockSpec((tm, D_MODEL), lambda i: (i, 0))],
        out_specs=pl.BlockSpec(memory_space=pl.ANY),
        out_shape=jax.ShapeDtypeStruct((cap, D_MODEL), F32),
        scratch_shapes=[pltpu.SMEM((2, TOP_K * tm), I32), pltpu.SemaphoreType.DMA((2,)),
                        pltpu.SemaphoreType.DMA],
        compiler_params=_params(1),
        name="moe_dispatch",
    )(dest_tiles, h)


def _expert_kernel(be_ref, nv_ref, x_ref, wgu_ref, bgu_ref, wd_ref, bd_ref, o_ref, wgu_sc, wd_sc):
    i = pl.program_id(0)
    n_valid = nv_ref[i]

    @pl.when(n_valid > 0)
    def _():
        prev = jnp.maximum(i - 1, 0)

        @pl.when((i == 0) | (be_ref[i] != be_ref[prev]))
        def _():
            wgu_sc[...] = wgu_ref[0].astype(BF16)
            wd_sc[...] = wd_ref[0].astype(BF16)

        tm = x_ref.shape[0]
        live = lax.broadcasted_iota(I32, (tm, 1), 0) < n_valid
        x = jnp.where(live, x_ref[...], 0.0).astype(BF16)
        gu = jnp.dot(x, wgu_sc[...], preferred_element_type=F32) + bgu_ref[0]
        gate = jnp.minimum(gu[:, :D_FF], SWIGLU_LIMIT)
        lin = jnp.clip(gu[:, D_FF:], -SWIGLU_LIMIT, SWIGLU_LIMIT)
        act = gate * jax.nn.sigmoid(SWIGLU_ALPHA * gate) * (lin + 1.0)
        o_ref[...] = jnp.dot(act.astype(BF16), wd_sc[...], preferred_element_type=F32) + bd_ref[0]


def _experts(block_expert, block_valid, xs, lp):
    cap = xs.shape[0]
    tm = EXPERT_TM
    grid_spec = pltpu.PrefetchScalarGridSpec(
        num_scalar_prefetch=2,
        grid=(cap // tm,),
        in_specs=[pl.BlockSpec((tm, D_MODEL), lambda i, be, nv: (i, 0)),
                  pl.BlockSpec((1, D_MODEL, 2 * D_FF), lambda i, be, nv: (be[i], 0, 0)),
                  pl.BlockSpec((1, 1, 2 * D_FF), lambda i, be, nv: (be[i], 0, 0)),
                  pl.BlockSpec((1, D_FF, D_MODEL), lambda i, be, nv: (be[i], 0, 0)),
                  pl.BlockSpec((1, 1, D_MODEL), lambda i, be, nv: (be[i], 0, 0))],
        out_specs=pl.BlockSpec((tm, D_MODEL), lambda i, be, nv: (i, 0)),
        scratch_shapes=[pltpu.VMEM((D_MODEL, 2 * D_FF), BF16), pltpu.VMEM((D_FF, D_MODEL), BF16)],
    )
    return pl.pallas_call(
        _expert_kernel,
        grid_spec=grid_spec,
        out_shape=jax.ShapeDtypeStruct((cap, D_MODEL), F32),
        compiler_params=_params(1),
        name="moe_experts",
    )(block_expert, block_valid, xs, lp['w_gu'], lp['b_gu'].reshape(N_EXPERTS, 1, 2 * D_FF),
      lp['w_down'], lp['b_down'].reshape(N_EXPERTS, 1, D_MODEL))


def _combine_kernel(dest_hbm, h_ref, gate_ref, ys_hbm, g_ref, b_ref, y_ref,
                    idx_smem, isem, rows_sc, gsem):
    tm = h_ref.shape[0]
    slot = _fetch_tile_indices(dest_hbm, idx_smem, isem)

    def row_copy(k, r, d):
        return pltpu.make_async_copy(ys_hbm.at[pl.ds(d, 1)], rows_sc.at[k, pl.ds(r, 1)], gsem)

    def issue(r, carry):
        for k in range(TOP_K):
            row_copy(k, r, idx_smem[slot, k * tm + r]).start()
        return carry

    lax.fori_loop(0, tm, issue, 0, unroll=8)

    def drain(r, carry):
        for k in range(TOP_K):
            row_copy(k, 0, 0).wait()
        return carry

    lax.fori_loop(0, tm, drain, 0, unroll=8)

    gates = gate_ref[...]
    f = gates[:, 0:1] * rows_sc[0]
    for k in range(1, TOP_K):
        f = f + gates[:, k:k + 1] * rows_sc[k]
    y_ref[...] = _layer_norm(DEEPNORM_ALPHA * h_ref[...] + f, g_ref[...], b_ref[...])


def _combine(dest_tiles, h, gates_t, ys, lp):
    t = h.shape[0]
    tm = TOKEN_TM
    c2 = lambda i: (0, 0)
    return pl.pallas_call(
        _combine_kernel,
        grid=(t // tm,),
        in_specs=[pl.BlockSpec(memory_space=pl.ANY),
                  pl.BlockSpec((tm, D_MODEL), lambda i: (i, 0)),
                  pl.BlockSpec((tm, TOP_K), lambda i: (i, 0)),
                  pl.BlockSpec(memory_space=pl.ANY),
                  pl.BlockSpec((1, D_MODEL), c2), pl.BlockSpec((1, D_MODEL), c2)],
        out_specs=pl.BlockSpec((tm, D_MODEL), lambda i: (i, 0)),
        out_shape=jax.ShapeDtypeStruct((t, D_MODEL), F32),
        scratch_shapes=[pltpu.SMEM((2, TOP_K * tm), I32), pltpu.SemaphoreType.DMA((2,)),
                        pltpu.VMEM((TOP_K, tm, D_MODEL), F32), pltpu.SemaphoreType.DMA],
        compiler_params=_params(1),
        name="moe_combine",
    )(dest_tiles, h, gates_t, ys, lp['ln2_g'].reshape(1, D_MODEL), lp['ln2_b'].reshape(1, D_MODEL))


def _moe_and_norm(h, idx, gates, rank, counts, lp):
    t = h.shape[0]
    tm = TOKEN_TM
    te = EXPERT_TM
    n_blocks = (t * TOP_K) // te + N_EXPERTS
    cap = n_blocks * te
    counts = counts.reshape(N_EXPERTS).astype(I32)
    padded = (counts + te - 1) // te * te
    pad_ends = jnp.cumsum(padded)
    pad_starts = pad_ends - padded
    experts = jnp.arange(N_EXPERTS, dtype=I32)
    dest = rank + jnp.sum(jnp.where(idx[None] == experts[:, None, None], pad_starts[:, None, None], 0),
                          axis=0)
    dest_tiles = dest.reshape(TOP_K, t // tm, tm).transpose(1, 0, 2).reshape(t // tm, TOP_K * tm)
    blk_start = jnp.arange(n_blocks, dtype=I32) * te
    n_used = pad_ends[-1] // te
    be = jnp.minimum(jnp.sum(blk_start[:, None] >= pad_ends[None, :], axis=1), N_EXPERTS - 1).astype(I32)
    nv = jnp.clip(counts[be] - (blk_start - pad_starts[be]), 0, te).astype(I32)
    last_used = be[jnp.maximum(n_used - 1, 0)]
    used = jnp.arange(n_blocks) < n_used
    be = jnp.where(used, be, last_used)
    nv = jnp.where(used, nv, 0)
    xs = _dispatch(dest_tiles, h, cap)
    ys = _experts(be, nv, xs, lp)
    return _combine(dest_tiles, h, gates.T, ys, lp)


def _layer(x, zr_attn, u_tm, h0_re, h0_im, lp, nb, st):
    b, s, _ = x.shape
    ys_tm, s_re, s_im = _ssm(u_tm, h0_re, h0_im, lp, b)
    ya, ym = zr_attn
    h, idx, gates, rank, counts = _merge(x, ys_tm, ya, ym, lp, nb, st)
    y = _moe_and_norm(h, idx, gates, rank, counts, lp)
    return (y.reshape(b, s, D_MODEL), s_re.reshape(b, N_GROUPS, SSM_STATE),
            s_im.reshape(b, N_GROUPS, SSM_STATE))


def kernel(x_prompt, x_sample, cache_attn_k, cache_attn_v, cache_mem_k, cache_mem_v, state_ssm_re, state_ssm_im, mem_prompt, w_in, lam_re, lam_im, log_dt, ssm_b_re, ssm_b_im, ssm_c_re, ssm_c_im, ssm_d, w_glu, b_glu, rel_bias, w_mem_kv, g_ssm, g_att, g_mem, w_out, ln1_g, ln1_b, w_router, b_router, w_gu, b_gu, w_down, b_down, ln2_g, ln2_b):
    assert w_in.shape[0] == 1, "single-layer step"
    lp = dict(w_in=w_in[0], lam_re=lam_re[0], lam_im=lam_im[0], log_dt=log_dt[0],
              ssm_b_re=ssm_b_re[0], ssm_b_im=ssm_b_im[0], ssm_c_re=ssm_c_re[0], ssm_c_im=ssm_c_im[0],
              ssm_d=ssm_d[0], w_glu=w_glu[0], b_glu=b_glu[0], rel_bias=rel_bias[0],
              w_mem_kv=w_mem_kv[0], g_ssm=g_ssm[0], g_att=g_att[0], g_mem=g_mem[0], w_out=w_out[0],
              ln1_g=ln1_g[0], ln1_b=ln1_b[0], w_router=w_router[0], b_router=b_router[0],
              w_gu=w_gu[0], b_gu=b_gu[0], w_down=w_down[0], b_down=b_down[0],
              ln2_g=ln2_g[0], ln2_b=ln2_b[0])

    bp, sp, _ = x_prompt.shape
    u_tm, zr = _in_proj(x_prompt, lp['w_in'], min(512, sp))
    mk, mv = _mem_kv(mem_prompt, lp['w_mem_kv'])
    ya, ym = _attn_prompt(zr, mk, mv, lp['rel_bias'])
    zeros = jnp.zeros((bp, D_STATE), F32)
    y_p, sr_p, si_p = _layer(x_prompt, (ya, ym), u_tm, zeros, zeros, lp, 1, min(TOKEN_TM, sp))
    w = min(BAND, sp)
    heads = lambda a: a.reshape(a.shape[0], a.shape[1], N_HEADS, HEAD_DIM)
    k_p = heads(zr[:, sp - w:, D_ATT:2 * D_ATT])
    v_p = heads(zr[:, sp - w:, 2 * D_ATT:3 * D_ATT])

    bs, ss, _ = x_sample.shape
    wc = cache_attn_k.shape[2]
    u_tm_s, zr_s = _in_proj(x_sample, lp['w_in'], ss)
    ya_s, ym_s, nk, nv = _attn_sample(
        zr_s, cache_attn_k[0].reshape(bs, wc, D_ATT), cache_attn_v[0].reshape(bs, wc, D_ATT),
        cache_mem_k[0].reshape(bs, N_MEM, D_MEM), cache_mem_v[0].reshape(bs, N_MEM, D_MEM),
        lp['rel_bias'])
    nb_s = TOKEN_TM // ss
    y_s, sr_s, si_s = _layer(x_sample, (ya_s, ym_s), u_tm_s, state_ssm_re[0], state_ssm_im[0], lp,
                             nb_s, ss)

    return (y_p, y_s,
            k_p[None], v_p[None], heads(mk)[None], heads(mv)[None], sr_p[None], si_p[None],
            heads(nk)[None], heads(nv)[None], sr_s[None], si_s[None])
```

```python
import functools

import jax
import jax.numpy as jnp
from jax import lax
from jax.experimental import pallas as pl
from jax.experimental.pallas import tpu as pltpu

F32 = jnp.float32
BF16 = jnp.bfloat16
I32 = jnp.int32

D_MODEL = 1024
D_SSM = 512
D_ATT = 256
D_MEM = 256
D_IN = D_SSM + 3 * D_ATT + D_MEM
D_REST = D_IN - D_SSM
HEAD_DIM = 64
N_HEADS = 4
N_GROUPS = 32
SSM_GROUP = 16
SSM_STATE = 64
D_STATE = N_GROUPS * SSM_STATE
CHUNK = 64
N_PREV_CHUNKS = 8
BAND = N_PREV_CHUNKS * CHUNK
REL_CLIP = 128
N_MEM = 256
N_EXPERTS = 32
TOP_K = 4
D_FF = D_MODEL
SWIGLU_LIMIT = 7.0
SWIGLU_ALPHA = 1.702
LN_EPS = 1e-5
NEG_INF = -1e30
ATT_SCALE = HEAD_DIM ** -0.5
DEEPNORM_ALPHA = 2.0 ** 0.25

V7X_VMEM_LIMIT = 56 * 1024 * 1024
ATT_TQ = 4 * CHUNK
SCAN_LANES = 512
SCAN_ROWS = 1024
LANES = 128
ROW_SUBLANES = D_MODEL // LANES
EXPERT_TM = 256
TOKEN_TM = 256

_NT = (((1,), (1,)), ((), ()))


def _params(n_axes, vmem=V7X_VMEM_LIMIT):
    return pltpu.CompilerParams(dimension_semantics=("arbitrary",) * n_axes,
                                vmem_limit_bytes=vmem)


def _in_proj_kernel(x_ref, w_ref, u_ref, z_ref, wb_ref):
    @pl.when(pl.program_id(0) == 0)
    def _():
        wb_ref[...] = w_ref[...].astype(BF16)

    nb, ts, _ = x_ref.shape
    x = x_ref[...].reshape(nb * ts, D_MODEL).astype(BF16)
    z = jnp.dot(x, wb_ref[...], preferred_element_type=F32)
    for b in range(nb):
        for c in range(D_SSM // LANES):
            u_ref[c, pl.ds(b, ts, stride=nb), :] = z[b * ts:(b + 1) * ts, LANES * c:LANES * (c + 1)]
    z_ref[...] = z[:, D_SSM:].reshape(nb, ts, D_REST)


def _in_proj(x, w_in, ts):
    b, s, _ = x.shape
    return pl.pallas_call(
        _in_proj_kernel,
        grid=(s // ts,),
        in_specs=[pl.BlockSpec((b, ts, D_MODEL), lambda j: (0, j, 0)),
                  pl.BlockSpec((D_MODEL, D_IN), lambda j: (0, 0))],
        out_specs=[pl.BlockSpec((D_SSM // LANES, ts * b, LANES), lambda j: (0, j, 0)),
                   pl.BlockSpec((b, ts, D_REST), lambda j: (0, j, 0))],
        out_shape=[jax.ShapeDtypeStruct((D_SSM // LANES, s * b, LANES), F32),
                   jax.ShapeDtypeStruct((b, s, D_REST), F32)],
        scratch_shapes=[pltpu.VMEM((D_MODEL, D_IN), BF16)],
        compiler_params=_params(1),
        name="in_proj",
    )(x, w_in)


def _mem_kv_kernel(m_ref, w_ref, mk_ref, mv_ref):
    kv = jnp.dot(m_ref[0].astype(BF16), w_ref[...].astype(BF16), preferred_element_type=F32)
    mk_ref[0] = kv[:, :D_MEM]
    mv_ref[0] = kv[:, D_MEM:]


def _mem_kv(mem, w_mem_kv):
    b = mem.shape[0]
    return pl.pallas_call(
        _mem_kv_kernel,
        grid=(b,),
        in_specs=[pl.BlockSpec((1, N_MEM, D_MODEL), lambda i: (i, 0, 0)),
                  pl.BlockSpec((D_MODEL, 2 * D_MEM), lambda i: (0, 0))],
        out_specs=[pl.BlockSpec((1, N_MEM, D_MEM), lambda i: (i, 0, 0)),
                   pl.BlockSpec((1, N_MEM, D_MEM), lambda i: (i, 0, 0))],
        out_shape=[jax.ShapeDtypeStruct((b, N_MEM, D_MEM), F32)] * 2,
        compiler_params=_params(1),
        name="mem_kv",
    )(mem, w_mem_kv)


def _ssm_kernel(u_ref, h0r_ref, h0i_ref, lr_ref, li_ref, ldt_ref, bre_ref, bim_ref,
                cre_ref, cim_ref, d_ref, wg_ref, bg_ref,
                y_ref, sr_ref, si_ref,
                a_sc, bbr_sc, bbi_sc, cr_sc, ci_sc, wg_sc, str_sc, sti_sc, xr_sc, xi_sc,
                *, n_batch):
    n_rows = u_ref.shape[1]
    n_steps = n_rows // n_batch

    @pl.when(pl.program_id(0) == 0)
    def _():
        lr = lr_ref[...]
        li = li_ref[...]
        dt = jnp.exp(ldt_ref[...])
        mag = jnp.exp(lr * dt)
        ar = mag * jnp.cos(li * dt)
        ai = mag * jnp.sin(li * dt)
        den = lr * lr + li * li
        fr = ((ar - 1.0) * lr + ai * li) / den
        fi = (ai * lr - (ar - 1.0) * li) / den
        a_sc[0:1, :] = ar
        a_sc[1:2, :] = ai
        for j in range(4):
            frj = fr[:, 512 * j:512 * (j + 1)]
            fij = fi[:, 512 * j:512 * (j + 1)]
            bbr_sc[j] = (frj * bre_ref[j] - fij * bim_ref[j]).astype(BF16)
            bbi_sc[j] = (frj * bim_ref[j] + fij * bre_ref[j]).astype(BF16)
            cr_sc[j] = cre_ref[j].astype(BF16)
            ci_sc[j] = cim_ref[j].astype(BF16)
        wg_sc[...] = wg_ref[...].astype(BF16)
        str_sc[...] = h0r_ref[...]
        sti_sc[...] = h0i_ref[...]

    for j in range(4):
        uc = u_ref[j].astype(BF16)
        xr_sc[:, 512 * j:512 * (j + 1)] = jnp.dot(uc, bbr_sc[j], preferred_element_type=F32)
        xi_sc[:, 512 * j:512 * (j + 1)] = jnp.dot(uc, bbi_sc[j], preferred_element_type=F32)

    for c in range(D_STATE // SCAN_LANES):
        lo = c * SCAN_LANES
        ar = jnp.broadcast_to(a_sc[0:1, lo:lo + SCAN_LANES], (n_batch, SCAN_LANES))
        ai = jnp.broadcast_to(a_sc[1:2, lo:lo + SCAN_LANES], (n_batch, SCAN_LANES))

        def step(t, carry, lo=lo, ar=ar, ai=ai):
            sr, si = carry
            r0 = pl.multiple_of(t * n_batch, n_batch)
            nr = ar * sr - ai * si + xr_sc[pl.ds(r0, n_batch), lo:lo + SCAN_LANES]
            ni = ar * si + ai * sr + xi_sc[pl.ds(r0, n_batch), lo:lo + SCAN_LANES]
            xr_sc[pl.ds(r0, n_batch), lo:lo + SCAN_LANES] = nr
            xi_sc[pl.ds(r0, n_batch), lo:lo + SCAN_LANES] = ni
            return nr, ni

        sr, si = lax.fori_loop(0, n_steps, step,
                               (str_sc[:, lo:lo + SCAN_LANES], sti_sc[:, lo:lo + SCAN_LANES]),
                               unroll=4)
        str_sc[:, lo:lo + SCAN_LANES] = sr
        sti_sc[:, lo:lo + SCAN_LANES] = si

    pieces = []
    for j in range(4):
        xr = xr_sc[:, 512 * j:512 * (j + 1)].astype(BF16)
        xi = xi_sc[:, 512 * j:512 * (j + 1)].astype(BF16)
        pieces.append(jnp.dot(xr, cr_sc[j], preferred_element_type=F32)
                      - jnp.dot(xi, ci_sc[j], preferred_element_type=F32))
    u = jnp.concatenate([u_ref[j] for j in range(4)], axis=1)
    y = jnp.concatenate(pieces, axis=1) + d_ref[...] * u
    y = jax.nn.gelu(y)
    z = jnp.dot(y.astype(BF16), wg_sc[...], preferred_element_type=F32) + bg_ref[...]
    out = z[:, :D_SSM] * jax.nn.sigmoid(z[:, D_SSM:])
    for j in range(D_SSM // LANES):
        y_ref[j] = out[:, LANES * j:LANES * (j + 1)]
    sr_ref[...] = str_sc[...]
    si_ref[...] = sti_sc[...]


def _block_diag_b(b):
    bt = b.transpose(0, 2, 1).reshape(4, 8, SSM_GROUP, SSM_STATE)
    same = jnp.eye(8, dtype=bool)[None, :, None, :, None]
    t = jnp.where(same, bt[:, :, :, None, :], 0.0)
    return t.reshape(4, 8 * SSM_GROUP, 8 * SSM_STATE)


def _block_diag_c(c):
    ct = c.transpose(0, 2, 1).reshape(4, 8, SSM_STATE, SSM_GROUP)
    same = jnp.eye(8, dtype=bool)[None, :, None, :, None]
    t = jnp.where(same, ct[:, :, :, None, :], 0.0)
    return t.reshape(4, 8 * SSM_STATE, 8 * SSM_GROUP)


def _ssm(u_rows, h0_re, h0_im, lp, n_batch):
    rows = u_rows.shape[1]
    planes = D_SSM // LANES
    tr = min(SCAN_ROWS, rows)
    flat = lambda a: a.reshape(1, D_STATE)
    ldt = jnp.repeat(lp['log_dt'], SSM_STATE).reshape(1, D_STATE)
    const2 = lambda i: (0, 0)
    const3 = lambda i: (0, 0, 0)
    y, sr, si = pl.pallas_call(
        functools.partial(_ssm_kernel, n_batch=n_batch),
        grid=(rows // tr,),
        in_specs=[pl.BlockSpec((planes, tr, LANES), lambda i: (0, i, 0)),
                  pl.BlockSpec((n_batch, D_STATE), const2),
                  pl.BlockSpec((n_batch, D_STATE), const2),
                  pl.BlockSpec((1, D_STATE), const2),
                  pl.BlockSpec((1, D_STATE), const2),
                  pl.BlockSpec((1, D_STATE), const2),
                  pl.BlockSpec((4, 128, 512), const3),
                  pl.BlockSpec((4, 128, 512), const3),
                  pl.BlockSpec((4, 512, 128), const3),
                  pl.BlockSpec((4, 512, 128), const3),
                  pl.BlockSpec((1, D_SSM), const2),
                  pl.BlockSpec((D_SSM, 2 * D_SSM), const2),
                  pl.BlockSpec((1, 2 * D_SSM), const2)],
        out_specs=[pl.BlockSpec((planes, tr, LANES), lambda i: (0, i, 0)),
                   pl.BlockSpec((n_batch, D_STATE), const2),
                   pl.BlockSpec((n_batch, D_STATE), const2)],
        out_shape=[jax.ShapeDtypeStruct((planes, rows, LANES), F32),
                   jax.ShapeDtypeStruct((n_batch, D_STATE), F32),
                   jax.ShapeDtypeStruct((n_batch, D_STATE), F32)],
        scratch_shapes=[pltpu.VMEM((2, D_STATE), F32),
                        pltpu.VMEM((4, 128, 512), BF16), pltpu.VMEM((4, 128, 512), BF16),
                        pltpu.VMEM((4, 512, 128), BF16), pltpu.VMEM((4, 512, 128), BF16),
                        pltpu.VMEM((D_SSM, 2 * D_SSM), BF16),
                        pltpu.VMEM((n_batch, D_STATE), F32), pltpu.VMEM((n_batch, D_STATE), F32),
                        pltpu.VMEM((tr, D_STATE), F32), pltpu.VMEM((tr, D_STATE), F32)],
        compiler_params=_params(1),
        name="ssm",
    )(u_rows, h0_re.reshape(n_batch, D_STATE), h0_im.reshape(n_batch, D_STATE),
      flat(lp['lam_re']), flat(lp['lam_im']), ldt,
      _block_diag_b(lp['ssm_b_re']), _block_diag_b(lp['ssm_b_im']),
      _block_diag_c(lp['ssm_c_re']), _block_diag_c(lp['ssm_c_im']),
      lp['ssm_d'].reshape(1, D_SSM), lp['w_glu'], lp['b_glu'].reshape(1, 2 * D_SSM))
    return y, sr, si


def _softmax_pv(s, v):
    m = jnp.max(s, axis=-1, keepdims=True)
    p = jnp.exp(s - m)
    l = jnp.sum(p, axis=-1, keepdims=True)
    return jnp.dot(p.astype(BF16), v, preferred_element_type=F32) / l


def _attend(q, k, v, out_ref, bias_ref=None, valid=None):
    qb = (q * ATT_SCALE).astype(BF16)
    for h in range(N_HEADS):
        sl = slice(HEAD_DIM * h, HEAD_DIM * (h + 1))
        s = lax.dot_general(qb[:, sl], k[:, sl], _NT, preferred_element_type=F32)
        if bias_ref is not None:
            s = s + bias_ref[h]
        if valid is not None:
            s = jnp.where(valid, s, NEG_INF)
        out_ref[0, :, sl] = _softmax_pv(s, v[:, sl])


def _attn_prompt_kernel(q_ref, k0_ref, k1_ref, k2_ref, v0_ref, v1_ref, v2_ref, qm_ref,
                        mk_ref, mv_ref, bias_ref, ya_ref, ym_ref):
    tq = q_ref.shape[1]
    k = jnp.concatenate([k0_ref[0], k1_ref[0], k2_ref[0]], axis=0).astype(BF16)
    v = jnp.concatenate([v0_ref[0], v1_ref[0], v2_ref[0]], axis=0).astype(BF16)
    kpos = (pl.program_id(1) - 2) * tq + lax.broadcasted_iota(I32, (1, 3 * tq), 1)
    _attend(q_ref[0], k, v, ya_ref, bias_ref, kpos >= 0)
    _attend(qm_ref[0], mk_ref[0].astype(BF16), mv_ref[0].astype(BF16), ym_ref)


def _attn_sample_kernel(q_ref, kn_ref, vn_ref, qm_ref, ck_ref, cv_ref, mk_ref, mv_ref, bias_ref,
                        ya_ref, ym_ref, nk_ref, nv_ref):
    n = kn_ref.shape[1]
    kk = jnp.concatenate([ck_ref[0], kn_ref[0]], axis=0)
    vv = jnp.concatenate([cv_ref[0], vn_ref[0]], axis=0)
    nk_ref[0] = kk[n:]
    nv_ref[0] = vv[n:]
    _attend(q_ref[0], kk.astype(BF16), vv.astype(BF16), ya_ref, bias_ref)
    _attend(qm_ref[0], mk_ref[0].astype(BF16), mv_ref[0].astype(BF16), ym_ref)


def _rel_bias(table, n_q, n_k, band_mask):
    period = n_q + n_k
    m = jnp.arange(period)
    offset = jnp.where(m < n_k, m, m - period)
    idx = jnp.clip(BAND - offset, -REL_CLIP, REL_CLIP) + REL_CLIP
    f = table.astype(F32)[:, idx]
    flat = jnp.tile(f, (1, n_q))[:, :n_q * (period - 1)]
    bias = flat.reshape(N_HEADS, n_q, period - 1)[:, :, :n_k]
    if band_mask:
        qi = jnp.arange(n_q)[:, None]
        kj = jnp.arange(n_k)[None, :]
        off = kj // CHUNK - qi // CHUNK
        ok = (off >= 0) & (off <= N_PREV_CHUNKS)
        bias = jnp.where(ok[None], bias, NEG_INF)
    return bias


def _attn_prompt(zr, mk, mv, table):
    b, s, _ = zr.shape
    tq = ATT_TQ
    bias = _rel_bias(table, tq, 3 * tq, True)
    col = lambda c: (lambda i, j: (i, j, c))
    prev = lambda c, d: (lambda i, j: (i, jnp.maximum(j - d, 0), c))
    blk = lambda: (1, tq, D_ATT)
    return pl.pallas_call(
        _attn_prompt_kernel,
        grid=(b, s // tq),
        in_specs=[pl.BlockSpec(blk(), col(0)),
                  pl.BlockSpec(blk(), prev(1, 2)), pl.BlockSpec(blk(), prev(1, 1)),
                  pl.BlockSpec(blk(), col(1)),
                  pl.BlockSpec(blk(), prev(2, 2)), pl.BlockSpec(blk(), prev(2, 1)),
                  pl.BlockSpec(blk(), col(2)),
                  pl.BlockSpec(blk(), col(3)),
                  pl.BlockSpec((1, N_MEM, D_MEM), lambda i, j: (i, 0, 0)),
                  pl.BlockSpec((1, N_MEM, D_MEM), lambda i, j: (i, 0, 0)),
                  pl.BlockSpec((N_HEADS, tq, 3 * tq), lambda i, j: (0, 0, 0))],
        out_specs=[pl.BlockSpec(blk(), col(0)), pl.BlockSpec(blk(), col(0))],
        out_shape=[jax.ShapeDtypeStruct((b, s, D_ATT), F32),
                   jax.ShapeDtypeStruct((b, s, D_MEM), F32)],
        compiler_params=_params(2),
        name="attn_prompt",
    )(zr, zr, zr, zr, zr, zr, zr, zr, mk, mv, bias)


def _attn_sample(zr, cache_k, cache_v, mk, mv, table):
    b, n, _ = zr.shape
    w = cache_k.shape[1]
    bias = _rel_bias(table, n, w + n, False)
    col = lambda c: (lambda i: (i, 0, c))
    blk = (1, n, D_ATT)
    cblk = (1, w, D_ATT)
    mblk = (1, N_MEM, D_MEM)
    row = lambda i: (i, 0, 0)
    return pl.pallas_call(
        _attn_sample_kernel,
        grid=(b,),
        in_specs=[pl.BlockSpec(blk, col(0)), pl.BlockSpec(blk, col(1)), pl.BlockSpec(blk, col(2)),
                  pl.BlockSpec(blk, col(3)),
                  pl.BlockSpec(cblk, row), pl.BlockSpec(cblk, row),
                  pl.BlockSpec(mblk, row), pl.BlockSpec(mblk, row),
                  pl.BlockSpec((N_HEADS, n, w + n), lambda i: (0, 0, 0))],
        out_specs=[pl.BlockSpec(blk, row), pl.BlockSpec(blk, row),
                   pl.BlockSpec(cblk, row), pl.BlockSpec(cblk, row)],
        out_shape=[jax.ShapeDtypeStruct((b, n, D_ATT), F32),
                   jax.ShapeDtypeStruct((b, n, D_MEM), F32),
                   jax.ShapeDtypeStruct((b, w, D_ATT), F32),
                   jax.ShapeDtypeStruct((b, w, D_ATT), F32)],
        compiler_params=_params(1),
        name="attn_sample",
    )(zr, zr, zr, zr, cache_k, cache_v, mk, mv, bias)


def _rms(x, g):
    return x * lax.rsqrt(jnp.mean(jnp.square(x), axis=-1, keepdims=True) + LN_EPS) * g


def _layer_norm(x, g, b):
    mu = jnp.mean(x, axis=-1, keepdims=True)
    xc = x - mu
    var = jnp.mean(jnp.square(xc), axis=-1, keepdims=True)
    return xc * lax.rsqrt(var + LN_EPS) * g + b


def _split_bf16(a):
    hi = a.astype(BF16)
    lo = (a - hi.astype(F32)).astype(BF16)
    return hi, lo


def _merge_kernel(x_ref, ys_ref, ya_ref, ym_ref, gs_ref, ga_ref, gm_ref, wo_ref, l1g_ref, l1b_ref,
                  wrt_ref, brt_ref,
                  h_ref, pos_ref, gate_ref, cnt_ref,
                  wo_sc, *, nb):
    st = x_ref.shape[1]
    tm = nb * st
    n_batch = ys_ref.shape[1] // st

    @pl.when((pl.program_id(0) == 0) & (pl.program_id(1) == 0))
    def _():
        wo_sc[...] = wo_ref[...].astype(BF16)

    x = x_ref[...].reshape(tm, D_MODEL)
    first = pl.program_id(1) * nb
    ys = jnp.concatenate(
        [jnp.concatenate([ys_ref[c, pl.ds(first + i, st, stride=n_batch), :] for c in range(D_SSM // LANES)],
                         axis=1) for i in range(nb)], axis=0)
    ya = ya_ref[...].reshape(tm, D_ATT)
    ym = ym_ref[...].reshape(tm, D_MEM)
    a = _rms(ys, gs_ref[...]).astype(BF16)
    b = _rms(ya, ga_ref[...]).astype(BF16)
    c = _rms(ym, gm_ref[...]).astype(BF16)
    mix = (jnp.dot(a, wo_sc[0:D_SSM, :], preferred_element_type=F32)
           + jnp.dot(b, wo_sc[D_SSM:D_SSM + D_ATT, :], preferred_element_type=F32)
           + jnp.dot(c, wo_sc[D_SSM + D_ATT:, :], preferred_element_type=F32))
    h = _layer_norm(DEEPNORM_ALPHA * x + mix, l1g_ref[...], l1b_ref[...])
    h_ref[...] = h

    h_hi, h_lo = _split_bf16(h)
    w_hi, w_lo = _split_bf16(wrt_ref[...])
    logits = (lax.dot_general(w_hi, h_hi, _NT, preferred_element_type=F32)
              + lax.dot_general(w_hi, h_lo, _NT, preferred_element_type=F32)
              + lax.dot_general(w_lo, h_hi, _NT, preferred_element_type=F32)
              + brt_ref[...])
    erow = lax.broadcasted_iota(I32, (N_EXPERTS, tm), 0).astype(F32)
    tops, picks = [], []
    l = logits
    for k in range(TOP_K):
        m = jnp.max(l, axis=0, keepdims=True)
        e = jnp.min(jnp.where(l == m, erow, float(N_EXPERTS)), axis=0, keepdims=True)
        pick = erow == e
        tops.append(m)
        picks.append(jnp.where(pick, 1.0, 0.0))
        l = jnp.where(pick, -jnp.inf, l)
    ex = [jnp.exp(t - tops[0]) for t in tops]
    den = ex[0] + ex[1] + ex[2] + ex[3]
    for k in range(TOP_K):
        gate_ref[k:k + 1, :] = ex[k] / den

    chosen = picks[0] + picks[1] + picks[2] + picks[3]
    chosen_b = chosen.astype(BF16)
    earlier_tok = (lax.broadcasted_iota(I32, (tm, tm), 0) < lax.broadcasted_iota(I32, (tm, tm), 1))
    within = jnp.dot(chosen_b, jnp.where(earlier_tok, 1.0, 0.0).astype(BF16),
                     preferred_element_type=F32)
    lower_exp = (lax.broadcasted_iota(I32, (N_EXPERTS, N_EXPERTS), 1)
                 < lax.broadcasted_iota(I32, (N_EXPERTS, N_EXPERTS), 0))
    below = jnp.dot(jnp.where(lower_exp, 1.0, 0.0).astype(BF16), chosen_b,
                    preferred_element_type=F32)
    slot = within + jnp.sum(below, axis=1, keepdims=True)
    for k in range(TOP_K):
        pos_ref[k:k + 1, :] = jnp.sum(picks[k] * slot, axis=0, keepdims=True).astype(I32)
    cnt_ref[0] = jnp.sum(chosen, axis=1, keepdims=True)


def _merge(x, ys_tm, ya, ym, lp, nb, st):
    b, s, _ = x.shape
    tm = nb * st
    assert tm == TOKEN_TM
    n_s = s // st
    t = b * s
    tile = lambda j, i: (i * n_s + j)
    c2 = lambda j, i: (0, 0)
    row3 = lambda j, i: (i, j, 0)
    vec = lambda a: a.reshape(1, -1)
    return pl.pallas_call(
        functools.partial(_merge_kernel, nb=nb),
        grid=(n_s, b // nb),
        in_specs=[pl.BlockSpec((nb, st, D_MODEL), row3),
                  pl.BlockSpec((D_SSM // LANES, st * b, LANES), lambda j, i: (0, j, 0)),
                  pl.BlockSpec((nb, st, D_ATT), row3),
                  pl.BlockSpec((nb, st, D_MEM), row3),
                  pl.BlockSpec((1, D_SSM), c2), pl.BlockSpec((1, D_ATT), c2),
                  pl.BlockSpec((1, D_MEM), c2),
                  pl.BlockSpec((D_MODEL, D_MODEL), c2),
                  pl.BlockSpec((1, D_MODEL), c2), pl.BlockSpec((1, D_MODEL), c2),
                  pl.BlockSpec((N_EXPERTS, D_MODEL), c2), pl.BlockSpec((N_EXPERTS, 1), c2)],
        out_specs=[pl.BlockSpec((tm, D_MODEL), lambda j, i: (tile(j, i), 0)),
                   pl.BlockSpec((TOP_K, tm), lambda j, i: (0, tile(j, i))),
                   pl.BlockSpec((TOP_K, tm), lambda j, i: (0, tile(j, i))),
                   pl.BlockSpec((1, N_EXPERTS, 1), lambda j, i: (tile(j, i), 0, 0))],
        out_shape=[jax.ShapeDtypeStruct((t, D_MODEL), F32),
                   jax.ShapeDtypeStruct((TOP_K, t), I32),
                   jax.ShapeDtypeStruct((TOP_K, t), F32),
                   jax.ShapeDtypeStruct((t // tm, N_EXPERTS, 1), F32)],
        scratch_shapes=[pltpu.VMEM((D_MODEL, D_MODEL), BF16)],
        compiler_params=_params(2),
        name="merge_router",
    )(x, ys_tm, ya, ym, vec(lp['g_ssm']), vec(lp['g_att']), vec(lp['g_mem']), lp['w_out'],
      vec(lp['ln1_g']), vec(lp['ln1_b']), lp['w_router'].T, lp['b_router'].reshape(N_EXPERTS, 1))


def _rows(start, size):
    return pl.ds(pl.multiple_of(start * ROW_SUBLANES, ROW_SUBLANES), size * ROW_SUBLANES)


def _store_rows(ref, value):
    n = value.shape[0]
    for j in range(ROW_SUBLANES):
        ref[pl.ds(j, n, stride=ROW_SUBLANES), :] = value[:, LANES * j:LANES * (j + 1)]


def _load_rows(ref):
    n = ref.shape[0] // ROW_SUBLANES
    return jnp.concatenate([ref[pl.ds(j, n, stride=ROW_SUBLANES), :] for j in range(ROW_SUBLANES)], axis=1)


def _for_each_run_piece(n, max_rows, fn):
    for bit in reversed(range(max_rows.bit_length())):
        size = 1 << bit
        start = (n >> (bit + 1)) << (bit + 1)

        @pl.when((n & size) != 0)
        def _(start=start, size=size):
            fn(start, size)


def _dispatch_kernel(n_ref, off_ref, dst_ref, padlo_ref, padn_ref, used_ref,
                     pos_ref, h_ref, xs_hbm, sorted_sc, zero_sc, sem, zsem):
    i = pl.program_id(0)
    tm = h_ref.shape[0]
    n_slots = TOP_K * tm
    n_blocks = xs_hbm.shape[0] // (EXPERT_TM * ROW_SUBLANES)

    @pl.when(i == 0)
    def _():
        zero_sc[...] = jnp.zeros_like(zero_sc)

        def pad_copy(e, start, size):
            return pltpu.make_async_copy(zero_sc.at[_rows(0, size)],
                                         xs_hbm.at[_rows(padlo_ref[e] + start, size)], zsem)

        def tail_copy(blk):
            return pltpu.make_async_copy(zero_sc, xs_hbm.at[_rows(blk * EXPERT_TM, EXPERT_TM)], zsem)

        for e in range(N_EXPERTS):
            _for_each_run_piece(padn_ref[e], EXPERT_TM - 1,
                                lambda start, size, e=e: pad_copy(e, start, size).start())

        def tail_start(blk, carry):
            tail_copy(blk).start()
            return carry

        lax.fori_loop(used_ref[0], n_blocks, tail_start, 0)
        for e in range(N_EXPERTS):
            _for_each_run_piece(padn_ref[e], EXPERT_TM - 1,
                                lambda start, size, e=e: pad_copy(e, start, size).wait())

        def tail_wait(blk, carry):
            tail_copy(blk).wait()
            return carry

        lax.fori_loop(used_ref[0], n_blocks, tail_wait, 0)

    pos = pos_ref[...]
    srow = lax.broadcasted_iota(I32, (n_slots, tm), 0)
    perm = jnp.where(srow == pos[0:1], 1.0,
                     jnp.where(srow == pos[1:2], 1.0,
                               jnp.where(srow == pos[2:3], 1.0,
                                         jnp.where(srow == pos[3:4], 1.0, 0.0))))
    _store_rows(sorted_sc, jnp.dot(perm.astype(BF16), h_ref[...].astype(BF16),
                                   preferred_element_type=F32))

    base = i * N_EXPERTS
    for e in range(N_EXPERTS):
        off = off_ref[base + e]
        dst = dst_ref[base + e]

        def run_start(start, size, off=off, dst=dst):
            pltpu.make_async_copy(sorted_sc.at[_rows(off + start, size)],
                                  xs_hbm.at[_rows(dst + start, size)], sem).start()

        _for_each_run_piece(n_ref[base + e], tm, run_start)
    pltpu.make_async_copy(sorted_sc, xs_hbm.at[_rows(0, n_slots)], sem).wait()


def _dispatch(run_n, run_off, run_dst, pad_lo, pad_n, n_used, pos, h, cap):
    t = h.shape[0]
    tm = TOKEN_TM
    grid_spec = pltpu.PrefetchScalarGridSpec(
        num_scalar_prefetch=6,
        grid=(t // tm,),
        in_specs=[pl.BlockSpec((TOP_K, tm), lambda i, *_: (0, i)),
                  pl.BlockSpec((tm, D_MODEL), lambda i, *_: (i, 0))],
        out_specs=pl.BlockSpec(memory_space=pl.ANY),
        scratch_shapes=[pltpu.VMEM((TOP_K * tm * ROW_SUBLANES, LANES), F32),
                        pltpu.VMEM((EXPERT_TM * ROW_SUBLANES, LANES), F32),
                        pltpu.SemaphoreType.DMA, pltpu.SemaphoreType.DMA],
    )
    return pl.pallas_call(
        _dispatch_kernel,
        grid_spec=grid_spec,
        out_shape=jax.ShapeDtypeStruct((cap * ROW_SUBLANES, LANES), F32),
        compiler_params=_params(1),
        name="moe_dispatch",
    )(run_n, run_off, run_dst, pad_lo, pad_n, n_used, pos, h)


def _expert_kernel(be_ref, used_ref, x_ref, wgu_ref, bgu_ref, wd_ref, bd_ref, o_ref, wgu_sc, wd_sc):
    i = pl.program_id(0)

    @pl.when(i < used_ref[0])
    def _():
        prev = jnp.maximum(i - 1, 0)

        @pl.when((i == 0) | (be_ref[i] != be_ref[prev]))
        def _():
            wgu_sc[...] = wgu_ref[0].astype(BF16)
            wd_sc[...] = wd_ref[0].astype(BF16)

        gu = jnp.dot(_load_rows(x_ref).astype(BF16), wgu_sc[...], preferred_element_type=F32) + bgu_ref[0]
        gate = jnp.minimum(gu[:, :D_FF], SWIGLU_LIMIT)
        lin = jnp.clip(gu[:, D_FF:], -SWIGLU_LIMIT, SWIGLU_LIMIT)
        act = gate * jax.nn.sigmoid(SWIGLU_ALPHA * gate) * (lin + 1.0)
        _store_rows(o_ref, jnp.dot(act.astype(BF16), wd_sc[...], preferred_element_type=F32) + bd_ref[0])

    @pl.when(i >= used_ref[0])
    def _():
        o_ref[...] = jnp.zeros_like(o_ref)


def _experts(block_expert, n_used, xs, lp):
    tm = EXPERT_TM * ROW_SUBLANES
    grid_spec = pltpu.PrefetchScalarGridSpec(
        num_scalar_prefetch=2,
        grid=(xs.shape[0] // tm,),
        in_specs=[pl.BlockSpec((tm, LANES), lambda i, be, nu: (i, 0)),
                  pl.BlockSpec((1, D_MODEL, 2 * D_FF), lambda i, be, nu: (be[i], 0, 0)),
                  pl.BlockSpec((1, 1, 2 * D_FF), lambda i, be, nu: (be[i], 0, 0)),
                  pl.BlockSpec((1, D_FF, D_MODEL), lambda i, be, nu: (be[i], 0, 0)),
                  pl.BlockSpec((1, 1, D_MODEL), lambda i, be, nu: (be[i], 0, 0))],
        out_specs=pl.BlockSpec((tm, LANES), lambda i, be, nu: (i, 0)),
        scratch_shapes=[pltpu.VMEM((D_MODEL, 2 * D_FF), BF16), pltpu.VMEM((D_FF, D_MODEL), BF16)],
    )
    return pl.pallas_call(
        _expert_kernel,
        grid_spec=grid_spec,
        out_shape=jax.ShapeDtypeStruct(xs.shape, F32),
        compiler_params=_params(1),
        name="moe_experts",
    )(block_expert, n_used, xs, lp['w_gu'], lp['b_gu'].reshape(N_EXPERTS, 1, 2 * D_FF),
      lp['w_down'], lp['b_down'].reshape(N_EXPERTS, 1, D_MODEL))


def _combine_kernel(n_ref, off_ref, dst_ref, pos_ref, gate_ref, h_ref, ys_hbm, g_ref, b_ref, y_ref,
                    sorted_sc, sem):
    i = pl.program_id(0)
    tm = h_ref.shape[0]
    n_slots = TOP_K * tm

    base = i * N_EXPERTS
    for e in range(N_EXPERTS):
        off = off_ref[base + e]
        dst = dst_ref[base + e]

        def run_start(start, size, off=off, dst=dst):
            pltpu.make_async_copy(ys_hbm.at[_rows(dst + start, size)],
                                  sorted_sc.at[_rows(off + start, size)], sem).start()

        _for_each_run_piece(n_ref[base + e], tm, run_start)
    pltpu.make_async_copy(ys_hbm.at[_rows(0, n_slots)], sorted_sc, sem).wait()

    pos = pos_ref[...]
    gates = gate_ref[...]
    scol = lax.broadcasted_iota(I32, (tm, n_slots), 1)
    w = jnp.where(scol == pos[:, 0:1], gates[:, 0:1],
                  jnp.where(scol == pos[:, 1:2], gates[:, 1:2],
                            jnp.where(scol == pos[:, 2:3], gates[:, 2:3],
                                      jnp.where(scol == pos[:, 3:4], gates[:, 3:4], 0.0))))
    f = jnp.dot(w.astype(BF16), _load_rows(sorted_sc).astype(BF16), preferred_element_type=F32)
    y_ref[...] = _layer_norm(DEEPNORM_ALPHA * h_ref[...] + f, g_ref[...], b_ref[...])


def _combine(run_n, run_off, run_dst, pos_t, gates_t, h, ys, lp):
    t = h.shape[0]
    tm = TOKEN_TM
    c2 = lambda i, *_: (0, 0)
    grid_spec = pltpu.PrefetchScalarGridSpec(
        num_scalar_prefetch=3,
        grid=(t // tm,),
        in_specs=[pl.BlockSpec((tm, TOP_K), lambda i, *_: (i, 0)),
                  pl.BlockSpec((tm, TOP_K), lambda i, *_: (i, 0)),
                  pl.BlockSpec((tm, D_MODEL), lambda i, *_: (i, 0)),
                  pl.BlockSpec(memory_space=pl.ANY),
                  pl.BlockSpec((1, D_MODEL), c2), pl.BlockSpec((1, D_MODEL), c2)],
        out_specs=pl.BlockSpec((tm, D_MODEL), lambda i, *_: (i, 0)),
        scratch_shapes=[pltpu.VMEM((TOP_K * tm * ROW_SUBLANES, LANES), F32), pltpu.SemaphoreType.DMA],
    )
    return pl.pallas_call(
        _combine_kernel,
        grid_spec=grid_spec,
        out_shape=jax.ShapeDtypeStruct((t, D_MODEL), F32),
        compiler_params=_params(1),
        name="moe_combine",
    )(run_n, run_off, run_dst, pos_t, gates_t, h, ys,
      lp['ln2_g'].reshape(1, D_MODEL), lp['ln2_b'].reshape(1, D_MODEL))


def _moe_and_norm(h, pos, gates, tile_counts, lp):
    t = h.shape[0]
    te = EXPERT_TM
    n_tiles = t // TOKEN_TM
    n_blocks = (t * TOP_K) // te + N_EXPERTS
    cap = n_blocks * te
    cnt = tile_counts.reshape(n_tiles, N_EXPERTS).astype(I32)
    counts = jnp.sum(cnt, axis=0)
    padded = (counts + te - 1) // te * te
    pad_ends = jnp.cumsum(padded)
    pad_starts = pad_ends - padded
    run_dst = pad_starts[None, :] + jnp.cumsum(cnt, axis=0) - cnt
    run_off = jnp.cumsum(cnt, axis=1) - cnt
    blk_start = jnp.arange(n_blocks, dtype=I32) * te
    n_used = (pad_ends[-1] // te).astype(I32)
    be = jnp.minimum(jnp.sum(blk_start[:, None] >= pad_ends[None, :], axis=1), N_EXPERTS - 1).astype(I32)
    be = jnp.where(blk_start < pad_ends[-1], be, be[jnp.maximum(n_used - 1, 0)])
    n_used = n_used.reshape(1)
    flat = lambda a: a.reshape(-1).astype(I32)
    xs = _dispatch(flat(cnt), flat(run_off), flat(run_dst), flat(pad_starts + counts),
                   flat(padded - counts), n_used, pos, h, cap)
    ys = _experts(be, n_used, xs, lp)
    return _combine(flat(cnt), flat(run_off), flat(run_dst), pos.T, gates.T, h, ys, lp)


def _layer(x, zr_attn, u_rows, h0_re, h0_im, lp, nb, st):
    b, s, _ = x.shape
    ys_tm, s_re, s_im = _ssm(u_rows, h0_re, h0_im, lp, b)
    ya, ym = zr_attn
    h, pos, gates, tile_counts = _merge(x, ys_tm, ya, ym, lp, nb, st)
    y = _moe_and_norm(h, pos, gates, tile_counts, lp)
    return (y.reshape(b, s, D_MODEL), s_re.reshape(b, N_GROUPS, SSM_STATE),
            s_im.reshape(b, N_GROUPS, SSM_STATE))


def kernel(x_prompt, x_sample, cache_attn_k, cache_attn_v, cache_mem_k, cache_mem_v, state_ssm_re, state_ssm_im, mem_prompt, w_in, lam_re, lam_im, log_dt, ssm_b_re, ssm_b_im, ssm_c_re, ssm_c_im, ssm_d, w_glu, b_glu, rel_bias, w_mem_kv, g_ssm, g_att, g_mem, w_out, ln1_g, ln1_b, w_router, b_router, w_gu, b_gu, w_down, b_down, ln2_g, ln2_b):
    assert w_in.shape[0] == 1, "single-layer step"
    lp = dict(w_in=w_in[0], lam_re=lam_re[0], lam_im=lam_im[0], log_dt=log_dt[0],
              ssm_b_re=ssm_b_re[0], ssm_b_im=ssm_b_im[0], ssm_c_re=ssm_c_re[0], ssm_c_im=ssm_c_im[0],
              ssm_d=ssm_d[0], w_glu=w_glu[0], b_glu=b_glu[0], rel_bias=rel_bias[0],
              w_mem_kv=w_mem_kv[0], g_ssm=g_ssm[0], g_att=g_att[0], g_mem=g_mem[0], w_out=w_out[0],
              ln1_g=ln1_g[0], ln1_b=ln1_b[0], w_router=w_router[0], b_router=b_router[0],
              w_gu=w_gu[0], b_gu=b_gu[0], w_down=w_down[0], b_down=b_down[0],
              ln2_g=ln2_g[0], ln2_b=ln2_b[0])

    bp, sp, _ = x_prompt.shape
    u_tm, zr = _in_proj(x_prompt, lp['w_in'], min(128, sp))
    mk, mv = _mem_kv(mem_prompt, lp['w_mem_kv'])
    ya, ym = _attn_prompt(zr, mk, mv, lp['rel_bias'])
    zeros = jnp.zeros((bp, D_STATE), F32)
    y_p, sr_p, si_p = _layer(x_prompt, (ya, ym), u_tm, zeros, zeros, lp, 1, min(TOKEN_TM, sp))
    w = min(BAND, sp)
    heads = lambda a: a.reshape(a.shape[0], a.shape[1], N_HEADS, HEAD_DIM)
    k_p = heads(zr[:, sp - w:, D_ATT:2 * D_ATT])
    v_p = heads(zr[:, sp - w:, 2 * D_ATT:3 * D_ATT])

    bs, ss, _ = x_sample.shape
    wc = cache_attn_k.shape[2]
    u_tm_s, zr_s = _in_proj(x_sample, lp['w_in'], ss)
    ya_s, ym_s, nk, nv = _attn_sample(
        zr_s, cache_attn_k[0].reshape(bs, wc, D_ATT), cache_attn_v[0].reshape(bs, wc, D_ATT),
        cache_mem_k[0].reshape(bs, N_MEM, D_MEM), cache_mem_v[0].reshape(bs, N_MEM, D_MEM),
        lp['rel_bias'])
    nb_s = TOKEN_TM // ss
    y_s, sr_s, si_s = _layer(x_sample, (ya_s, ym_s), u_tm_s, state_ssm_re[0], state_ssm_im[0], lp,
                             nb_s, ss)

    return (y_p, y_s,
            k_p[None], v_p[None], heads(mk)[None], heads(mv)[None], sr_p[None], si_p[None],
            heads(nk)[None], heads(nv)[None], sr_s[None], si_s[None])
```

```python
import functools

import jax
import jax.numpy as jnp
from jax import lax
from jax.experimental import pallas as pl
from jax.experimental.pallas import tpu as pltpu

F32 = jnp.float32
BF16 = jnp.bfloat16
I32 = jnp.int32

D_MODEL = 1024
D_SSM = 512
D_ATT = 256
D_MEM = 256
D_IN = D_SSM + 3 * D_ATT + D_MEM
D_REST = D_IN - D_SSM
HEAD_DIM = 64
N_HEADS = 4
N_GROUPS = 32
SSM_GROUP = 16
SSM_STATE = 64
D_STATE = N_GROUPS * SSM_STATE
CHUNK = 64
N_PREV_CHUNKS = 8
BAND = N_PREV_CHUNKS * CHUNK
REL_CLIP = 128
N_MEM = 256
N_EXPERTS = 32
TOP_K = 4
D_FF = D_MODEL
SWIGLU_LIMIT = 7.0
SWIGLU_ALPHA = 1.702
LN_EPS = 1e-5
NEG_INF = -1e30
ATT_SCALE = HEAD_DIM ** -0.5
DEEPNORM_ALPHA = 2.0 ** 0.25

V7X_VMEM_LIMIT = 56 * 1024 * 1024
ATT_TQ = 4 * CHUNK
SCAN_LANES = 512
SCAN_ROWS = 1024
LANES = 128
ROW_SUBLANES = D_MODEL // LANES
EXPERT_TM = 256
TOKEN_TM = 512

_NT = (((1,), (1,)), ((), ()))


def _params(n_axes, vmem=V7X_VMEM_LIMIT):
    return pltpu.CompilerParams(dimension_semantics=("arbitrary",) * n_axes,
                                vmem_limit_bytes=vmem)


def _in_proj_kernel(x_ref, w_ref, u_ref, z_ref, wb_ref):
    @pl.when(pl.program_id(0) == 0)
    def _():
        wb_ref[...] = w_ref[...].astype(BF16)

    nb, ts, _ = x_ref.shape
    x = x_ref[...].reshape(nb * ts, D_MODEL).astype(BF16)
    z = jnp.dot(x, wb_ref[...], preferred_element_type=F32)
    for b in range(nb):
        for c in range(D_SSM // LANES):
            u_ref[c, pl.ds(b, ts, stride=nb), :] = z[b * ts:(b + 1) * ts, LANES * c:LANES * (c + 1)]
    z_ref[...] = z[:, D_SSM:].reshape(nb, ts, D_REST)


def _in_proj(x, w_in, ts):
    b, s, _ = x.shape
    return pl.pallas_call(
        _in_proj_kernel,
        grid=(s // ts,),
        in_specs=[pl.BlockSpec((b, ts, D_MODEL), lambda j: (0, j, 0)),
                  pl.BlockSpec((D_MODEL, D_IN), lambda j: (0, 0))],
        out_specs=[pl.BlockSpec((D_SSM // LANES, ts * b, LANES), lambda j: (0, j, 0)),
                   pl.BlockSpec((b, ts, D_REST), lambda j: (0, j, 0))],
        out_shape=[jax.ShapeDtypeStruct((D_SSM // LANES, s * b, LANES), F32),
                   jax.ShapeDtypeStruct((b, s, D_REST), F32)],
        scratch_shapes=[pltpu.VMEM((D_MODEL, D_IN), BF16)],
        compiler_params=_params(1),
        name="in_proj",
    )(x, w_in)


def _mem_kv_kernel(m_ref, w_ref, mk_ref, mv_ref):
    kv = jnp.dot(m_ref[0].astype(BF16), w_ref[...].astype(BF16), preferred_element_type=F32)
    mk_ref[0] = kv[:, :D_MEM]
    mv_ref[0] = kv[:, D_MEM:]


def _mem_kv(mem, w_mem_kv):
    b = mem.shape[0]
    return pl.pallas_call(
        _mem_kv_kernel,
        grid=(b,),
        in_specs=[pl.BlockSpec((1, N_MEM, D_MODEL), lambda i: (i, 0, 0)),
                  pl.BlockSpec((D_MODEL, 2 * D_MEM), lambda i: (0, 0))],
        out_specs=[pl.BlockSpec((1, N_MEM, D_MEM), lambda i: (i, 0, 0)),
                   pl.BlockSpec((1, N_MEM, D_MEM), lambda i: (i, 0, 0))],
        out_shape=[jax.ShapeDtypeStruct((b, N_MEM, D_MEM), F32)] * 2,
        compiler_params=_params(1),
        name="mem_kv",
    )(mem, w_mem_kv)


def _ssm_kernel(u_ref, h0r_ref, h0i_ref, lr_ref, li_ref, ldt_ref, bre_ref, bim_ref,
                cre_ref, cim_ref, d_ref, wg_ref, bg_ref,
                y_ref, sr_ref, si_ref,
                a_sc, bbr_sc, bbi_sc, cr_sc, ci_sc, wg_sc, str_sc, sti_sc, xr_sc, xi_sc,
                *, n_batch):
    n_rows = u_ref.shape[1]
    n_steps = n_rows // n_batch

    @pl.when(pl.program_id(0) == 0)
    def _():
        lr = lr_ref[...]
        li = li_ref[...]
        dt = jnp.exp(ldt_ref[...])
        mag = jnp.exp(lr * dt)
        ar = mag * jnp.cos(li * dt)
        ai = mag * jnp.sin(li * dt)
        den = lr * lr + li * li
        fr = ((ar - 1.0) * lr + ai * li) / den
        fi = (ai * lr - (ar - 1.0) * li) / den
        a_sc[0:1, :] = ar
        a_sc[1:2, :] = ai
        for j in range(4):
            frj = fr[:, 512 * j:512 * (j + 1)]
            fij = fi[:, 512 * j:512 * (j + 1)]
            bbr_sc[j] = (frj * bre_ref[j] - fij * bim_ref[j]).astype(BF16)
            bbi_sc[j] = (frj * bim_ref[j] + fij * bre_ref[j]).astype(BF16)
            cr_sc[j] = cre_ref[j].astype(BF16)
            ci_sc[j] = cim_ref[j].astype(BF16)
        wg_sc[...] = wg_ref[...].astype(BF16)
        str_sc[...] = h0r_ref[...]
        sti_sc[...] = h0i_ref[...]

    for j in range(4):
        uc = u_ref[j].astype(BF16)
        xr_sc[:, 512 * j:512 * (j + 1)] = jnp.dot(uc, bbr_sc[j], preferred_element_type=F32)
        xi_sc[:, 512 * j:512 * (j + 1)] = jnp.dot(uc, bbi_sc[j], preferred_element_type=F32)

    for c in range(D_STATE // SCAN_LANES):
        lo = c * SCAN_LANES
        ar = jnp.broadcast_to(a_sc[0:1, lo:lo + SCAN_LANES], (n_batch, SCAN_LANES))
        ai = jnp.broadcast_to(a_sc[1:2, lo:lo + SCAN_LANES], (n_batch, SCAN_LANES))

        def step(t, carry, lo=lo, ar=ar, ai=ai):
            sr, si = carry
            r0 = pl.multiple_of(t * n_batch, n_batch)
            nr = ar * sr - ai * si + xr_sc[pl.ds(r0, n_batch), lo:lo + SCAN_LANES]
            ni = ar * si + ai * sr + xi_sc[pl.ds(r0, n_batch), lo:lo + SCAN_LANES]
            xr_sc[pl.ds(r0, n_batch), lo:lo + SCAN_LANES] = nr
            xi_sc[pl.ds(r0, n_batch), lo:lo + SCAN_LANES] = ni
            return nr, ni

        sr, si = lax.fori_loop(0, n_steps, step,
                               (str_sc[:, lo:lo + SCAN_LANES], sti_sc[:, lo:lo + SCAN_LANES]),
                               unroll=4)
        str_sc[:, lo:lo + SCAN_LANES] = sr
        sti_sc[:, lo:lo + SCAN_LANES] = si

    pieces = []
    for j in range(4):
        xr = xr_sc[:, 512 * j:512 * (j + 1)].astype(BF16)
        xi = xi_sc[:, 512 * j:512 * (j + 1)].astype(BF16)
        pieces.append(jnp.dot(xr, cr_sc[j], preferred_element_type=F32)
                      - jnp.dot(xi, ci_sc[j], preferred_element_type=F32))
    u = jnp.concatenate([u_ref[j] for j in range(4)], axis=1)
    y = jnp.concatenate(pieces, axis=1) + d_ref[...] * u
    y = jax.nn.gelu(y)
    z = jnp.dot(y.astype(BF16), wg_sc[...], preferred_element_type=F32) + bg_ref[...]
    out = z[:, :D_SSM] * jax.nn.sigmoid(z[:, D_SSM:])
    for j in range(D_SSM // LANES):
        y_ref[j] = out[:, LANES * j:LANES * (j + 1)]
    sr_ref[...] = str_sc[...]
    si_ref[...] = sti_sc[...]


def _block_diag_b(b):
    bt = b.transpose(0, 2, 1).reshape(4, 8, SSM_GROUP, SSM_STATE)
    same = jnp.eye(8, dtype=bool)[None, :, None, :, None]
    t = jnp.where(same, bt[:, :, :, None, :], 0.0)
    return t.reshape(4, 8 * SSM_GROUP, 8 * SSM_STATE)


def _block_diag_c(c):
    ct = c.transpose(0, 2, 1).reshape(4, 8, SSM_STATE, SSM_GROUP)
    same = jnp.eye(8, dtype=bool)[None, :, None, :, None]
    t = jnp.where(same, ct[:, :, :, None, :], 0.0)
    return t.reshape(4, 8 * SSM_STATE, 8 * SSM_GROUP)


def _ssm(u_rows, h0_re, h0_im, lp, n_batch):
    rows = u_rows.shape[1]
    planes = D_SSM // LANES
    tr = min(SCAN_ROWS, rows)
    flat = lambda a: a.reshape(1, D_STATE)
    ldt = jnp.repeat(lp['log_dt'], SSM_STATE).reshape(1, D_STATE)
    const2 = lambda i: (0, 0)
    const3 = lambda i: (0, 0, 0)
    y, sr, si = pl.pallas_call(
        functools.partial(_ssm_kernel, n_batch=n_batch),
        grid=(rows // tr,),
        in_specs=[pl.BlockSpec((planes, tr, LANES), lambda i: (0, i, 0)),
                  pl.BlockSpec((n_batch, D_STATE), const2),
                  pl.BlockSpec((n_batch, D_STATE), const2),
                  pl.BlockSpec((1, D_STATE), const2),
                  pl.BlockSpec((1, D_STATE), const2),
                  pl.BlockSpec((1, D_STATE), const2),
                  pl.BlockSpec((4, 128, 512), const3),
                  pl.BlockSpec((4, 128, 512), const3),
                  pl.BlockSpec((4, 512, 128), const3),
                  pl.BlockSpec((4, 512, 128), const3),
                  pl.BlockSpec((1, D_SSM), const2),
                  pl.BlockSpec((D_SSM, 2 * D_SSM), const2),
                  pl.BlockSpec((1, 2 * D_SSM), const2)],
        out_specs=[pl.BlockSpec((planes, tr, LANES), lambda i: (0, i, 0)),
                   pl.BlockSpec((n_batch, D_STATE), const2),
                   pl.BlockSpec((n_batch, D_STATE), const2)],
        out_shape=[jax.ShapeDtypeStruct((planes, rows, LANES), F32),
                   jax.ShapeDtypeStruct((n_batch, D_STATE), F32),
                   jax.ShapeDtypeStruct((n_batch, D_STATE), F32)],
        scratch_shapes=[pltpu.VMEM((2, D_STATE), F32),
                        pltpu.VMEM((4, 128, 512), BF16), pltpu.VMEM((4, 128, 512), BF16),
                        pltpu.VMEM((4, 512, 128), BF16), pltpu.VMEM((4, 512, 128), BF16),
                        pltpu.VMEM((D_SSM, 2 * D_SSM), BF16),
                        pltpu.VMEM((n_batch, D_STATE), F32), pltpu.VMEM((n_batch, D_STATE), F32),
                        pltpu.VMEM((tr, D_STATE), F32), pltpu.VMEM((tr, D_STATE), F32)],
        compiler_params=_params(1),
        name="ssm",
    )(u_rows, h0_re.reshape(n_batch, D_STATE), h0_im.reshape(n_batch, D_STATE),
      flat(lp['lam_re']), flat(lp['lam_im']), ldt,
      _block_diag_b(lp['ssm_b_re']), _block_diag_b(lp['ssm_b_im']),
      _block_diag_c(lp['ssm_c_re']), _block_diag_c(lp['ssm_c_im']),
      lp['ssm_d'].reshape(1, D_SSM), lp['w_glu'], lp['b_glu'].reshape(1, 2 * D_SSM))
    return y, sr, si


def _softmax_pv(s, v):
    m = jnp.max(s, axis=-1, keepdims=True)
    p = jnp.exp(s - m)
    l = jnp.sum(p, axis=-1, keepdims=True)
    return jnp.dot(p.astype(BF16), v, preferred_element_type=F32) / l


def _attend(q, k, v, out_ref, bias_ref=None, valid=None):
    qb = (q * ATT_SCALE).astype(BF16)
    for h in range(N_HEADS):
        sl = slice(HEAD_DIM * h, HEAD_DIM * (h + 1))
        s = lax.dot_general(qb[:, sl], k[:, sl], _NT, preferred_element_type=F32)
        if bias_ref is not None:
            s = s + bias_ref[h]
        if valid is not None:
            s = jnp.where(valid, s, NEG_INF)
        out_ref[0, :, sl] = _softmax_pv(s, v[:, sl])


def _attn_prompt_kernel(q_ref, k0_ref, k1_ref, k2_ref, v0_ref, v1_ref, v2_ref, qm_ref,
                        mk_ref, mv_ref, bias_ref, ya_ref, ym_ref):
    tq = q_ref.shape[1]
    k = jnp.concatenate([k0_ref[0], k1_ref[0], k2_ref[0]], axis=0).astype(BF16)
    v = jnp.concatenate([v0_ref[0], v1_ref[0], v2_ref[0]], axis=0).astype(BF16)
    kpos = (pl.program_id(1) - 2) * tq + lax.broadcasted_iota(I32, (1, 3 * tq), 1)
    _attend(q_ref[0], k, v, ya_ref, bias_ref, kpos >= 0)
    _attend(qm_ref[0], mk_ref[0].astype(BF16), mv_ref[0].astype(BF16), ym_ref)


def _attn_sample_kernel(q_ref, kn_ref, vn_ref, qm_ref, ck_ref, cv_ref, mk_ref, mv_ref, bias_ref,
                        ya_ref, ym_ref, nk_ref, nv_ref):
    n = kn_ref.shape[1]
    kk = jnp.concatenate([ck_ref[0], kn_ref[0]], axis=0)
    vv = jnp.concatenate([cv_ref[0], vn_ref[0]], axis=0)
    nk_ref[0] = kk[n:]
    nv_ref[0] = vv[n:]
    _attend(q_ref[0], kk.astype(BF16), vv.astype(BF16), ya_ref, bias_ref)
    _attend(qm_ref[0], mk_ref[0].astype(BF16), mv_ref[0].astype(BF16), ym_ref)


def _rel_bias(table, n_q, n_k, band_mask):
    period = n_q + n_k
    m = jnp.arange(period)
    offset = jnp.where(m < n_k, m, m - period)
    idx = jnp.clip(BAND - offset, -REL_CLIP, REL_CLIP) + REL_CLIP
    f = table.astype(F32)[:, idx]
    flat = jnp.tile(f, (1, n_q))[:, :n_q * (period - 1)]
    bias = flat.reshape(N_HEADS, n_q, period - 1)[:, :, :n_k]
    if band_mask:
        qi = jnp.arange(n_q)[:, None]
        kj = jnp.arange(n_k)[None, :]
        off = kj // CHUNK - qi // CHUNK
        ok = (off >= 0) & (off <= N_PREV_CHUNKS)
        bias = jnp.where(ok[None], bias, NEG_INF)
    return bias


def _attn_prompt(zr, mk, mv, table):
    b, s, _ = zr.shape
    tq = ATT_TQ
    bias = _rel_bias(table, tq, 3 * tq, True)
    col = lambda c: (lambda i, j: (i, j, c))
    prev = lambda c, d: (lambda i, j: (i, jnp.maximum(j - d, 0), c))
    blk = lambda: (1, tq, D_ATT)
    return pl.pallas_call(
        _attn_prompt_kernel,
        grid=(b, s // tq),
        in_specs=[pl.BlockSpec(blk(), col(0)),
                  pl.BlockSpec(blk(), prev(1, 2)), pl.BlockSpec(blk(), prev(1, 1)),
                  pl.BlockSpec(blk(), col(1)),
                  pl.BlockSpec(blk(), prev(2, 2)), pl.BlockSpec(blk(), prev(2, 1)),
                  pl.BlockSpec(blk(), col(2)),
                  pl.BlockSpec(blk(), col(3)),
                  pl.BlockSpec((1, N_MEM, D_MEM), lambda i, j: (i, 0, 0)),
                  pl.BlockSpec((1, N_MEM, D_MEM), lambda i, j: (i, 0, 0)),
                  pl.BlockSpec((N_HEADS, tq, 3 * tq), lambda i, j: (0, 0, 0))],
        out_specs=[pl.BlockSpec(blk(), col(0)), pl.BlockSpec(blk(), col(0))],
        out_shape=[jax.ShapeDtypeStruct((b, s, D_ATT), F32),
                   jax.ShapeDtypeStruct((b, s, D_MEM), F32)],
        compiler_params=_params(2),
        name="attn_prompt",
    )(zr, zr, zr, zr, zr, zr, zr, zr, mk, mv, bias)


def _attn_sample(zr, cache_k, cache_v, mk, mv, table):
    b, n, _ = zr.shape
    w = cache_k.shape[1]
    bias = _rel_bias(table, n, w + n, False)
    col = lambda c: (lambda i: (i, 0, c))
    blk = (1, n, D_ATT)
    cblk = (1, w, D_ATT)
    mblk = (1, N_MEM, D_MEM)
    row = lambda i: (i, 0, 0)
    return pl.pallas_call(
        _attn_sample_kernel,
        grid=(b,),
        in_specs=[pl.BlockSpec(blk, col(0)), pl.BlockSpec(blk, col(1)), pl.BlockSpec(blk, col(2)),
                  pl.BlockSpec(blk, col(3)),
                  pl.BlockSpec(cblk, row), pl.BlockSpec(cblk, row),
                  pl.BlockSpec(mblk, row), pl.BlockSpec(mblk, row),
                  pl.BlockSpec((N_HEADS, n, w + n), lambda i: (0, 0, 0))],
        out_specs=[pl.BlockSpec(blk, row), pl.BlockSpec(blk, row),
                   pl.BlockSpec(cblk, row), pl.BlockSpec(cblk, row)],
        out_shape=[jax.ShapeDtypeStruct((b, n, D_ATT), F32),
                   jax.ShapeDtypeStruct((b, n, D_MEM), F32),
                   jax.ShapeDtypeStruct((b, w, D_ATT), F32),
                   jax.ShapeDtypeStruct((b, w, D_ATT), F32)],
        compiler_params=_params(1),
        name="attn_sample",
    )(zr, zr, zr, zr, cache_k, cache_v, mk, mv, bias)


def _rms(x, g):
    return x * lax.rsqrt(jnp.mean(jnp.square(x), axis=-1, keepdims=True) + LN_EPS) * g


def _layer_norm(x, g, b):
    mu = jnp.mean(x, axis=-1, keepdims=True)
    xc = x - mu
    var = jnp.mean(jnp.square(xc), axis=-1, keepdims=True)
    return xc * lax.rsqrt(var + LN_EPS) * g + b


def _split_bf16(a):
    hi = a.astype(BF16)
    lo = (a - hi.astype(F32)).astype(BF16)
    return hi, lo


def _merge_kernel(x_ref, ys_ref, ya_ref, ym_ref, gs_ref, ga_ref, gm_ref, wo_ref, l1g_ref, l1b_ref,
                  wrt_ref, brt_ref,
                  h_ref, pos_ref, gate_ref, cnt_ref,
                  wo_sc, *, nb):
    st = x_ref.shape[1]
    tm = nb * st
    n_batch = ys_ref.shape[1] // st

    @pl.when((pl.program_id(0) == 0) & (pl.program_id(1) == 0))
    def _():
        wo_sc[...] = wo_ref[...].astype(BF16)

    x = x_ref[...].reshape(tm, D_MODEL)
    first = pl.program_id(1) * nb
    ys = jnp.concatenate(
        [jnp.concatenate([ys_ref[c, pl.ds(first + i, st, stride=n_batch), :] for c in range(D_SSM // LANES)],
                         axis=1) for i in range(nb)], axis=0)
    ya = ya_ref[...].reshape(tm, D_ATT)
    ym = ym_ref[...].reshape(tm, D_MEM)
    a = _rms(ys, gs_ref[...]).astype(BF16)
    b = _rms(ya, ga_ref[...]).astype(BF16)
    c = _rms(ym, gm_ref[...]).astype(BF16)
    mix = (jnp.dot(a, wo_sc[0:D_SSM, :], preferred_element_type=F32)
           + jnp.dot(b, wo_sc[D_SSM:D_SSM + D_ATT, :], preferred_element_type=F32)
           + jnp.dot(c, wo_sc[D_SSM + D_ATT:, :], preferred_element_type=F32))
    h = _layer_norm(DEEPNORM_ALPHA * x + mix, l1g_ref[...], l1b_ref[...])
    h_ref[...] = h

    h_hi, h_lo = _split_bf16(h)
    w_hi, w_lo = _split_bf16(wrt_ref[...])
    logits = (lax.dot_general(w_hi, h_hi, _NT, preferred_element_type=F32)
              + lax.dot_general(w_hi, h_lo, _NT, preferred_element_type=F32)
              + lax.dot_general(w_lo, h_hi, _NT, preferred_element_type=F32)
              + brt_ref[...])
    erow = lax.broadcasted_iota(I32, (N_EXPERTS, tm), 0).astype(F32)
    tops, picks = [], []
    l = logits
    for k in range(TOP_K):
        m = jnp.max(l, axis=0, keepdims=True)
        e = jnp.min(jnp.where(l == m, erow, float(N_EXPERTS)), axis=0, keepdims=True)
        pick = erow == e
        tops.append(m)
        picks.append(jnp.where(pick, 1.0, 0.0))
        l = jnp.where(pick, -jnp.inf, l)
    ex = [jnp.exp(t - tops[0]) for t in tops]
    den = ex[0] + ex[1] + ex[2] + ex[3]
    for k in range(TOP_K):
        gate_ref[k:k + 1, :] = ex[k] / den

    chosen = picks[0] + picks[1] + picks[2] + picks[3]
    chosen_b = chosen.astype(BF16)
    earlier_tok = (lax.broadcasted_iota(I32, (tm, tm), 0) < lax.broadcasted_iota(I32, (tm, tm), 1))
    within = jnp.dot(chosen_b, jnp.where(earlier_tok, 1.0, 0.0).astype(BF16),
                     preferred_element_type=F32)
    lower_exp = (lax.broadcasted_iota(I32, (N_EXPERTS, N_EXPERTS), 1)
                 < lax.broadcasted_iota(I32, (N_EXPERTS, N_EXPERTS), 0))
    below = jnp.dot(jnp.where(lower_exp, 1.0, 0.0).astype(BF16), chosen_b,
                    preferred_element_type=F32)
    slot = within + jnp.sum(below, axis=1, keepdims=True)
    for k in range(TOP_K):
        pos_ref[k:k + 1, :] = jnp.sum(picks[k] * slot, axis=0, keepdims=True).astype(I32)
    cnt_ref[0] = jnp.sum(chosen, axis=1, keepdims=True)


def _merge(x, ys_tm, ya, ym, lp, nb, st):
    b, s, _ = x.shape
    tm = nb * st
    assert tm == TOKEN_TM
    n_s = s // st
    t = b * s
    tile = lambda j, i: (i * n_s + j)
    c2 = lambda j, i: (0, 0)
    row3 = lambda j, i: (i, j, 0)
    vec = lambda a: a.reshape(1, -1)
    return pl.pallas_call(
        functools.partial(_merge_kernel, nb=nb),
        grid=(n_s, b // nb),
        in_specs=[pl.BlockSpec((nb, st, D_MODEL), row3),
                  pl.BlockSpec((D_SSM // LANES, st * b, LANES), lambda j, i: (0, j, 0)),
                  pl.BlockSpec((nb, st, D_ATT), row3),
                  pl.BlockSpec((nb, st, D_MEM), row3),
                  pl.BlockSpec((1, D_SSM), c2), pl.BlockSpec((1, D_ATT), c2),
                  pl.BlockSpec((1, D_MEM), c2),
                  pl.BlockSpec((D_MODEL, D_MODEL), c2),
                  pl.BlockSpec((1, D_MODEL), c2), pl.BlockSpec((1, D_MODEL), c2),
                  pl.BlockSpec((N_EXPERTS, D_MODEL), c2), pl.BlockSpec((N_EXPERTS, 1), c2)],
        out_specs=[pl.BlockSpec((tm, D_MODEL), lambda j, i: (tile(j, i), 0)),
                   pl.BlockSpec((TOP_K, tm), lambda j, i: (0, tile(j, i))),
                   pl.BlockSpec((TOP_K, tm), lambda j, i: (0, tile(j, i))),
                   pl.BlockSpec((1, N_EXPERTS, 1), lambda j, i: (tile(j, i), 0, 0))],
        out_shape=[jax.ShapeDtypeStruct((t, D_MODEL), F32),
                   jax.ShapeDtypeStruct((TOP_K, t), I32),
                   jax.ShapeDtypeStruct((TOP_K, t), F32),
                   jax.ShapeDtypeStruct((t // tm, N_EXPERTS, 1), F32)],
        scratch_shapes=[pltpu.VMEM((D_MODEL, D_MODEL), BF16)],
        compiler_params=_params(2),
        name="merge_router",
    )(x, ys_tm, ya, ym, vec(lp['g_ssm']), vec(lp['g_att']), vec(lp['g_mem']), lp['w_out'],
      vec(lp['ln1_g']), vec(lp['ln1_b']), lp['w_router'].T, lp['b_router'].reshape(N_EXPERTS, 1))


def _rows(start, size):
    return pl.ds(pl.multiple_of(start * ROW_SUBLANES, ROW_SUBLANES), size * ROW_SUBLANES)


def _store_rows(ref, value):
    n = value.shape[0]
    for j in range(ROW_SUBLANES):
        ref[pl.ds(j, n, stride=ROW_SUBLANES), :] = value[:, LANES * j:LANES * (j + 1)]


def _load_rows(ref, dtype=F32):
    n = ref.shape[0] // ROW_SUBLANES
    return jnp.concatenate([ref[pl.ds(j, n, stride=ROW_SUBLANES), :].astype(dtype)
                            for j in range(ROW_SUBLANES)], axis=1)


def _for_each_run_piece(n, max_rows, fn):
    for bit in reversed(range(max_rows.bit_length())):
        size = 1 << bit
        start = (n >> (bit + 1)) << (bit + 1)

        @pl.when((n & size) != 0)
        def _(start=start, size=size):
            fn(start, size)


def _dispatch_kernel(n_ref, off_ref, dst_ref, padlo_ref, padn_ref, used_ref,
                     pos_ref, h_ref, xs_hbm, sorted_sc, zero_sc, sem, zsem):
    i = pl.program_id(0)
    tm = h_ref.shape[0]
    n_slots = TOP_K * tm
    n_blocks = xs_hbm.shape[0] // (EXPERT_TM * ROW_SUBLANES)

    @pl.when(i == 0)
    def _():
        zero_sc[...] = jnp.zeros_like(zero_sc)

        def pad_copy(e, start, size):
            return pltpu.make_async_copy(zero_sc.at[_rows(0, size)],
                                         xs_hbm.at[_rows(padlo_ref[e] + start, size)], zsem)

        def tail_copy(blk):
            return pltpu.make_async_copy(zero_sc, xs_hbm.at[_rows(blk * EXPERT_TM, EXPERT_TM)], zsem)

        for e in range(N_EXPERTS):
            _for_each_run_piece(padn_ref[e], EXPERT_TM - 1,
                                lambda start, size, e=e: pad_copy(e, start, size).start())

        def tail_start(blk, carry):
            tail_copy(blk).start()
            return carry

        lax.fori_loop(used_ref[0], n_blocks, tail_start, 0)
        for e in range(N_EXPERTS):
            _for_each_run_piece(padn_ref[e], EXPERT_TM - 1,
                                lambda start, size, e=e: pad_copy(e, start, size).wait())

        def tail_wait(blk, carry):
            tail_copy(blk).wait()
            return carry

        lax.fori_loop(used_ref[0], n_blocks, tail_wait, 0)

    n_tiles = pl.num_programs(0)
    slot = lax.rem(i, 2)
    buf = sorted_sc.at[slot]

    def wait_tile(sl):
        pltpu.make_async_copy(sorted_sc.at[sl], xs_hbm.at[_rows(0, n_slots)], sem.at[sl]).wait()

    @pl.when(i >= 2)
    def _():
        wait_tile(slot)

    pos = pos_ref[...]
    srow = lax.broadcasted_iota(I32, (n_slots, tm), 0)
    perm = jnp.where(srow == pos[0:1], 1.0,
                     jnp.where(srow == pos[1:2], 1.0,
                               jnp.where(srow == pos[2:3], 1.0,
                                         jnp.where(srow == pos[3:4], 1.0, 0.0))))
    _store_rows(buf, jnp.dot(perm.astype(BF16), h_ref[...].astype(BF16), preferred_element_type=F32))

    base = i * N_EXPERTS
    for e in range(N_EXPERTS):
        off = off_ref[base + e]
        dst = dst_ref[base + e]

        def run_start(start, size, off=off, dst=dst):
            pltpu.make_async_copy(buf.at[_rows(off + start, size)],
                                  xs_hbm.at[_rows(dst + start, size)], sem.at[slot]).start()

        _for_each_run_piece(n_ref[base + e], tm, run_start)

    @pl.when(i == n_tiles - 1)
    def _():
        @pl.when(i >= 1)
        def _():
            wait_tile(1 - slot)

        wait_tile(slot)


def _dispatch(run_n, run_off, run_dst, pad_lo, pad_n, n_used, pos, h, cap):
    t = h.shape[0]
    tm = TOKEN_TM
    grid_spec = pltpu.PrefetchScalarGridSpec(
        num_scalar_prefetch=6,
        grid=(t // tm,),
        in_specs=[pl.BlockSpec((TOP_K, tm), lambda i, *_: (0, i)),
                  pl.BlockSpec((tm, D_MODEL), lambda i, *_: (i, 0))],
        out_specs=pl.BlockSpec(memory_space=pl.ANY),
        scratch_shapes=[pltpu.VMEM((2, TOP_K * tm * ROW_SUBLANES, LANES), F32),
                        pltpu.VMEM((EXPERT_TM * ROW_SUBLANES, LANES), F32),
                        pltpu.SemaphoreType.DMA((2,)), pltpu.SemaphoreType.DMA],
    )
    return pl.pallas_call(
        _dispatch_kernel,
        grid_spec=grid_spec,
        out_shape=jax.ShapeDtypeStruct((cap * ROW_SUBLANES, LANES), F32),
        compiler_params=_params(1),
        name="moe_dispatch",
    )(run_n, run_off, run_dst, pad_lo, pad_n, n_used, pos, h)


def _expert_kernel(be_ref, used_ref, x_ref, wgu_ref, bgu_ref, wd_ref, bd_ref, o_ref, wgu_sc, wd_sc):
    i = pl.program_id(0)

    @pl.when(i < used_ref[0])
    def _():
        prev = jnp.maximum(i - 1, 0)

        @pl.when((i == 0) | (be_ref[i] != be_ref[prev]))
        def _():
            wgu_sc[...] = wgu_ref[0].astype(BF16)
            wd_sc[...] = wd_ref[0].astype(BF16)

        gu = jnp.dot(_load_rows(x_ref).astype(BF16), wgu_sc[...], preferred_element_type=F32) + bgu_ref[0]
        gate = jnp.minimum(gu[:, :D_FF], SWIGLU_LIMIT)
        lin = jnp.clip(gu[:, D_FF:], -SWIGLU_LIMIT, SWIGLU_LIMIT)
        act = gate * jax.nn.sigmoid(SWIGLU_ALPHA * gate) * (lin + 1.0)
        _store_rows(o_ref, jnp.dot(act.astype(BF16), wd_sc[...], preferred_element_type=F32) + bd_ref[0])

    @pl.when(i >= used_ref[0])
    def _():
        o_ref[...] = jnp.zeros_like(o_ref)


def _experts(block_expert, n_used, xs, lp):
    tm = EXPERT_TM * ROW_SUBLANES
    grid_spec = pltpu.PrefetchScalarGridSpec(
        num_scalar_prefetch=2,
        grid=(xs.shape[0] // tm,),
        in_specs=[pl.BlockSpec((tm, LANES), lambda i, be, nu: (i, 0)),
                  pl.BlockSpec((1, D_MODEL, 2 * D_FF), lambda i, be, nu: (be[i], 0, 0)),
                  pl.BlockSpec((1, 1, 2 * D_FF), lambda i, be, nu: (be[i], 0, 0)),
                  pl.BlockSpec((1, D_FF, D_MODEL), lambda i, be, nu: (be[i], 0, 0)),
                  pl.BlockSpec((1, 1, D_MODEL), lambda i, be, nu: (be[i], 0, 0))],
        out_specs=pl.BlockSpec((tm, LANES), lambda i, be, nu: (i, 0)),
        scratch_shapes=[pltpu.VMEM((D_MODEL, 2 * D_FF), BF16), pltpu.VMEM((D_FF, D_MODEL), BF16)],
    )
    return pl.pallas_call(
        _expert_kernel,
        grid_spec=grid_spec,
        out_shape=jax.ShapeDtypeStruct(xs.shape, F32),
        compiler_params=_params(1),
        name="moe_experts",
    )(block_expert, n_used, xs, lp['w_gu'], lp['b_gu'].reshape(N_EXPERTS, 1, 2 * D_FF),
      lp['w_down'], lp['b_down'].reshape(N_EXPERTS, 1, D_MODEL))


def _combine_kernel(n_ref, off_ref, dst_ref, pos_ref, gate_ref, h_ref, ys_hbm, g_ref, b_ref, y_ref,
                    sorted_sc, sem):
    i = pl.program_id(0)
    n_tiles = pl.num_programs(0) - 1
    tm = h_ref.shape[0]
    n_slots = TOP_K * tm

    @pl.when(i < n_tiles)
    def _():
        slot = lax.rem(i, 2)
        buf = sorted_sc.at[slot]
        base = i * N_EXPERTS
        for e in range(N_EXPERTS):
            off = off_ref[base + e]
            dst = dst_ref[base + e]

            def run_start(start, size, off=off, dst=dst):
                pltpu.make_async_copy(ys_hbm.at[_rows(dst + start, size)],
                                      buf.at[_rows(off + start, size)], sem.at[slot]).start()

            _for_each_run_piece(n_ref[base + e], tm, run_start)

    @pl.when(i >= 1)
    def _():
        slot = lax.rem(i - 1, 2)
        buf = sorted_sc.at[slot]
        pltpu.make_async_copy(ys_hbm.at[_rows(0, n_slots)], buf, sem.at[slot]).wait()

        pos = pos_ref[...]
        gates = gate_ref[...]
        scol = lax.broadcasted_iota(I32, (tm, n_slots), 1)
        w = jnp.where(scol == pos[:, 0:1], gates[:, 0:1],
                      jnp.where(scol == pos[:, 1:2], gates[:, 1:2],
                                jnp.where(scol == pos[:, 2:3], gates[:, 2:3],
                                          jnp.where(scol == pos[:, 3:4], gates[:, 3:4], 0.0))))
        f = jnp.dot(w.astype(BF16), _load_rows(buf, BF16), preferred_element_type=F32)
        y_ref[...] = _layer_norm(DEEPNORM_ALPHA * h_ref[...] + f, g_ref[...], b_ref[...])


def _combine(run_n, run_off, run_dst, pos_t, gates_t, h, ys, lp):
    t = h.shape[0]
    tm = TOKEN_TM
    c2 = lambda i, *_: (0, 0)
    done = lambda i, *_: (jnp.maximum(i - 1, 0), 0)
    grid_spec = pltpu.PrefetchScalarGridSpec(
        num_scalar_prefetch=3,
        grid=(t // tm + 1,),
        in_specs=[pl.BlockSpec((tm, TOP_K), done),
                  pl.BlockSpec((tm, TOP_K), done),
                  pl.BlockSpec((tm, D_MODEL), done),
                  pl.BlockSpec(memory_space=pl.ANY),
                  pl.BlockSpec((1, D_MODEL), c2), pl.BlockSpec((1, D_MODEL), c2)],
        out_specs=pl.BlockSpec((tm, D_MODEL), done),
        scratch_shapes=[pltpu.VMEM((2, TOP_K * tm * ROW_SUBLANES, LANES), F32),
                        pltpu.SemaphoreType.DMA((2,))],
    )
    return pl.pallas_call(
        _combine_kernel,
        grid_spec=grid_spec,
        out_shape=jax.ShapeDtypeStruct((t, D_MODEL), F32),
        compiler_params=_params(1),
        name="moe_combine",
    )(run_n, run_off, run_dst, pos_t, gates_t, h, ys,
      lp['ln2_g'].reshape(1, D_MODEL), lp['ln2_b'].reshape(1, D_MODEL))


def _moe_and_norm(h, pos, gates, tile_counts, lp):
    t = h.shape[0]
    te = EXPERT_TM
    n_tiles = t // TOKEN_TM
    n_blocks = (t * TOP_K) // te + N_EXPERTS
    cap = n_blocks * te
    cnt = tile_counts.reshape(n_tiles, N_EXPERTS).astype(I32)
    counts = jnp.sum(cnt, axis=0)
    padded = (counts + te - 1) // te * te
    pad_ends = jnp.cumsum(padded)
    pad_starts = pad_ends - padded
    run_dst = pad_starts[None, :] + jnp.cumsum(cnt, axis=0) - cnt
    run_off = jnp.cumsum(cnt, axis=1) - cnt
    blk_start = jnp.arange(n_blocks, dtype=I32) * te
    n_used = (pad_ends[-1] // te).astype(I32)
    be = jnp.minimum(jnp.sum(blk_start[:, None] >= pad_ends[None, :], axis=1), N_EXPERTS - 1).astype(I32)
    be = jnp.where(blk_start < pad_ends[-1], be, be[jnp.maximum(n_used - 1, 0)])
    n_used = n_used.reshape(1)
    flat = lambda a: a.reshape(-1).astype(I32)
    xs = _dispatch(flat(cnt), flat(run_off), flat(run_dst), flat(pad_starts + counts),
                   flat(padded - counts), n_used, pos, h, cap)
    ys = _experts(be, n_used, xs, lp)
    return _combine(flat(cnt), flat(run_off), flat(run_dst), pos.T, gates.T, h, ys, lp)


def _layer(x, zr_attn, u_rows, h0_re, h0_im, lp, nb, st):
    b, s, _ = x.shape
    ys_tm, s_re, s_im = _ssm(u_rows, h0_re, h0_im, lp, b)
    ya, ym = zr_attn
    h, pos, gates, tile_counts = _merge(x, ys_tm, ya, ym, lp, nb, st)
    y = _moe_and_norm(h, pos, gates, tile_counts, lp)
    return (y.reshape(b, s, D_MODEL), s_re.reshape(b, N_GROUPS, SSM_STATE),
            s_im.reshape(b, N_GROUPS, SSM_STATE))


def kernel(x_prompt, x_sample, cache_attn_k, cache_attn_v, cache_mem_k, cache_mem_v, state_ssm_re, state_ssm_im, mem_prompt, w_in, lam_re, lam_im, log_dt, ssm_b_re, ssm_b_im, ssm_c_re, ssm_c_im, ssm_d, w_glu, b_glu, rel_bias, w_mem_kv, g_ssm, g_att, g_mem, w_out, ln1_g, ln1_b, w_router, b_router, w_gu, b_gu, w_down, b_down, ln2_g, ln2_b):
    assert w_in.shape[0] == 1, "single-layer step"
    lp = dict(w_in=w_in[0], lam_re=lam_re[0], lam_im=lam_im[0], log_dt=log_dt[0],
              ssm_b_re=ssm_b_re[0], ssm_b_im=ssm_b_im[0], ssm_c_re=ssm_c_re[0], ssm_c_im=ssm_c_im[0],
              ssm_d=ssm_d[0], w_glu=w_glu[0], b_glu=b_glu[0], rel_bias=rel_bias[0],
              w_mem_kv=w_mem_kv[0], g_ssm=g_ssm[0], g_att=g_att[0], g_mem=g_mem[0], w_out=w_out[0],
              ln1_g=ln1_g[0], ln1_b=ln1_b[0], w_router=w_router[0], b_router=b_router[0],
              w_gu=w_gu[0], b_gu=b_gu[0], w_down=w_down[0], b_down=b_down[0],
              ln2_g=ln2_g[0], ln2_b=ln2_b[0])

    bp, sp, _ = x_prompt.shape
    u_tm, zr = _in_proj(x_prompt, lp['w_in'], min(128, sp))
    mk, mv = _mem_kv(mem_prompt, lp['w_mem_kv'])
    ya, ym = _attn_prompt(zr, mk, mv, lp['rel_bias'])
    zeros = jnp.zeros((bp, D_STATE), F32)
    y_p, sr_p, si_p = _layer(x_prompt, (ya, ym), u_tm, zeros, zeros, lp, 1, min(TOKEN_TM, sp))
    w = min(BAND, sp)
    heads = lambda a: a.reshape(a.shape[0], a.shape[1], N_HEADS, HEAD_DIM)
    k_p = heads(zr[:, sp - w:, D_ATT:2 * D_ATT])
    v_p = heads(zr[:, sp - w:, 2 * D_ATT:3 * D_ATT])

    bs, ss, _ = x_sample.shape
    wc = cache_attn_k.shape[2]
    u_tm_s, zr_s = _in_proj(x_sample, lp['w_in'], ss)
    ya_s, ym_s, nk, nv = _attn_sample(
        zr_s, cache_attn_k[0].reshape(bs, wc, D_ATT), cache_attn_v[0].reshape(bs, wc, D_ATT),
        cache_mem_k[0].reshape(bs, N_MEM, D_MEM), cache_mem_v[0].reshape(bs, N_MEM, D_MEM),
        lp['rel_bias'])
    nb_s = TOKEN_TM // ss
    y_s, sr_s, si_s = _layer(x_sample, (ya_s, ym_s), u_tm_s, state_ssm_re[0], state_ssm_im[0], lp,
                             nb_s, ss)

    return (y_p, y_s,
            k_p[None], v_p[None], heads(mk)[None], heads(mv)[None], sr_p[None], si_p[None],
            heads(nk)[None], heads(nv)[None], sr_s[None], si_s[None])
```

```python
import functools

import jax
import jax.numpy as jnp
from jax import lax
from jax.experimental import pallas as pl
from jax.experimental.pallas import tpu as pltpu

F32 = jnp.float32
BF16 = jnp.bfloat16
I32 = jnp.int32

D_MODEL = 1024
D_SSM = 512
D_ATT = 256
D_MEM = 256
D_IN = D_SSM + 3 * D_ATT + D_MEM
D_REST = D_IN - D_SSM
HEAD_DIM = 64
N_HEADS = 4
N_GROUPS = 32
SSM_GROUP = 16
SSM_STATE = 64
D_STATE = N_GROUPS * SSM_STATE
CHUNK = 64
N_PREV_CHUNKS = 8
BAND = N_PREV_CHUNKS * CHUNK
REL_CLIP = 128
N_MEM = 256
N_EXPERTS = 32
TOP_K = 4
D_FF = D_MODEL
SWIGLU_LIMIT = 7.0
SWIGLU_ALPHA = 1.702
LN_EPS = 1e-5
NEG_INF = -1e30
ATT_SCALE = HEAD_DIM ** -0.5
DEEPNORM_ALPHA = 2.0 ** 0.25

V7X_VMEM_LIMIT = 56 * 1024 * 1024
ATT_TQ = 4 * CHUNK
SCAN_LANES = 512
SCAN_ROWS = 1024
LANES = 128
ROW_SUBLANES = D_MODEL // LANES
EXPERT_TM = 256
TOKEN_TM = 512

_NT = (((1,), (1,)), ((), ()))


def _params(n_axes, vmem=V7X_VMEM_LIMIT):
    return pltpu.CompilerParams(dimension_semantics=("arbitrary",) * n_axes,
                                vmem_limit_bytes=vmem)


def _in_proj_kernel(x_ref, w_ref, u_ref, z_ref, wb_ref):
    @pl.when(pl.program_id(0) == 0)
    def _():
        wb_ref[...] = w_ref[...].astype(BF16)

    nb, ts, _ = x_ref.shape
    x = x_ref[...].reshape(nb * ts, D_MODEL).astype(BF16)
    z = jnp.dot(x, wb_ref[...], preferred_element_type=F32)
    for b in range(nb):
        for c in range(D_SSM // LANES):
            u_ref[c, pl.ds(b, ts, stride=nb), :] = z[b * ts:(b + 1) * ts, LANES * c:LANES * (c + 1)]
    z_ref[...] = z[:, D_SSM:].reshape(nb, ts, D_REST)


def _in_proj(x, w_in, ts):
    b, s, _ = x.shape
    return pl.pallas_call(
        _in_proj_kernel,
        grid=(s // ts,),
        in_specs=[pl.BlockSpec((b, ts, D_MODEL), lambda j: (0, j, 0)),
                  pl.BlockSpec((D_MODEL, D_IN), lambda j: (0, 0))],
        out_specs=[pl.BlockSpec((D_SSM // LANES, ts * b, LANES), lambda j: (0, j, 0)),
                   pl.BlockSpec((b, ts, D_REST), lambda j: (0, j, 0))],
        out_shape=[jax.ShapeDtypeStruct((D_SSM // LANES, s * b, LANES), F32),
                   jax.ShapeDtypeStruct((b, s, D_REST), F32)],
        scratch_shapes=[pltpu.VMEM((D_MODEL, D_IN), BF16)],
        compiler_params=_params(1),
        name="in_proj",
    )(x, w_in)


def _mem_kv_kernel(m_ref, w_ref, mk_ref, mv_ref):
    kv = jnp.dot(m_ref[0].astype(BF16), w_ref[...].astype(BF16), preferred_element_type=F32)
    mk_ref[0] = kv[:, :D_MEM]
    mv_ref[0] = kv[:, D_MEM:]


def _mem_kv(mem, w_mem_kv):
    b = mem.shape[0]
    return pl.pallas_call(
        _mem_kv_kernel,
        grid=(b,),
        in_specs=[pl.BlockSpec((1, N_MEM, D_MODEL), lambda i: (i, 0, 0)),
                  pl.BlockSpec((D_MODEL, 2 * D_MEM), lambda i: (0, 0))],
        out_specs=[pl.BlockSpec((1, N_MEM, D_MEM), lambda i: (i, 0, 0)),
                   pl.BlockSpec((1, N_MEM, D_MEM), lambda i: (i, 0, 0))],
        out_shape=[jax.ShapeDtypeStruct((b, N_MEM, D_MEM), F32)] * 2,
        compiler_params=_params(1),
        name="mem_kv",
    )(mem, w_mem_kv)


def _ssm_kernel(u_ref, h0r_ref, h0i_ref, lr_ref, li_ref, ldt_ref, bre_ref, bim_ref,
                cre_ref, cim_ref, d_ref, wg_ref, bg_ref,
                y_ref, sr_ref, si_ref,
                a_sc, bbr_sc, bbi_sc, cr_sc, ci_sc, wg_sc, str_sc, sti_sc, xr_sc, xi_sc,
                *, n_batch):
    n_rows = u_ref.shape[1]
    n_steps = n_rows // n_batch

    @pl.when(pl.program_id(0) == 0)
    def _():
        lr = lr_ref[...]
        li = li_ref[...]
        dt = jnp.exp(ldt_ref[...])
        mag = jnp.exp(lr * dt)
        ar = mag * jnp.cos(li * dt)
        ai = mag * jnp.sin(li * dt)
        den = lr * lr + li * li
        fr = ((ar - 1.0) * lr + ai * li) / den
        fi = (ai * lr - (ar - 1.0) * li) / den
        a_sc[0:1, :] = ar
        a_sc[1:2, :] = ai
        for j in range(4):
            frj = fr[:, 512 * j:512 * (j + 1)]
            fij = fi[:, 512 * j:512 * (j + 1)]
            bbr_sc[j] = (frj * bre_ref[j] - fij * bim_ref[j]).astype(BF16)
            bbi_sc[j] = (frj * bim_ref[j] + fij * bre_ref[j]).astype(BF16)
            cr_sc[j] = cre_ref[j].astype(BF16)
            ci_sc[j] = cim_ref[j].astype(BF16)
        wg_sc[...] = wg_ref[...].astype(BF16)
        str_sc[...] = h0r_ref[...]
        sti_sc[...] = h0i_ref[...]

    for j in range(4):
        uc = u_ref[j].astype(BF16)
        xr_sc[:, 512 * j:512 * (j + 1)] = jnp.dot(uc, bbr_sc[j], preferred_element_type=F32)
        xi_sc[:, 512 * j:512 * (j + 1)] = jnp.dot(uc, bbi_sc[j], preferred_element_type=F32)

    for c in range(D_STATE // SCAN_LANES):
        lo = c * SCAN_LANES
        ar = jnp.broadcast_to(a_sc[0:1, lo:lo + SCAN_LANES], (n_batch, SCAN_LANES))
        ai = jnp.broadcast_to(a_sc[1:2, lo:lo + SCAN_LANES], (n_batch, SCAN_LANES))

        def step(t, carry, lo=lo, ar=ar, ai=ai):
            sr, si = carry
            r0 = pl.multiple_of(t * n_batch, n_batch)
            nr = ar * sr - ai * si + xr_sc[pl.ds(r0, n_batch), lo:lo + SCAN_LANES]
            ni = ar * si + ai * sr + xi_sc[pl.ds(r0, n_batch), lo:lo + SCAN_LANES]
            xr_sc[pl.ds(r0, n_batch), lo:lo + SCAN_LANES] = nr
            xi_sc[pl.ds(r0, n_batch), lo:lo + SCAN_LANES] = ni
            return nr, ni

        sr, si = lax.fori_loop(0, n_steps, step,
                               (str_sc[:, lo:lo + SCAN_LANES], sti_sc[:, lo:lo + SCAN_LANES]),
                               unroll=4)
        str_sc[:, lo:lo + SCAN_LANES] = sr
        sti_sc[:, lo:lo + SCAN_LANES] = si

    pieces = []
    for j in range(4):
        xr = xr_sc[:, 512 * j:512 * (j + 1)].astype(BF16)
        xi = xi_sc[:, 512 * j:512 * (j + 1)].astype(BF16)
        pieces.append(jnp.dot(xr, cr_sc[j], preferred_element_type=F32)
                      - jnp.dot(xi, ci_sc[j], preferred_element_type=F32))
    u = jnp.concatenate([u_ref[j] for j in range(4)], axis=1)
    y = jnp.concatenate(pieces, axis=1) + d_ref[...] * u
    y = jax.nn.gelu(y)
    z = jnp.dot(y.astype(BF16), wg_sc[...], preferred_element_type=F32) + bg_ref[...]
    out = z[:, :D_SSM] * jax.nn.sigmoid(z[:, D_SSM:])
    for j in range(D_SSM // LANES):
        y_ref[j] = out[:, LANES * j:LANES * (j + 1)]
    sr_ref[...] = str_sc[...]
    si_ref[...] = sti_sc[...]


def _block_diag_b(b):
    bt = b.transpose(0, 2, 1).reshape(4, 8, SSM_GROUP, SSM_STATE)
    same = jnp.eye(8, dtype=bool)[None, :, None, :, None]
    t = jnp.where(same, bt[:, :, :, None, :], 0.0)
    return t.reshape(4, 8 * SSM_GROUP, 8 * SSM_STATE)


def _block_diag_c(c):
    ct = c.transpose(0, 2, 1).reshape(4, 8, SSM_STATE, SSM_GROUP)
    same = jnp.eye(8, dtype=bool)[None, :, None, :, None]
    t = jnp.where(same, ct[:, :, :, None, :], 0.0)
    return t.reshape(4, 8 * SSM_STATE, 8 * SSM_GROUP)


def _ssm(u_rows, h0_re, h0_im, lp, n_batch):
    rows = u_rows.shape[1]
    planes = D_SSM // LANES
    tr = min(SCAN_ROWS, rows)
    flat = lambda a: a.reshape(1, D_STATE)
    ldt = jnp.repeat(lp['log_dt'], SSM_STATE).reshape(1, D_STATE)
    const2 = lambda i: (0, 0)
    const3 = lambda i: (0, 0, 0)
    y, sr, si = pl.pallas_call(
        functools.partial(_ssm_kernel, n_batch=n_batch),
        grid=(rows // tr,),
        in_specs=[pl.BlockSpec((planes, tr, LANES), lambda i: (0, i, 0)),
                  pl.BlockSpec((n_batch, D_STATE), const2),
                  pl.BlockSpec((n_batch, D_STATE), const2),
                  pl.BlockSpec((1, D_STATE), const2),
                  pl.BlockSpec((1, D_STATE), const2),
                  pl.BlockSpec((1, D_STATE), const2),
                  pl.BlockSpec((4, 128, 512), const3),
                  pl.BlockSpec((4, 128, 512), const3),
                  pl.BlockSpec((4, 512, 128), const3),
                  pl.BlockSpec((4, 512, 128), const3),
                  pl.BlockSpec((1, D_SSM), const2),
                  pl.BlockSpec((D_SSM, 2 * D_SSM), const2),
                  pl.BlockSpec((1, 2 * D_SSM), const2)],
        out_specs=[pl.BlockSpec((planes, tr, LANES), lambda i: (0, i, 0)),
                   pl.BlockSpec((n_batch, D_STATE), const2),
                   pl.BlockSpec((n_batch, D_STATE), const2)],
        out_shape=[jax.ShapeDtypeStruct((planes, rows, LANES), F32),
                   jax.ShapeDtypeStruct((n_batch, D_STATE), F32),
                   jax.ShapeDtypeStruct((n_batch, D_STATE), F32)],
        scratch_shapes=[pltpu.VMEM((2, D_STATE), F32),
                        pltpu.VMEM((4, 128, 512), BF16), pltpu.VMEM((4, 128, 512), BF16),
                        pltpu.VMEM((4, 512, 128), BF16), pltpu.VMEM((4, 512, 128), BF16),
                        pltpu.VMEM((D_SSM, 2 * D_SSM), BF16),
                        pltpu.VMEM((n_batch, D_STATE), F32), pltpu.VMEM((n_batch, D_STATE), F32),
                        pltpu.VMEM((tr, D_STATE), F32), pltpu.VMEM((tr, D_STATE), F32)],
        compiler_params=_params(1),
        name="ssm",
    )(u_rows, h0_re.reshape(n_batch, D_STATE), h0_im.reshape(n_batch, D_STATE),
      flat(lp['lam_re']), flat(lp['lam_im']), ldt,
      _block_diag_b(lp['ssm_b_re']), _block_diag_b(lp['ssm_b_im']),
      _block_diag_c(lp['ssm_c_re']), _block_diag_c(lp['ssm_c_im']),
      lp['ssm_d'].reshape(1, D_SSM), lp['w_glu'], lp['b_glu'].reshape(1, 2 * D_SSM))
    return y, sr, si


def _softmax_pv(s, v):
    m = jnp.max(s, axis=-1, keepdims=True)
    p = jnp.exp(s - m)
    l = jnp.sum(p, axis=-1, keepdims=True)
    return jnp.dot(p.astype(BF16), v, preferred_element_type=F32) / l


def _attend(q, k, v, out_ref, bias_ref=None, valid=None):
    qb = (q * ATT_SCALE).astype(BF16)
    for h in range(N_HEADS):
        sl = slice(HEAD_DIM * h, HEAD_DIM * (h + 1))
        s = lax.dot_general(qb[:, sl], k[:, sl], _NT, preferred_element_type=F32)
        if bias_ref is not None:
            s = s + bias_ref[h]
        if valid is not None:
            s = jnp.where(valid, s, NEG_INF)
        out_ref[0, :, sl] = _softmax_pv(s, v[:, sl])


def _attn_prompt_kernel(q_ref, k0_ref, k1_ref, k2_ref, v0_ref, v1_ref, v2_ref, qm_ref,
                        mk_ref, mv_ref, bias_ref, ya_ref, ym_ref):
    tq = q_ref.shape[1]
    k = jnp.concatenate([k0_ref[0], k1_ref[0], k2_ref[0]], axis=0).astype(BF16)
    v = jnp.concatenate([v0_ref[0], v1_ref[0], v2_ref[0]], axis=0).astype(BF16)
    kpos = (pl.program_id(1) - 2) * tq + lax.broadcasted_iota(I32, (1, 3 * tq), 1)
    _attend(q_ref[0], k, v, ya_ref, bias_ref, kpos >= 0)
    _attend(qm_ref[0], mk_ref[0].astype(BF16), mv_ref[0].astype(BF16), ym_ref)


def _attn_sample_kernel(q_ref, kn_ref, vn_ref, qm_ref, ck_ref, cv_ref, mk_ref, mv_ref, bias_ref,
                        ya_ref, ym_ref, nk_ref, nv_ref):
    n = kn_ref.shape[1]
    kk = jnp.concatenate([ck_ref[0], kn_ref[0]], axis=0)
    vv = jnp.concatenate([cv_ref[0], vn_ref[0]], axis=0)
    nk_ref[0] = kk[n:]
    nv_ref[0] = vv[n:]
    _attend(q_ref[0], kk.astype(BF16), vv.astype(BF16), ya_ref, bias_ref)
    _attend(qm_ref[0], mk_ref[0].astype(BF16), mv_ref[0].astype(BF16), ym_ref)


def _rel_bias(table, n_q, n_k, band_mask):
    period = n_q + n_k
    m = jnp.arange(period)
    offset = jnp.where(m < n_k, m, m - period)
    idx = jnp.clip(BAND - offset, -REL_CLIP, REL_CLIP) + REL_CLIP
    f = table.astype(F32)[:, idx]
    flat = jnp.tile(f, (1, n_q))[:, :n_q * (period - 1)]
    bias = flat.reshape(N_HEADS, n_q, period - 1)[:, :, :n_k]
    if band_mask:
        qi = jnp.arange(n_q)[:, None]
        kj = jnp.arange(n_k)[None, :]
        off = kj // CHUNK - qi // CHUNK
        ok = (off >= 0) & (off <= N_PREV_CHUNKS)
        bias = jnp.where(ok[None], bias, NEG_INF)
    return bias


def _attn_prompt(zr, mk, mv, table):
    b, s, _ = zr.shape
    tq = ATT_TQ
    bias = _rel_bias(table, tq, 3 * tq, True)
    col = lambda c: (lambda i, j: (i, j, c))
    prev = lambda c, d: (lambda i, j: (i, jnp.maximum(j - d, 0), c))
    blk = lambda: (1, tq, D_ATT)
    return pl.pallas_call(
        _attn_prompt_kernel,
        grid=(b, s // tq),
        in_specs=[pl.BlockSpec(blk(), col(0)),
                  pl.BlockSpec(blk(), prev(1, 2)), pl.BlockSpec(blk(), prev(1, 1)),
                  pl.BlockSpec(blk(), col(1)),
                  pl.BlockSpec(blk(), prev(2, 2)), pl.BlockSpec(blk(), prev(2, 1)),
                  pl.BlockSpec(blk(), col(2)),
                  pl.BlockSpec(blk(), col(3)),
                  pl.BlockSpec((1, N_MEM, D_MEM), lambda i, j: (i, 0, 0)),
                  pl.BlockSpec((1, N_MEM, D_MEM), lambda i, j: (i, 0, 0)),
                  pl.BlockSpec((N_HEADS, tq, 3 * tq), lambda i, j: (0, 0, 0))],
        out_specs=[pl.BlockSpec(blk(), col(0)), pl.BlockSpec(blk(), col(0))],
        out_shape=[jax.ShapeDtypeStruct((b, s, D_ATT), F32),
                   jax.ShapeDtypeStruct((b, s, D_MEM), F32)],
        compiler_params=_params(2),
        name="attn_prompt",
    )(zr, zr, zr, zr, zr, zr, zr, zr, mk, mv, bias)


def _attn_sample(zr, cache_k, cache_v, mk, mv, table):
    b, n, _ = zr.shape
    w = cache_k.shape[1]
    bias = _rel_bias(table, n, w + n, False)
    col = lambda c: (lambda i: (i, 0, c))
    blk = (1, n, D_ATT)
    cblk = (1, w, D_ATT)
    mblk = (1, N_MEM, D_MEM)
    row = lambda i: (i, 0, 0)
    return pl.pallas_call(
        _attn_sample_kernel,
        grid=(b,),
        in_specs=[pl.BlockSpec(blk, col(0)), pl.BlockSpec(blk, col(1)), pl.BlockSpec(blk, col(2)),
                  pl.BlockSpec(blk, col(3)),
                  pl.BlockSpec(cblk, row), pl.BlockSpec(cblk, row),
                  pl.BlockSpec(mblk, row), pl.BlockSpec(mblk, row),
                  pl.BlockSpec((N_HEADS, n, w + n), lambda i: (0, 0, 0))],
        out_specs=[pl.BlockSpec(blk, row), pl.BlockSpec(blk, row),
                   pl.BlockSpec(cblk, row), pl.BlockSpec(cblk, row)],
        out_shape=[jax.ShapeDtypeStruct((b, n, D_ATT), F32),
                   jax.ShapeDtypeStruct((b, n, D_MEM), F32),
                   jax.ShapeDtypeStruct((b, w, D_ATT), F32),
                   jax.ShapeDtypeStruct((b, w, D_ATT), F32)],
        compiler_params=_params(1),
        name="attn_sample",
    )(zr, zr, zr, zr, cache_k, cache_v, mk, mv, bias)


def _rms(x, g):
    return x * lax.rsqrt(jnp.mean(jnp.square(x), axis=-1, keepdims=True) + LN_EPS) * g


def _layer_norm(x, g, b):
    mu = jnp.mean(x, axis=-1, keepdims=True)
    xc = x - mu
    var = jnp.mean(jnp.square(xc), axis=-1, keepdims=True)
    return xc * lax.rsqrt(var + LN_EPS) * g + b


def _split_bf16(a):
    hi = a.astype(BF16)
    lo = (a - hi.astype(F32)).astype(BF16)
    return hi, lo


def _merge_kernel(*refs, nb, n_carried):
    (x_ref, ys_ref, ya_ref, ym_ref, gs_ref, ga_ref, gm_ref, wo_ref, l1g_ref, l1b_ref,
     wrt_ref, brt_ref) = refs[:12]
    h_ref, pos_ref, gate_ref, cnt_ref, wo_sc = refs[12 + n_carried:]
    st = x_ref.shape[1]
    tm = nb * st
    n_batch = ys_ref.shape[1] // st

    @pl.when((pl.program_id(0) == 0) & (pl.program_id(1) == 0))
    def _():
        wo_sc[...] = wo_ref[...].astype(BF16)

    x = x_ref[...].reshape(tm, D_MODEL)
    first = pl.program_id(1) * nb
    ys = jnp.concatenate(
        [jnp.concatenate([ys_ref[c, pl.ds(first + i, st, stride=n_batch), :] for c in range(D_SSM // LANES)],
                         axis=1) for i in range(nb)], axis=0)
    ya = ya_ref[...].reshape(tm, D_ATT)
    ym = ym_ref[...].reshape(tm, D_MEM)
    a = _rms(ys, gs_ref[...]).astype(BF16)
    b = _rms(ya, ga_ref[...]).astype(BF16)
    c = _rms(ym, gm_ref[...]).astype(BF16)
    mix = (jnp.dot(a, wo_sc[0:D_SSM, :], preferred_element_type=F32)
           + jnp.dot(b, wo_sc[D_SSM:D_SSM + D_ATT, :], preferred_element_type=F32)
           + jnp.dot(c, wo_sc[D_SSM + D_ATT:, :], preferred_element_type=F32))
    h = _layer_norm(DEEPNORM_ALPHA * x + mix, l1g_ref[...], l1b_ref[...])
    h_ref[...] = h

    h_hi, h_lo = _split_bf16(h)
    w_hi, w_lo = _split_bf16(wrt_ref[...])
    logits = (lax.dot_general(w_hi, h_hi, _NT, preferred_element_type=F32)
              + lax.dot_general(w_hi, h_lo, _NT, preferred_element_type=F32)
              + lax.dot_general(w_lo, h_hi, _NT, preferred_element_type=F32)
              + brt_ref[...])
    erow = lax.broadcasted_iota(I32, (N_EXPERTS, tm), 0).astype(F32)
    tops, picks = [], []
    l = logits
    for k in range(TOP_K):
        m = jnp.max(l, axis=0, keepdims=True)
        e = jnp.min(jnp.where(l == m, erow, float(N_EXPERTS)), axis=0, keepdims=True)
        pick = erow == e
        tops.append(m)
        picks.append(jnp.where(pick, 1.0, 0.0))
        l = jnp.where(pick, -jnp.inf, l)
    ex = [jnp.exp(t - tops[0]) for t in tops]
    den = ex[0] + ex[1] + ex[2] + ex[3]
    for k in range(TOP_K):
        gate_ref[k:k + 1, :] = ex[k] / den

    chosen = picks[0] + picks[1] + picks[2] + picks[3]
    chosen_b = chosen.astype(BF16)
    earlier_tok = (lax.broadcasted_iota(I32, (tm, tm), 0) < lax.broadcasted_iota(I32, (tm, tm), 1))
    within = jnp.dot(chosen_b, jnp.where(earlier_tok, 1.0, 0.0).astype(BF16),
                     preferred_element_type=F32)
    lower_exp = (lax.broadcasted_iota(I32, (N_EXPERTS, N_EXPERTS), 1)
                 < lax.broadcasted_iota(I32, (N_EXPERTS, N_EXPERTS), 0))
    below = jnp.dot(jnp.where(lower_exp, 1.0, 0.0).astype(BF16), chosen_b,
                    preferred_element_type=F32)
    slot = within + jnp.sum(below, axis=1, keepdims=True)
    for k in range(TOP_K):
        pos_ref[k:k + 1, :] = jnp.sum(picks[k] * slot, axis=0, keepdims=True).astype(I32)
    cnt_ref[0] = jnp.sum(chosen, axis=1, keepdims=True)


def _merge(x, ys_tm, ya, ym, lp, nb, st, t_all, tile0, carried=None):
    b, s, _ = x.shape
    tm = nb * st
    assert tm == TOKEN_TM
    n_s = s // st
    tile = lambda j, i: (tile0 + i * n_s + j)
    c2 = lambda j, i: (0, 0)
    row3 = lambda j, i: (i, j, 0)
    vec = lambda a: a.reshape(1, -1)
    carried = () if carried is None else tuple(carried)
    return pl.pallas_call(
        functools.partial(_merge_kernel, nb=nb, n_carried=len(carried)),
        grid=(n_s, b // nb),
        in_specs=[pl.BlockSpec((nb, st, D_MODEL), row3),
                  pl.BlockSpec((D_SSM // LANES, st * b, LANES), lambda j, i: (0, j, 0)),
                  pl.BlockSpec((nb, st, D_ATT), row3),
                  pl.BlockSpec((nb, st, D_MEM), row3),
                  pl.BlockSpec((1, D_SSM), c2), pl.BlockSpec((1, D_ATT), c2),
                  pl.BlockSpec((1, D_MEM), c2),
                  pl.BlockSpec((D_MODEL, D_MODEL), c2),
                  pl.BlockSpec((1, D_MODEL), c2), pl.BlockSpec((1, D_MODEL), c2),
                  pl.BlockSpec((N_EXPERTS, D_MODEL), c2), pl.BlockSpec((N_EXPERTS, 1), c2)]
                 + [pl.BlockSpec(memory_space=pl.ANY)] * len(carried),
        out_specs=[pl.BlockSpec((tm, D_MODEL), lambda j, i: (tile(j, i), 0)),
                   pl.BlockSpec((TOP_K, tm), lambda j, i: (0, tile(j, i))),
                   pl.BlockSpec((TOP_K, tm), lambda j, i: (0, tile(j, i))),
                   pl.BlockSpec((1, N_EXPERTS, 1), lambda j, i: (tile(j, i), 0, 0))],
        out_shape=[jax.ShapeDtypeStruct((t_all, D_MODEL), F32),
                   jax.ShapeDtypeStruct((TOP_K, t_all), I32),
                   jax.ShapeDtypeStruct((TOP_K, t_all), F32),
                   jax.ShapeDtypeStruct((t_all // tm, N_EXPERTS, 1), F32)],
        scratch_shapes=[pltpu.VMEM((D_MODEL, D_MODEL), BF16)],
        input_output_aliases={12 + k: k for k in range(len(carried))},
        compiler_params=_params(2),
        name="merge_router",
    )(x, ys_tm, ya, ym, vec(lp['g_ssm']), vec(lp['g_att']), vec(lp['g_mem']), lp['w_out'],
      vec(lp['ln1_g']), vec(lp['ln1_b']), lp['w_router'].T, lp['b_router'].reshape(N_EXPERTS, 1),
      *carried)


def _rows(start, size):
    return pl.ds(pl.multiple_of(start * ROW_SUBLANES, ROW_SUBLANES), size * ROW_SUBLANES)


def _store_rows(ref, value):
    n = value.shape[0]
    for j in range(ROW_SUBLANES):
        ref[pl.ds(j, n, stride=ROW_SUBLANES), :] = value[:, LANES * j:LANES * (j + 1)]


def _load_rows(ref, dtype=F32):
    n = ref.shape[0] // ROW_SUBLANES
    return jnp.concatenate([ref[pl.ds(j, n, stride=ROW_SUBLANES), :].astype(dtype)
                            for j in range(ROW_SUBLANES)], axis=1)


def _for_each_run_piece(n, max_rows, fn):
    for bit in reversed(range(max_rows.bit_length())):
        size = 1 << bit
        start = (n >> (bit + 1)) << (bit + 1)

        @pl.when((n & size) != 0)
        def _(start=start, size=size):
            fn(start, size)


def _dispatch_kernel(n_ref, off_ref, dst_ref, padlo_ref, padn_ref, used_ref,
                     pos_ref, h_ref, xs_hbm, sorted_sc, zero_sc, sem, zsem):
    i = pl.program_id(0)
    tm = h_ref.shape[0]
    n_slots = TOP_K * tm
    n_blocks = xs_hbm.shape[0] // (EXPERT_TM * ROW_SUBLANES)

    @pl.when(i == 0)
    def _():
        zero_sc[...] = jnp.zeros_like(zero_sc)

        def pad_copy(e, start, size):
            return pltpu.make_async_copy(zero_sc.at[_rows(0, size)],
                                         xs_hbm.at[_rows(padlo_ref[e] + start, size)], zsem)

        def tail_copy(blk):
            return pltpu.make_async_copy(zero_sc, xs_hbm.at[_rows(blk * EXPERT_TM, EXPERT_TM)], zsem)

        for e in range(N_EXPERTS):
            _for_each_run_piece(padn_ref[e], EXPERT_TM - 1,
                                lambda start, size, e=e: pad_copy(e, start, size).start())

        def tail_start(blk, carry):
            tail_copy(blk).start()
            return carry

        lax.fori_loop(used_ref[0], n_blocks, tail_start, 0)
        for e in range(N_EXPERTS):
            _for_each_run_piece(padn_ref[e], EXPERT_TM - 1,
                                lambda start, size, e=e: pad_copy(e, start, size).wait())

        def tail_wait(blk, carry):
            tail_copy(blk).wait()
            return carry

        lax.fori_loop(used_ref[0], n_blocks, tail_wait, 0)

    n_tiles = pl.num_programs(0)
    slot = lax.rem(i, 2)
    buf = sorted_sc.at[slot]

    def wait_tile(sl):
        pltpu.make_async_copy(sorted_sc.at[sl], xs_hbm.at[_rows(0, n_slots)], sem.at[sl]).wait()

    @pl.when(i >= 2)
    def _():
        wait_tile(slot)

    pos = pos_ref[...]
    srow = lax.broadcasted_iota(I32, (n_slots, tm), 0)
    perm = jnp.where(srow == pos[0:1], 1.0,
                     jnp.where(srow == pos[1:2], 1.0,
                               jnp.where(srow == pos[2:3], 1.0,
                                         jnp.where(srow == pos[3:4], 1.0, 0.0))))
    _store_rows(buf, jnp.dot(perm.astype(BF16), h_ref[...].astype(BF16), preferred_element_type=F32))

    base = i * N_EXPERTS
    for e in range(N_EXPERTS):
        off = off_ref[base + e]
        dst = dst_ref[base + e]

        def run_start(start, size, off=off, dst=dst):
            pltpu.make_async_copy(buf.at[_rows(off + start, size)],
                                  xs_hbm.at[_rows(dst + start, size)], sem.at[slot]).start()

        _for_each_run_piece(n_ref[base + e], tm, run_start)

    @pl.when(i == n_tiles - 1)
    def _():
        @pl.when(i >= 1)
        def _():
            wait_tile(1 - slot)

        wait_tile(slot)


def _dispatch(run_n, run_off, run_dst, pad_lo, pad_n, n_used, pos, h, cap):
    t = h.shape[0]
    tm = TOKEN_TM
    grid_spec = pltpu.PrefetchScalarGridSpec(
        num_scalar_prefetch=6,
        grid=(t // tm,),
        in_specs=[pl.BlockSpec((TOP_K, tm), lambda i, *_: (0, i)),
                  pl.BlockSpec((tm, D_MODEL), lambda i, *_: (i, 0))],
        out_specs=pl.BlockSpec(memory_space=pl.ANY),
        scratch_shapes=[pltpu.VMEM((2, TOP_K * tm * ROW_SUBLANES, LANES), F32),
                        pltpu.VMEM((EXPERT_TM * ROW_SUBLANES, LANES), F32),
                        pltpu.SemaphoreType.DMA((2,)), pltpu.SemaphoreType.DMA],
    )
    return pl.pallas_call(
        _dispatch_kernel,
        grid_spec=grid_spec,
        out_shape=jax.ShapeDtypeStruct((cap * ROW_SUBLANES, LANES), F32),
        compiler_params=_params(1),
        name="moe_dispatch",
    )(run_n, run_off, run_dst, pad_lo, pad_n, n_used, pos, h)


def _expert_kernel(be_ref, first_ref, ord_ref, seq_ref, used_ref,
                   x_ref, bgu_ref, bd_ref, wgu_hbm, wd_hbm, o_ref,
                   wgu_st, wd_st, wgu_sc, wd_sc, sem):
    i = pl.program_id(0)

    def weight_copies(e):
        return (pltpu.make_async_copy(wgu_hbm.at[e], wgu_st, sem.at[0]),
                pltpu.make_async_copy(wd_hbm.at[e], wd_st, sem.at[1]))

    @pl.when(i == 0)
    def _():
        for c in weight_copies(seq_ref[0]):
            c.start()

    @pl.when(i < used_ref[0])
    def _():
        @pl.when(first_ref[i] == 1)
        def _():
            k = ord_ref[i]
            for c in weight_copies(seq_ref[k]):
                c.wait()
            wgu_sc[...] = wgu_st[...].astype(BF16)
            wd_sc[...] = wd_st[...].astype(BF16)

            @pl.when(k + 1 < used_ref[1])
            def _():
                for c in weight_copies(seq_ref[k + 1]):
                    c.start()

        gu = jnp.dot(_load_rows(x_ref, BF16), wgu_sc[...], preferred_element_type=F32) + bgu_ref[0]
        gate = jnp.minimum(gu[:, :D_FF], SWIGLU_LIMIT)
        lin = jnp.clip(gu[:, D_FF:], -SWIGLU_LIMIT, SWIGLU_LIMIT)
        act = gate * jax.nn.sigmoid(SWIGLU_ALPHA * gate) * (lin + 1.0)
        _store_rows(o_ref, jnp.dot(act.astype(BF16), wd_sc[...], preferred_element_type=F32) + bd_ref[0])

    @pl.when(i >= used_ref[0])
    def _():
        o_ref[...] = jnp.zeros_like(o_ref)


def _experts(block_expert, block_first, block_ord, expert_seq, n_used, xs, lp):
    tm = EXPERT_TM * ROW_SUBLANES
    grid_spec = pltpu.PrefetchScalarGridSpec(
        num_scalar_prefetch=5,
        grid=(xs.shape[0] // tm,),
        in_specs=[pl.BlockSpec((tm, LANES), lambda i, be, *_: (i, 0)),
                  pl.BlockSpec((1, 1, 2 * D_FF), lambda i, be, *_: (be[i], 0, 0)),
                  pl.BlockSpec((1, 1, D_MODEL), lambda i, be, *_: (be[i], 0, 0)),
                  pl.BlockSpec(memory_space=pl.ANY),
                  pl.BlockSpec(memory_space=pl.ANY)],
        out_specs=pl.BlockSpec((tm, LANES), lambda i, be, *_: (i, 0)),
        scratch_shapes=[pltpu.VMEM((D_MODEL, 2 * D_FF), F32), pltpu.VMEM((D_FF, D_MODEL), F32),
                        pltpu.VMEM((D_MODEL, 2 * D_FF), BF16), pltpu.VMEM((D_FF, D_MODEL), BF16),
                        pltpu.SemaphoreType.DMA((2,))],
    )
    return pl.pallas_call(
        _expert_kernel,
        grid_spec=grid_spec,
        out_shape=jax.ShapeDtypeStruct(xs.shape, F32),
        compiler_params=_params(1),
        name="moe_experts",
    )(block_expert, block_first, block_ord, expert_seq, n_used, xs,
      lp['b_gu'].reshape(N_EXPERTS, 1, 2 * D_FF), lp['b_down'].reshape(N_EXPERTS, 1, D_MODEL),
      lp['w_gu'], lp['w_down'])


def _combine_kernel(n_ref, off_ref, dst_ref, pos_ref, gate_ref, h_ref, ys_hbm, g_ref, b_ref,
                    y1_ref, y2_ref, sorted_sc, sem, *, n_first):
    i = pl.program_id(0)
    n_tiles = pl.num_programs(0) - 1
    tm = h_ref.shape[0]
    n_slots = TOP_K * tm

    @pl.when(i < n_tiles)
    def _():
        slot = lax.rem(i, 2)
        buf = sorted_sc.at[slot]
        base = i * N_EXPERTS
        for e in range(N_EXPERTS):
            off = off_ref[base + e]
            dst = dst_ref[base + e]

            def run_start(start, size, off=off, dst=dst):
                pltpu.make_async_copy(ys_hbm.at[_rows(dst + start, size)],
                                      buf.at[_rows(off + start, size)], sem.at[slot]).start()

            _for_each_run_piece(n_ref[base + e], tm, run_start)

    @pl.when(i >= 1)
    def _():
        slot = lax.rem(i - 1, 2)
        buf = sorted_sc.at[slot]
        pltpu.make_async_copy(ys_hbm.at[_rows(0, n_slots)], buf, sem.at[slot]).wait()

        pos = pos_ref[...]
        gates = gate_ref[...]
        scol = lax.broadcasted_iota(I32, (tm, n_slots), 1)
        w = jnp.where(scol == pos[:, 0:1], gates[:, 0:1],
                      jnp.where(scol == pos[:, 1:2], gates[:, 1:2],
                                jnp.where(scol == pos[:, 2:3], gates[:, 2:3],
                                          jnp.where(scol == pos[:, 3:4], gates[:, 3:4], 0.0))))
        f = jnp.dot(w.astype(BF16), _load_rows(buf, BF16), preferred_element_type=F32)
        y = _layer_norm(DEEPNORM_ALPHA * h_ref[...] + f, g_ref[...], b_ref[...])

        @pl.when(i - 1 < n_first)
        def _():
            y1_ref[...] = y

        @pl.when(i - 1 >= n_first)
        def _():
            y2_ref[...] = y


def _combine(run_n, run_off, run_dst, pos_t, gates_t, h, ys, lp, t_first):
    t = h.shape[0]
    tm = TOKEN_TM
    n_first = t_first // tm
    n_rest = (t - t_first) // tm
    c2 = lambda i, *_: (0, 0)
    done = lambda i, *_: (jnp.maximum(i - 1, 0), 0)
    done1 = lambda i, *_: (jnp.clip(i - 1, 0, n_first - 1), 0)
    done2 = lambda i, *_: (jnp.clip(i - 1 - n_first, 0, n_rest - 1), 0)
    grid_spec = pltpu.PrefetchScalarGridSpec(
        num_scalar_prefetch=3,
        grid=(t // tm + 1,),
        in_specs=[pl.BlockSpec((tm, TOP_K), done),
                  pl.BlockSpec((tm, TOP_K), done),
                  pl.BlockSpec((tm, D_MODEL), done),
                  pl.BlockSpec(memory_space=pl.ANY),
                  pl.BlockSpec((1, D_MODEL), c2), pl.BlockSpec((1, D_MODEL), c2)],
        out_specs=[pl.BlockSpec((tm, D_MODEL), done1), pl.BlockSpec((tm, D_MODEL), done2)],
        scratch_shapes=[pltpu.VMEM((2, TOP_K * tm * ROW_SUBLANES, LANES), F32),
                        pltpu.SemaphoreType.DMA((2,))],
    )
    return pl.pallas_call(
        functools.partial(_combine_kernel, n_first=n_first),
        grid_spec=grid_spec,
        out_shape=[jax.ShapeDtypeStruct((t_first, D_MODEL), F32),
                   jax.ShapeDtypeStruct((t - t_first, D_MODEL), F32)],
        compiler_params=_params(1),
        name="moe_combine",
    )(run_n, run_off, run_dst, pos_t, gates_t, h, ys,
      lp['ln2_g'].reshape(1, D_MODEL), lp['ln2_b'].reshape(1, D_MODEL))


def _moe_and_norm(h, pos, gates, tile_counts, lp, t_first):
    t = h.shape[0]
    te = EXPERT_TM
    n_tiles = t // TOKEN_TM
    n_blocks = (t * TOP_K) // te + N_EXPERTS
    cap = n_blocks * te
    cnt = tile_counts.reshape(n_tiles, N_EXPERTS).astype(I32)
    counts = jnp.sum(cnt, axis=0)
    padded = (counts + te - 1) // te * te
    pad_ends = jnp.cumsum(padded)
    pad_starts = pad_ends - padded
    run_dst = pad_starts[None, :] + jnp.cumsum(cnt, axis=0) - cnt
    run_off = jnp.cumsum(cnt, axis=1) - cnt
    blk_start = jnp.arange(n_blocks, dtype=I32) * te
    n_used = (pad_ends[-1] // te).astype(I32)
    be = jnp.minimum(jnp.sum(blk_start[:, None] >= pad_ends[None, :], axis=1), N_EXPERTS - 1).astype(I32)
    be = jnp.where(blk_start < pad_ends[-1], be, be[jnp.maximum(n_used - 1, 0)])
    in_use = counts > 0
    expert_seq = jnp.argsort(~in_use, stable=True).astype(I32)
    block_ord = (jnp.cumsum(in_use.astype(I32)) - 1)[be]
    block_first = (blk_start == pad_starts[be]) & (blk_start < pad_ends[-1])
    used = jnp.stack([n_used, jnp.sum(in_use.astype(I32))]).astype(I32)
    flat = lambda a: a.reshape(-1).astype(I32)
    xs = _dispatch(flat(cnt), flat(run_off), flat(run_dst), flat(pad_starts + counts),
                   flat(padded - counts), used, pos, h, cap)
    ys = _experts(be, flat(block_first), flat(block_ord), expert_seq, used, xs, lp)
    return _combine(flat(cnt), flat(run_off), flat(run_dst), pos.T, gates.T, h, ys, lp, t_first)


def kernel(x_prompt, x_sample, cache_attn_k, cache_attn_v, cache_mem_k, cache_mem_v, state_ssm_re, state_ssm_im, mem_prompt, w_in, lam_re, lam_im, log_dt, ssm_b_re, ssm_b_im, ssm_c_re, ssm_c_im, ssm_d, w_glu, b_glu, rel_bias, w_mem_kv, g_ssm, g_att, g_mem, w_out, ln1_g, ln1_b, w_router, b_router, w_gu, b_gu, w_down, b_down, ln2_g, ln2_b):
    assert w_in.shape[0] == 1, "single-layer step"
    lp = dict(w_in=w_in[0], lam_re=lam_re[0], lam_im=lam_im[0], log_dt=log_dt[0],
              ssm_b_re=ssm_b_re[0], ssm_b_im=ssm_b_im[0], ssm_c_re=ssm_c_re[0], ssm_c_im=ssm_c_im[0],
              ssm_d=ssm_d[0], w_glu=w_glu[0], b_glu=b_glu[0], rel_bias=rel_bias[0],
              w_mem_kv=w_mem_kv[0], g_ssm=g_ssm[0], g_att=g_att[0], g_mem=g_mem[0], w_out=w_out[0],
              ln1_g=ln1_g[0], ln1_b=ln1_b[0], w_router=w_router[0], b_router=b_router[0],
              w_gu=w_gu[0], b_gu=b_gu[0], w_down=w_down[0], b_down=b_down[0],
              ln2_g=ln2_g[0], ln2_b=ln2_b[0])

    bp, sp, _ = x_prompt.shape
    bs, ss, _ = x_sample.shape
    t_p = bp * sp
    t_all = t_p + bs * ss
    heads = lambda a: a.reshape(a.shape[0], a.shape[1], N_HEADS, HEAD_DIM)
    state = lambda a: a.reshape(a.shape[0], N_GROUPS, SSM_STATE)

    u_p, zr = _in_proj(x_prompt, lp['w_in'], min(128, sp))
    mk, mv = _mem_kv(mem_prompt, lp['w_mem_kv'])
    ya, ym = _attn_prompt(zr, mk, mv, lp['rel_bias'])
    zeros = jnp.zeros((bp, D_STATE), F32)
    ys_p, sr_p, si_p = _ssm(u_p, zeros, zeros, lp, bp)
    merged = _merge(x_prompt, ys_p, ya, ym, lp, 1, TOKEN_TM, t_all, 0)
    w = min(BAND, sp)
    k_p = heads(zr[:, sp - w:, D_ATT:2 * D_ATT])
    v_p = heads(zr[:, sp - w:, 2 * D_ATT:3 * D_ATT])

    wc = cache_attn_k.shape[2]
    u_s, zr_s = _in_proj(x_sample, lp['w_in'], ss)
    ya_s, ym_s, nk, nv = _attn_sample(
        zr_s, cache_attn_k[0].reshape(bs, wc, D_ATT), cache_attn_v[0].reshape(bs, wc, D_ATT),
        cache_mem_k[0].reshape(bs, N_MEM, D_MEM), cache_mem_v[0].reshape(bs, N_MEM, D_MEM),
        lp['rel_bias'])
    ys_s, sr_s, si_s = _ssm(u_s, state_ssm_re[0], state_ssm_im[0], lp, bs)
    merged = _merge(x_sample, ys_s, ya_s, ym_s, lp, TOKEN_TM // ss, ss, t_all, t_p // TOKEN_TM,
                    carried=merged)

    y_p, y_s = _moe_and_norm(*merged, lp, t_p)

    return (y_p.reshape(bp, sp, D_MODEL), y_s.reshape(bs, ss, D_MODEL),
            k_p[None], v_p[None], heads(mk)[None], heads(mv)[None], state(sr_p)[None], state(si_p)[None],
            heads(nk)[None], heads(nv)[None], state(sr_s)[None], state(si_s)[None])
```

```python
import functools

import jax
import jax.numpy as jnp
from jax import lax
from jax.experimental import pallas as pl
from jax.experimental.pallas import tpu as pltpu

F32 = jnp.float32
BF16 = jnp.bfloat16
I32 = jnp.int32

D_MODEL = 1024
D_SSM = 512
D_ATT = 256
D_MEM = 256
D_IN = D_SSM + 3 * D_ATT + D_MEM
D_REST = D_IN - D_SSM
HEAD_DIM = 64
N_HEADS = 4
N_GROUPS = 32
SSM_GROUP = 16
SSM_STATE = 64
D_STATE = N_GROUPS * SSM_STATE
CHUNK = 64
N_PREV_CHUNKS = 8
BAND = N_PREV_CHUNKS * CHUNK
REL_CLIP = 128
N_MEM = 256
N_EXPERTS = 32
TOP_K = 4
D_FF = D_MODEL
SWIGLU_LIMIT = 7.0
SWIGLU_ALPHA = 1.702
LN_EPS = 1e-5
NEG_INF = -1e30
ATT_SCALE = HEAD_DIM ** -0.5
DEEPNORM_ALPHA = 2.0 ** 0.25

V7X_VMEM_LIMIT = 56 * 1024 * 1024
ATT_TQ = 4 * CHUNK
SCAN_LANES = 512
SCAN_ROWS = 1024
LANES = 128
ROW_SUBLANES = D_MODEL // LANES
EXPERT_TM = 512
TOKEN_TM = 512

_NT = (((1,), (1,)), ((), ()))


def _params(n_axes, vmem=V7X_VMEM_LIMIT):
    return pltpu.CompilerParams(dimension_semantics=("arbitrary",) * n_axes,
                                vmem_limit_bytes=vmem)


def _in_proj_kernel(x_ref, w_ref, u_ref, z_ref, wb_ref):
    @pl.when(pl.program_id(0) == 0)
    def _():
        wb_ref[...] = w_ref[...].astype(BF16)

    nb, ts, _ = x_ref.shape
    x = x_ref[...].reshape(nb * ts, D_MODEL).astype(BF16)
    z = jnp.dot(x, wb_ref[...], preferred_element_type=F32)
    for b in range(nb):
        for c in range(D_SSM // LANES):
            u_ref[c, pl.ds(b, ts, stride=nb), :] = z[b * ts:(b + 1) * ts, LANES * c:LANES * (c + 1)]
    z_ref[...] = z[:, D_SSM:].reshape(nb, ts, D_REST)


def _in_proj(x, w_in, ts):
    b, s, _ = x.shape
    return pl.pallas_call(
        _in_proj_kernel,
        grid=(s // ts,),
        in_specs=[pl.BlockSpec((b, ts, D_MODEL), lambda j: (0, j, 0)),
                  pl.BlockSpec((D_MODEL, D_IN), lambda j: (0, 0))],
        out_specs=[pl.BlockSpec((D_SSM // LANES, ts * b, LANES), lambda j: (0, j, 0)),
                   pl.BlockSpec((b, ts, D_REST), lambda j: (0, j, 0))],
        out_shape=[jax.ShapeDtypeStruct((D_SSM // LANES, s * b, LANES), F32),
                   jax.ShapeDtypeStruct((b, s, D_REST), F32)],
        scratch_shapes=[pltpu.VMEM((D_MODEL, D_IN), BF16)],
        compiler_params=_params(1),
        name="in_proj",
    )(x, w_in)


def _mem_kv_kernel(m_ref, w_ref, mk_ref, mv_ref):
    kv = jnp.dot(m_ref[0].astype(BF16), w_ref[...].astype(BF16), preferred_element_type=F32)
    mk_ref[0] = kv[:, :D_MEM]
    mv_ref[0] = kv[:, D_MEM:]


def _mem_kv(mem, w_mem_kv):
    b = mem.shape[0]
    return pl.pallas_call(
        _mem_kv_kernel,
        grid=(b,),
        in_specs=[pl.BlockSpec((1, N_MEM, D_MODEL), lambda i: (i, 0, 0)),
                  pl.BlockSpec((D_MODEL, 2 * D_MEM), lambda i: (0, 0))],
        out_specs=[pl.BlockSpec((1, N_MEM, D_MEM), lambda i: (i, 0, 0)),
                   pl.BlockSpec((1, N_MEM, D_MEM), lambda i: (i, 0, 0))],
        out_shape=[jax.ShapeDtypeStruct((b, N_MEM, D_MEM), F32)] * 2,
        compiler_params=_params(1),
        name="mem_kv",
    )(mem, w_mem_kv)


def _ssm_kernel(u_ref, h0r_ref, h0i_ref, lr_ref, li_ref, ldt_ref, bre_ref, bim_ref,
                cre_ref, cim_ref, d_ref, wg_ref, bg_ref,
                y_ref, sr_ref, si_ref,
                a_sc, bbr_sc, bbi_sc, cr_sc, ci_sc, wg_sc, str_sc, sti_sc, xr_sc, xi_sc,
                *, n_batch):
    n_rows = u_ref.shape[1]
    n_steps = n_rows // n_batch

    @pl.when(pl.program_id(0) == 0)
    def _():
        lr = lr_ref[...]
        li = li_ref[...]
        dt = jnp.exp(ldt_ref[...])
        mag = jnp.exp(lr * dt)
        ar = mag * jnp.cos(li * dt)
        ai = mag * jnp.sin(li * dt)
        den = lr * lr + li * li
        fr = ((ar - 1.0) * lr + ai * li) / den
        fi = (ai * lr - (ar - 1.0) * li) / den
        a_sc[0:1, :] = ar
        a_sc[1:2, :] = ai
        for j in range(4):
            frj = fr[:, 512 * j:512 * (j + 1)]
            fij = fi[:, 512 * j:512 * (j + 1)]
            bbr_sc[j] = (frj * bre_ref[j] - fij * bim_ref[j]).astype(BF16)
            bbi_sc[j] = (frj * bim_ref[j] + fij * bre_ref[j]).astype(BF16)
            cr_sc[j] = cre_ref[j].astype(BF16)
            ci_sc[j] = cim_ref[j].astype(BF16)
        wg_sc[...] = wg_ref[...].astype(BF16)
        str_sc[...] = h0r_ref[...]
        sti_sc[...] = h0i_ref[...]

    for j in range(4):
        uc = u_ref[j].astype(BF16)
        xr_sc[:, 512 * j:512 * (j + 1)] = jnp.dot(uc, bbr_sc[j], preferred_element_type=F32)
        xi_sc[:, 512 * j:512 * (j + 1)] = jnp.dot(uc, bbi_sc[j], preferred_element_type=F32)

    for c in range(D_STATE // SCAN_LANES):
        lo = c * SCAN_LANES
        ar = jnp.broadcast_to(a_sc[0:1, lo:lo + SCAN_LANES], (n_batch, SCAN_LANES))
        ai = jnp.broadcast_to(a_sc[1:2, lo:lo + SCAN_LANES], (n_batch, SCAN_LANES))

        def step(t, carry, lo=lo, ar=ar, ai=ai):
            sr, si = carry
            r0 = pl.multiple_of(t * n_batch, n_batch)
            nr = ar * sr - ai * si + xr_sc[pl.ds(r0, n_batch), lo:lo + SCAN_LANES]
            ni = ar * si + ai * sr + xi_sc[pl.ds(r0, n_batch), lo:lo + SCAN_LANES]
            xr_sc[pl.ds(r0, n_batch), lo:lo + SCAN_LANES] = nr
            xi_sc[pl.ds(r0, n_batch), lo:lo + SCAN_LANES] = ni
            return nr, ni

        sr, si = lax.fori_loop(0, n_steps, step,
                               (str_sc[:, lo:lo + SCAN_LANES], sti_sc[:, lo:lo + SCAN_LANES]),
                               unroll=4)
        str_sc[:, lo:lo + SCAN_LANES] = sr
        sti_sc[:, lo:lo + SCAN_LANES] = si

    pieces = []
    for j in range(4):
        xr = xr_sc[:, 512 * j:512 * (j + 1)].astype(BF16)
        xi = xi_sc[:, 512 * j:512 * (j + 1)].astype(BF16)
        pieces.append(jnp.dot(xr, cr_sc[j], preferred_element_type=F32)
                      - jnp.dot(xi, ci_sc[j], preferred_element_type=F32))
    u = jnp.concatenate([u_ref[j] for j in range(4)], axis=1)
    y = jnp.concatenate(pieces, axis=1) + d_ref[...] * u
    y = jax.nn.gelu(y)
    z = jnp.dot(y.astype(BF16), wg_sc[...], preferred_element_type=F32) + bg_ref[...]
    out = z[:, :D_SSM] * jax.nn.sigmoid(z[:, D_SSM:])
    for j in range(D_SSM // LANES):
        y_ref[j] = out[:, LANES * j:LANES * (j + 1)]
    sr_ref[...] = str_sc[...]
    si_ref[...] = sti_sc[...]


def _block_diag_b(b):
    bt = b.transpose(0, 2, 1).reshape(4, 8, SSM_GROUP, SSM_STATE)
    same = jnp.eye(8, dtype=bool)[None, :, None, :, None]
    t = jnp.where(same, bt[:, :, :, None, :], 0.0)
    return t.reshape(4, 8 * SSM_GROUP, 8 * SSM_STATE)


def _block_diag_c(c):
    ct = c.transpose(0, 2, 1).reshape(4, 8, SSM_STATE, SSM_GROUP)
    same = jnp.eye(8, dtype=bool)[None, :, None, :, None]
    t = jnp.where(same, ct[:, :, :, None, :], 0.0)
    return t.reshape(4, 8 * SSM_STATE, 8 * SSM_GROUP)


def _ssm(u_rows, h0_re, h0_im, lp, n_batch):
    rows = u_rows.shape[1]
    planes = D_SSM // LANES
    tr = min(SCAN_ROWS, rows)
    flat = lambda a: a.reshape(1, D_STATE)
    ldt = jnp.repeat(lp['log_dt'], SSM_STATE).reshape(1, D_STATE)
    const2 = lambda i: (0, 0)
    const3 = lambda i: (0, 0, 0)
    y, sr, si = pl.pallas_call(
        functools.partial(_ssm_kernel, n_batch=n_batch),
        grid=(rows // tr,),
        in_specs=[pl.BlockSpec((planes, tr, LANES), lambda i: (0, i, 0)),
                  pl.BlockSpec((n_batch, D_STATE), const2),
                  pl.BlockSpec((n_batch, D_STATE), const2),
                  pl.BlockSpec((1, D_STATE), const2),
                  pl.BlockSpec((1, D_STATE), const2),
                  pl.BlockSpec((1, D_STATE), const2),
                  pl.BlockSpec((4, 128, 512), const3),
                  pl.BlockSpec((4, 128, 512), const3),
                  pl.BlockSpec((4, 512, 128), const3),
                  pl.BlockSpec((4, 512, 128), const3),
                  pl.BlockSpec((1, D_SSM), const2),
                  pl.BlockSpec((D_SSM, 2 * D_SSM), const2),
                  pl.BlockSpec((1, 2 * D_SSM), const2)],
        out_specs=[pl.BlockSpec((planes, tr, LANES), lambda i: (0, i, 0)),
                   pl.BlockSpec((n_batch, D_STATE), const2),
                   pl.BlockSpec((n_batch, D_STATE), const2)],
        out_shape=[jax.ShapeDtypeStruct((planes, rows, LANES), F32),
                   jax.ShapeDtypeStruct((n_batch, D_STATE), F32),
                   jax.ShapeDtypeStruct((n_batch, D_STATE), F32)],
        scratch_shapes=[pltpu.VMEM((2, D_STATE), F32),
                        pltpu.VMEM((4, 128, 512), BF16), pltpu.VMEM((4, 128, 512), BF16),
                        pltpu.VMEM((4, 512, 128), BF16), pltpu.VMEM((4, 512, 128), BF16),
                        pltpu.VMEM((D_SSM, 2 * D_SSM), BF16),
                        pltpu.VMEM((n_batch, D_STATE), F32), pltpu.VMEM((n_batch, D_STATE), F32),
                        pltpu.VMEM((tr, D_STATE), F32), pltpu.VMEM((tr, D_STATE), F32)],
        compiler_params=_params(1),
        name="ssm",
    )(u_rows, h0_re.reshape(n_batch, D_STATE), h0_im.reshape(n_batch, D_STATE),
      flat(lp['lam_re']), flat(lp['lam_im']), ldt,
      _block_diag_b(lp['ssm_b_re']), _block_diag_b(lp['ssm_b_im']),
      _block_diag_c(lp['ssm_c_re']), _block_diag_c(lp['ssm_c_im']),
      lp['ssm_d'].reshape(1, D_SSM), lp['w_glu'], lp['b_glu'].reshape(1, 2 * D_SSM))
    return y, sr, si


def _softmax_pv(s, v):
    m = jnp.max(s, axis=-1, keepdims=True)
    p = jnp.exp(s - m)
    l = jnp.sum(p, axis=-1, keepdims=True)
    return jnp.dot(p.astype(BF16), v, preferred_element_type=F32) / l


def _attend(q, k, v, out_ref, bias_ref=None, valid=None):
    qb = (q * ATT_SCALE).astype(BF16)
    for h in range(N_HEADS):
        sl = slice(HEAD_DIM * h, HEAD_DIM * (h + 1))
        s = lax.dot_general(qb[:, sl], k[:, sl], _NT, preferred_element_type=F32)
        if bias_ref is not None:
            s = s + bias_ref[h]
        if valid is not None:
            s = jnp.where(valid, s, NEG_INF)
        out_ref[0, :, sl] = _softmax_pv(s, v[:, sl])


def _attn_prompt_kernel(q_ref, k0_ref, k1_ref, k2_ref, v0_ref, v1_ref, v2_ref, qm_ref,
                        mk_ref, mv_ref, bias_ref, ya_ref, ym_ref, bias_sc):
    tq = q_ref.shape[1]

    @pl.when((pl.program_id(0) == 0) & (pl.program_id(1) == 0))
    def _():
        q_chunk = lax.broadcasted_iota(I32, (tq, 3 * tq), 0) // CHUNK
        k_chunk = lax.broadcasted_iota(I32, (tq, 3 * tq), 1) // CHUNK
        ahead = k_chunk - q_chunk
        for h in range(N_HEADS):
            bias_sc[h] = jnp.where(ahead >= 0, jnp.where(ahead <= N_PREV_CHUNKS, bias_ref[h], NEG_INF),
                                   NEG_INF)

    k = jnp.concatenate([k0_ref[0], k1_ref[0], k2_ref[0]], axis=0).astype(BF16)
    v = jnp.concatenate([v0_ref[0], v1_ref[0], v2_ref[0]], axis=0).astype(BF16)
    kpos = (pl.program_id(1) - 2) * tq + lax.broadcasted_iota(I32, (1, 3 * tq), 1)
    _attend(q_ref[0], k, v, ya_ref, bias_sc, kpos >= 0)
    _attend(qm_ref[0], mk_ref[0].astype(BF16), mv_ref[0].astype(BF16), ym_ref)


def _attn_sample_kernel(q_ref, kn_ref, vn_ref, qm_ref, ck_ref, cv_ref, mk_ref, mv_ref, bias_ref,
                        ya_ref, ym_ref, nk_ref, nv_ref):
    n = kn_ref.shape[1]
    kk = jnp.concatenate([ck_ref[0], kn_ref[0]], axis=0)
    vv = jnp.concatenate([cv_ref[0], vn_ref[0]], axis=0)
    nk_ref[0] = kk[n:]
    nv_ref[0] = vv[n:]
    _attend(q_ref[0], kk.astype(BF16), vv.astype(BF16), ya_ref, bias_ref)
    _attend(qm_ref[0], mk_ref[0].astype(BF16), mv_ref[0].astype(BF16), ym_ref)


def _rel_bias(table, n_q, n_k):
    period = n_q + n_k
    m = jnp.arange(period)
    offset = jnp.where(m < n_k, m, m - period)
    idx = jnp.clip(BAND - offset, -REL_CLIP, REL_CLIP) + REL_CLIP
    f = table.astype(F32)[:, idx]
    flat = jnp.tile(f, (1, n_q))[:, :n_q * (period - 1)]
    return flat.reshape(N_HEADS, n_q, period - 1)[:, :, :n_k]


def _attn_prompt(zr, mk, mv, table):
    b, s, _ = zr.shape
    tq = ATT_TQ
    bias = _rel_bias(table, tq, 3 * tq)
    col = lambda c: (lambda i, j: (i, j, c))
    prev = lambda c, d: (lambda i, j: (i, jnp.maximum(j - d, 0), c))
    blk = lambda: (1, tq, D_ATT)
    return pl.pallas_call(
        _attn_prompt_kernel,
        grid=(b, s // tq),
        in_specs=[pl.BlockSpec(blk(), col(0)),
                  pl.BlockSpec(blk(), prev(1, 2)), pl.BlockSpec(blk(), prev(1, 1)),
                  pl.BlockSpec(blk(), col(1)),
                  pl.BlockSpec(blk(), prev(2, 2)), pl.BlockSpec(blk(), prev(2, 1)),
                  pl.BlockSpec(blk(), col(2)),
                  pl.BlockSpec(blk(), col(3)),
                  pl.BlockSpec((1, N_MEM, D_MEM), lambda i, j: (i, 0, 0)),
                  pl.BlockSpec((1, N_MEM, D_MEM), lambda i, j: (i, 0, 0)),
                  pl.BlockSpec((N_HEADS, tq, 3 * tq), lambda i, j: (0, 0, 0))],
        out_specs=[pl.BlockSpec(blk(), col(0)), pl.BlockSpec(blk(), col(0))],
        out_shape=[jax.ShapeDtypeStruct((b, s, D_ATT), F32),
                   jax.ShapeDtypeStruct((b, s, D_MEM), F32)],
        scratch_shapes=[pltpu.VMEM((N_HEADS, tq, 3 * tq), F32)],
        compiler_params=_params(2),
        name="attn_prompt",
    )(zr, zr, zr, zr, zr, zr, zr, zr, mk, mv, bias)


def _attn_sample(zr, cache_k, cache_v, mk, mv, table):
    b, n, _ = zr.shape
    w = cache_k.shape[1]
    bias = _rel_bias(table, n, w + n)
    col = lambda c: (lambda i: (i, 0, c))
    blk = (1, n, D_ATT)
    cblk = (1, w, D_ATT)
    mblk = (1, N_MEM, D_MEM)
    row = lambda i: (i, 0, 0)
    return pl.pallas_call(
        _attn_sample_kernel,
        grid=(b,),
        in_specs=[pl.BlockSpec(blk, col(0)), pl.BlockSpec(blk, col(1)), pl.BlockSpec(blk, col(2)),
                  pl.BlockSpec(blk, col(3)),
                  pl.BlockSpec(cblk, row), pl.BlockSpec(cblk, row),
                  pl.BlockSpec(mblk, row), pl.BlockSpec(mblk, row),
                  pl.BlockSpec((N_HEADS, n, w + n), lambda i: (0, 0, 0))],
        out_specs=[pl.BlockSpec(blk, row), pl.BlockSpec(blk, row),
                   pl.BlockSpec(cblk, row), pl.BlockSpec(cblk, row)],
        out_shape=[jax.ShapeDtypeStruct((b, n, D_ATT), F32),
                   jax.ShapeDtypeStruct((b, n, D_MEM), F32),
                   jax.ShapeDtypeStruct((b, w, D_ATT), F32),
                   jax.ShapeDtypeStruct((b, w, D_ATT), F32)],
        compiler_params=_params(1),
        name="attn_sample",
    )(zr, zr, zr, zr, cache_k, cache_v, mk, mv, bias)


def _rms(x, g):
    return x * lax.rsqrt(jnp.mean(jnp.square(x), axis=-1, keepdims=True) + LN_EPS) * g


def _layer_norm(x, g, b):
    mu = jnp.mean(x, axis=-1, keepdims=True)
    xc = x - mu
    var = jnp.mean(jnp.square(xc), axis=-1, keepdims=True)
    return xc * lax.rsqrt(var + LN_EPS) * g + b


def _split_bf16(a):
    hi = a.astype(BF16)
    lo = (a - hi.astype(F32)).astype(BF16)
    return hi, lo


def _merge_kernel(*refs, nb, n_carried):
    (x_ref, ys_ref, ya_ref, ym_ref, gs_ref, ga_ref, gm_ref, wo_ref, l1g_ref, l1b_ref,
     wrt_ref, brt_ref) = refs[:12]
    h_ref, pos_ref, gate_ref, cnt_ref, wo_sc = refs[12 + n_carried:]
    st = x_ref.shape[1]
    tm = nb * st
    n_batch = ys_ref.shape[1] // st

    @pl.when((pl.program_id(0) == 0) & (pl.program_id(1) == 0))
    def _():
        wo_sc[...] = wo_ref[...].astype(BF16)

    x = x_ref[...].reshape(tm, D_MODEL)
    first = pl.program_id(1) * nb
    ys = jnp.concatenate(
        [jnp.concatenate([ys_ref[c, pl.ds(first + i, st, stride=n_batch), :] for c in range(D_SSM // LANES)],
                         axis=1) for i in range(nb)], axis=0)
    ya = ya_ref[...].reshape(tm, D_ATT)
    ym = ym_ref[...].reshape(tm, D_MEM)
    a = _rms(ys, gs_ref[...]).astype(BF16)
    b = _rms(ya, ga_ref[...]).astype(BF16)
    c = _rms(ym, gm_ref[...]).astype(BF16)
    mix = (jnp.dot(a, wo_sc[0:D_SSM, :], preferred_element_type=F32)
           + jnp.dot(b, wo_sc[D_SSM:D_SSM + D_ATT, :], preferred_element_type=F32)
           + jnp.dot(c, wo_sc[D_SSM + D_ATT:, :], preferred_element_type=F32))
    h = _layer_norm(DEEPNORM_ALPHA * x + mix, l1g_ref[...], l1b_ref[...])
    h_ref[...] = h

    h_hi, h_lo = _split_bf16(h)
    w_hi, w_lo = _split_bf16(wrt_ref[...])
    logits = (lax.dot_general(w_hi, h_hi, _NT, preferred_element_type=F32)
              + lax.dot_general(w_hi, h_lo, _NT, preferred_element_type=F32)
              + lax.dot_general(w_lo, h_hi, _NT, preferred_element_type=F32)
              + brt_ref[...])
    erow = lax.broadcasted_iota(I32, (N_EXPERTS, tm), 0).astype(F32)
    tops, picks = [], []
    l = logits
    for k in range(TOP_K):
        m = jnp.max(l, axis=0, keepdims=True)
        e = jnp.min(jnp.where(l == m, erow, float(N_EXPERTS)), axis=0, keepdims=True)
        pick = erow == e
        tops.append(m)
        picks.append(jnp.where(pick, 1.0, 0.0))
        l = jnp.where(pick, -jnp.inf, l)
    ex = [jnp.exp(t - tops[0]) for t in tops]
    den = ex[0] + ex[1] + ex[2] + ex[3]
    for k in range(TOP_K):
        gate_ref[k:k + 1, :] = ex[k] / den

    chosen = picks[0] + picks[1] + picks[2] + picks[3]
    chosen_b = chosen.astype(BF16)
    earlier_tok = (lax.broadcasted_iota(I32, (tm, tm), 0) < lax.broadcasted_iota(I32, (tm, tm), 1))
    within = jnp.dot(chosen_b, jnp.where(earlier_tok, 1.0, 0.0).astype(BF16),
                     preferred_element_type=F32)
    lower_exp = (lax.broadcasted_iota(I32, (N_EXPERTS, N_EXPERTS), 1)
                 < lax.broadcasted_iota(I32, (N_EXPERTS, N_EXPERTS), 0))
    below = jnp.dot(jnp.where(lower_exp, 1.0, 0.0).astype(BF16), chosen_b,
                    preferred_element_type=F32)
    slot = within + jnp.sum(below, axis=1, keepdims=True)
    for k in range(TOP_K):
        pos_ref[k:k + 1, :] = jnp.sum(picks[k] * slot, axis=0, keepdims=True).astype(I32)
    cnt_ref[0] = jnp.sum(chosen, axis=1, keepdims=True)


def _merge(x, ys_tm, ya, ym, lp, nb, st, t_all, tile0, carried=None):
    b, s, _ = x.shape
    tm = nb * st
    assert tm == TOKEN_TM
    n_s = s // st
    tile = lambda j, i: (tile0 + i * n_s + j)
    c2 = lambda j, i: (0, 0)
    row3 = lambda j, i: (i, j, 0)
    vec = lambda a: a.reshape(1, -1)
    carried = () if carried is None else tuple(carried)
    return pl.pallas_call(
        functools.partial(_merge_kernel, nb=nb, n_carried=len(carried)),
        grid=(n_s, b // nb),
        in_specs=[pl.BlockSpec((nb, st, D_MODEL), row3),
                  pl.BlockSpec((D_SSM // LANES, st * b, LANES), lambda j, i: (0, j, 0)),
                  pl.BlockSpec((nb, st, D_ATT), row3),
                  pl.BlockSpec((nb, st, D_MEM), row3),
                  pl.BlockSpec((1, D_SSM), c2), pl.BlockSpec((1, D_ATT), c2),
                  pl.BlockSpec((1, D_MEM), c2),
                  pl.BlockSpec((D_MODEL, D_MODEL), c2),
                  pl.BlockSpec((1, D_MODEL), c2), pl.BlockSpec((1, D_MODEL), c2),
                  pl.BlockSpec((N_EXPERTS, D_MODEL), c2), pl.BlockSpec((N_EXPERTS, 1), c2)]
                 + [pl.BlockSpec(memory_space=pl.ANY)] * len(carried),
        out_specs=[pl.BlockSpec((tm, D_MODEL), lambda j, i: (tile(j, i), 0)),
                   pl.BlockSpec((TOP_K, tm), lambda j, i: (0, tile(j, i))),
                   pl.BlockSpec((TOP_K, tm), lambda j, i: (0, tile(j, i))),
                   pl.BlockSpec((1, N_EXPERTS, 1), lambda j, i: (tile(j, i), 0, 0))],
        out_shape=[jax.ShapeDtypeStruct((t_all, D_MODEL), F32),
                   jax.ShapeDtypeStruct((TOP_K, t_all), I32),
                   jax.ShapeDtypeStruct((TOP_K, t_all), F32),
                   jax.ShapeDtypeStruct((t_all // tm, N_EXPERTS, 1), F32)],
        scratch_shapes=[pltpu.VMEM((D_MODEL, D_MODEL), BF16)],
        input_output_aliases={12 + k: k for k in range(len(carried))},
        compiler_params=_params(2),
        name="merge_router",
    )(x, ys_tm, ya, ym, vec(lp['g_ssm']), vec(lp['g_att']), vec(lp['g_mem']), lp['w_out'],
      vec(lp['ln1_g']), vec(lp['ln1_b']), lp['w_router'].T, lp['b_router'].reshape(N_EXPERTS, 1),
      *carried)


def _rows(start, size):
    return pl.ds(pl.multiple_of(start * ROW_SUBLANES, ROW_SUBLANES), size * ROW_SUBLANES)


def _store_rows(ref, value):
    n = value.shape[0]
    for j in range(ROW_SUBLANES):
        ref[pl.ds(j, n, stride=ROW_SUBLANES), :] = value[:, LANES * j:LANES * (j + 1)]


def _load_rows(ref, dtype=F32):
    n = ref.shape[0] // ROW_SUBLANES
    return jnp.concatenate([ref[pl.ds(j, n, stride=ROW_SUBLANES), :].astype(dtype)
                            for j in range(ROW_SUBLANES)], axis=1)


def _for_each_run_piece(n, max_rows, fn):
    for bit in reversed(range(max_rows.bit_length())):
        size = 1 << bit
        start = (n >> (bit + 1)) << (bit + 1)

        @pl.when((n & size) != 0)
        def _(start=start, size=size):
            fn(start, size)


def _dispatch_kernel(n_ref, off_ref, dst_ref, padlo_ref, padn_ref, used_ref,
                     pos_ref, h_ref, xs_hbm, sorted_sc, zero_sc, sem, zsem):
    i = pl.program_id(0)
    tm = h_ref.shape[0]
    n_slots = TOP_K * tm
    n_blocks = xs_hbm.shape[0] // (EXPERT_TM * ROW_SUBLANES)

    @pl.when(i == 0)
    def _():
        zero_sc[...] = jnp.zeros_like(zero_sc)

        def pad_copy(e, start, size):
            return pltpu.make_async_copy(zero_sc.at[_rows(0, size)],
                                         xs_hbm.at[_rows(padlo_ref[e] + start, size)], zsem)

        def tail_copy(blk):
            return pltpu.make_async_copy(zero_sc, xs_hbm.at[_rows(blk * EXPERT_TM, EXPERT_TM)], zsem)

        for e in range(N_EXPERTS):
            _for_each_run_piece(padn_ref[e], EXPERT_TM - 1,
                                lambda start, size, e=e: pad_copy(e, start, size).start())

        def tail_start(blk, carry):
            tail_copy(blk).start()
            return carry

        lax.fori_loop(used_ref[0], n_blocks, tail_start, 0)
        for e in range(N_EXPERTS):
            _for_each_run_piece(padn_ref[e], EXPERT_TM - 1,
                                lambda start, size, e=e: pad_copy(e, start, size).wait())

        def tail_wait(blk, carry):
            tail_copy(blk).wait()
            return carry

        lax.fori_loop(used_ref[0], n_blocks, tail_wait, 0)

    n_tiles = pl.num_programs(0)
    slot = lax.rem(i, 2)
    buf = sorted_sc.at[slot]

    def wait_tile(sl):
        pltpu.make_async_copy(sorted_sc.at[sl], xs_hbm.at[_rows(0, n_slots)], sem.at[sl]).wait()

    @pl.when(i >= 2)
    def _():
        wait_tile(slot)

    pos = pos_ref[...]
    srow = lax.broadcasted_iota(I32, (n_slots, tm), 0)
    perm = jnp.where(srow == pos[0:1], 1.0,
                     jnp.where(srow == pos[1:2], 1.0,
                               jnp.where(srow == pos[2:3], 1.0,
                                         jnp.where(srow == pos[3:4], 1.0, 0.0))))
    _store_rows(buf, jnp.dot(perm.astype(BF16), h_ref[...].astype(BF16), preferred_element_type=F32))

    base = i * N_EXPERTS
    for e in range(N_EXPERTS):
        off = off_ref[base + e]
        dst = dst_ref[base + e]

        def run_start(start, size, off=off, dst=dst):
            pltpu.make_async_copy(buf.at[_rows(off + start, size)],
                                  xs_hbm.at[_rows(dst + start, size)], sem.at[slot]).start()

        _for_each_run_piece(n_ref[base + e], tm, run_start)

    @pl.when(i == n_tiles - 1)
    def _():
        @pl.when(i >= 1)
        def _():
            wait_tile(1 - slot)

        wait_tile(slot)


def _dispatch(run_n, run_off, run_dst, pad_lo, pad_n, n_used, pos, h, cap):
    t = h.shape[0]
    tm = TOKEN_TM
    grid_spec = pltpu.PrefetchScalarGridSpec(
        num_scalar_prefetch=6,
        grid=(t // tm,),
        in_specs=[pl.BlockSpec((TOP_K, tm), lambda i, *_: (0, i)),
                  pl.BlockSpec((tm, D_MODEL), lambda i, *_: (i, 0))],
        out_specs=pl.BlockSpec(memory_space=pl.ANY),
        scratch_shapes=[pltpu.VMEM((2, TOP_K * tm * ROW_SUBLANES, LANES), F32),
                        pltpu.VMEM((EXPERT_TM * ROW_SUBLANES, LANES), F32),
                        pltpu.SemaphoreType.DMA((2,)), pltpu.SemaphoreType.DMA],
    )
    return pl.pallas_call(
        _dispatch_kernel,
        grid_spec=grid_spec,
        out_shape=jax.ShapeDtypeStruct((cap * ROW_SUBLANES, LANES), F32),
        compiler_params=_params(1),
        name="moe_dispatch",
    )(run_n, run_off, run_dst, pad_lo, pad_n, n_used, pos, h)


def _expert_kernel(be_ref, first_ref, ord_ref, seq_ref, used_ref,
                   x_ref, bgu_ref, bd_ref, wgu_hbm, wd_hbm, o_ref,
                   wgu_st, wd_st, wgu_sc, wd_sc, sem):
    i = pl.program_id(0)

    def weight_copies(e):
        return (pltpu.make_async_copy(wgu_hbm.at[e], wgu_st, sem.at[0]),
                pltpu.make_async_copy(wd_hbm.at[e], wd_st, sem.at[1]))

    @pl.when(i == 0)
    def _():
        for c in weight_copies(seq_ref[0]):
            c.start()

    @pl.when(i < used_ref[0])
    def _():
        @pl.when(first_ref[i] == 1)
        def _():
            k = ord_ref[i]
            for c in weight_copies(seq_ref[k]):
                c.wait()
            wgu_sc[...] = wgu_st[...].astype(BF16)
            wd_sc[...] = wd_st[...].astype(BF16)

            @pl.when(k + 1 < used_ref[1])
            def _():
                for c in weight_copies(seq_ref[k + 1]):
                    c.start()

        gu = jnp.dot(_load_rows(x_ref, BF16), wgu_sc[...], preferred_element_type=F32) + bgu_ref[0]
        gate = jnp.minimum(gu[:, :D_FF], SWIGLU_LIMIT)
        lin = jnp.clip(gu[:, D_FF:], -SWIGLU_LIMIT, SWIGLU_LIMIT)
        act = gate * jax.nn.sigmoid(SWIGLU_ALPHA * gate) * (lin + 1.0)
        _store_rows(o_ref, jnp.dot(act.astype(BF16), wd_sc[...], preferred_element_type=F32) + bd_ref[0])

    @pl.when(i >= used_ref[0])
    def _():
        o_ref[...] = jnp.zeros_like(o_ref)


def _experts(block_expert, block_first, block_ord, expert_seq, n_used, xs, lp):
    tm = EXPERT_TM * ROW_SUBLANES
    grid_spec = pltpu.PrefetchScalarGridSpec(
        num_scalar_prefetch=5,
        grid=(xs.shape[0] // tm,),
        in_specs=[pl.BlockSpec((tm, LANES), lambda i, be, *_: (i, 0)),
                  pl.BlockSpec((1, 1, 2 * D_FF), lambda i, be, *_: (be[i], 0, 0)),
                  pl.BlockSpec((1, 1, D_MODEL), lambda i, be, *_: (be[i], 0, 0)),
                  pl.BlockSpec(memory_space=pl.ANY),
                  pl.BlockSpec(memory_space=pl.ANY)],
        out_specs=pl.BlockSpec((tm, LANES), lambda i, be, *_: (i, 0)),
        scratch_shapes=[pltpu.VMEM((D_MODEL, 2 * D_FF), F32), pltpu.VMEM((D_FF, D_MODEL), F32),
                        pltpu.VMEM((D_MODEL, 2 * D_FF), BF16), pltpu.VMEM((D_FF, D_MODEL), BF16),
                        pltpu.SemaphoreType.DMA((2,))],
    )
    return pl.pallas_call(
        _expert_kernel,
        grid_spec=grid_spec,
        out_shape=jax.ShapeDtypeStruct(xs.shape, F32),
        compiler_params=_params(1),
        name="moe_experts",
    )(block_expert, block_first, block_ord, expert_seq, n_used, xs,
      lp['b_gu'].reshape(N_EXPERTS, 1, 2 * D_FF), lp['b_down'].reshape(N_EXPERTS, 1, D_MODEL),
      lp['w_gu'], lp['w_down'])


def _combine_kernel(n_ref, off_ref, dst_ref, pos_ref, gate_ref, h_ref, ys_hbm, g_ref, b_ref,
                    y1_ref, y2_ref, sorted_sc, sem, *, n_first):
    i = pl.program_id(0)
    n_tiles = pl.num_programs(0) - 1
    tm = h_ref.shape[0]
    n_slots = TOP_K * tm

    @pl.when(i < n_tiles)
    def _():
        slot = lax.rem(i, 2)
        buf = sorted_sc.at[slot]
        base = i * N_EXPERTS
        for e in range(N_EXPERTS):
            off = off_ref[base + e]
            dst = dst_ref[base + e]

            def run_start(start, size, off=off, dst=dst):
                pltpu.make_async_copy(ys_hbm.at[_rows(dst + start, size)],
                                      buf.at[_rows(off + start, size)], sem.at[slot]).start()

            _for_each_run_piece(n_ref[base + e], tm, run_start)

    @pl.when(i >= 1)
    def _():
        slot = lax.rem(i - 1, 2)
        buf = sorted_sc.at[slot]
        pltpu.make_async_copy(ys_hbm.at[_rows(0, n_slots)], buf, sem.at[slot]).wait()

        pos = pos_ref[...]
        gates = gate_ref[...]
        scol = lax.broadcasted_iota(I32, (tm, n_slots), 1)
        w = jnp.where(scol == pos[:, 0:1], gates[:, 0:1],
                      jnp.where(scol == pos[:, 1:2], gates[:, 1:2],
                                jnp.where(scol == pos[:, 2:3], gates[:, 2:3],
                                          jnp.where(scol == pos[:, 3:4], gates[:, 3:4], 0.0))))
        f = jnp.dot(w.astype(BF16), _load_rows(buf, BF16), preferred_element_type=F32)
        y = _layer_norm(DEEPNORM_ALPHA * h_ref[...] + f, g_ref[...], b_ref[...])

        @pl.when(i - 1 < n_first)
        def _():
            y1_ref[...] = y

        @pl.when(i - 1 >= n_first)
        def _():
            y2_ref[...] = y


def _combine(run_n, run_off, run_dst, pos_t, gates_t, h, ys, lp, t_first):
    t = h.shape[0]
    tm = TOKEN_TM
    n_first = t_first // tm
    n_rest = (t - t_first) // tm
    c2 = lambda i, *_: (0, 0)
    done = lambda i, *_: (jnp.maximum(i - 1, 0), 0)
    done1 = lambda i, *_: (jnp.clip(i - 1, 0, n_first - 1), 0)
    done2 = lambda i, *_: (jnp.clip(i - 1 - n_first, 0, n_rest - 1), 0)
    grid_spec = pltpu.PrefetchScalarGridSpec(
        num_scalar_prefetch=3,
        grid=(t // tm + 1,),
        in_specs=[pl.BlockSpec((tm, TOP_K), done),
                  pl.BlockSpec((tm, TOP_K), done),
                  pl.BlockSpec((tm, D_MODEL), done),
                  pl.BlockSpec(memory_space=pl.ANY),
                  pl.BlockSpec((1, D_MODEL), c2), pl.BlockSpec((1, D_MODEL), c2)],
        out_specs=[pl.BlockSpec((tm, D_MODEL), done1), pl.BlockSpec((tm, D_MODEL), done2)],
        scratch_shapes=[pltpu.VMEM((2, TOP_K * tm * ROW_SUBLANES, LANES), F32),
                        pltpu.SemaphoreType.DMA((2,))],
    )
    return pl.pallas_call(
        functools.partial(_combine_kernel, n_first=n_first),
        grid_spec=grid_spec,
        out_shape=[jax.ShapeDtypeStruct((t_first, D_MODEL), F32),
                   jax.ShapeDtypeStruct((t - t_first, D_MODEL), F32)],
        compiler_params=_params(1),
        name="moe_combine",
    )(run_n, run_off, run_dst, pos_t, gates_t, h, ys,
      lp['ln2_g'].reshape(1, D_MODEL), lp['ln2_b'].reshape(1, D_MODEL))


def _moe_and_norm(h, pos, gates, tile_counts, lp, t_first):
    t = h.shape[0]
    te = EXPERT_TM
    n_tiles = t // TOKEN_TM
    n_blocks = (t * TOP_K) // te + N_EXPERTS
    cap = n_blocks * te
    cnt = tile_counts.reshape(n_tiles, N_EXPERTS).astype(I32)
    counts = jnp.sum(cnt, axis=0)
    padded = (counts + te - 1) // te * te
    pad_ends = jnp.cumsum(padded)
    pad_starts = pad_ends - padded
    run_dst = pad_starts[None, :] + jnp.cumsum(cnt, axis=0) - cnt
    run_off = jnp.cumsum(cnt, axis=1) - cnt
    blk_start = jnp.arange(n_blocks, dtype=I32) * te
    n_used = (pad_ends[-1] // te).astype(I32)
    be = jnp.minimum(jnp.sum(blk_start[:, None] >= pad_ends[None, :], axis=1), N_EXPERTS - 1).astype(I32)
    be = jnp.where(blk_start < pad_ends[-1], be, be[jnp.maximum(n_used - 1, 0)])
    in_use = counts > 0
    expert_seq = jnp.argsort(~in_use, stable=True).astype(I32)
    block_ord = (jnp.cumsum(in_use.astype(I32)) - 1)[be]
    block_first = (blk_start == pad_starts[be]) & (blk_start < pad_ends[-1])
    used = jnp.stack([n_used, jnp.sum(in_use.astype(I32))]).astype(I32)
    flat = lambda a: a.reshape(-1).astype(I32)
    xs = _dispatch(flat(cnt), flat(run_off), flat(run_dst), flat(pad_starts + counts),
                   flat(padded - counts), used, pos, h, cap)
    ys = _experts(be, flat(block_first), flat(block_ord), expert_seq, used, xs, lp)
    return _combine(flat(cnt), flat(run_off), flat(run_dst), pos.T, gates.T, h, ys, lp, t_first)


def kernel(x_prompt, x_sample, cache_attn_k, cache_attn_v, cache_mem_k, cache_mem_v, state_ssm_re, state_ssm_im, mem_prompt, w_in, lam_re, lam_im, log_dt, ssm_b_re, ssm_b_im, ssm_c_re, ssm_c_im, ssm_d, w_glu, b_glu, rel_bias, w_mem_kv, g_ssm, g_att, g_mem, w_out, ln1_g, ln1_b, w_router, b_router, w_gu, b_gu, w_down, b_down, ln2_g, ln2_b):
    assert w_in.shape[0] == 1, "single-layer step"
    lp = dict(w_in=w_in[0], lam_re=lam_re[0], lam_im=lam_im[0], log_dt=log_dt[0],
              ssm_b_re=ssm_b_re[0], ssm_b_im=ssm_b_im[0], ssm_c_re=ssm_c_re[0], ssm_c_im=ssm_c_im[0],
              ssm_d=ssm_d[0], w_glu=w_glu[0], b_glu=b_glu[0], rel_bias=rel_bias[0],
              w_mem_kv=w_mem_kv[0], g_ssm=g_ssm[0], g_att=g_att[0], g_mem=g_mem[0], w_out=w_out[0],
              ln1_g=ln1_g[0], ln1_b=ln1_b[0], w_router=w_router[0], b_router=b_router[0],
              w_gu=w_gu[0], b_gu=b_gu[0], w_down=w_down[0], b_down=b_down[0],
              ln2_g=ln2_g[0], ln2_b=ln2_b[0])

    bp, sp, _ = x_prompt.shape
    bs, ss, _ = x_sample.shape
    t_p = bp * sp
    t_all = t_p + bs * ss
    heads = lambda a: a.reshape(a.shape[0], a.shape[1], N_HEADS, HEAD_DIM)
    state = lambda a: a.reshape(a.shape[0], N_GROUPS, SSM_STATE)

    u_p, zr = _in_proj(x_prompt, lp['w_in'], min(128, sp))
    mk, mv = _mem_kv(mem_prompt, lp['w_mem_kv'])
    ya, ym = _attn_prompt(zr, mk, mv, lp['rel_bias'])
    zeros = jnp.zeros((bp, D_STATE), F32)
    ys_p, sr_p, si_p = _ssm(u_p, zeros, zeros, lp, bp)
    merged = _merge(x_prompt, ys_p, ya, ym, lp, 1, TOKEN_TM, t_all, 0)
    w = min(BAND, sp)
    k_p = heads(zr[:, sp - w:, D_ATT:2 * D_ATT])
    v_p = heads(zr[:, sp - w:, 2 * D_ATT:3 * D_ATT])

    wc = cache_attn_k.shape[2]
    u_s, zr_s = _in_proj(x_sample, lp['w_in'], ss)
    ya_s, ym_s, nk, nv = _attn_sample(
        zr_s, cache_attn_k[0].reshape(bs, wc, D_ATT), cache_attn_v[0].reshape(bs, wc, D_ATT),
        cache_mem_k[0].reshape(bs, N_MEM, D_MEM), cache_mem_v[0].reshape(bs, N_MEM, D_MEM),
        lp['rel_bias'])
    ys_s, sr_s, si_s = _ssm(u_s, state_ssm_re[0], state_ssm_im[0], lp, bs)
    merged = _merge(x_sample, ys_s, ya_s, ym_s, lp, TOKEN_TM // ss, ss, t_all, t_p // TOKEN_TM,
                    carried=merged)

    y_p, y_s = _moe_and_norm(*merged, lp, t_p)

    return (y_p.reshape(bp, sp, D_MODEL), y_s.reshape(bs, ss, D_MODEL),
            k_p[None], v_p[None], heads(mk)[None], heads(mv)[None], state(sr_p)[None], state(si_p)[None],
            heads(nk)[None], heads(nv)[None], state(sr_s)[None], state(si_s)[None])
```

```python
import functools

import jax
import jax.numpy as jnp
from jax import lax
from jax.experimental import pallas as pl
from jax.experimental.pallas import tpu as pltpu

F32 = jnp.float32
BF16 = jnp.bfloat16
I32 = jnp.int32

D_MODEL = 1024
D_SSM = 512
D_ATT = 256
D_MEM = 256
D_IN = D_SSM + 3 * D_ATT + D_MEM
D_REST = D_IN - D_SSM
HEAD_DIM = 64
N_HEADS = 4
N_GROUPS = 32
SSM_GROUP = 16
SSM_STATE = 64
D_STATE = N_GROUPS * SSM_STATE
CHUNK = 64
N_PREV_CHUNKS = 8
BAND = N_PREV_CHUNKS * CHUNK
REL_CLIP = 128
N_MEM = 256
N_EXPERTS = 32
TOP_K = 4
D_FF = D_MODEL
SWIGLU_LIMIT = 7.0
SWIGLU_ALPHA = 1.702
LN_EPS = 1e-5
NEG_INF = -1e30
ATT_SCALE = HEAD_DIM ** -0.5
DEEPNORM_ALPHA = 2.0 ** 0.25

V7X_VMEM_LIMIT = 56 * 1024 * 1024
ATT_TQ = 4 * CHUNK
SCAN_LANES = 1024
SCAN_ROWS = 1024
LANES = 128
ROW_SUBLANES = D_MODEL // LANES
EXPERT_TM = 512
TOKEN_TM = 512
N_SORT_BUFS = 3
SORT_CHUNKS = 8

_NT = (((1,), (1,)), ((), ()))


def _params(n_axes, vmem=V7X_VMEM_LIMIT):
    return pltpu.CompilerParams(dimension_semantics=("arbitrary",) * n_axes,
                                vmem_limit_bytes=vmem)


def _in_proj_kernel(x_ref, w_ref, u_ref, z_ref, wb_ref):
    @pl.when(pl.program_id(0) == 0)
    def _():
        wb_ref[...] = w_ref[...].astype(BF16)

    nb, ts, _ = x_ref.shape
    x = x_ref[...].reshape(nb * ts, D_MODEL).astype(BF16)
    z = jnp.dot(x, wb_ref[...], preferred_element_type=F32)
    for b in range(nb):
        for c in range(D_SSM // LANES):
            u_ref[c, pl.ds(b, ts, stride=nb), :] = z[b * ts:(b + 1) * ts, LANES * c:LANES * (c + 1)]
    z_ref[...] = z[:, D_SSM:].reshape(nb, ts, D_REST)


def _in_proj(x, w_in, ts):
    b, s, _ = x.shape
    return pl.pallas_call(
        _in_proj_kernel,
        grid=(s // ts,),
        in_specs=[pl.BlockSpec((b, ts, D_MODEL), lambda j: (0, j, 0)),
                  pl.BlockSpec((D_MODEL, D_IN), lambda j: (0, 0))],
        out_specs=[pl.BlockSpec((D_SSM // LANES, ts * b, LANES), lambda j: (0, j, 0)),
                   pl.BlockSpec((b, ts, D_REST), lambda j: (0, j, 0))],
        out_shape=[jax.ShapeDtypeStruct((D_SSM // LANES, s * b, LANES), F32),
                   jax.ShapeDtypeStruct((b, s, D_REST), F32)],
        scratch_shapes=[pltpu.VMEM((D_MODEL, D_IN), BF16)],
        compiler_params=_params(1),
        name="in_proj",
    )(x, w_in)


def _mem_kv_kernel(m_ref, w_ref, mk_ref, mv_ref):
    kv = jnp.dot(m_ref[0].astype(BF16), w_ref[...].astype(BF16), preferred_element_type=F32)
    mk_ref[0] = kv[:, :D_MEM]
    mv_ref[0] = kv[:, D_MEM:]


def _mem_kv(mem, w_mem_kv):
    b = mem.shape[0]
    return pl.pallas_call(
        _mem_kv_kernel,
        grid=(b,),
        in_specs=[pl.BlockSpec((1, N_MEM, D_MODEL), lambda i: (i, 0, 0)),
                  pl.BlockSpec((D_MODEL, 2 * D_MEM), lambda i: (0, 0))],
        out_specs=[pl.BlockSpec((1, N_MEM, D_MEM), lambda i: (i, 0, 0)),
                   pl.BlockSpec((1, N_MEM, D_MEM), lambda i: (i, 0, 0))],
        out_shape=[jax.ShapeDtypeStruct((b, N_MEM, D_MEM), F32)] * 2,
        compiler_params=_params(1),
        name="mem_kv",
    )(mem, w_mem_kv)


def _ssm_kernel(u_ref, h0r_ref, h0i_ref, lr_ref, li_ref, ldt_ref, bre_ref, bim_ref,
                cre_ref, cim_ref, d_ref, wg_ref, bg_ref,
                y_ref, sr_ref, si_ref,
                a_sc, bbr_sc, bbi_sc, cr_sc, ci_sc, wg_sc, str_sc, sti_sc, xr_sc, xi_sc,
                *, n_batch):
    n_rows = u_ref.shape[1]
    n_steps = n_rows // n_batch

    @pl.when(pl.program_id(0) == 0)
    def _():
        lr = lr_ref[...]
        li = li_ref[...]
        dt = jnp.exp(ldt_ref[...])
        mag = jnp.exp(lr * dt)
        ar = mag * jnp.cos(li * dt)
        ai = mag * jnp.sin(li * dt)
        den = lr * lr + li * li
        fr = ((ar - 1.0) * lr + ai * li) / den
        fi = (ai * lr - (ar - 1.0) * li) / den
        a_sc[0:1, :] = ar
        a_sc[1:2, :] = ai
        for j in range(4):
            frj = fr[:, 512 * j:512 * (j + 1)]
            fij = fi[:, 512 * j:512 * (j + 1)]
            bbr_sc[j] = (frj * bre_ref[j] - fij * bim_ref[j]).astype(BF16)
            bbi_sc[j] = (frj * bim_ref[j] + fij * bre_ref[j]).astype(BF16)
            cr_sc[j] = cre_ref[j].astype(BF16)
            ci_sc[j] = cim_ref[j].astype(BF16)
        wg_sc[...] = wg_ref[...].astype(BF16)
        str_sc[...] = h0r_ref[...]
        sti_sc[...] = h0i_ref[...]

    for j in range(4):
        uc = u_ref[j].astype(BF16)
        xr_sc[:, 512 * j:512 * (j + 1)] = jnp.dot(uc, bbr_sc[j], preferred_element_type=F32)
        xi_sc[:, 512 * j:512 * (j + 1)] = jnp.dot(uc, bbi_sc[j], preferred_element_type=F32)

    for c in range(D_STATE // SCAN_LANES):
        lo = c * SCAN_LANES
        ar = jnp.broadcast_to(a_sc[0:1, lo:lo + SCAN_LANES], (n_batch, SCAN_LANES))
        ai = jnp.broadcast_to(a_sc[1:2, lo:lo + SCAN_LANES], (n_batch, SCAN_LANES))

        def step(t, carry, lo=lo, ar=ar, ai=ai):
            sr, si = carry
            r0 = pl.multiple_of(t * n_batch, n_batch)
            nr = ar * sr - ai * si + xr_sc[pl.ds(r0, n_batch), lo:lo + SCAN_LANES]
            ni = ar * si + ai * sr + xi_sc[pl.ds(r0, n_batch), lo:lo + SCAN_LANES]
            xr_sc[pl.ds(r0, n_batch), lo:lo + SCAN_LANES] = nr
            xi_sc[pl.ds(r0, n_batch), lo:lo + SCAN_LANES] = ni
            return nr, ni

        sr, si = lax.fori_loop(0, n_steps, step,
                               (str_sc[:, lo:lo + SCAN_LANES], sti_sc[:, lo:lo + SCAN_LANES]),
                               unroll=4)
        str_sc[:, lo:lo + SCAN_LANES] = sr
        sti_sc[:, lo:lo + SCAN_LANES] = si

    pieces = []
    for j in range(4):
        xr = xr_sc[:, 512 * j:512 * (j + 1)].astype(BF16)
        xi = xi_sc[:, 512 * j:512 * (j + 1)].astype(BF16)
        pieces.append(jnp.dot(xr, cr_sc[j], preferred_element_type=F32)
                      - jnp.dot(xi, ci_sc[j], preferred_element_type=F32))
    u = jnp.concatenate([u_ref[j] for j in range(4)], axis=1)
    y = jnp.concatenate(pieces, axis=1) + d_ref[...] * u
    y = jax.nn.gelu(y)
    z = jnp.dot(y.astype(BF16), wg_sc[...], preferred_element_type=F32) + bg_ref[...]
    out = z[:, :D_SSM] * jax.nn.sigmoid(z[:, D_SSM:])
    for j in range(D_SSM // LANES):
        y_ref[j] = out[:, LANES * j:LANES * (j + 1)]
    sr_ref[...] = str_sc[...]
    si_ref[...] = sti_sc[...]


def _block_diag_b(b):
    bt = b.transpose(0, 2, 1).reshape(4, 8, SSM_GROUP, SSM_STATE)
    same = jnp.eye(8, dtype=bool)[None, :, None, :, None]
    t = jnp.where(same, bt[:, :, :, None, :], 0.0)
    return t.reshape(4, 8 * SSM_GROUP, 8 * SSM_STATE)


def _block_diag_c(c):
    ct = c.transpose(0, 2, 1).reshape(4, 8, SSM_STATE, SSM_GROUP)
    same = jnp.eye(8, dtype=bool)[None, :, None, :, None]
    t = jnp.where(same, ct[:, :, :, None, :], 0.0)
    return t.reshape(4, 8 * SSM_STATE, 8 * SSM_GROUP)


def _ssm(u_rows, h0_re, h0_im, lp, n_batch):
    rows = u_rows.shape[1]
    planes = D_SSM // LANES
    tr = min(SCAN_ROWS, rows)
    flat = lambda a: a.reshape(1, D_STATE)
    ldt = jnp.repeat(lp['log_dt'], SSM_STATE).reshape(1, D_STATE)
    const2 = lambda i: (0, 0)
    const3 = lambda i: (0, 0, 0)
    y, sr, si = pl.pallas_call(
        functools.partial(_ssm_kernel, n_batch=n_batch),
        grid=(rows // tr,),
        in_specs=[pl.BlockSpec((planes, tr, LANES), lambda i: (0, i, 0)),
                  pl.BlockSpec((n_batch, D_STATE), const2),
                  pl.BlockSpec((n_batch, D_STATE), const2),
                  pl.BlockSpec((1, D_STATE), const2),
                  pl.BlockSpec((1, D_STATE), const2),
                  pl.BlockSpec((1, D_STATE), const2),
                  pl.BlockSpec((4, 128, 512), const3),
                  pl.BlockSpec((4, 128, 512), const3),
                  pl.BlockSpec((4, 512, 128), const3),
                  pl.BlockSpec((4, 512, 128), const3),
                  pl.BlockSpec((1, D_SSM), const2),
                  pl.BlockSpec((D_SSM, 2 * D_SSM), const2),
                  pl.BlockSpec((1, 2 * D_SSM), const2)],
        out_specs=[pl.BlockSpec((planes, tr, LANES), lambda i: (0, i, 0)),
                   pl.BlockSpec((n_batch, D_STATE), const2),
                   pl.BlockSpec((n_batch, D_STATE), const2)],
        out_shape=[jax.ShapeDtypeStruct((planes, rows, LANES), F32),
                   jax.ShapeDtypeStruct((n_batch, D_STATE), F32),
                   jax.ShapeDtypeStruct((n_batch, D_STATE), F32)],
        scratch_shapes=[pltpu.VMEM((2, D_STATE), F32),
                        pltpu.VMEM((4, 128, 512), BF16), pltpu.VMEM((4, 128, 512), BF16),
                        pltpu.VMEM((4, 512, 128), BF16), pltpu.VMEM((4, 512, 128), BF16),
                        pltpu.VMEM((D_SSM, 2 * D_SSM), BF16),
                        pltpu.VMEM((n_batch, D_STATE), F32), pltpu.VMEM((n_batch, D_STATE), F32),
                        pltpu.VMEM((tr, D_STATE), F32), pltpu.VMEM((tr, D_STATE), F32)],
        compiler_params=_params(1),
        name="ssm",
    )(u_rows, h0_re.reshape(n_batch, D_STATE), h0_im.reshape(n_batch, D_STATE),
      flat(lp['lam_re']), flat(lp['lam_im']), ldt,
      _block_diag_b(lp['ssm_b_re']), _block_diag_b(lp['ssm_b_im']),
      _block_diag_c(lp['ssm_c_re']), _block_diag_c(lp['ssm_c_im']),
      lp['ssm_d'].reshape(1, D_SSM), lp['w_glu'], lp['b_glu'].reshape(1, 2 * D_SSM))
    return y, sr, si


def _softmax_pv(s, v):
    m = jnp.max(s, axis=-1, keepdims=True)
    p = jnp.exp(s - m)
    l = jnp.sum(p, axis=-1, keepdims=True)
    return jnp.dot(p.astype(BF16), v, preferred_element_type=F32) / l


def _attend(q, k, v, out_ref, bias_ref=None, valid=None):
    qb = (q * ATT_SCALE).astype(BF16)
    for h in range(N_HEADS):
        sl = slice(HEAD_DIM * h, HEAD_DIM * (h + 1))
        s = lax.dot_general(qb[:, sl], k[:, sl], _NT, preferred_element_type=F32)
        if bias_ref is not None:
            s = s + bias_ref[h]
        if valid is not None:
            s = jnp.where(valid, s, NEG_INF)
        out_ref[0, :, sl] = _softmax_pv(s, v[:, sl])


def _attn_prompt_kernel(q_ref, k0_ref, k1_ref, k2_ref, v0_ref, v1_ref, v2_ref, qm_ref,
                        mk_ref, mv_ref, bias_ref, ya_ref, ym_ref, bias_sc):
    tq = q_ref.shape[1]

    @pl.when((pl.program_id(0) == 0) & (pl.program_id(1) == 0))
    def _():
        q_chunk = lax.broadcasted_iota(I32, (tq, 3 * tq), 0) // CHUNK
        k_chunk = lax.broadcasted_iota(I32, (tq, 3 * tq), 1) // CHUNK
        ahead = k_chunk - q_chunk
        for h in range(N_HEADS):
            bias_sc[h] = jnp.where(ahead >= 0, jnp.where(ahead <= N_PREV_CHUNKS, bias_ref[h], NEG_INF),
                                   NEG_INF)

    k = jnp.concatenate([k0_ref[0], k1_ref[0], k2_ref[0]], axis=0).astype(BF16)
    v = jnp.concatenate([v0_ref[0], v1_ref[0], v2_ref[0]], axis=0).astype(BF16)
    kpos = (pl.program_id(1) - 2) * tq + lax.broadcasted_iota(I32, (1, 3 * tq), 1)
    _attend(q_ref[0], k, v, ya_ref, bias_sc, kpos >= 0)
    _attend(qm_ref[0], mk_ref[0].astype(BF16), mv_ref[0].astype(BF16), ym_ref)


def _attn_sample_kernel(q_ref, kn_ref, vn_ref, qm_ref, ck_ref, cv_ref, mk_ref, mv_ref, bias_ref,
                        ya_ref, ym_ref, nk_ref, nv_ref):
    n = kn_ref.shape[1]
    kk = jnp.concatenate([ck_ref[0], kn_ref[0]], axis=0)
    vv = jnp.concatenate([cv_ref[0], vn_ref[0]], axis=0)
    nk_ref[0] = kk[n:]
    nv_ref[0] = vv[n:]
    _attend(q_ref[0], kk.astype(BF16), vv.astype(BF16), ya_ref, bias_ref)
    _attend(qm_ref[0], mk_ref[0].astype(BF16), mv_ref[0].astype(BF16), ym_ref)


def _rel_bias(table, n_q, n_k):
    period = n_q + n_k
    m = jnp.arange(period)
    offset = jnp.where(m < n_k, m, m - period)
    idx = jnp.clip(BAND - offset, -REL_CLIP, REL_CLIP) + REL_CLIP
    f = table.astype(F32)[:, idx]
    flat = jnp.tile(f, (1, n_q))[:, :n_q * (period - 1)]
    return flat.reshape(N_HEADS, n_q, period - 1)[:, :, :n_k]


def _attn_prompt(zr, mk, mv, table):
    b, s, _ = zr.shape
    tq = ATT_TQ
    bias = _rel_bias(table, tq, 3 * tq)
    col = lambda c: (lambda i, j: (i, j, c))
    prev = lambda c, d: (lambda i, j: (i, jnp.maximum(j - d, 0), c))
    blk = lambda: (1, tq, D_ATT)
    return pl.pallas_call(
        _attn_prompt_kernel,
        grid=(b, s // tq),
        in_specs=[pl.BlockSpec(blk(), col(0)),
                  pl.BlockSpec(blk(), prev(1, 2)), pl.BlockSpec(blk(), prev(1, 1)),
                  pl.BlockSpec(blk(), col(1)),
                  pl.BlockSpec(blk(), prev(2, 2)), pl.BlockSpec(blk(), prev(2, 1)),
                  pl.BlockSpec(blk(), col(2)),
                  pl.BlockSpec(blk(), col(3)),
                  pl.BlockSpec((1, N_MEM, D_MEM), lambda i, j: (i, 0, 0)),
                  pl.BlockSpec((1, N_MEM, D_MEM), lambda i, j: (i, 0, 0)),
                  pl.BlockSpec((N_HEADS, tq, 3 * tq), lambda i, j: (0, 0, 0))],
        out_specs=[pl.BlockSpec(blk(), col(0)), pl.BlockSpec(blk(), col(0))],
        out_shape=[jax.ShapeDtypeStruct((b, s, D_ATT), F32),
                   jax.ShapeDtypeStruct((b, s, D_MEM), F32)],
        scratch_shapes=[pltpu.VMEM((N_HEADS, tq, 3 * tq), F32)],
        compiler_params=_params(2),
        name="attn_prompt",
    )(zr, zr, zr, zr, zr, zr, zr, zr, mk, mv, bias)


def _attn_sample(zr, cache_k, cache_v, mk, mv, table):
    b, n, _ = zr.shape
    w = cache_k.shape[1]
    bias = _rel_bias(table, n, w + n)
    col = lambda c: (lambda i: (i, 0, c))
    blk = (1, n, D_ATT)
    cblk = (1, w, D_ATT)
    mblk = (1, N_MEM, D_MEM)
    row = lambda i: (i, 0, 0)
    return pl.pallas_call(
        _attn_sample_kernel,
        grid=(b,),
        in_specs=[pl.BlockSpec(blk, col(0)), pl.BlockSpec(blk, col(1)), pl.BlockSpec(blk, col(2)),
                  pl.BlockSpec(blk, col(3)),
                  pl.BlockSpec(cblk, row), pl.BlockSpec(cblk, row),
                  pl.BlockSpec(mblk, row), pl.BlockSpec(mblk, row),
                  pl.BlockSpec((N_HEADS, n, w + n), lambda i: (0, 0, 0))],
        out_specs=[pl.BlockSpec(blk, row), pl.BlockSpec(blk, row),
                   pl.BlockSpec(cblk, row), pl.BlockSpec(cblk, row)],
        out_shape=[jax.ShapeDtypeStruct((b, n, D_ATT), F32),
                   jax.ShapeDtypeStruct((b, n, D_MEM), F32),
                   jax.ShapeDtypeStruct((b, w, D_ATT), F32),
                   jax.ShapeDtypeStruct((b, w, D_ATT), F32)],
        compiler_params=_params(1),
        name="attn_sample",
    )(zr, zr, zr, zr, cache_k, cache_v, mk, mv, bias)


def _rms(x, g):
    return x * lax.rsqrt(jnp.mean(jnp.square(x), axis=-1, keepdims=True) + LN_EPS) * g


def _layer_norm(x, g, b):
    mu = jnp.mean(x, axis=-1, keepdims=True)
    xc = x - mu
    var = jnp.mean(jnp.square(xc), axis=-1, keepdims=True)
    return xc * lax.rsqrt(var + LN_EPS) * g + b


def _split_bf16(a):
    hi = a.astype(BF16)
    lo = (a - hi.astype(F32)).astype(BF16)
    return hi, lo


def _merge_kernel(*refs, nb, n_carried):
    (x_ref, ys_ref, ya_ref, ym_ref, gs_ref, ga_ref, gm_ref, wo_ref, l1g_ref, l1b_ref,
     wrt_ref, brt_ref) = refs[:12]
    h_ref, pos_ref, gate_ref, cnt_ref, wo_sc = refs[12 + n_carried:]
    st = x_ref.shape[1]
    tm = nb * st
    n_batch = ys_ref.shape[1] // st

    @pl.when((pl.program_id(0) == 0) & (pl.program_id(1) == 0))
    def _():
        wo_sc[...] = wo_ref[...].astype(BF16)

    x = x_ref[...].reshape(tm, D_MODEL)
    first = pl.program_id(1) * nb
    ys = jnp.concatenate(
        [jnp.concatenate([ys_ref[c, pl.ds(first + i, st, stride=n_batch), :] for c in range(D_SSM // LANES)],
                         axis=1) for i in range(nb)], axis=0)
    ya = ya_ref[...].reshape(tm, D_ATT)
    ym = ym_ref[...].reshape(tm, D_MEM)
    a = _rms(ys, gs_ref[...]).astype(BF16)
    b = _rms(ya, ga_ref[...]).astype(BF16)
    c = _rms(ym, gm_ref[...]).astype(BF16)
    mix = (jnp.dot(a, wo_sc[0:D_SSM, :], preferred_element_type=F32)
           + jnp.dot(b, wo_sc[D_SSM:D_SSM + D_ATT, :], preferred_element_type=F32)
           + jnp.dot(c, wo_sc[D_SSM + D_ATT:, :], preferred_element_type=F32))
    h = _layer_norm(DEEPNORM_ALPHA * x + mix, l1g_ref[...], l1b_ref[...])
    h_ref[...] = h

    h_hi, h_lo = _split_bf16(h)
    w_hi, w_lo = _split_bf16(wrt_ref[...])
    logits = (lax.dot_general(w_hi, h_hi, _NT, preferred_element_type=F32)
              + lax.dot_general(w_hi, h_lo, _NT, preferred_element_type=F32)
              + lax.dot_general(w_lo, h_hi, _NT, preferred_element_type=F32)
              + brt_ref[...])
    erow = lax.broadcasted_iota(I32, (N_EXPERTS, tm), 0).astype(F32)
    tops, picks = [], []
    l = logits
    for k in range(TOP_K):
        m = jnp.max(l, axis=0, keepdims=True)
        e = jnp.min(jnp.where(l == m, erow, float(N_EXPERTS)), axis=0, keepdims=True)
        pick = erow == e
        tops.append(m)
        picks.append(jnp.where(pick, 1.0, 0.0))
        l = jnp.where(pick, -jnp.inf, l)
    ex = [jnp.exp(t - tops[0]) for t in tops]
    den = ex[0] + ex[1] + ex[2] + ex[3]
    for k in range(TOP_K):
        gate_ref[k:k + 1, :] = ex[k] / den

    chosen = picks[0] + picks[1] + picks[2] + picks[3]
    chosen_b = chosen.astype(BF16)
    earlier_tok = (lax.broadcasted_iota(I32, (tm, tm), 0) < lax.broadcasted_iota(I32, (tm, tm), 1))
    within = jnp.dot(chosen_b, jnp.where(earlier_tok, 1.0, 0.0).astype(BF16),
                     preferred_element_type=F32)
    lower_exp = (lax.broadcasted_iota(I32, (N_EXPERTS, N_EXPERTS), 1)
                 < lax.broadcasted_iota(I32, (N_EXPERTS, N_EXPERTS), 0))
    below = jnp.dot(jnp.where(lower_exp, 1.0, 0.0).astype(BF16), chosen_b,
                    preferred_element_type=F32)
    slot = within + jnp.sum(below, axis=1, keepdims=True)
    for k in range(TOP_K):
        pos_ref[k:k + 1, :] = jnp.sum(picks[k] * slot, axis=0, keepdims=True).astype(I32)
    cnt_ref[0] = jnp.sum(chosen, axis=1, keepdims=True)


def _merge(x, ys_tm, ya, ym, lp, nb, st, t_all, tile0, carried=None):
    b, s, _ = x.shape
    tm = nb * st
    assert tm == TOKEN_TM
    n_s = s // st
    tile = lambda j, i: (tile0 + i * n_s + j)
    c2 = lambda j, i: (0, 0)
    row3 = lambda j, i: (i, j, 0)
    vec = lambda a: a.reshape(1, -1)
    carried = () if carried is None else tuple(carried)
    return pl.pallas_call(
        functools.partial(_merge_kernel, nb=nb, n_carried=len(carried)),
        grid=(n_s, b // nb),
        in_specs=[pl.BlockSpec((nb, st, D_MODEL), row3),
                  pl.BlockSpec((D_SSM // LANES, st * b, LANES), lambda j, i: (0, j, 0)),
                  pl.BlockSpec((nb, st, D_ATT), row3),
                  pl.BlockSpec((nb, st, D_MEM), row3),
                  pl.BlockSpec((1, D_SSM), c2), pl.BlockSpec((1, D_ATT), c2),
                  pl.BlockSpec((1, D_MEM), c2),
                  pl.BlockSpec((D_MODEL, D_MODEL), c2),
                  pl.BlockSpec((1, D_MODEL), c2), pl.BlockSpec((1, D_MODEL), c2),
                  pl.BlockSpec((N_EXPERTS, D_MODEL), c2), pl.BlockSpec((N_EXPERTS, 1), c2)]
                 + [pl.BlockSpec(memory_space=pl.ANY)] * len(carried),
        out_specs=[pl.BlockSpec((tm, D_MODEL), lambda j, i: (tile(j, i), 0)),
                   pl.BlockSpec((TOP_K, tm), lambda j, i: (0, tile(j, i))),
                   pl.BlockSpec((TOP_K, tm), lambda j, i: (0, tile(j, i))),
                   pl.BlockSpec((1, N_EXPERTS, 1), lambda j, i: (tile(j, i), 0, 0))],
        out_shape=[jax.ShapeDtypeStruct((t_all, D_MODEL), F32),
                   jax.ShapeDtypeStruct((TOP_K, t_all), I32),
                   jax.ShapeDtypeStruct((TOP_K, t_all), F32),
                   jax.ShapeDtypeStruct((t_all // tm, N_EXPERTS, 1), F32)],
        scratch_shapes=[pltpu.VMEM((D_MODEL, D_MODEL), BF16)],
        input_output_aliases={12 + k: k for k in range(len(carried))},
        compiler_params=_params(2),
        name="merge_router",
    )(x, ys_tm, ya, ym, vec(lp['g_ssm']), vec(lp['g_att']), vec(lp['g_mem']), lp['w_out'],
      vec(lp['ln1_g']), vec(lp['ln1_b']), lp['w_router'].T, lp['b_router'].reshape(N_EXPERTS, 1),
      *carried)


def _rows(start, size):
    return pl.ds(pl.multiple_of(start * ROW_SUBLANES, ROW_SUBLANES), size * ROW_SUBLANES)


def _store_rows(ref, value, row0=0):
    n = value.shape[0]
    for j in range(ROW_SUBLANES):
        ref[pl.ds(row0 * ROW_SUBLANES + j, n, stride=ROW_SUBLANES), :] = value[:, LANES * j:LANES * (j + 1)]


def _load_rows(ref, dtype=F32):
    n = ref.shape[0] // ROW_SUBLANES
    return jnp.concatenate([ref[pl.ds(j, n, stride=ROW_SUBLANES), :].astype(dtype)
                            for j in range(ROW_SUBLANES)], axis=1)


def _for_each_run_piece(n, max_rows, fn):
    for bit in reversed(range(max_rows.bit_length())):
        size = 1 << bit
        start = (n >> (bit + 1)) << (bit + 1)

        @pl.when((n & size) != 0)
        def _(start=start, size=size):
            fn(start, size)


def _dispatch_kernel(n_ref, off_ref, dst_ref, padlo_ref, padn_ref, used_ref,
                     pos_ref, h_ref, xs_hbm, sorted_sc, zero_sc, sem, zsem):
    i = pl.program_id(0)
    tm = h_ref.shape[0]
    n_slots = TOP_K * tm
    n_blocks = xs_hbm.shape[0] // (EXPERT_TM * ROW_SUBLANES)

    @pl.when(i == 0)
    def _():
        zero_sc[...] = jnp.zeros_like(zero_sc)

        def pad_copy(e, start, size):
            return pltpu.make_async_copy(zero_sc.at[_rows(0, size)],
                                         xs_hbm.at[_rows(padlo_ref[e] + start, size)], zsem)

        def tail_copy(blk):
            return pltpu.make_async_copy(zero_sc, xs_hbm.at[_rows(blk * EXPERT_TM, EXPERT_TM)], zsem)

        for e in range(N_EXPERTS):
            _for_each_run_piece(padn_ref[e], EXPERT_TM - 1,
                                lambda start, size, e=e: pad_copy(e, start, size).start())

        def tail_start(blk, carry):
            tail_copy(blk).start()
            return carry

        lax.fori_loop(used_ref[0], n_blocks, tail_start, 0)
        for e in range(N_EXPERTS):
            _for_each_run_piece(padn_ref[e], EXPERT_TM - 1,
                                lambda start, size, e=e: pad_copy(e, start, size).wait())

        def tail_wait(blk, carry):
            tail_copy(blk).wait()
            return carry

        lax.fori_loop(used_ref[0], n_blocks, tail_wait, 0)

    n_tiles = pl.num_programs(0) - 1
    slot = lax.rem(i, N_SORT_BUFS)
    prev_slot = lax.rem(i + N_SORT_BUFS - 1, N_SORT_BUFS)
    buf = sorted_sc.at[slot]
    prev = sorted_sc.at[prev_slot]

    def wait_tile(sl):
        pltpu.make_async_copy(sorted_sc.at[sl], xs_hbm.at[_rows(0, n_slots)], sem.at[sl]).wait()

    @pl.when(i >= N_SORT_BUFS)
    def _():
        wait_tile(slot)

    pos = pos_ref[...]
    hb = h_ref[...].astype(BF16)
    base = jnp.maximum(i - 1, 0) * N_EXPERTS
    rows_c = n_slots // SORT_CHUNKS
    experts_c = N_EXPERTS // SORT_CHUNKS
    for c in range(SORT_CHUNKS):
        for e in range(c * experts_c, (c + 1) * experts_c):
            off = off_ref[base + e]
            dst = dst_ref[base + e]

            def run_start(start, size, off=off, dst=dst):
                pltpu.make_async_copy(prev.at[_rows(off + start, size)],
                                      xs_hbm.at[_rows(dst + start, size)], sem.at[prev_slot]).start()

            _for_each_run_piece(jnp.where(i >= 1, n_ref[base + e], 0), tm, run_start)

        srow = lax.broadcasted_iota(I32, (rows_c, tm), 0) + c * rows_c
        perm = jnp.where(srow == pos[0:1], 1.0,
                         jnp.where(srow == pos[1:2], 1.0,
                                   jnp.where(srow == pos[2:3], 1.0,
                                             jnp.where(srow == pos[3:4], 1.0, 0.0)))).astype(BF16)
        _store_rows(buf, jnp.dot(perm, hb, preferred_element_type=F32), c * rows_c)

    @pl.when(i == n_tiles)
    def _():
        wait_tile(prev_slot)

        @pl.when(i >= 2)
        def _():
            wait_tile(lax.rem(i + N_SORT_BUFS - 2, N_SORT_BUFS))


def _dispatch(run_n, run_off, run_dst, pad_lo, pad_n, n_used, pos, h, cap):
    tm = TOKEN_TM
    n_tiles = h.shape[0] // tm
    grid_spec = pltpu.PrefetchScalarGridSpec(
        num_scalar_prefetch=6,
        grid=(n_tiles + 1,),
        in_specs=[pl.BlockSpec((TOP_K, tm), lambda i, *_: (0, jnp.minimum(i, n_tiles - 1))),
                  pl.BlockSpec((tm, D_MODEL), lambda i, *_: (jnp.minimum(i, n_tiles - 1), 0))],
        out_specs=pl.BlockSpec(memory_space=pl.ANY),
        scratch_shapes=[pltpu.VMEM((N_SORT_BUFS, TOP_K * tm * ROW_SUBLANES, LANES), F32),
                        pltpu.VMEM((EXPERT_TM * ROW_SUBLANES, LANES), F32),
                        pltpu.SemaphoreType.DMA((N_SORT_BUFS,)), pltpu.SemaphoreType.DMA],
    )
    return pl.pallas_call(
        _dispatch_kernel,
        grid_spec=grid_spec,
        out_shape=jax.ShapeDtypeStruct((cap * ROW_SUBLANES, LANES), F32),
        compiler_params=_params(1),
        name="moe_dispatch",
    )(run_n, run_off, run_dst, pad_lo, pad_n, n_used, pos, h)


def _expert_kernel(be_ref, first_ref, ord_ref, seq_ref, used_ref,
                   x_ref, bgu_ref, bd_ref, wgu_hbm, wd_hbm, o_ref,
                   wgu_st, wd_st, wgu_sc, wd_sc, sem):
    i = pl.program_id(0)

    def weight_copies(e):
        return (pltpu.make_async_copy(wgu_hbm.at[e], wgu_st, sem.at[0]),
                pltpu.make_async_copy(wd_hbm.at[e], wd_st, sem.at[1]))

    @pl.when(i == 0)
    def _():
        for c in weight_copies(seq_ref[0]):
            c.start()

    @pl.when(i < used_ref[0])
    def _():
        @pl.when(first_ref[i] == 1)
        def _():
            k = ord_ref[i]
            for c in weight_copies(seq_ref[k]):
                c.wait()
            wgu_sc[...] = wgu_st[...].astype(BF16)
            wd_sc[...] = wd_st[...].astype(BF16)

            @pl.when(k + 1 < used_ref[1])
            def _():
                for c in weight_copies(seq_ref[k + 1]):
                    c.start()

        gu = jnp.dot(_load_rows(x_ref, BF16), wgu_sc[...], preferred_element_type=F32) + bgu_ref[0]
        gate = jnp.minimum(gu[:, :D_FF], SWIGLU_LIMIT)
        lin = jnp.clip(gu[:, D_FF:], -SWIGLU_LIMIT, SWIGLU_LIMIT)
        act = gate * jax.nn.sigmoid(SWIGLU_ALPHA * gate) * (lin + 1.0)
        _store_rows(o_ref, jnp.dot(act.astype(BF16), wd_sc[...], preferred_element_type=F32) + bd_ref[0])

    @pl.when(i >= used_ref[0])
    def _():
        o_ref[...] = jnp.zeros_like(o_ref)


def _experts(block_expert, block_first, block_ord, expert_seq, n_used, xs, lp):
    tm = EXPERT_TM * ROW_SUBLANES
    grid_spec = pltpu.PrefetchScalarGridSpec(
        num_scalar_prefetch=5,
        grid=(xs.shape[0] // tm,),
        in_specs=[pl.BlockSpec((tm, LANES), lambda i, be, *_: (i, 0)),
                  pl.BlockSpec((1, 1, 2 * D_FF), lambda i, be, *_: (be[i], 0, 0)),
                  pl.BlockSpec((1, 1, D_MODEL), lambda i, be, *_: (be[i], 0, 0)),
                  pl.BlockSpec(memory_space=pl.ANY),
                  pl.BlockSpec(memory_space=pl.ANY)],
        out_specs=pl.BlockSpec((tm, LANES), lambda i, be, *_: (i, 0)),
        scratch_shapes=[pltpu.VMEM((D_MODEL, 2 * D_FF), F32), pltpu.VMEM((D_FF, D_MODEL), F32),
                        pltpu.VMEM((D_MODEL, 2 * D_FF), BF16), pltpu.VMEM((D_FF, D_MODEL), BF16),
                        pltpu.SemaphoreType.DMA((2,))],
    )
    return pl.pallas_call(
        _expert_kernel,
        grid_spec=grid_spec,
        out_shape=jax.ShapeDtypeStruct(xs.shape, F32),
        compiler_params=_params(1),
        name="moe_experts",
    )(block_expert, block_first, block_ord, expert_seq, n_used, xs,
      lp['b_gu'].reshape(N_EXPERTS, 1, 2 * D_FF), lp['b_down'].reshape(N_EXPERTS, 1, D_MODEL),
      lp['w_gu'], lp['w_down'])


def _combine_kernel(n_ref, off_ref, dst_ref, pos_ref, gate_ref, h_ref, ys_hbm, g_ref, b_ref,
                    y1_ref, y2_ref, sorted_sc, w_sc, sem, *, n_first):
    i = pl.program_id(0)
    n_tiles = pl.num_programs(0) - 1
    tm = h_ref.shape[0]
    n_slots = TOP_K * tm
    slot = lax.rem(i, 2)
    buf = sorted_sc.at[slot]
    pos = pos_ref[...]
    gates = gate_ref[...]
    rows_per = tm // N_EXPERTS
    base = jnp.minimum(i, n_tiles - 1) * N_EXPERTS
    for e in range(N_EXPERTS):
        off = off_ref[base + e]
        dst = dst_ref[base + e]

        def run_start(start, size, off=off, dst=dst):
            pltpu.make_async_copy(ys_hbm.at[_rows(dst + start, size)],
                                  buf.at[_rows(off + start, size)], sem.at[slot]).start()

        _for_each_run_piece(jnp.where(i < n_tiles, n_ref[base + e], 0), tm, run_start)

        r = slice(e * rows_per, (e + 1) * rows_per)
        scol = lax.broadcasted_iota(I32, (rows_per, n_slots), 1)
        w_sc[r, :] = jnp.where(
            scol == pos[r, 0:1], gates[r, 0:1],
            jnp.where(scol == pos[r, 1:2], gates[r, 1:2],
                      jnp.where(scol == pos[r, 2:3], gates[r, 2:3],
                                jnp.where(scol == pos[r, 3:4], gates[r, 3:4], 0.0)))).astype(BF16)

    @pl.when(i >= 1)
    def _():
        done = sorted_sc.at[1 - slot]
        pltpu.make_async_copy(ys_hbm.at[_rows(0, n_slots)], done, sem.at[1 - slot]).wait()

        f = jnp.dot(w_sc[...], _load_rows(done, BF16), preferred_element_type=F32)
        y = _layer_norm(DEEPNORM_ALPHA * h_ref[...] + f, g_ref[...], b_ref[...])

        @pl.when(i - 1 < n_first)
        def _():
            y1_ref[...] = y

        @pl.when(i - 1 >= n_first)
        def _():
            y2_ref[...] = y


def _combine(run_n, run_off, run_dst, pos_t, gates_t, h, ys, lp, t_first):
    t = h.shape[0]
    tm = TOKEN_TM
    n_first = t_first // tm
    n_rest = (t - t_first) // tm
    c2 = lambda i, *_: (0, 0)
    done = lambda i, *_: (jnp.maximum(i - 1, 0), 0)
    done1 = lambda i, *_: (jnp.clip(i - 1, 0, n_first - 1), 0)
    done2 = lambda i, *_: (jnp.clip(i - 1 - n_first, 0, n_rest - 1), 0)
    grid_spec = pltpu.PrefetchScalarGridSpec(
        num_scalar_prefetch=3,
        grid=(t // tm + 1,),
        in_specs=[pl.BlockSpec((tm, TOP_K), done),
                  pl.BlockSpec((tm, TOP_K), done),
                  pl.BlockSpec((tm, D_MODEL), done),
                  pl.BlockSpec(memory_space=pl.ANY),
                  pl.BlockSpec((1, D_MODEL), c2), pl.BlockSpec((1, D_MODEL), c2)],
        out_specs=[pl.BlockSpec((tm, D_MODEL), done1), pl.BlockSpec((tm, D_MODEL), done2)],
        scratch_shapes=[pltpu.VMEM((2, TOP_K * tm * ROW_SUBLANES, LANES), F32),
                        pltpu.VMEM((tm, TOP_K * tm), BF16),
                        pltpu.SemaphoreType.DMA((2,))],
    )
    return pl.pallas_call(
        functools.partial(_combine_kernel, n_first=n_first),
        grid_spec=grid_spec,
        out_shape=[jax.ShapeDtypeStruct((t_first, D_MODEL), F32),
                   jax.ShapeDtypeStruct((t - t_first, D_MODEL), F32)],
        compiler_params=_params(1),
        name="moe_combine",
    )(run_n, run_off, run_dst, pos_t, gates_t, h, ys,
      lp['ln2_g'].reshape(1, D_MODEL), lp['ln2_b'].reshape(1, D_MODEL))


def _moe_and_norm(h, pos, gates, tile_counts, lp, t_first):
    t = h.shape[0]
    te = EXPERT_TM
    n_tiles = t // TOKEN_TM
    n_blocks = (t * TOP_K) // te + N_EXPERTS
    cap = n_blocks * te
    cnt = tile_counts.reshape(n_tiles, N_EXPERTS).astype(I32)
    counts = jnp.sum(cnt, axis=0)
    padded = (counts + te - 1) // te * te
    pad_ends = jnp.cumsum(padded)
    pad_starts = pad_ends - padded
    run_dst = pad_starts[None, :] + jnp.cumsum(cnt, axis=0) - cnt
    run_off = jnp.cumsum(cnt, axis=1) - cnt
    blk_start = jnp.arange(n_blocks, dtype=I32) * te
    expert_of = lambda slot_idx: jnp.minimum(jnp.sum(slot_idx[..., None] >= pad_ends, axis=-1), N_EXPERTS - 1)
    total = pad_ends[-1]
    be = jnp.where(blk_start < total, expert_of(blk_start), expert_of(jnp.maximum(total - 1, 0))).astype(I32)
    is_e = be[:, None] == jnp.arange(N_EXPERTS, dtype=I32)[None, :]
    pick = lambda table: jnp.sum(jnp.where(is_e, table[None, :], 0), axis=1)
    in_use = counts > 0
    ordinal = jnp.cumsum(in_use.astype(I32)) - 1
    rank = jnp.arange(N_EXPERTS, dtype=I32)
    expert_seq = jnp.sum(jnp.where(in_use[None, :] & (ordinal[None, :] == rank[:, None]), rank[None, :], 0),
                         axis=1)
    block_ord = pick(ordinal)
    block_first = (blk_start == pick(pad_starts)) & (blk_start < total)
    used = jnp.stack([total // te, jnp.sum(in_use.astype(I32))]).astype(I32)
    flat = lambda a: a.reshape(-1).astype(I32)
    xs = _dispatch(flat(cnt), flat(run_off), flat(run_dst), flat(pad_starts + counts),
                   flat(padded - counts), used, pos, h, cap)
    ys = _experts(be, flat(block_first), flat(block_ord), expert_seq, used, xs, lp)
    return _combine(flat(cnt), flat(run_off), flat(run_dst), pos.T, gates.T, h, ys, lp, t_first)


def kernel(x_prompt, x_sample, cache_attn_k, cache_attn_v, cache_mem_k, cache_mem_v, state_ssm_re, state_ssm_im, mem_prompt, w_in, lam_re, lam_im, log_dt, ssm_b_re, ssm_b_im, ssm_c_re, ssm_c_im, ssm_d, w_glu, b_glu, rel_bias, w_mem_kv, g_ssm, g_att, g_mem, w_out, ln1_g, ln1_b, w_router, b_router, w_gu, b_gu, w_down, b_down, ln2_g, ln2_b):
    assert w_in.shape[0] == 1, "single-layer step"
    lp = dict(w_in=w_in[0], lam_re=lam_re[0], lam_im=lam_im[0], log_dt=log_dt[0],
              ssm_b_re=ssm_b_re[0], ssm_b_im=ssm_b_im[0], ssm_c_re=ssm_c_re[0], ssm_c_im=ssm_c_im[0],
              ssm_d=ssm_d[0], w_glu=w_glu[0], b_glu=b_glu[0], rel_bias=rel_bias[0],
              w_mem_kv=w_mem_kv[0], g_ssm=g_ssm[0], g_att=g_att[0], g_mem=g_mem[0], w_out=w_out[0],
              ln1_g=ln1_g[0], ln1_b=ln1_b[0], w_router=w_router[0], b_router=b_router[0],
              w_gu=w_gu[0], b_gu=b_gu[0], w_down=w_down[0], b_down=b_down[0],
              ln2_g=ln2_g[0], ln2_b=ln2_b[0])

    bp, sp, _ = x_prompt.shape
    bs, ss, _ = x_sample.shape
    t_p = bp * sp
    t_all = t_p + bs * ss
    heads = lambda a: a.reshape(a.shape[0], a.shape[1], N_HEADS, HEAD_DIM)
    state = lambda a: a.reshape(a.shape[0], N_GROUPS, SSM_STATE)

    u_p, zr = _in_proj(x_prompt, lp['w_in'], min(128, sp))
    mk, mv = _mem_kv(mem_prompt, lp['w_mem_kv'])
    ya, ym = _attn_prompt(zr, mk, mv, lp['rel_bias'])
    zeros = jnp.zeros((bp, D_STATE), F32)
    ys_p, sr_p, si_p = _ssm(u_p, zeros, zeros, lp, bp)
    merged = _merge(x_prompt, ys_p, ya, ym, lp, 1, TOKEN_TM, t_all, 0)
    w = min(BAND, sp)
    k_p = heads(zr[:, sp - w:, D_ATT:2 * D_ATT])
    v_p = heads(zr[:, sp - w:, 2 * D_ATT:3 * D_ATT])

    wc = cache_attn_k.shape[2]
    u_s, zr_s = _in_proj(x_sample, lp['w_in'], ss)
    ya_s, ym_s, nk, nv = _attn_sample(
        zr_s, cache_attn_k[0].reshape(bs, wc, D_ATT), cache_attn_v[0].reshape(bs, wc, D_ATT),
        cache_mem_k[0].reshape(bs, N_MEM, D_MEM), cache_mem_v[0].reshape(bs, N_MEM, D_MEM),
        lp['rel_bias'])
    ys_s, sr_s, si_s = _ssm(u_s, state_ssm_re[0], state_ssm_im[0], lp, bs)
    merged = _merge(x_sample, ys_s, ya_s, ym_s, lp, TOKEN_TM // ss, ss, t_all, t_p // TOKEN_TM,
                    carried=merged)

    y_p, y_s = _moe_and_norm(*merged, lp, t_p)

    return (y_p.reshape(bp, sp, D_MODEL), y_s.reshape(bs, ss, D_MODEL),
            k_p[None], v_p[None], heads(mk)[None], heads(mv)[None], state(sr_p)[None], state(si_p)[None],
            heads(nk)[None], heads(nv)[None], state(sr_s)[None], state(si_s)[None])
```

```python
import functools

import jax
import jax.numpy as jnp
from jax import lax
from jax.experimental import pallas as pl
from jax.experimental.pallas import tpu as pltpu

F32 = jnp.float32
BF16 = jnp.bfloat16
I32 = jnp.int32

D_MODEL = 1024
D_SSM = 512
D_ATT = 256
D_MEM = 256
D_IN = D_SSM + 3 * D_ATT + D_MEM
D_REST = D_IN - D_SSM
HEAD_DIM = 64
N_HEADS = 4
N_GROUPS = 32
SSM_GROUP = 16
SSM_STATE = 64
D_STATE = N_GROUPS * SSM_STATE
CHUNK = 64
N_PREV_CHUNKS = 8
BAND = N_PREV_CHUNKS * CHUNK
REL_CLIP = 128
N_MEM = 256
N_EXPERTS = 32
TOP_K = 4
D_FF = D_MODEL
SWIGLU_LIMIT = 7.0
SWIGLU_ALPHA = 1.702
LN_EPS = 1e-5
NEG_INF = -1e30
ATT_SCALE = HEAD_DIM ** -0.5
DEEPNORM_ALPHA = 2.0 ** 0.25

V7X_VMEM_LIMIT = 56 * 1024 * 1024
ATT_TQ = 4 * CHUNK
SCAN_LANES = 1024
SCAN_ROWS = 1024
LANES = 128
ROW_SUBLANES = D_MODEL // LANES
EXPERT_TM = 512
TOKEN_TM = 512
N_SORT_BUFS = 3
SORT_CHUNKS = 8

_NT = (((1,), (1,)), ((), ()))


def _params(n_axes, vmem=V7X_VMEM_LIMIT):
    return pltpu.CompilerParams(dimension_semantics=("arbitrary",) * n_axes,
                                vmem_limit_bytes=vmem)


def _in_proj_kernel(x_ref, w_ref, u_ref, z_ref, zb_ref, wb_ref):
    @pl.when(pl.program_id(0) == 0)
    def _():
        wb_ref[...] = w_ref[...].astype(BF16)

    nb, ts, _ = x_ref.shape
    x = x_ref[...].reshape(nb * ts, D_MODEL).astype(BF16)
    z = jnp.dot(x, wb_ref[...], preferred_element_type=F32)
    for b in range(nb):
        for c in range(D_SSM // LANES):
            u_ref[c, pl.ds(b, ts, stride=nb), :] = z[b * ts:(b + 1) * ts, LANES * c:LANES * (c + 1)]
    zr = z[:, D_SSM:]
    z_ref[...] = zr.reshape(nb, ts, D_REST)
    zb = jnp.concatenate([zr[:, :D_ATT] * ATT_SCALE, zr[:, D_ATT:3 * D_ATT], zr[:, 3 * D_ATT:] * ATT_SCALE],
                         axis=1)
    zb_ref[...] = zb.astype(BF16).reshape(nb, ts, D_REST)


def _in_proj(x, w_in, ts):
    b, s, _ = x.shape
    return pl.pallas_call(
        _in_proj_kernel,
        grid=(s // ts,),
        in_specs=[pl.BlockSpec((b, ts, D_MODEL), lambda j: (0, j, 0)),
                  pl.BlockSpec((D_MODEL, D_IN), lambda j: (0, 0))],
        out_specs=[pl.BlockSpec((D_SSM // LANES, ts * b, LANES), lambda j: (0, j, 0)),
                   pl.BlockSpec((b, ts, D_REST), lambda j: (0, j, 0)),
                   pl.BlockSpec((b, ts, D_REST), lambda j: (0, j, 0))],
        out_shape=[jax.ShapeDtypeStruct((D_SSM // LANES, s * b, LANES), F32),
                   jax.ShapeDtypeStruct((b, s, D_REST), F32),
                   jax.ShapeDtypeStruct((b, s, D_REST), BF16)],
        scratch_shapes=[pltpu.VMEM((D_MODEL, D_IN), BF16)],
        compiler_params=_params(1),
        name="in_proj",
    )(x, w_in)


def _mem_kv_kernel(m_ref, w_ref, mk_ref, mv_ref):
    kv = jnp.dot(m_ref[0].astype(BF16), w_ref[...].astype(BF16), preferred_element_type=F32)
    mk_ref[0] = kv[:, :D_MEM]
    mv_ref[0] = kv[:, D_MEM:]


def _mem_kv(mem, w_mem_kv):
    b = mem.shape[0]
    return pl.pallas_call(
        _mem_kv_kernel,
        grid=(b,),
        in_specs=[pl.BlockSpec((1, N_MEM, D_MODEL), lambda i: (i, 0, 0)),
                  pl.BlockSpec((D_MODEL, 2 * D_MEM), lambda i: (0, 0))],
        out_specs=[pl.BlockSpec((1, N_MEM, D_MEM), lambda i: (i, 0, 0)),
                   pl.BlockSpec((1, N_MEM, D_MEM), lambda i: (i, 0, 0))],
        out_shape=[jax.ShapeDtypeStruct((b, N_MEM, D_MEM), F32)] * 2,
        compiler_params=_params(1),
        name="mem_kv",
    )(mem, w_mem_kv)


def _ssm_kernel(u_ref, h0r_ref, h0i_ref, lr_ref, li_ref, ldt_ref, bre_ref, bim_ref,
                cre_ref, cim_ref, d_ref, wg_ref, bg_ref,
                y_ref, sr_ref, si_ref,
                a_sc, bbr_sc, bbi_sc, cr_sc, ci_sc, wg_sc, str_sc, sti_sc, xr_sc, xi_sc,
                *, n_batch):
    n_rows = u_ref.shape[1]
    n_steps = n_rows // n_batch

    @pl.when(pl.program_id(0) == 0)
    def _():
        lr = lr_ref[...]
        li = li_ref[...]
        dt = jnp.exp(ldt_ref[...])
        mag = jnp.exp(lr * dt)
        ar = mag * jnp.cos(li * dt)
        ai = mag * jnp.sin(li * dt)
        den = lr * lr + li * li
        fr = ((ar - 1.0) * lr + ai * li) / den
        fi = (ai * lr - (ar - 1.0) * li) / den
        a_sc[0:1, :] = ar
        a_sc[1:2, :] = ai
        for j in range(4):
            frj = fr[:, 512 * j:512 * (j + 1)]
            fij = fi[:, 512 * j:512 * (j + 1)]
            bbr_sc[j] = (frj * bre_ref[j] - fij * bim_ref[j]).astype(BF16)
            bbi_sc[j] = (frj * bim_ref[j] + fij * bre_ref[j]).astype(BF16)
            cr_sc[j] = cre_ref[j].astype(BF16)
            ci_sc[j] = cim_ref[j].astype(BF16)
        wg_sc[...] = wg_ref[...].astype(BF16)
        str_sc[...] = h0r_ref[...]
        sti_sc[...] = h0i_ref[...]

    for j in range(4):
        uc = u_ref[j].astype(BF16)
        xr_sc[:, 512 * j:512 * (j + 1)] = jnp.dot(uc, bbr_sc[j], preferred_element_type=F32)
        xi_sc[:, 512 * j:512 * (j + 1)] = jnp.dot(uc, bbi_sc[j], preferred_element_type=F32)

    for c in range(D_STATE // SCAN_LANES):
        lo = c * SCAN_LANES
        ar = jnp.broadcast_to(a_sc[0:1, lo:lo + SCAN_LANES], (n_batch, SCAN_LANES))
        ai = jnp.broadcast_to(a_sc[1:2, lo:lo + SCAN_LANES], (n_batch, SCAN_LANES))

        def step(t, carry, lo=lo, ar=ar, ai=ai):
            sr, si = carry
            r0 = pl.multiple_of(t * n_batch, n_batch)
            nr = ar * sr - ai * si + xr_sc[pl.ds(r0, n_batch), lo:lo + SCAN_LANES]
            ni = ar * si + ai * sr + xi_sc[pl.ds(r0, n_batch), lo:lo + SCAN_LANES]
            xr_sc[pl.ds(r0, n_batch), lo:lo + SCAN_LANES] = nr
            xi_sc[pl.ds(r0, n_batch), lo:lo + SCAN_LANES] = ni
            return nr, ni

        sr, si = lax.fori_loop(0, n_steps, step,
                               (str_sc[:, lo:lo + SCAN_LANES], sti_sc[:, lo:lo + SCAN_LANES]),
                               unroll=4)
        str_sc[:, lo:lo + SCAN_LANES] = sr
        sti_sc[:, lo:lo + SCAN_LANES] = si

    pieces = []
    for j in range(4):
        xr = xr_sc[:, 512 * j:512 * (j + 1)].astype(BF16)
        xi = xi_sc[:, 512 * j:512 * (j + 1)].astype(BF16)
        pieces.append(jnp.dot(xr, cr_sc[j], preferred_element_type=F32)
                      - jnp.dot(xi, ci_sc[j], preferred_element_type=F32))
    u = jnp.concatenate([u_ref[j] for j in range(4)], axis=1)
    y = jnp.concatenate(pieces, axis=1) + d_ref[...] * u
    y = jax.nn.gelu(y)
    z = jnp.dot(y.astype(BF16), wg_sc[...], preferred_element_type=F32) + bg_ref[...]
    out = z[:, :D_SSM] * jax.nn.sigmoid(z[:, D_SSM:])
    for j in range(D_SSM // LANES):
        y_ref[j] = out[:, LANES * j:LANES * (j + 1)]
    sr_ref[...] = str_sc[...]
    si_ref[...] = sti_sc[...]


def _block_diag_b(b):
    bt = b.transpose(0, 2, 1).reshape(4, 8, SSM_GROUP, SSM_STATE)
    same = jnp.eye(8, dtype=bool)[None, :, None, :, None]
    t = jnp.where(same, bt[:, :, :, None, :], 0.0)
    return t.reshape(4, 8 * SSM_GROUP, 8 * SSM_STATE)


def _block_diag_c(c):
    ct = c.transpose(0, 2, 1).reshape(4, 8, SSM_STATE, SSM_GROUP)
    same = jnp.eye(8, dtype=bool)[None, :, None, :, None]
    t = jnp.where(same, ct[:, :, :, None, :], 0.0)
    return t.reshape(4, 8 * SSM_STATE, 8 * SSM_GROUP)


def _ssm(u_rows, h0_re, h0_im, lp, n_batch):
    rows = u_rows.shape[1]
    planes = D_SSM // LANES
    tr = min(SCAN_ROWS, rows)
    flat = lambda a: a.reshape(1, D_STATE)
    ldt = jnp.repeat(lp['log_dt'], SSM_STATE).reshape(1, D_STATE)
    const2 = lambda i: (0, 0)
    const3 = lambda i: (0, 0, 0)
    y, sr, si = pl.pallas_call(
        functools.partial(_ssm_kernel, n_batch=n_batch),
        grid=(rows // tr,),
        in_specs=[pl.BlockSpec((planes, tr, LANES), lambda i: (0, i, 0)),
                  pl.BlockSpec((n_batch, D_STATE), const2),
                  pl.BlockSpec((n_batch, D_STATE), const2),
                  pl.BlockSpec((1, D_STATE), const2),
                  pl.BlockSpec((1, D_STATE), const2),
                  pl.BlockSpec((1, D_STATE), const2),
                  pl.BlockSpec((4, 128, 512), const3),
                  pl.BlockSpec((4, 128, 512), const3),
                  pl.BlockSpec((4, 512, 128), const3),
                  pl.BlockSpec((4, 512, 128), const3),
                  pl.BlockSpec((1, D_SSM), const2),
                  pl.BlockSpec((D_SSM, 2 * D_SSM), const2),
                  pl.BlockSpec((1, 2 * D_SSM), const2)],
        out_specs=[pl.BlockSpec((planes, tr, LANES), lambda i: (0, i, 0)),
                   pl.BlockSpec((n_batch, D_STATE), const2),
                   pl.BlockSpec((n_batch, D_STATE), const2)],
        out_shape=[jax.ShapeDtypeStruct((planes, rows, LANES), F32),
                   jax.ShapeDtypeStruct((n_batch, D_STATE), F32),
                   jax.ShapeDtypeStruct((n_batch, D_STATE), F32)],
        scratch_shapes=[pltpu.VMEM((2, D_STATE), F32),
                        pltpu.VMEM((4, 128, 512), BF16), pltpu.VMEM((4, 128, 512), BF16),
                        pltpu.VMEM((4, 512, 128), BF16), pltpu.VMEM((4, 512, 128), BF16),
                        pltpu.VMEM((D_SSM, 2 * D_SSM), BF16),
                        pltpu.VMEM((n_batch, D_STATE), F32), pltpu.VMEM((n_batch, D_STATE), F32),
                        pltpu.VMEM((tr, D_STATE), F32), pltpu.VMEM((tr, D_STATE), F32)],
        compiler_params=_params(1),
        name="ssm",
    )(u_rows, h0_re.reshape(n_batch, D_STATE), h0_im.reshape(n_batch, D_STATE),
      flat(lp['lam_re']), flat(lp['lam_im']), ldt,
      _block_diag_b(lp['ssm_b_re']), _block_diag_b(lp['ssm_b_im']),
      _block_diag_c(lp['ssm_c_re']), _block_diag_c(lp['ssm_c_im']),
      lp['ssm_d'].reshape(1, D_SSM), lp['w_glu'], lp['b_glu'].reshape(1, 2 * D_SSM))
    return y, sr, si


def _softmax_pv(s, v):
    m = jnp.max(s, axis=-1, keepdims=True)
    p = jnp.exp(s - m)
    l = jnp.sum(p, axis=-1, keepdims=True)
    return jnp.dot(p.astype(BF16), v, preferred_element_type=F32) / l


def _attend(qb, k, v, out_ref, row0=0, bias_ref=None, valid=None):
    tq = qb.shape[0]
    for h in range(N_HEADS):
        sl = slice(HEAD_DIM * h, HEAD_DIM * (h + 1))
        s = lax.dot_general(qb[:, sl], k[:, sl], _NT, preferred_element_type=F32)
        if bias_ref is not None:
            s = s + bias_ref[h]
        if valid is not None:
            s = jnp.where(valid, s, NEG_INF)
        out_ref[0, row0:row0 + tq, sl] = _softmax_pv(s, v[:, sl])


def _attn_prompt_kernel(q_ref, kp_ref, kc_ref, vp_ref, vc_ref, qm_ref, mk_ref, mv_ref, bias_ref,
                        ya_ref, ym_ref, bias_sc):
    tq = ATT_TQ

    @pl.when((pl.program_id(0) == 0) & (pl.program_id(1) == 0))
    def _():
        q_chunk = lax.broadcasted_iota(I32, (tq, 3 * tq), 0) // CHUNK
        k_chunk = lax.broadcasted_iota(I32, (tq, 3 * tq), 1) // CHUNK
        ahead = k_chunk - q_chunk
        for h in range(N_HEADS):
            bias_sc[h] = jnp.where(ahead >= 0, jnp.where(ahead <= N_PREV_CHUNKS, bias_ref[h], NEG_INF),
                                   NEG_INF)

    k = jnp.concatenate([kp_ref[0], kc_ref[0]], axis=0)
    v = jnp.concatenate([vp_ref[0], vc_ref[0]], axis=0)
    for half in range(2):
        first_key = (2 * pl.program_id(1) - 2 + half) * tq
        kpos = first_key + lax.broadcasted_iota(I32, (1, 3 * tq), 1)
        _attend(q_ref[0, half * tq:(half + 1) * tq, :], k[half * tq:(half + 3) * tq],
                v[half * tq:(half + 3) * tq], ya_ref, half * tq, bias_sc, kpos >= 0)
    _attend(qm_ref[0], mk_ref[0].astype(BF16), mv_ref[0].astype(BF16), ym_ref)


def _attn_sample_kernel(q_ref, kn_ref, vn_ref, qm_ref, ck_ref, cv_ref, mk_ref, mv_ref, bias_ref,
                        ya_ref, ym_ref, nk_ref, nv_ref):
    n = kn_ref.shape[1]
    kk = jnp.concatenate([ck_ref[0], kn_ref[0]], axis=0)
    vv = jnp.concatenate([cv_ref[0], vn_ref[0]], axis=0)
    nk_ref[0] = kk[n:]
    nv_ref[0] = vv[n:]
    _attend(q_ref[0], kk.astype(BF16), vv.astype(BF16), ya_ref, 0, bias_ref)
    _attend(qm_ref[0], mk_ref[0].astype(BF16), mv_ref[0].astype(BF16), ym_ref)


def _rel_bias(table, n_q, n_k):
    period = n_q + n_k
    m = jnp.arange(period)
    offset = jnp.where(m < n_k, m, m - period)
    idx = jnp.clip(BAND - offset, -REL_CLIP, REL_CLIP) + REL_CLIP
    f = table.astype(F32)[:, idx]
    flat = jnp.tile(f, (1, n_q))[:, :n_q * (period - 1)]
    return flat.reshape(N_HEADS, n_q, period - 1)[:, :, :n_k]


def _attn_prompt(zb, mk, mv, table):
    b, s, _ = zb.shape
    tq = ATT_TQ
    bias = _rel_bias(table, tq, 3 * tq)
    col = lambda c: (lambda i, j: (i, j, c))
    prev = lambda c: (lambda i, j: (i, jnp.maximum(j - 1, 0), c))
    blk = (1, 2 * tq, D_ATT)
    return pl.pallas_call(
        _attn_prompt_kernel,
        grid=(b, s // (2 * tq)),
        in_specs=[pl.BlockSpec(blk, col(0)),
                  pl.BlockSpec(blk, prev(1)), pl.BlockSpec(blk, col(1)),
                  pl.BlockSpec(blk, prev(2)), pl.BlockSpec(blk, col(2)),
                  pl.BlockSpec(blk, col(3)),
                  pl.BlockSpec((1, N_MEM, D_MEM), lambda i, j: (i, 0, 0)),
                  pl.BlockSpec((1, N_MEM, D_MEM), lambda i, j: (i, 0, 0)),
                  pl.BlockSpec((N_HEADS, tq, 3 * tq), lambda i, j: (0, 0, 0))],
        out_specs=[pl.BlockSpec(blk, col(0)), pl.BlockSpec(blk, col(0))],
        out_shape=[jax.ShapeDtypeStruct((b, s, D_ATT), F32),
                   jax.ShapeDtypeStruct((b, s, D_MEM), F32)],
        scratch_shapes=[pltpu.VMEM((N_HEADS, tq, 3 * tq), F32)],
        compiler_params=_params(2),
        name="attn_prompt",
    )(zb, zb, zb, zb, zb, zb, mk, mv, bias)


def _attn_sample(zr, zb, cache_k, cache_v, mk, mv, table):
    b, n, _ = zr.shape
    w = cache_k.shape[1]
    bias = _rel_bias(table, n, w + n)
    col = lambda c: (lambda i: (i, 0, c))
    blk = (1, n, D_ATT)
    cblk = (1, w, D_ATT)
    mblk = (1, N_MEM, D_MEM)
    row = lambda i: (i, 0, 0)
    return pl.pallas_call(
        _attn_sample_kernel,
        grid=(b,),
        in_specs=[pl.BlockSpec(blk, col(0)), pl.BlockSpec(blk, col(1)), pl.BlockSpec(blk, col(2)),
                  pl.BlockSpec(blk, col(3)),
                  pl.BlockSpec(cblk, row), pl.BlockSpec(cblk, row),
                  pl.BlockSpec(mblk, row), pl.BlockSpec(mblk, row),
                  pl.BlockSpec((N_HEADS, n, w + n), lambda i: (0, 0, 0))],
        out_specs=[pl.BlockSpec(blk, row), pl.BlockSpec(blk, row),
                   pl.BlockSpec(cblk, row), pl.BlockSpec(cblk, row)],
        out_shape=[jax.ShapeDtypeStruct((b, n, D_ATT), F32),
                   jax.ShapeDtypeStruct((b, n, D_MEM), F32),
                   jax.ShapeDtypeStruct((b, w, D_ATT), F32),
                   jax.ShapeDtypeStruct((b, w, D_ATT), F32)],
        compiler_params=_params(1),
        name="attn_sample",
    )(zb, zr, zr, zb, cache_k, cache_v, mk, mv, bias)


def _rms(x, g):
    return x * lax.rsqrt(jnp.mean(jnp.square(x), axis=-1, keepdims=True) + LN_EPS) * g


def _layer_norm(x, g, b):
    mu = jnp.mean(x, axis=-1, keepdims=True)
    xc = x - mu
    var = jnp.mean(jnp.square(xc), axis=-1, keepdims=True)
    return xc * lax.rsqrt(var + LN_EPS) * g + b


def _split_bf16(a):
    hi = a.astype(BF16)
    lo = (a - hi.astype(F32)).astype(BF16)
    return hi, lo


def _merge_kernel(*refs, nb, n_carried):
    (x_ref, ys_ref, ya_ref, ym_ref, gs_ref, ga_ref, gm_ref, wo_ref, l1g_ref, l1b_ref,
     wrt_ref, brt_ref) = refs[:12]
    h_ref, pos_ref, gate_ref, cnt_ref, wo_sc = refs[12 + n_carried:]
    st = x_ref.shape[1]
    tm = nb * st
    n_batch = ys_ref.shape[1] // st

    @pl.when((pl.program_id(0) == 0) & (pl.program_id(1) == 0))
    def _():
        wo_sc[...] = wo_ref[...].astype(BF16)

    x = x_ref[...].reshape(tm, D_MODEL)
    first = pl.program_id(1) * nb
    ys = jnp.concatenate(
        [jnp.concatenate([ys_ref[c, pl.ds(first + i, st, stride=n_batch), :] for c in range(D_SSM // LANES)],
                         axis=1) for i in range(nb)], axis=0)
    ya = ya_ref[...].reshape(tm, D_ATT)
    ym = ym_ref[...].reshape(tm, D_MEM)
    a = _rms(ys, gs_ref[...]).astype(BF16)
    b = _rms(ya, ga_ref[...]).astype(BF16)
    c = _rms(ym, gm_ref[...]).astype(BF16)
    mix = (jnp.dot(a, wo_sc[0:D_SSM, :], preferred_element_type=F32)
           + jnp.dot(b, wo_sc[D_SSM:D_SSM + D_ATT, :], preferred_element_type=F32)
           + jnp.dot(c, wo_sc[D_SSM + D_ATT:, :], preferred_element_type=F32))
    h = _layer_norm(DEEPNORM_ALPHA * x + mix, l1g_ref[...], l1b_ref[...])
    h_ref[...] = h

    h_hi, h_lo = _split_bf16(h)
    w_hi, w_lo = _split_bf16(wrt_ref[...])
    logits = (lax.dot_general(w_hi, h_hi, _NT, preferred_element_type=F32)
              + lax.dot_general(w_hi, h_lo, _NT, preferred_element_type=F32)
              + lax.dot_general(w_lo, h_hi, _NT, preferred_element_type=F32)
              + brt_ref[...])
    erow = lax.broadcasted_iota(I32, (N_EXPERTS, tm), 0).astype(F32)
    tops, picks = [], []
    l = logits
    for k in range(TOP_K):
        m = jnp.max(l, axis=0, keepdims=True)
        e = jnp.min(jnp.where(l == m, erow, float(N_EXPERTS)), axis=0, keepdims=True)
        pick = erow == e
        tops.append(m)
        picks.append(jnp.where(pick, 1.0, 0.0))
        l = jnp.where(pick, -jnp.inf, l)
    ex = [jnp.exp(t - tops[0]) for t in tops]
    den = ex[0] + ex[1] + ex[2] + ex[3]
    for k in range(TOP_K):
        gate_ref[k:k + 1, :] = ex[k] / den

    chosen = picks[0] + picks[1] + picks[2] + picks[3]
    chosen_b = chosen.astype(BF16)
    earlier_tok = (lax.broadcasted_iota(I32, (tm, tm), 0) < lax.broadcasted_iota(I32, (tm, tm), 1))
    within = jnp.dot(chosen_b, jnp.where(earlier_tok, 1.0, 0.0).astype(BF16),
                     preferred_element_type=F32)
    lower_exp = (lax.broadcasted_iota(I32, (N_EXPERTS, N_EXPERTS), 1)
                 < lax.broadcasted_iota(I32, (N_EXPERTS, N_EXPERTS), 0))
    below = jnp.dot(jnp.where(lower_exp, 1.0, 0.0).astype(BF16), chosen_b,
                    preferred_element_type=F32)
    slot = within + jnp.sum(below, axis=1, keepdims=True)
    for k in range(TOP_K):
        pos_ref[k:k + 1, :] = jnp.sum(picks[k] * slot, axis=0, keepdims=True).astype(I32)
    cnt_ref[0] = jnp.sum(chosen, axis=1, keepdims=True)


def _merge(x, ys_tm, ya, ym, lp, nb, st, t_all, tile0, carried=None):
    b, s, _ = x.shape
    tm = nb * st
    assert tm == TOKEN_TM
    n_s = s // st
    tile = lambda j, i: (tile0 + i * n_s + j)
    c2 = lambda j, i: (0, 0)
    row3 = lambda j, i: (i, j, 0)
    vec = lambda a: a.reshape(1, -1)
    carried = () if carried is None else tuple(carried)
    return pl.pallas_call(
        functools.partial(_merge_kernel, nb=nb, n_carried=len(carried)),
        grid=(n_s, b // nb),
        in_specs=[pl.BlockSpec((nb, st, D_MODEL), row3),
                  pl.BlockSpec((D_SSM // LANES, st * b, LANES), lambda j, i: (0, j, 0)),
                  pl.BlockSpec((nb, st, D_ATT), row3),
                  pl.BlockSpec((nb, st, D_MEM), row3),
                  pl.BlockSpec((1, D_SSM), c2), pl.BlockSpec((1, D_ATT), c2),
                  pl.BlockSpec((1, D_MEM), c2),
                  pl.BlockSpec((D_MODEL, D_MODEL), c2),
                  pl.BlockSpec((1, D_MODEL), c2), pl.BlockSpec((1, D_MODEL), c2),
                  pl.BlockSpec((N_EXPERTS, D_MODEL), c2), pl.BlockSpec((N_EXPERTS, 1), c2)]
                 + [pl.BlockSpec(memory_space=pl.ANY)] * len(carried),
        out_specs=[pl.BlockSpec((tm, D_MODEL), lambda j, i: (tile(j, i), 0)),
                   pl.BlockSpec((TOP_K, tm), lambda j, i: (0, tile(j, i))),
                   pl.BlockSpec((TOP_K, tm), lambda j, i: (0, tile(j, i))),
                   pl.BlockSpec((1, N_EXPERTS, 1), lambda j, i: (tile(j, i), 0, 0))],
        out_shape=[jax.ShapeDtypeStruct((t_all, D_MODEL), F32),
                   jax.ShapeDtypeStruct((TOP_K, t_all), I32),
                   jax.ShapeDtypeStruct((TOP_K, t_all), F32),
                   jax.ShapeDtypeStruct((t_all // tm, N_EXPERTS, 1), F32)],
        scratch_shapes=[pltpu.VMEM((D_MODEL, D_MODEL), BF16)],
        input_output_aliases={12 + k: k for k in range(len(carried))},
        compiler_params=_params(2),
        name="merge_router",
    )(x, ys_tm, ya, ym, vec(lp['g_ssm']), vec(lp['g_att']), vec(lp['g_mem']), lp['w_out'],
      vec(lp['ln1_g']), vec(lp['ln1_b']), lp['w_router'].T, lp['b_router'].reshape(N_EXPERTS, 1),
      *carried)


def _rows(start, size):
    return pl.ds(pl.multiple_of(start * ROW_SUBLANES, ROW_SUBLANES), size * ROW_SUBLANES)


def _store_rows(ref, value, row0=0):
    n = value.shape[0]
    for j in range(ROW_SUBLANES):
        ref[pl.ds(row0 * ROW_SUBLANES + j, n, stride=ROW_SUBLANES), :] = value[:, LANES * j:LANES * (j + 1)]


def _load_rows(ref, dtype=F32):
    n = ref.shape[0] // ROW_SUBLANES
    return jnp.concatenate([ref[pl.ds(j, n, stride=ROW_SUBLANES), :].astype(dtype)
                            for j in range(ROW_SUBLANES)], axis=1)


def _for_each_run_piece(n, max_rows, fn):
    for bit in reversed(range(max_rows.bit_length())):
        size = 1 << bit
        start = (n >> (bit + 1)) << (bit + 1)

        @pl.when((n & size) != 0)
        def _(start=start, size=size):
            fn(start, size)


def _dispatch_kernel(n_ref, off_ref, dst_ref, padlo_ref, padn_ref, used_ref,
                     pos_ref, h_ref, xs_hbm, sorted_sc, zero_sc, sem, zsem):
    i = pl.program_id(0)
    tm = h_ref.shape[0]
    n_slots = TOP_K * tm
    n_blocks = xs_hbm.shape[0] // (EXPERT_TM * ROW_SUBLANES)

    @pl.when(i == 0)
    def _():
        zero_sc[...] = jnp.zeros_like(zero_sc)

        def pad_copy(e, start, size):
            return pltpu.make_async_copy(zero_sc.at[_rows(0, size)],
                                         xs_hbm.at[_rows(padlo_ref[e] + start, size)], zsem)

        def tail_copy(blk):
            return pltpu.make_async_copy(zero_sc, xs_hbm.at[_rows(blk * EXPERT_TM, EXPERT_TM)], zsem)

        for e in range(N_EXPERTS):
            _for_each_run_piece(padn_ref[e], EXPERT_TM - 1,
                                lambda start, size, e=e: pad_copy(e, start, size).start())

        def tail_start(blk, carry):
            tail_copy(blk).start()
            return carry

        lax.fori_loop(used_ref[0], n_blocks, tail_start, 0)
        for e in range(N_EXPERTS):
            _for_each_run_piece(padn_ref[e], EXPERT_TM - 1,
                                lambda start, size, e=e: pad_copy(e, start, size).wait())

        def tail_wait(blk, carry):
            tail_copy(blk).wait()
            return carry

        lax.fori_loop(used_ref[0], n_blocks, tail_wait, 0)

    n_tiles = pl.num_programs(0) - 1
    slot = lax.rem(i, N_SORT_BUFS)
    prev_slot = lax.rem(i + N_SORT_BUFS - 1, N_SORT_BUFS)
    buf = sorted_sc.at[slot]
    prev = sorted_sc.at[prev_slot]

    def wait_tile(sl):
        pltpu.make_async_copy(sorted_sc.at[sl], xs_hbm.at[_rows(0, n_slots)], sem.at[sl]).wait()

    @pl.when(i >= N_SORT_BUFS)
    def _():
        wait_tile(slot)

    pos = pos_ref[...]
    hb = h_ref[...].astype(BF16)
    base = jnp.maximum(i - 1, 0) * N_EXPERTS
    rows_c = n_slots // SORT_CHUNKS
    experts_c = N_EXPERTS // SORT_CHUNKS
    for c in range(SORT_CHUNKS):
        for e in range(c * experts_c, (c + 1) * experts_c):
            off = off_ref[base + e]
            dst = dst_ref[base + e]

            def run_start(start, size, off=off, dst=dst):
                pltpu.make_async_copy(prev.at[_rows(off + start, size)],
                                      xs_hbm.at[_rows(dst + start, size)], sem.at[prev_slot]).start()

            _for_each_run_piece(jnp.where(i >= 1, n_ref[base + e], 0), tm, run_start)

        srow = lax.broadcasted_iota(I32, (rows_c, tm), 0) + c * rows_c
        perm = jnp.where(srow == pos[0:1], 1.0,
                         jnp.where(srow == pos[1:2], 1.0,
                                   jnp.where(srow == pos[2:3], 1.0,
                                             jnp.where(srow == pos[3:4], 1.0, 0.0)))).astype(BF16)
        _store_rows(buf, jnp.dot(perm, hb, preferred_element_type=F32), c * rows_c)

    @pl.when(i == n_tiles)
    def _():
        wait_tile(prev_slot)

        @pl.when(i >= 2)
        def _():
            wait_tile(lax.rem(i + N_SORT_BUFS - 2, N_SORT_BUFS))


def _dispatch(run_n, run_off, run_dst, pad_lo, pad_n, n_used, pos, h, cap):
    tm = TOKEN_TM
    n_tiles = h.shape[0] // tm
    grid_spec = pltpu.PrefetchScalarGridSpec(
        num_scalar_prefetch=6,
        grid=(n_tiles + 1,),
        in_specs=[pl.BlockSpec((TOP_K, tm), lambda i, *_: (0, jnp.minimum(i, n_tiles - 1))),
                  pl.BlockSpec((tm, D_MODEL), lambda i, *_: (jnp.minimum(i, n_tiles - 1), 0))],
        out_specs=pl.BlockSpec(memory_space=pl.ANY),
        scratch_shapes=[pltpu.VMEM((N_SORT_BUFS, TOP_K * tm * ROW_SUBLANES, LANES), F32),
                        pltpu.VMEM((EXPERT_TM * ROW_SUBLANES, LANES), F32),
                        pltpu.SemaphoreType.DMA((N_SORT_BUFS,)), pltpu.SemaphoreType.DMA],
    )
    return pl.pallas_call(
        _dispatch_kernel,
        grid_spec=grid_spec,
        out_shape=jax.ShapeDtypeStruct((cap * ROW_SUBLANES, LANES), F32),
        compiler_params=_params(1),
        name="moe_dispatch",
    )(run_n, run_off, run_dst, pad_lo, pad_n, n_used, pos, h)


def _expert_kernel(be_ref, first_ref, ord_ref, seq_ref, used_ref,
                   x_ref, bgu_ref, bd_ref, wgu_hbm, wd_hbm, o_ref,
                   wgu_st, wd_st, wgu_sc, wd_sc, sem):
    i = pl.program_id(0)

    def weight_copies(e):
        return (pltpu.make_async_copy(wgu_hbm.at[e], wgu_st, sem.at[0]),
                pltpu.make_async_copy(wd_hbm.at[e], wd_st, sem.at[1]))

    @pl.when(i == 0)
    def _():
        for c in weight_copies(seq_ref[0]):
            c.start()

    @pl.when(i < used_ref[0])
    def _():
        @pl.when(first_ref[i] == 1)
        def _():
            k = ord_ref[i]
            for c in weight_copies(seq_ref[k]):
                c.wait()
            wgu_sc[...] = wgu_st[...].astype(BF16)
            wd_sc[...] = wd_st[...].astype(BF16)

            @pl.when(k + 1 < used_ref[1])
            def _():
                for c in weight_copies(seq_ref[k + 1]):
                    c.start()

        gu = jnp.dot(_load_rows(x_ref, BF16), wgu_sc[...], preferred_element_type=F32) + bgu_ref[0]
        gate = jnp.minimum(gu[:, :D_FF], SWIGLU_LIMIT)
        lin = jnp.clip(gu[:, D_FF:], -SWIGLU_LIMIT, SWIGLU_LIMIT)
        act = gate * jax.nn.sigmoid(SWIGLU_ALPHA * gate) * (lin + 1.0)
        _store_rows(o_ref, jnp.dot(act.astype(BF16), wd_sc[...], preferred_element_type=F32) + bd_ref[0])

    @pl.when(i >= used_ref[0])
    def _():
        o_ref[...] = jnp.zeros_like(o_ref)


def _experts(block_expert, block_first, block_ord, expert_seq, n_used, xs, lp):
    tm = EXPERT_TM * ROW_SUBLANES
    grid_spec = pltpu.PrefetchScalarGridSpec(
        num_scalar_prefetch=5,
        grid=(xs.shape[0] // tm,),
        in_specs=[pl.BlockSpec((tm, LANES), lambda i, be, *_: (i, 0)),
                  pl.BlockSpec((1, 1, 2 * D_FF), lambda i, be, *_: (be[i], 0, 0)),
                  pl.BlockSpec((1, 1, D_MODEL), lambda i, be, *_: (be[i], 0, 0)),
                  pl.BlockSpec(memory_space=pl.ANY),
                  pl.BlockSpec(memory_space=pl.ANY)],
        out_specs=pl.BlockSpec((tm, LANES), lambda i, be, *_: (i, 0)),
        scratch_shapes=[pltpu.VMEM((D_MODEL, 2 * D_FF), F32), pltpu.VMEM((D_FF, D_MODEL), F32),
                        pltpu.VMEM((D_MODEL, 2 * D_FF), BF16), pltpu.VMEM((D_FF, D_MODEL), BF16),
                        pltpu.SemaphoreType.DMA((2,))],
    )
    return pl.pallas_call(
        _expert_kernel,
        grid_spec=grid_spec,
        out_shape=jax.ShapeDtypeStruct(xs.shape, F32),
        compiler_params=_params(1),
        name="moe_experts",
    )(block_expert, block_first, block_ord, expert_seq, n_used, xs,
      lp['b_gu'].reshape(N_EXPERTS, 1, 2 * D_FF), lp['b_down'].reshape(N_EXPERTS, 1, D_MODEL),
      lp['w_gu'], lp['w_down'])


def _combine_kernel(n_ref, off_ref, dst_ref, pos_ref, gate_ref, h_ref, ys_hbm, g_ref, b_ref,
                    y1_ref, y2_ref, sorted_sc, w_sc, sem, *, n_first):
    i = pl.program_id(0)
    n_tiles = pl.num_programs(0) - 1
    tm = h_ref.shape[0]
    n_slots = TOP_K * tm
    slot = lax.rem(i, 2)
    buf = sorted_sc.at[slot]
    pos = pos_ref[...]
    gates = gate_ref[...]
    rows_per = tm // N_EXPERTS
    base = jnp.minimum(i, n_tiles - 1) * N_EXPERTS
    for e in range(N_EXPERTS):
        off = off_ref[base + e]
        dst = dst_ref[base + e]

        def run_start(start, size, off=off, dst=dst):
            pltpu.make_async_copy(ys_hbm.at[_rows(dst + start, size)],
                                  buf.at[_rows(off + start, size)], sem.at[slot]).start()

        _for_each_run_piece(jnp.where(i < n_tiles, n_ref[base + e], 0), tm, run_start)

        r = slice(e * rows_per, (e + 1) * rows_per)
        scol = lax.broadcasted_iota(I32, (rows_per, n_slots), 1)
        w_sc[r, :] = jnp.where(
            scol == pos[r, 0:1], gates[r, 0:1],
            jnp.where(scol == pos[r, 1:2], gates[r, 1:2],
                      jnp.where(scol == pos[r, 2:3], gates[r, 2:3],
                                jnp.where(scol == pos[r, 3:4], gates[r, 3:4], 0.0)))).astype(BF16)

    @pl.when(i >= 1)
    def _():
        done = sorted_sc.at[1 - slot]
        pltpu.make_async_copy(ys_hbm.at[_rows(0, n_slots)], done, sem.at[1 - slot]).wait()

        f = jnp.dot(w_sc[...], _load_rows(done, BF16), preferred_element_type=F32)
        y = _layer_norm(DEEPNORM_ALPHA * h_ref[...] + f, g_ref[...], b_ref[...])

        @pl.when(i - 1 < n_first)
        def _():
            y1_ref[...] = y

        @pl.when(i - 1 >= n_first)
        def _():
            y2_ref[...] = y


def _combine(run_n, run_off, run_dst, pos_t, gates_t, h, ys, lp, t_first):
    t = h.shape[0]
    tm = TOKEN_TM
    n_first = t_first // tm
    n_rest = (t - t_first) // tm
    c2 = lambda i, *_: (0, 0)
    done = lambda i, *_: (jnp.maximum(i - 1, 0), 0)
    done1 = lambda i, *_: (jnp.clip(i - 1, 0, n_first - 1), 0)
    done2 = lambda i, *_: (jnp.clip(i - 1 - n_first, 0, n_rest - 1), 0)
    grid_spec = pltpu.PrefetchScalarGridSpec(
        num_scalar_prefetch=3,
        grid=(t // tm + 1,),
        in_specs=[pl.BlockSpec((tm, TOP_K), done),
                  pl.BlockSpec((tm, TOP_K), done),
                  pl.BlockSpec((tm, D_MODEL), done),
                  pl.BlockSpec(memory_space=pl.ANY),
                  pl.BlockSpec((1, D_MODEL), c2), pl.BlockSpec((1, D_MODEL), c2)],
        out_specs=[pl.BlockSpec((tm, D_MODEL), done1), pl.BlockSpec((tm, D_MODEL), done2)],
        scratch_shapes=[pltpu.VMEM((2, TOP_K * tm * ROW_SUBLANES, LANES), F32),
                        pltpu.VMEM((tm, TOP_K * tm), BF16),
                        pltpu.SemaphoreType.DMA((2,))],
    )
    return pl.pallas_call(
        functools.partial(_combine_kernel, n_first=n_first),
        grid_spec=grid_spec,
        out_shape=[jax.ShapeDtypeStruct((t_first, D_MODEL), F32),
                   jax.ShapeDtypeStruct((t - t_first, D_MODEL), F32)],
        compiler_params=_params(1),
        name="moe_combine",
    )(run_n, run_off, run_dst, pos_t, gates_t, h, ys,
      lp['ln2_g'].reshape(1, D_MODEL), lp['ln2_b'].reshape(1, D_MODEL))


def _moe_and_norm(h, pos, gates, tile_counts, lp, t_first):
    t = h.shape[0]
    te = EXPERT_TM
    n_tiles = t // TOKEN_TM
    n_blocks = (t * TOP_K) // te + N_EXPERTS
    cap = n_blocks * te
    cnt = tile_counts.reshape(n_tiles, N_EXPERTS).astype(I32)
    counts = jnp.sum(cnt, axis=0)
    padded = (counts + te - 1) // te * te
    pad_ends = jnp.cumsum(padded)
    pad_starts = pad_ends - padded
    run_dst = pad_starts[None, :] + jnp.cumsum(cnt, axis=0) - cnt
    run_off = jnp.cumsum(cnt, axis=1) - cnt
    blk_start = jnp.arange(n_blocks, dtype=I32) * te
    expert_of = lambda slot_idx: jnp.minimum(jnp.sum(slot_idx[..., None] >= pad_ends, axis=-1), N_EXPERTS - 1)
    total = pad_ends[-1]
    be = jnp.where(blk_start < total, expert_of(blk_start), expert_of(jnp.maximum(total - 1, 0))).astype(I32)
    is_e = be[:, None] == jnp.arange(N_EXPERTS, dtype=I32)[None, :]
    pick = lambda table: jnp.sum(jnp.where(is_e, table[None, :], 0), axis=1)
    in_use = counts > 0
    ordinal = jnp.cumsum(in_use.astype(I32)) - 1
    rank = jnp.arange(N_EXPERTS, dtype=I32)
    expert_seq = jnp.sum(jnp.where(in_use[None, :] & (ordinal[None, :] == rank[:, None]), rank[None, :], 0),
                         axis=1)
    block_ord = pick(ordinal)
    block_first = (blk_start == pick(pad_starts)) & (blk_start < total)
    used = jnp.stack([total // te, jnp.sum(in_use.astype(I32))]).astype(I32)
    flat = lambda a: a.reshape(-1).astype(I32)
    xs = _dispatch(flat(cnt), flat(run_off), flat(run_dst), flat(pad_starts + counts),
                   flat(padded - counts), used, pos, h, cap)
    ys = _experts(be, flat(block_first), flat(block_ord), expert_seq, used, xs, lp)
    return _combine(flat(cnt), flat(run_off), flat(run_dst), pos.T, gates.T, h, ys, lp, t_first)


def kernel(x_prompt, x_sample, cache_attn_k, cache_attn_v, cache_mem_k, cache_mem_v, state_ssm_re, state_ssm_im, mem_prompt, w_in, lam_re, lam_im, log_dt, ssm_b_re, ssm_b_im, ssm_c_re, ssm_c_im, ssm_d, w_glu, b_glu, rel_bias, w_mem_kv, g_ssm, g_att, g_mem, w_out, ln1_g, ln1_b, w_router, b_router, w_gu, b_gu, w_down, b_down, ln2_g, ln2_b):
    assert w_in.shape[0] == 1, "single-layer step"
    lp = dict(w_in=w_in[0], lam_re=lam_re[0], lam_im=lam_im[0], log_dt=log_dt[0],
              ssm_b_re=ssm_b_re[0], ssm_b_im=ssm_b_im[0], ssm_c_re=ssm_c_re[0], ssm_c_im=ssm_c_im[0],
              ssm_d=ssm_d[0], w_glu=w_glu[0], b_glu=b_glu[0], rel_bias=rel_bias[0],
              w_mem_kv=w_mem_kv[0], g_ssm=g_ssm[0], g_att=g_att[0], g_mem=g_mem[0], w_out=w_out[0],
              ln1_g=ln1_g[0], ln1_b=ln1_b[0], w_router=w_router[0], b_router=b_router[0],
              w_gu=w_gu[0], b_gu=b_gu[0], w_down=w_down[0], b_down=b_down[0],
              ln2_g=ln2_g[0], ln2_b=ln2_b[0])

    bp, sp, _ = x_prompt.shape
    bs, ss, _ = x_sample.shape
    t_p = bp * sp
    t_all = t_p + bs * ss
    heads = lambda a: a.reshape(a.shape[0], a.shape[1], N_HEADS, HEAD_DIM)
    state = lambda a: a.reshape(a.shape[0], N_GROUPS, SSM_STATE)

    u_p, zr, zb = _in_proj(x_prompt, lp['w_in'], min(128, sp))
    mk, mv = _mem_kv(mem_prompt, lp['w_mem_kv'])
    ya, ym = _attn_prompt(zb, mk, mv, lp['rel_bias'])
    zeros = jnp.zeros((bp, D_STATE), F32)
    ys_p, sr_p, si_p = _ssm(u_p, zeros, zeros, lp, bp)
    merged = _merge(x_prompt, ys_p, ya, ym, lp, 1, TOKEN_TM, t_all, 0)
    w = min(BAND, sp)
    k_p = heads(zr[:, sp - w:, D_ATT:2 * D_ATT])
    v_p = heads(zr[:, sp - w:, 2 * D_ATT:3 * D_ATT])

    wc = cache_attn_k.shape[2]
    u_s, zr_s, zb_s = _in_proj(x_sample, lp['w_in'], ss)
    ya_s, ym_s, nk, nv = _attn_sample(
        zr_s, zb_s, cache_attn_k[0].reshape(bs, wc, D_ATT), cache_attn_v[0].reshape(bs, wc, D_ATT),
        cache_mem_k[0].reshape(bs, N_MEM, D_MEM), cache_mem_v[0].reshape(bs, N_MEM, D_MEM),
        lp['rel_bias'])
    ys_s, sr_s, si_s = _ssm(u_s, state_ssm_re[0], state_ssm_im[0], lp, bs)
    merged = _merge(x_sample, ys_s, ya_s, ym_s, lp, TOKEN_TM // ss, ss, t_all, t_p // TOKEN_TM,
                    carried=merged)

    y_p, y_s = _moe_and_norm(*merged, lp, t_p)

    return (y_p.reshape(bp, sp, D_MODEL), y_s.reshape(bs, ss, D_MODEL),
            k_p[None], v_p[None], heads(mk)[None], heads(mv)[None], state(sr_p)[None], state(si_p)[None],
            heads(nk)[None], heads(nv)[None], state(sr_s)[None], state(si_s)[None])
```

```python
import functools

import jax
import jax.numpy as jnp
from jax import lax
from jax.experimental import pallas as pl
from jax.experimental.pallas import tpu as pltpu

F32 = jnp.float32
BF16 = jnp.bfloat16
I32 = jnp.int32

D_MODEL = 1024
D_SSM = 512
D_ATT = 256
D_MEM = 256
D_IN = D_SSM + 3 * D_ATT + D_MEM
D_REST = D_IN - D_SSM
HEAD_DIM = 64
N_HEADS = 4
N_GROUPS = 32
SSM_GROUP = 16
SSM_STATE = 64
D_STATE = N_GROUPS * SSM_STATE
CHUNK = 64
N_PREV_CHUNKS = 8
BAND = N_PREV_CHUNKS * CHUNK
REL_CLIP = 128
N_MEM = 256
N_EXPERTS = 32
TOP_K = 4
D_FF = D_MODEL
SWIGLU_LIMIT = 7.0
SWIGLU_ALPHA = 1.702
LN_EPS = 1e-5
NEG_INF = -1e30
ATT_SCALE = HEAD_DIM ** -0.5
DEEPNORM_ALPHA = 2.0 ** 0.25

V7X_VMEM_LIMIT = 56 * 1024 * 1024
ATT_TQ = 4 * CHUNK
SCAN_LANES = 1024
SCAN_ROWS = 1024
LANES = 128
ROW_SUBLANES = D_MODEL // LANES
EXPERT_TM = 512
TOKEN_TM = 512
N_SORT_BUFS = 3
SORT_CHUNKS = 16

_NT = (((1,), (1,)), ((), ()))


def _params(n_axes, vmem=V7X_VMEM_LIMIT):
    return pltpu.CompilerParams(dimension_semantics=("arbitrary",) * n_axes,
                                vmem_limit_bytes=vmem)


def _in_proj_kernel(x_ref, w_ref, u_ref, z_ref, zb_ref, wb_ref):
    @pl.when(pl.program_id(0) == 0)
    def _():
        wb_ref[...] = w_ref[...].astype(BF16)

    nb, ts, _ = x_ref.shape
    x = x_ref[...].reshape(nb * ts, D_MODEL).astype(BF16)
    z = jnp.dot(x, wb_ref[...], preferred_element_type=F32)
    for b in range(nb):
        for c in range(D_SSM // LANES):
            u_ref[c, pl.ds(b, ts, stride=nb), :] = z[b * ts:(b + 1) * ts, LANES * c:LANES * (c + 1)]
    zr = z[:, D_SSM:]
    z_ref[...] = zr.reshape(nb, ts, D_REST)
    zb = jnp.concatenate([zr[:, :D_ATT] * ATT_SCALE, zr[:, D_ATT:3 * D_ATT], zr[:, 3 * D_ATT:] * ATT_SCALE],
                         axis=1)
    zb_ref[...] = zb.astype(BF16).reshape(nb, ts, D_REST)


def _in_proj(x, w_in, ts):
    b, s, _ = x.shape
    return pl.pallas_call(
        _in_proj_kernel,
        grid=(s // ts,),
        in_specs=[pl.BlockSpec((b, ts, D_MODEL), lambda j: (0, j, 0)),
                  pl.BlockSpec((D_MODEL, D_IN), lambda j: (0, 0))],
        out_specs=[pl.BlockSpec((D_SSM // LANES, ts * b, LANES), lambda j: (0, j, 0)),
                   pl.BlockSpec((b, ts, D_REST), lambda j: (0, j, 0)),
                   pl.BlockSpec((b, ts, D_REST), lambda j: (0, j, 0))],
        out_shape=[jax.ShapeDtypeStruct((D_SSM // LANES, s * b, LANES), F32),
                   jax.ShapeDtypeStruct((b, s, D_REST), F32),
                   jax.ShapeDtypeStruct((b, s, D_REST), BF16)],
        scratch_shapes=[pltpu.VMEM((D_MODEL, D_IN), BF16)],
        compiler_params=_params(1),
        name="in_proj",
    )(x, w_in)


def _mem_kv_kernel(m_ref, w_ref, mk_ref, mv_ref):
    kv = jnp.dot(m_ref[0].astype(BF16), w_ref[...].astype(BF16), preferred_element_type=F32)
    mk_ref[0] = kv[:, :D_MEM]
    mv_ref[0] = kv[:, D_MEM:]


def _mem_kv(mem, w_mem_kv):
    b = mem.shape[0]
    return pl.pallas_call(
        _mem_kv_kernel,
        grid=(b,),
        in_specs=[pl.BlockSpec((1, N_MEM, D_MODEL), lambda i: (i, 0, 0)),
                  pl.BlockSpec((D_MODEL, 2 * D_MEM), lambda i: (0, 0))],
        out_specs=[pl.BlockSpec((1, N_MEM, D_MEM), lambda i: (i, 0, 0)),
                   pl.BlockSpec((1, N_MEM, D_MEM), lambda i: (i, 0, 0))],
        out_shape=[jax.ShapeDtypeStruct((b, N_MEM, D_MEM), F32)] * 2,
        compiler_params=_params(1),
        name="mem_kv",
    )(mem, w_mem_kv)


def _ssm_kernel(u_ref, h0r_ref, h0i_ref, lr_ref, li_ref, ldt_ref, bre_ref, bim_ref,
                cre_ref, cim_ref, d_ref, wg_ref, bg_ref,
                y_ref, sr_ref, si_ref,
                a_sc, bbr_sc, bbi_sc, cr_sc, ci_sc, wg_sc, str_sc, sti_sc, xr_sc, xi_sc,
                *, n_batch):
    n_rows = u_ref.shape[1]
    n_steps = n_rows // n_batch

    @pl.when(pl.program_id(0) == 0)
    def _():
        lr = lr_ref[...]
        li = li_ref[...]
        dt = jnp.exp(ldt_ref[...])
        mag = jnp.exp(lr * dt)
        ar = mag * jnp.cos(li * dt)
        ai = mag * jnp.sin(li * dt)
        den = lr * lr + li * li
        fr = ((ar - 1.0) * lr + ai * li) / den
        fi = (ai * lr - (ar - 1.0) * li) / den
        a_sc[0:1, :] = ar
        a_sc[1:2, :] = ai
        for j in range(4):
            frj = fr[:, 512 * j:512 * (j + 1)]
            fij = fi[:, 512 * j:512 * (j + 1)]
            bbr_sc[j] = (frj * bre_ref[j] - fij * bim_ref[j]).astype(BF16)
            bbi_sc[j] = (frj * bim_ref[j] + fij * bre_ref[j]).astype(BF16)
            cr_sc[j] = cre_ref[j].astype(BF16)
            ci_sc[j] = cim_ref[j].astype(BF16)
        wg_sc[...] = wg_ref[...].astype(BF16)
        str_sc[...] = h0r_ref[...]
        sti_sc[...] = h0i_ref[...]

    for j in range(4):
        uc = u_ref[j].astype(BF16)
        xr_sc[:, 512 * j:512 * (j + 1)] = jnp.dot(uc, bbr_sc[j], preferred_element_type=F32)
        xi_sc[:, 512 * j:512 * (j + 1)] = jnp.dot(uc, bbi_sc[j], preferred_element_type=F32)

    for c in range(D_STATE // SCAN_LANES):
        lo = c * SCAN_LANES
        ar = jnp.broadcast_to(a_sc[0:1, lo:lo + SCAN_LANES], (n_batch, SCAN_LANES))
        ai = jnp.broadcast_to(a_sc[1:2, lo:lo + SCAN_LANES], (n_batch, SCAN_LANES))

        def step(t, carry, lo=lo, ar=ar, ai=ai):
            sr, si = carry
            r0 = pl.multiple_of(t * n_batch, n_batch)
            nr = ar * sr - ai * si + xr_sc[pl.ds(r0, n_batch), lo:lo + SCAN_LANES]
            ni = ar * si + ai * sr + xi_sc[pl.ds(r0, n_batch), lo:lo + SCAN_LANES]
            xr_sc[pl.ds(r0, n_batch), lo:lo + SCAN_LANES] = nr
            xi_sc[pl.ds(r0, n_batch), lo:lo + SCAN_LANES] = ni
            return nr, ni

        sr, si = lax.fori_loop(0, n_steps, step,
                               (str_sc[:, lo:lo + SCAN_LANES], sti_sc[:, lo:lo + SCAN_LANES]),
                               unroll=True)
        str_sc[:, lo:lo + SCAN_LANES] = sr
        sti_sc[:, lo:lo + SCAN_LANES] = si

    pieces = []
    for j in range(4):
        xr = xr_sc[:, 512 * j:512 * (j + 1)].astype(BF16)
        xi = xi_sc[:, 512 * j:512 * (j + 1)].astype(BF16)
        pieces.append(jnp.dot(xr, cr_sc[j], preferred_element_type=F32)
                      - jnp.dot(xi, ci_sc[j], preferred_element_type=F32))
    u = jnp.concatenate([u_ref[j] for j in range(4)], axis=1)
    y = jnp.concatenate(pieces, axis=1) + d_ref[...] * u
    y = jax.nn.gelu(y)
    z = jnp.dot(y.astype(BF16), wg_sc[...], preferred_element_type=F32) + bg_ref[...]
    out = z[:, :D_SSM] * jax.nn.sigmoid(z[:, D_SSM:])
    for j in range(D_SSM // LANES):
        y_ref[j] = out[:, LANES * j:LANES * (j + 1)]
    sr_ref[...] = str_sc[...]
    si_ref[...] = sti_sc[...]


def _block_diag_b(b):
    bt = b.transpose(0, 2, 1).reshape(4, 8, SSM_GROUP, SSM_STATE)
    same = jnp.eye(8, dtype=bool)[None, :, None, :, None]
    t = jnp.where(same, bt[:, :, :, None, :], 0.0)
    return t.reshape(4, 8 * SSM_GROUP, 8 * SSM_STATE)


def _block_diag_c(c):
    ct = c.transpose(0, 2, 1).reshape(4, 8, SSM_STATE, SSM_GROUP)
    same = jnp.eye(8, dtype=bool)[None, :, None, :, None]
    t = jnp.where(same, ct[:, :, :, None, :], 0.0)
    return t.reshape(4, 8 * SSM_STATE, 8 * SSM_GROUP)


def _ssm(u_rows, h0_re, h0_im, lp, n_batch):
    rows = u_rows.shape[1]
    planes = D_SSM // LANES
    tr = min(SCAN_ROWS, rows)
    flat = lambda a: a.reshape(1, D_STATE)
    ldt = jnp.repeat(lp['log_dt'], SSM_STATE).reshape(1, D_STATE)
    const2 = lambda i: (0, 0)
    const3 = lambda i: (0, 0, 0)
    y, sr, si = pl.pallas_call(
        functools.partial(_ssm_kernel, n_batch=n_batch),
        grid=(rows // tr,),
        in_specs=[pl.BlockSpec((planes, tr, LANES), lambda i: (0, i, 0)),
                  pl.BlockSpec((n_batch, D_STATE), const2),
                  pl.BlockSpec((n_batch, D_STATE), const2),
                  pl.BlockSpec((1, D_STATE), const2),
                  pl.BlockSpec((1, D_STATE), const2),
                  pl.BlockSpec((1, D_STATE), const2),
                  pl.BlockSpec((4, 128, 512), const3),
                  pl.BlockSpec((4, 128, 512), const3),
                  pl.BlockSpec((4, 512, 128), const3),
                  pl.BlockSpec((4, 512, 128), const3),
                  pl.BlockSpec((1, D_SSM), const2),
                  pl.BlockSpec((D_SSM, 2 * D_SSM), const2),
                  pl.BlockSpec((1, 2 * D_SSM), const2)],
        out_specs=[pl.BlockSpec((planes, tr, LANES), lambda i: (0, i, 0)),
                   pl.BlockSpec((n_batch, D_STATE), const2),
                   pl.BlockSpec((n_batch, D_STATE), const2)],
        out_shape=[jax.ShapeDtypeStruct((planes, rows, LANES), F32),
                   jax.ShapeDtypeStruct((n_batch, D_STATE), F32),
                   jax.ShapeDtypeStruct((n_batch, D_STATE), F32)],
        scratch_shapes=[pltpu.VMEM((2, D_STATE), F32),
                        pltpu.VMEM((4, 128, 512), BF16), pltpu.VMEM((4, 128, 512), BF16),
                        pltpu.VMEM((4, 512, 128), BF16), pltpu.VMEM((4, 512, 128), BF16),
                        pltpu.VMEM((D_SSM, 2 * D_SSM), BF16),
                        pltpu.VMEM((n_batch, D_STATE), F32), pltpu.VMEM((n_batch, D_STATE), F32),
                        pltpu.VMEM((tr, D_STATE), F32), pltpu.VMEM((tr, D_STATE), F32)],
        compiler_params=_params(1),
        name="ssm",
    )(u_rows, h0_re.reshape(n_batch, D_STATE), h0_im.reshape(n_batch, D_STATE),
      flat(lp['lam_re']), flat(lp['lam_im']), ldt,
      _block_diag_b(lp['ssm_b_re']), _block_diag_b(lp['ssm_b_im']),
      _block_diag_c(lp['ssm_c_re']), _block_diag_c(lp['ssm_c_im']),
      lp['ssm_d'].reshape(1, D_SSM), lp['w_glu'], lp['b_glu'].reshape(1, 2 * D_SSM))
    return y, sr, si


def _softmax_pv(s, v):
    m = jnp.max(s, axis=-1, keepdims=True)
    p = jnp.exp(s - m)
    l = jnp.sum(p, axis=-1, keepdims=True)
    return jnp.dot(p.astype(BF16), v, preferred_element_type=F32) / l


def _attend(qb, k, v, out_ref, row0=0, bias_ref=None, valid=None):
    tq = qb.shape[0]
    for h in range(N_HEADS):
        sl = slice(HEAD_DIM * h, HEAD_DIM * (h + 1))
        s = lax.dot_general(qb[:, sl], k[:, sl], _NT, preferred_element_type=F32)
        if bias_ref is not None:
            s = s + bias_ref[h]
        if valid is not None:
            s = jnp.where(valid, s, NEG_INF)
        out_ref[0, row0:row0 + tq, sl] = _softmax_pv(s, v[:, sl])


def _attn_prompt_kernel(q_ref, kp_ref, kc_ref, vp_ref, vc_ref, qm_ref, mk_ref, mv_ref, bias_ref,
                        ya_ref, ym_ref, bias_sc):
    tq = ATT_TQ

    @pl.when((pl.program_id(0) == 0) & (pl.program_id(1) == 0))
    def _():
        q_chunk = lax.broadcasted_iota(I32, (tq, 3 * tq), 0) // CHUNK
        k_chunk = lax.broadcasted_iota(I32, (tq, 3 * tq), 1) // CHUNK
        ahead = k_chunk - q_chunk
        for h in range(N_HEADS):
            bias_sc[h] = jnp.where(ahead >= 0, jnp.where(ahead <= N_PREV_CHUNKS, bias_ref[h], NEG_INF),
                                   NEG_INF)

    k = jnp.concatenate([kp_ref[0], kc_ref[0]], axis=0)
    v = jnp.concatenate([vp_ref[0], vc_ref[0]], axis=0)
    for half in range(2):
        first_key = (2 * pl.program_id(1) - 2 + half) * tq
        kpos = first_key + lax.broadcasted_iota(I32, (1, 3 * tq), 1)
        _attend(q_ref[0, half * tq:(half + 1) * tq, :], k[half * tq:(half + 3) * tq],
                v[half * tq:(half + 3) * tq], ya_ref, half * tq, bias_sc, kpos >= 0)
    _attend(qm_ref[0], mk_ref[0].astype(BF16), mv_ref[0].astype(BF16), ym_ref)


def _attn_sample_kernel(q_ref, kn_ref, vn_ref, qm_ref, ck_ref, cv_ref, mk_ref, mv_ref, bias_ref,
                        ya_ref, ym_ref, nk_ref, nv_ref):
    n = kn_ref.shape[1]
    kk = jnp.concatenate([ck_ref[0], kn_ref[0]], axis=0)
    vv = jnp.concatenate([cv_ref[0], vn_ref[0]], axis=0)
    nk_ref[0] = kk[n:]
    nv_ref[0] = vv[n:]
    _attend(q_ref[0], kk.astype(BF16), vv.astype(BF16), ya_ref, 0, bias_ref)
    _attend(qm_ref[0], mk_ref[0].astype(BF16), mv_ref[0].astype(BF16), ym_ref)


def _rel_bias(table, n_q, n_k):
    period = n_q + n_k
    m = jnp.arange(period)
    offset = jnp.where(m < n_k, m, m - period)
    idx = jnp.clip(BAND - offset, -REL_CLIP, REL_CLIP) + REL_CLIP
    f = table.astype(F32)[:, idx]
    flat = jnp.tile(f, (1, n_q))[:, :n_q * (period - 1)]
    return flat.reshape(N_HEADS, n_q, period - 1)[:, :, :n_k]


def _attn_prompt(zb, mk, mv, table):
    b, s, _ = zb.shape
    tq = ATT_TQ
    bias = _rel_bias(table, tq, 3 * tq)
    col = lambda c: (lambda i, j: (i, j, c))
    prev = lambda c: (lambda i, j: (i, jnp.maximum(j - 1, 0), c))
    blk = (1, 2 * tq, D_ATT)
    return pl.pallas_call(
        _attn_prompt_kernel,
        grid=(b, s // (2 * tq)),
        in_specs=[pl.BlockSpec(blk, col(0)),
                  pl.BlockSpec(blk, prev(1)), pl.BlockSpec(blk, col(1)),
                  pl.BlockSpec(blk, prev(2)), pl.BlockSpec(blk, col(2)),
                  pl.BlockSpec(blk, col(3)),
                  pl.BlockSpec((1, N_MEM, D_MEM), lambda i, j: (i, 0, 0)),
                  pl.BlockSpec((1, N_MEM, D_MEM), lambda i, j: (i, 0, 0)),
                  pl.BlockSpec((N_HEADS, tq, 3 * tq), lambda i, j: (0, 0, 0))],
        out_specs=[pl.BlockSpec(blk, col(0)), pl.BlockSpec(blk, col(0))],
        out_shape=[jax.ShapeDtypeStruct((b, s, D_ATT), F32),
                   jax.ShapeDtypeStruct((b, s, D_MEM), F32)],
        scratch_shapes=[pltpu.VMEM((N_HEADS, tq, 3 * tq), F32)],
        compiler_params=_params(2),
        name="attn_prompt",
    )(zb, zb, zb, zb, zb, zb, mk, mv, bias)


def _attn_sample(zr, zb, cache_k, cache_v, mk, mv, table):
    b, n, _ = zr.shape
    w = cache_k.shape[1]
    bias = _rel_bias(table, n, w + n)
    col = lambda c: (lambda i: (i, 0, c))
    blk = (1, n, D_ATT)
    cblk = (1, w, D_ATT)
    mblk = (1, N_MEM, D_MEM)
    row = lambda i: (i, 0, 0)
    return pl.pallas_call(
        _attn_sample_kernel,
        grid=(b,),
        in_specs=[pl.BlockSpec(blk, col(0)), pl.BlockSpec(blk, col(1)), pl.BlockSpec(blk, col(2)),
                  pl.BlockSpec(blk, col(3)),
                  pl.BlockSpec(cblk, row), pl.BlockSpec(cblk, row),
                  pl.BlockSpec(mblk, row), pl.BlockSpec(mblk, row),
                  pl.BlockSpec((N_HEADS, n, w + n), lambda i: (0, 0, 0))],
        out_specs=[pl.BlockSpec(blk, row), pl.BlockSpec(blk, row),
                   pl.BlockSpec(cblk, row), pl.BlockSpec(cblk, row)],
        out_shape=[jax.ShapeDtypeStruct((b, n, D_ATT), F32),
                   jax.ShapeDtypeStruct((b, n, D_MEM), F32),
                   jax.ShapeDtypeStruct((b, w, D_ATT), F32),
                   jax.ShapeDtypeStruct((b, w, D_ATT), F32)],
        compiler_params=_params(1),
        name="attn_sample",
    )(zb, zr, zr, zb, cache_k, cache_v, mk, mv, bias)


def _rms(x, g):
    return x * lax.rsqrt(jnp.mean(jnp.square(x), axis=-1, keepdims=True) + LN_EPS) * g


def _layer_norm(x, g, b):
    mu = jnp.mean(x, axis=-1, keepdims=True)
    xc = x - mu
    var = jnp.mean(jnp.square(xc), axis=-1, keepdims=True)
    return xc * lax.rsqrt(var + LN_EPS) * g + b


def _split_bf16(a):
    hi = a.astype(BF16)
    lo = (a - hi.astype(F32)).astype(BF16)
    return hi, lo


def _merge_kernel(*refs, nb, n_carried):
    (x_ref, ys_ref, ya_ref, ym_ref, gs_ref, ga_ref, gm_ref, wo_ref, l1g_ref, l1b_ref,
     wrt_ref, brt_ref) = refs[:12]
    h_ref, pos_ref, gate_ref, cnt_ref, wo_sc = refs[12 + n_carried:]
    st = x_ref.shape[1]
    tm = nb * st
    n_batch = ys_ref.shape[1] // st

    @pl.when((pl.program_id(0) == 0) & (pl.program_id(1) == 0))
    def _():
        wo_sc[...] = wo_ref[...].astype(BF16)

    x = x_ref[...].reshape(tm, D_MODEL)
    first = pl.program_id(1) * nb
    ys = jnp.concatenate(
        [jnp.concatenate([ys_ref[c, pl.ds(first + i, st, stride=n_batch), :] for c in range(D_SSM // LANES)],
                         axis=1) for i in range(nb)], axis=0)
    ya = ya_ref[...].reshape(tm, D_ATT)
    ym = ym_ref[...].reshape(tm, D_MEM)
    a = _rms(ys, gs_ref[...]).astype(BF16)
    b = _rms(ya, ga_ref[...]).astype(BF16)
    c = _rms(ym, gm_ref[...]).astype(BF16)
    mix = (jnp.dot(a, wo_sc[0:D_SSM, :], preferred_element_type=F32)
           + jnp.dot(b, wo_sc[D_SSM:D_SSM + D_ATT, :], preferred_element_type=F32)
           + jnp.dot(c, wo_sc[D_SSM + D_ATT:, :], preferred_element_type=F32))
    h = _layer_norm(DEEPNORM_ALPHA * x + mix, l1g_ref[...], l1b_ref[...])
    h_ref[...] = h

    h_hi, h_lo = _split_bf16(h)
    w_hi, w_lo = _split_bf16(wrt_ref[...])
    logits = (lax.dot_general(w_hi, h_hi, _NT, preferred_element_type=F32)
              + lax.dot_general(w_hi, h_lo, _NT, preferred_element_type=F32)
              + lax.dot_general(w_lo, h_hi, _NT, preferred_element_type=F32)
              + brt_ref[...])
    erow = lax.broadcasted_iota(I32, (N_EXPERTS, tm), 0).astype(F32)
    tops, picks = [], []
    l = logits
    for k in range(TOP_K):
        m = jnp.max(l, axis=0, keepdims=True)
        e = jnp.min(jnp.where(l == m, erow, float(N_EXPERTS)), axis=0, keepdims=True)
        pick = erow == e
        tops.append(m)
        picks.append(jnp.where(pick, 1.0, 0.0))
        l = jnp.where(pick, -jnp.inf, l)
    ex = [jnp.exp(t - tops[0]) for t in tops]
    den = ex[0] + ex[1] + ex[2] + ex[3]
    for k in range(TOP_K):
        gate_ref[k:k + 1, :] = ex[k] / den

    chosen = picks[0] + picks[1] + picks[2] + picks[3]
    chosen_b = chosen.astype(BF16)
    earlier_tok = (lax.broadcasted_iota(I32, (tm, tm), 0) < lax.broadcasted_iota(I32, (tm, tm), 1))
    within = jnp.dot(chosen_b, jnp.where(earlier_tok, 1.0, 0.0).astype(BF16),
                     preferred_element_type=F32)
    lower_exp = (lax.broadcasted_iota(I32, (N_EXPERTS, N_EXPERTS), 1)
                 < lax.broadcasted_iota(I32, (N_EXPERTS, N_EXPERTS), 0))
    below = jnp.dot(jnp.where(lower_exp, 1.0, 0.0).astype(BF16), chosen_b,
                    preferred_element_type=F32)
    slot = within + jnp.sum(below, axis=1, keepdims=True)
    for k in range(TOP_K):
        pos_ref[k:k + 1, :] = jnp.sum(picks[k] * slot, axis=0, keepdims=True).astype(I32)
    cnt_ref[0] = jnp.sum(chosen, axis=1, keepdims=True)


def _merge(x, ys_tm, ya, ym, lp, nb, st, t_all, tile0, carried=None):
    b, s, _ = x.shape
    tm = nb * st
    assert tm == TOKEN_TM
    n_s = s // st
    tile = lambda j, i: (tile0 + i * n_s + j)
    c2 = lambda j, i: (0, 0)
    row3 = lambda j, i: (i, j, 0)
    vec = lambda a: a.reshape(1, -1)
    carried = () if carried is None else tuple(carried)
    return pl.pallas_call(
        functools.partial(_merge_kernel, nb=nb, n_carried=len(carried)),
        grid=(n_s, b // nb),
        in_specs=[pl.BlockSpec((nb, st, D_MODEL), row3),
                  pl.BlockSpec((D_SSM // LANES, st * b, LANES), lambda j, i: (0, j, 0)),
                  pl.BlockSpec((nb, st, D_ATT), row3),
                  pl.BlockSpec((nb, st, D_MEM), row3),
                  pl.BlockSpec((1, D_SSM), c2), pl.BlockSpec((1, D_ATT), c2),
                  pl.BlockSpec((1, D_MEM), c2),
                  pl.BlockSpec((D_MODEL, D_MODEL), c2),
                  pl.BlockSpec((1, D_MODEL), c2), pl.BlockSpec((1, D_MODEL), c2),
                  pl.BlockSpec((N_EXPERTS, D_MODEL), c2), pl.BlockSpec((N_EXPERTS, 1), c2)]
                 + [pl.BlockSpec(memory_space=pl.ANY)] * len(carried),
        out_specs=[pl.BlockSpec((tm, D_MODEL), lambda j, i: (tile(j, i), 0)),
                   pl.BlockSpec((TOP_K, tm), lambda j, i: (0, tile(j, i))),
                   pl.BlockSpec((TOP_K, tm), lambda j, i: (0, tile(j, i))),
                   pl.BlockSpec((1, N_EXPERTS, 1), lambda j, i: (tile(j, i), 0, 0))],
        out_shape=[jax.ShapeDtypeStruct((t_all, D_MODEL), F32),
                   jax.ShapeDtypeStruct((TOP_K, t_all), I32),
                   jax.ShapeDtypeStruct((TOP_K, t_all), F32),
                   jax.ShapeDtypeStruct((t_all // tm, N_EXPERTS, 1), F32)],
        scratch_shapes=[pltpu.VMEM((D_MODEL, D_MODEL), BF16)],
        input_output_aliases={12 + k: k for k in range(len(carried))},
        compiler_params=_params(2),
        name="merge_router",
    )(x, ys_tm, ya, ym, vec(lp['g_ssm']), vec(lp['g_att']), vec(lp['g_mem']), lp['w_out'],
      vec(lp['ln1_g']), vec(lp['ln1_b']), lp['w_router'].T, lp['b_router'].reshape(N_EXPERTS, 1),
      *carried)


def _rows(start, size):
    return pl.ds(pl.multiple_of(start * ROW_SUBLANES, ROW_SUBLANES), size * ROW_SUBLANES)


def _store_rows(ref, value, row0=0):
    n = value.shape[0]
    for j in range(ROW_SUBLANES):
        ref[pl.ds(row0 * ROW_SUBLANES + j, n, stride=ROW_SUBLANES), :] = value[:, LANES * j:LANES * (j + 1)]


def _load_rows(ref, dtype=F32):
    n = ref.shape[0] // ROW_SUBLANES
    return jnp.concatenate([ref[pl.ds(j, n, stride=ROW_SUBLANES), :].astype(dtype)
                            for j in range(ROW_SUBLANES)], axis=1)


def _for_each_run_piece(n, max_rows, fn):
    for bit in reversed(range(max_rows.bit_length())):
        size = 1 << bit
        start = (n >> (bit + 1)) << (bit + 1)

        @pl.when((n & size) != 0)
        def _(start=start, size=size):
            fn(start, size)


def _dispatch_kernel(n_ref, off_ref, dst_ref, padlo_ref, padn_ref, used_ref,
                     pos_ref, h_ref, xs_hbm, sorted_sc, zero_sc, sem, zsem):
    i = pl.program_id(0)
    tm = h_ref.shape[0]
    n_slots = TOP_K * tm
    n_blocks = xs_hbm.shape[0] // (EXPERT_TM * ROW_SUBLANES)

    @pl.when(i == 0)
    def _():
        zero_sc[...] = jnp.zeros_like(zero_sc)

        def pad_copy(e, start, size):
            return pltpu.make_async_copy(zero_sc.at[_rows(0, size)],
                                         xs_hbm.at[_rows(padlo_ref[e] + start, size)], zsem)

        def tail_copy(blk):
            return pltpu.make_async_copy(zero_sc, xs_hbm.at[_rows(blk * EXPERT_TM, EXPERT_TM)], zsem)

        for e in range(N_EXPERTS):
            _for_each_run_piece(padn_ref[e], EXPERT_TM - 1,
                                lambda start, size, e=e: pad_copy(e, start, size).start())

        def tail_start(blk, carry):
            tail_copy(blk).start()
            return carry

        lax.fori_loop(used_ref[0], n_blocks, tail_start, 0)
        for e in range(N_EXPERTS):
            _for_each_run_piece(padn_ref[e], EXPERT_TM - 1,
                                lambda start, size, e=e: pad_copy(e, start, size).wait())

        def tail_wait(blk, carry):
            tail_copy(blk).wait()
            return carry

        lax.fori_loop(used_ref[0], n_blocks, tail_wait, 0)

    n_tiles = pl.num_programs(0) - 1
    slot = lax.rem(i, N_SORT_BUFS)
    prev_slot = lax.rem(i + N_SORT_BUFS - 1, N_SORT_BUFS)
    buf = sorted_sc.at[slot]
    prev = sorted_sc.at[prev_slot]

    def wait_tile(sl):
        pltpu.make_async_copy(sorted_sc.at[sl], xs_hbm.at[_rows(0, n_slots)], sem.at[sl]).wait()

    @pl.when(i >= N_SORT_BUFS)
    def _():
        wait_tile(slot)

    pos = pos_ref[...]
    hb = h_ref[...].astype(BF16)
    base = jnp.maximum(i - 1, 0) * N_EXPERTS
    rows_c = n_slots // SORT_CHUNKS
    experts_c = N_EXPERTS // SORT_CHUNKS
    for c in range(SORT_CHUNKS):
        for e in range(c * experts_c, (c + 1) * experts_c):
            off = off_ref[base + e]
            dst = dst_ref[base + e]

            def run_start(start, size, off=off, dst=dst):
                pltpu.make_async_copy(prev.at[_rows(off + start, size)],
                                      xs_hbm.at[_rows(dst + start, size)], sem.at[prev_slot]).start()

            _for_each_run_piece(jnp.where(i >= 1, n_ref[base + e], 0), tm, run_start)

        srow = lax.broadcasted_iota(I32, (rows_c, tm), 0) + c * rows_c
        perm = jnp.where(srow == pos[0:1], 1.0,
                         jnp.where(srow == pos[1:2], 1.0,
                                   jnp.where(srow == pos[2:3], 1.0,
                                             jnp.where(srow == pos[3:4], 1.0, 0.0)))).astype(BF16)
        _store_rows(buf, jnp.dot(perm, hb, preferred_element_type=F32), c * rows_c)

    @pl.when(i == n_tiles)
    def _():
        wait_tile(prev_slot)

        @pl.when(i >= 2)
        def _():
            wait_tile(lax.rem(i + N_SORT_BUFS - 2, N_SORT_BUFS))


def _dispatch(run_n, run_off, run_dst, pad_lo, pad_n, n_used, pos, h, cap):
    tm = TOKEN_TM
    n_tiles = h.shape[0] // tm
    grid_spec = pltpu.PrefetchScalarGridSpec(
        num_scalar_prefetch=6,
        grid=(n_tiles + 1,),
        in_specs=[pl.BlockSpec((TOP_K, tm), lambda i, *_: (0, jnp.minimum(i, n_tiles - 1))),
                  pl.BlockSpec((tm, D_MODEL), lambda i, *_: (jnp.minimum(i, n_tiles - 1), 0))],
        out_specs=pl.BlockSpec(memory_space=pl.ANY),
        scratch_shapes=[pltpu.VMEM((N_SORT_BUFS, TOP_K * tm * ROW_SUBLANES, LANES), F32),
                        pltpu.VMEM((EXPERT_TM * ROW_SUBLANES, LANES), F32),
                        pltpu.SemaphoreType.DMA((N_SORT_BUFS,)), pltpu.SemaphoreType.DMA],
    )
    return pl.pallas_call(
        _dispatch_kernel,
        grid_spec=grid_spec,
        out_shape=jax.ShapeDtypeStruct((cap * ROW_SUBLANES, LANES), F32),
        compiler_params=_params(1),
        name="moe_dispatch",
    )(run_n, run_off, run_dst, pad_lo, pad_n, n_used, pos, h)


def _expert_kernel(be_ref, first_ref, ord_ref, seq_ref, used_ref,
                   x_ref, bgu_ref, bd_ref, wgu_hbm, wd_hbm, o_ref,
                   wgu_st, wd_st, wgu_sc, wd_sc, sem):
    i = pl.program_id(0)

    def weight_copies(e):
        return (pltpu.make_async_copy(wgu_hbm.at[e], wgu_st, sem.at[0]),
                pltpu.make_async_copy(wd_hbm.at[e], wd_st, sem.at[1]))

    @pl.when(i == 0)
    def _():
        for c in weight_copies(seq_ref[0]):
            c.start()

    @pl.when(i < used_ref[0])
    def _():
        @pl.when(first_ref[i] == 1)
        def _():
            k = ord_ref[i]
            for c in weight_copies(seq_ref[k]):
                c.wait()
            wgu_sc[...] = wgu_st[...].astype(BF16)
            wd_sc[...] = wd_st[...].astype(BF16)

            @pl.when(k + 1 < used_ref[1])
            def _():
                for c in weight_copies(seq_ref[k + 1]):
                    c.start()

        gu = jnp.dot(_load_rows(x_ref, BF16), wgu_sc[...], preferred_element_type=F32) + bgu_ref[0]
        gate = jnp.minimum(gu[:, :D_FF], SWIGLU_LIMIT)
        lin = jnp.clip(gu[:, D_FF:], -SWIGLU_LIMIT, SWIGLU_LIMIT)
        act = gate * jax.nn.sigmoid(SWIGLU_ALPHA * gate) * (lin + 1.0)
        _store_rows(o_ref, jnp.dot(act.astype(BF16), wd_sc[...], preferred_element_type=F32) + bd_ref[0])

    @pl.when(i >= used_ref[0])
    def _():
        o_ref[...] = jnp.zeros_like(o_ref)


def _experts(block_expert, block_first, block_ord, expert_seq, n_used, xs, lp):
    tm = EXPERT_TM * ROW_SUBLANES
    grid_spec = pltpu.PrefetchScalarGridSpec(
        num_scalar_prefetch=5,
        grid=(xs.shape[0] // tm,),
        in_specs=[pl.BlockSpec((tm, LANES), lambda i, be, *_: (i, 0)),
                  pl.BlockSpec((1, 1, 2 * D_FF), lambda i, be, *_: (be[i], 0, 0)),
                  pl.BlockSpec((1, 1, D_MODEL), lambda i, be, *_: (be[i], 0, 0)),
                  pl.BlockSpec(memory_space=pl.ANY),
                  pl.BlockSpec(memory_space=pl.ANY)],
        out_specs=pl.BlockSpec((tm, LANES), lambda i, be, *_: (i, 0)),
        scratch_shapes=[pltpu.VMEM((D_MODEL, 2 * D_FF), F32), pltpu.VMEM((D_FF, D_MODEL), F32),
                        pltpu.VMEM((D_MODEL, 2 * D_FF), BF16), pltpu.VMEM((D_FF, D_MODEL), BF16),
                        pltpu.SemaphoreType.DMA((2,))],
    )
    return pl.pallas_call(
        _expert_kernel,
        grid_spec=grid_spec,
        out_shape=jax.ShapeDtypeStruct(xs.shape, F32),
        compiler_params=_params(1),
        name="moe_experts",
    )(block_expert, block_first, block_ord, expert_seq, n_used, xs,
      lp['b_gu'].reshape(N_EXPERTS, 1, 2 * D_FF), lp['b_down'].reshape(N_EXPERTS, 1, D_MODEL),
      lp['w_gu'], lp['w_down'])


def _combine_kernel(n_ref, off_ref, dst_ref, pos_ref, gate_ref, h_ref, ys_hbm, g_ref, b_ref,
                    y1_ref, y2_ref, sorted_sc, w_sc, sem, *, n_first):
    i = pl.program_id(0)
    n_tiles = pl.num_programs(0) - 1
    tm = h_ref.shape[0]
    n_slots = TOP_K * tm
    slot = lax.rem(i, 2)
    buf = sorted_sc.at[slot]
    pos = pos_ref[...]
    gates = gate_ref[...]
    rows_per = tm // N_EXPERTS
    base = jnp.minimum(i, n_tiles - 1) * N_EXPERTS
    for e in range(N_EXPERTS):
        off = off_ref[base + e]
        dst = dst_ref[base + e]

        def run_start(start, size, off=off, dst=dst):
            pltpu.make_async_copy(ys_hbm.at[_rows(dst + start, size)],
                                  buf.at[_rows(off + start, size)], sem.at[slot]).start()

        _for_each_run_piece(jnp.where(i < n_tiles, n_ref[base + e], 0), tm, run_start)

        r = slice(e * rows_per, (e + 1) * rows_per)
        scol = lax.broadcasted_iota(I32, (rows_per, n_slots), 1)
        w_sc[r, :] = jnp.where(
            scol == pos[r, 0:1], gates[r, 0:1],
            jnp.where(scol == pos[r, 1:2], gates[r, 1:2],
                      jnp.where(scol == pos[r, 2:3], gates[r, 2:3],
                                jnp.where(scol == pos[r, 3:4], gates[r, 3:4], 0.0)))).astype(BF16)

    @pl.when(i >= 1)
    def _():
        done = sorted_sc.at[1 - slot]
        pltpu.make_async_copy(ys_hbm.at[_rows(0, n_slots)], done, sem.at[1 - slot]).wait()

        f = jnp.dot(w_sc[...], _load_rows(done, BF16), preferred_element_type=F32)
        y = _layer_norm(DEEPNORM_ALPHA * h_ref[...] + f, g_ref[...], b_ref[...])

        @pl.when(i - 1 < n_first)
        def _():
            y1_ref[...] = y

        @pl.when(i - 1 >= n_first)
        def _():
            y2_ref[...] = y


def _combine(run_n, run_off, run_dst, pos_t, gates_t, h, ys, lp, t_first):
    t = h.shape[0]
    tm = TOKEN_TM
    n_first = t_first // tm
    n_rest = (t - t_first) // tm
    c2 = lambda i, *_: (0, 0)
    done = lambda i, *_: (jnp.maximum(i - 1, 0), 0)
    done1 = lambda i, *_: (jnp.clip(i - 1, 0, n_first - 1), 0)
    done2 = lambda i, *_: (jnp.clip(i - 1 - n_first, 0, n_rest - 1), 0)
    grid_spec = pltpu.PrefetchScalarGridSpec(
        num_scalar_prefetch=3,
        grid=(t // tm + 1,),
        in_specs=[pl.BlockSpec((tm, TOP_K), done),
                  pl.BlockSpec((tm, TOP_K), done),
                  pl.BlockSpec((tm, D_MODEL), done),
                  pl.BlockSpec(memory_space=pl.ANY),
                  pl.BlockSpec((1, D_MODEL), c2), pl.BlockSpec((1, D_MODEL), c2)],
        out_specs=[pl.BlockSpec((tm, D_MODEL), done1), pl.BlockSpec((tm, D_MODEL), done2)],
        scratch_shapes=[pltpu.VMEM((2, TOP_K * tm * ROW_SUBLANES, LANES), F32),
                        pltpu.VMEM((tm, TOP_K * tm), BF16),
                        pltpu.SemaphoreType.DMA((2,))],
    )
    return pl.pallas_call(
        functools.partial(_combine_kernel, n_first=n_first),
        grid_spec=grid_spec,
        out_shape=[jax.ShapeDtypeStruct((t_first, D_MODEL), F32),
                   jax.ShapeDtypeStruct((t - t_first, D_MODEL), F32)],
        compiler_params=_params(1),
        name="moe_combine",
    )(run_n, run_off, run_dst, pos_t, gates_t, h, ys,
      lp['ln2_g'].reshape(1, D_MODEL), lp['ln2_b'].reshape(1, D_MODEL))


def _moe_and_norm(h, pos, gates, tile_counts, lp, t_first):
    t = h.shape[0]
    te = EXPERT_TM
    n_tiles = t // TOKEN_TM
    n_blocks = (t * TOP_K) // te + N_EXPERTS
    cap = n_blocks * te
    cnt = tile_counts.reshape(n_tiles, N_EXPERTS).astype(I32)
    counts = jnp.sum(cnt, axis=0)
    padded = (counts + te - 1) // te * te
    pad_ends = jnp.cumsum(padded)
    pad_starts = pad_ends - padded
    run_dst = pad_starts[None, :] + jnp.cumsum(cnt, axis=0) - cnt
    run_off = jnp.cumsum(cnt, axis=1) - cnt
    blk_start = jnp.arange(n_blocks, dtype=I32) * te
    expert_of = lambda slot_idx: jnp.minimum(jnp.sum(slot_idx[..., None] >= pad_ends, axis=-1), N_EXPERTS - 1)
    total = pad_ends[-1]
    be = jnp.where(blk_start < total, expert_of(blk_start), expert_of(jnp.maximum(total - 1, 0))).astype(I32)
    is_e = be[:, None] == jnp.arange(N_EXPERTS, dtype=I32)[None, :]
    pick = lambda table: jnp.sum(jnp.where(is_e, table[None, :], 0), axis=1)
    in_use = counts > 0
    ordinal = jnp.cumsum(in_use.astype(I32)) - 1
    rank = jnp.arange(N_EXPERTS, dtype=I32)
    expert_seq = jnp.sum(jnp.where(in_use[None, :] & (ordinal[None, :] == rank[:, None]), rank[None, :], 0),
                         axis=1)
    block_ord = pick(ordinal)
    block_first = (blk_start == pick(pad_starts)) & (blk_start < total)
    used = jnp.stack([total // te, jnp.sum(in_use.astype(I32))]).astype(I32)
    flat = lambda a: a.reshape(-1).astype(I32)
    xs = _dispatch(flat(cnt), flat(run_off), flat(run_dst), flat(pad_starts + counts),
                   flat(padded - counts), used, pos, h, cap)
    ys = _experts(be, flat(block_first), flat(block_ord), expert_seq, used, xs, lp)
    return _combine(flat(cnt), flat(run_off), flat(run_dst), pos.T, gates.T, h, ys, lp, t_first)


def kernel(x_prompt, x_sample, cache_attn_k, cache_attn_v, cache_mem_k, cache_mem_v, state_ssm_re, state_ssm_im, mem_prompt, w_in, lam_re, lam_im, log_dt, ssm_b_re, ssm_b_im, ssm_c_re, ssm_c_im, ssm_d, w_glu, b_glu, rel_bias, w_mem_kv, g_ssm, g_att, g_mem, w_out, ln1_g, ln1_b, w_router, b_router, w_gu, b_gu, w_down, b_down, ln2_g, ln2_b):
    assert w_in.shape[0] == 1, "single-layer step"
    lp = dict(w_in=w_in[0], lam_re=lam_re[0], lam_im=lam_im[0], log_dt=log_dt[0],
              ssm_b_re=ssm_b_re[0], ssm_b_im=ssm_b_im[0], ssm_c_re=ssm_c_re[0], ssm_c_im=ssm_c_im[0],
              ssm_d=ssm_d[0], w_glu=w_glu[0], b_glu=b_glu[0], rel_bias=rel_bias[0],
              w_mem_kv=w_mem_kv[0], g_ssm=g_ssm[0], g_att=g_att[0], g_mem=g_mem[0], w_out=w_out[0],
              ln1_g=ln1_g[0], ln1_b=ln1_b[0], w_router=w_router[0], b_router=b_router[0],
              w_gu=w_gu[0], b_gu=b_gu[0], w_down=w_down[0], b_down=b_down[0],
              ln2_g=ln2_g[0], ln2_b=ln2_b[0])

    bp, sp, _ = x_prompt.shape
    bs, ss, _ = x_sample.shape
    t_p = bp * sp
    t_all = t_p + bs * ss
    heads = lambda a: a.reshape(a.shape[0], a.shape[1], N_HEADS, HEAD_DIM)
    state = lambda a: a.reshape(a.shape[0], N_GROUPS, SSM_STATE)

    u_p, zr, zb = _in_proj(x_prompt, lp['w_in'], min(128, sp))
    mk, mv = _mem_kv(mem_prompt, lp['w_mem_kv'])
    ya, ym = _attn_prompt(zb, mk, mv, lp['rel_bias'])
    zeros = jnp.zeros((bp, D_STATE), F32)
    ys_p, sr_p, si_p = _ssm(u_p, zeros, zeros, lp, bp)
    merged = _merge(x_prompt, ys_p, ya, ym, lp, 1, TOKEN_TM, t_all, 0)
    w = min(BAND, sp)
    k_p = heads(zr[:, sp - w:, D_ATT:2 * D_ATT])
    v_p = heads(zr[:, sp - w:, 2 * D_ATT:3 * D_ATT])

    wc = cache_attn_k.shape[2]
    u_s, zr_s, zb_s = _in_proj(x_sample, lp['w_in'], ss)
    ya_s, ym_s, nk, nv = _attn_sample(
        zr_s, zb_s, cache_attn_k[0].reshape(bs, wc, D_ATT), cache_attn_v[0].reshape(bs, wc, D_ATT),
        cache_mem_k[0].reshape(bs, N_MEM, D_MEM), cache_mem_v[0].reshape(bs, N_MEM, D_MEM),
        lp['rel_bias'])
    ys_s, sr_s, si_s = _ssm(u_s, state_ssm_re[0], state_ssm_im[0], lp, bs)
    merged = _merge(x_sample, ys_s, ya_s, ym_s, lp, TOKEN_TM // ss, ss, t_all, t_p // TOKEN_TM,
                    carried=merged)

    y_p, y_s = _moe_and_norm(*merged, lp, t_p)

    return (y_p.reshape(bp, sp, D_MODEL), y_s.reshape(bs, ss, D_MODEL),
            k_p[None], v_p[None], heads(mk)[None], heads(mv)[None], state(sr_p)[None], state(si_p)[None],
            heads(nk)[None], heads(nv)[None], state(sr_s)[None], state(si_s)[None])
```

```python
import functools

import jax
import jax.numpy as jnp
from jax import lax
from jax.experimental import pallas as pl
from jax.experimental.pallas import tpu as pltpu

F32 = jnp.float32
BF16 = jnp.bfloat16
I32 = jnp.int32

D_MODEL = 1024
D_SSM = 512
D_ATT = 256
D_MEM = 256
D_IN = D_SSM + 3 * D_ATT + D_MEM
D_REST = D_IN - D_SSM
HEAD_DIM = 64
N_HEADS = 4
N_GROUPS = 32
SSM_GROUP = 16
SSM_STATE = 64
D_STATE = N_GROUPS * SSM_STATE
CHUNK = 64
N_PREV_CHUNKS = 8
BAND = N_PREV_CHUNKS * CHUNK
REL_CLIP = 128
N_MEM = 256
N_EXPERTS = 32
TOP_K = 4
D_FF = D_MODEL
SWIGLU_LIMIT = 7.0
SWIGLU_ALPHA = 1.702
LN_EPS = 1e-5
NEG_INF = -1e30
ATT_SCALE = HEAD_DIM ** -0.5
DEEPNORM_ALPHA = 2.0 ** 0.25

V7X_VMEM_LIMIT = 56 * 1024 * 1024
ATT_TQ = 4 * CHUNK
SCAN_LANES = 1024
SCAN_ROWS = 1024
LANES = 128
ROW_SUBLANES = D_MODEL // LANES
EXPERT_TM = 512
TOKEN_TM = 512
N_SORT_BUFS = 3
SORT_CHUNKS = 16

_NT = (((1,), (1,)), ((), ()))


def _params(n_axes, vmem=V7X_VMEM_LIMIT):
    return pltpu.CompilerParams(dimension_semantics=("arbitrary",) * n_axes,
                                vmem_limit_bytes=vmem)


def _in_proj_kernel(x_ref, w_ref, u_ref, kv_ref, zb_ref, wb_ref):
    @pl.when(pl.program_id(0) == 0)
    def _():
        wb_ref[...] = w_ref[...].astype(BF16)

    nb, ts, _ = x_ref.shape
    x = x_ref[...].reshape(nb * ts, D_MODEL).astype(BF16)
    z = jnp.dot(x, wb_ref[...], preferred_element_type=F32)
    for b in range(nb):
        for c in range(D_SSM // LANES):
            u_ref[c, pl.ds(b, ts, stride=nb), :] = z[b * ts:(b + 1) * ts, LANES * c:LANES * (c + 1)]
    zr = z[:, D_SSM:]
    kv_ref[...] = zr[:, D_ATT:3 * D_ATT].reshape(nb, ts, 2 * D_ATT)
    zb = jnp.concatenate([zr[:, :D_ATT] * ATT_SCALE, zr[:, D_ATT:3 * D_ATT], zr[:, 3 * D_ATT:] * ATT_SCALE],
                         axis=1)
    zb_ref[...] = zb.astype(BF16).reshape(nb, ts, D_REST)


def _in_proj(x, w_in, ts, tail):
    b, s, _ = x.shape
    skipped = (s - tail) // ts
    return pl.pallas_call(
        _in_proj_kernel,
        grid=(s // ts,),
        in_specs=[pl.BlockSpec((b, ts, D_MODEL), lambda j: (0, j, 0)),
                  pl.BlockSpec((D_MODEL, D_IN), lambda j: (0, 0))],
        out_specs=[pl.BlockSpec((D_SSM // LANES, ts * b, LANES), lambda j: (0, j, 0)),
                   pl.BlockSpec((b, ts, 2 * D_ATT), lambda j: (0, jnp.maximum(j - skipped, 0), 0)),
                   pl.BlockSpec((b, ts, D_REST), lambda j: (0, j, 0))],
        out_shape=[jax.ShapeDtypeStruct((D_SSM // LANES, s * b, LANES), F32),
                   jax.ShapeDtypeStruct((b, tail, 2 * D_ATT), F32),
                   jax.ShapeDtypeStruct((b, s, D_REST), BF16)],
        scratch_shapes=[pltpu.VMEM((D_MODEL, D_IN), BF16)],
        compiler_params=_params(1),
        name="in_proj",
    )(x, w_in)


def _mem_kv_kernel(m_ref, w_ref, mk_ref, mv_ref):
    kv = jnp.dot(m_ref[0].astype(BF16), w_ref[...].astype(BF16), preferred_element_type=F32)
    mk_ref[0] = kv[:, :D_MEM]
    mv_ref[0] = kv[:, D_MEM:]


def _mem_kv(mem, w_mem_kv):
    b = mem.shape[0]
    return pl.pallas_call(
        _mem_kv_kernel,
        grid=(b,),
        in_specs=[pl.BlockSpec((1, N_MEM, D_MODEL), lambda i: (i, 0, 0)),
                  pl.BlockSpec((D_MODEL, 2 * D_MEM), lambda i: (0, 0))],
        out_specs=[pl.BlockSpec((1, N_MEM, D_MEM), lambda i: (i, 0, 0)),
                   pl.BlockSpec((1, N_MEM, D_MEM), lambda i: (i, 0, 0))],
        out_shape=[jax.ShapeDtypeStruct((b, N_MEM, D_MEM), F32)] * 2,
        compiler_params=_params(1),
        name="mem_kv",
    )(mem, w_mem_kv)


def _ssm_kernel(u_ref, h0r_ref, h0i_ref, lr_ref, li_ref, ldt_ref, bre_ref, bim_ref,
                cre_ref, cim_ref, d_ref, wg_ref, bg_ref,
                y_ref, sr_ref, si_ref,
                a_sc, bbr_sc, bbi_sc, cr_sc, ci_sc, wg_sc, str_sc, sti_sc, xr_sc, xi_sc,
                *, n_batch):
    n_rows = u_ref.shape[1]
    n_steps = n_rows // n_batch

    @pl.when(pl.program_id(0) == 0)
    def _():
        lr = lr_ref[...]
        li = li_ref[...]
        dt = jnp.exp(ldt_ref[...])
        mag = jnp.exp(lr * dt)
        ar = mag * jnp.cos(li * dt)
        ai = mag * jnp.sin(li * dt)
        den = lr * lr + li * li
        fr = ((ar - 1.0) * lr + ai * li) / den
        fi = (ai * lr - (ar - 1.0) * li) / den
        a_sc[0:1, :] = ar
        a_sc[1:2, :] = ai
        for j in range(4):
            frj = fr[:, 512 * j:512 * (j + 1)]
            fij = fi[:, 512 * j:512 * (j + 1)]
            bbr_sc[j] = (frj * bre_ref[j] - fij * bim_ref[j]).astype(BF16)
            bbi_sc[j] = (frj * bim_ref[j] + fij * bre_ref[j]).astype(BF16)
            cr_sc[j] = cre_ref[j].astype(BF16)
            ci_sc[j] = cim_ref[j].astype(BF16)
        wg_sc[...] = wg_ref[...].astype(BF16)
        str_sc[...] = h0r_ref[...]
        sti_sc[...] = h0i_ref[...]

    for j in range(4):
        uc = u_ref[j].astype(BF16)
        xr_sc[:, 512 * j:512 * (j + 1)] = jnp.dot(uc, bbr_sc[j], preferred_element_type=F32)
        xi_sc[:, 512 * j:512 * (j + 1)] = jnp.dot(uc, bbi_sc[j], preferred_element_type=F32)

    for c in range(D_STATE // SCAN_LANES):
        lo = c * SCAN_LANES
        ar = jnp.broadcast_to(a_sc[0:1, lo:lo + SCAN_LANES], (n_batch, SCAN_LANES))
        ai = jnp.broadcast_to(a_sc[1:2, lo:lo + SCAN_LANES], (n_batch, SCAN_LANES))

        def step(t, carry, lo=lo, ar=ar, ai=ai):
            sr, si = carry
            r0 = pl.multiple_of(t * n_batch, n_batch)
            nr = ar * sr - ai * si + xr_sc[pl.ds(r0, n_batch), lo:lo + SCAN_LANES]
            ni = ar * si + ai * sr + xi_sc[pl.ds(r0, n_batch), lo:lo + SCAN_LANES]
            xr_sc[pl.ds(r0, n_batch), lo:lo + SCAN_LANES] = nr
            xi_sc[pl.ds(r0, n_batch), lo:lo + SCAN_LANES] = ni
            return nr, ni

        sr, si = lax.fori_loop(0, n_steps, step,
                               (str_sc[:, lo:lo + SCAN_LANES], sti_sc[:, lo:lo + SCAN_LANES]),
                               unroll=True)
        str_sc[:, lo:lo + SCAN_LANES] = sr
        sti_sc[:, lo:lo + SCAN_LANES] = si

    pieces = []
    for j in range(4):
        xr = xr_sc[:, 512 * j:512 * (j + 1)].astype(BF16)
        xi = xi_sc[:, 512 * j:512 * (j + 1)].astype(BF16)
        pieces.append(jnp.dot(xr, cr_sc[j], preferred_element_type=F32)
                      - jnp.dot(xi, ci_sc[j], preferred_element_type=F32))
    u = jnp.concatenate([u_ref[j] for j in range(4)], axis=1)
    y = jnp.concatenate(pieces, axis=1) + d_ref[...] * u
    y = jax.nn.gelu(y)
    z = jnp.dot(y.astype(BF16), wg_sc[...], preferred_element_type=F32) + bg_ref[...]
    out = z[:, :D_SSM] * jax.nn.sigmoid(z[:, D_SSM:])
    for j in range(D_SSM // LANES):
        y_ref[j] = out[:, LANES * j:LANES * (j + 1)]
    sr_ref[...] = str_sc[...]
    si_ref[...] = sti_sc[...]


def _block_diag_b(b):
    bt = b.transpose(0, 2, 1).reshape(4, 8, SSM_GROUP, SSM_STATE)
    same = jnp.eye(8, dtype=bool)[None, :, None, :, None]
    t = jnp.where(same, bt[:, :, :, None, :], 0.0)
    return t.reshape(4, 8 * SSM_GROUP, 8 * SSM_STATE)


def _block_diag_c(c):
    ct = c.transpose(0, 2, 1).reshape(4, 8, SSM_STATE, SSM_GROUP)
    same = jnp.eye(8, dtype=bool)[None, :, None, :, None]
    t = jnp.where(same, ct[:, :, :, None, :], 0.0)
    return t.reshape(4, 8 * SSM_STATE, 8 * SSM_GROUP)


def _ssm(u_rows, h0_re, h0_im, lp, n_batch):
    rows = u_rows.shape[1]
    planes = D_SSM // LANES
    tr = min(SCAN_ROWS, rows)
    flat = lambda a: a.reshape(1, D_STATE)
    ldt = jnp.repeat(lp['log_dt'], SSM_STATE).reshape(1, D_STATE)
    const2 = lambda i: (0, 0)
    const3 = lambda i: (0, 0, 0)
    y, sr, si = pl.pallas_call(
        functools.partial(_ssm_kernel, n_batch=n_batch),
        grid=(rows // tr,),
        in_specs=[pl.BlockSpec((planes, tr, LANES), lambda i: (0, i, 0)),
                  pl.BlockSpec((n_batch, D_STATE), const2),
                  pl.BlockSpec((n_batch, D_STATE), const2),
                  pl.BlockSpec((1, D_STATE), const2),
                  pl.BlockSpec((1, D_STATE), const2),
                  pl.BlockSpec((1, D_STATE), const2),
                  pl.BlockSpec((4, 128, 512), const3),
                  pl.BlockSpec((4, 128, 512), const3),
                  pl.BlockSpec((4, 512, 128), const3),
                  pl.BlockSpec((4, 512, 128), const3),
                  pl.BlockSpec((1, D_SSM), const2),
                  pl.BlockSpec((D_SSM, 2 * D_SSM), const2),
                  pl.BlockSpec((1, 2 * D_SSM), const2)],
        out_specs=[pl.BlockSpec((planes, tr, LANES), lambda i: (0, i, 0)),
                   pl.BlockSpec((n_batch, D_STATE), const2),
                   pl.BlockSpec((n_batch, D_STATE), const2)],
        out_shape=[jax.ShapeDtypeStruct((planes, rows, LANES), F32),
                   jax.ShapeDtypeStruct((n_batch, D_STATE), F32),
                   jax.ShapeDtypeStruct((n_batch, D_STATE), F32)],
        scratch_shapes=[pltpu.VMEM((2, D_STATE), F32),
                        pltpu.VMEM((4, 128, 512), BF16), pltpu.VMEM((4, 128, 512), BF16),
                        pltpu.VMEM((4, 512, 128), BF16), pltpu.VMEM((4, 512, 128), BF16),
                        pltpu.VMEM((D_SSM, 2 * D_SSM), BF16),
                        pltpu.VMEM((n_batch, D_STATE), F32), pltpu.VMEM((n_batch, D_STATE), F32),
                        pltpu.VMEM((tr, D_STATE), F32), pltpu.VMEM((tr, D_STATE), F32)],
        compiler_params=_params(1),
        name="ssm",
    )(u_rows, h0_re.reshape(n_batch, D_STATE), h0_im.reshape(n_batch, D_STATE),
      flat(lp['lam_re']), flat(lp['lam_im']), ldt,
      _block_diag_b(lp['ssm_b_re']), _block_diag_b(lp['ssm_b_im']),
      _block_diag_c(lp['ssm_c_re']), _block_diag_c(lp['ssm_c_im']),
      lp['ssm_d'].reshape(1, D_SSM), lp['w_glu'], lp['b_glu'].reshape(1, 2 * D_SSM))
    return y, sr, si


def _softmax_pv(s, v):
    m = jnp.max(s, axis=-1, keepdims=True)
    p = jnp.exp(s - m)
    l = jnp.sum(p, axis=-1, keepdims=True)
    return jnp.dot(p.astype(BF16), v, preferred_element_type=F32) / l


def _attend(qb, k, v, out_ref, row0=0, bias_ref=None, valid=None):
    tq = qb.shape[0]
    for h in range(N_HEADS):
        sl = slice(HEAD_DIM * h, HEAD_DIM * (h + 1))
        s = lax.dot_general(qb[:, sl], k[:, sl], _NT, preferred_element_type=F32)
        if bias_ref is not None:
            s = s + bias_ref[h]
        if valid is not None:
            s = jnp.where(valid, s, NEG_INF)
        out_ref[0, row0:row0 + tq, sl] = _softmax_pv(s, v[:, sl])


def _attn_prompt_kernel(q_ref, kp_ref, kc_ref, vp_ref, vc_ref, qm_ref, mk_ref, mv_ref, bias_ref,
                        ya_ref, ym_ref, bias_sc):
    tq = ATT_TQ

    @pl.when((pl.program_id(0) == 0) & (pl.program_id(1) == 0))
    def _():
        q_chunk = lax.broadcasted_iota(I32, (tq, 3 * tq), 0) // CHUNK
        k_chunk = lax.broadcasted_iota(I32, (tq, 3 * tq), 1) // CHUNK
        ahead = k_chunk - q_chunk
        for h in range(N_HEADS):
            bias_sc[h] = jnp.where(ahead >= 0, jnp.where(ahead <= N_PREV_CHUNKS, bias_ref[h], NEG_INF),
                                   NEG_INF)

    k = jnp.concatenate([kp_ref[0], kc_ref[0]], axis=0)
    v = jnp.concatenate([vp_ref[0], vc_ref[0]], axis=0)
    for half in range(2):
        first_key = (2 * pl.program_id(1) - 2 + half) * tq
        kpos = first_key + lax.broadcasted_iota(I32, (1, 3 * tq), 1)
        _attend(q_ref[0, half * tq:(half + 1) * tq, :], k[half * tq:(half + 3) * tq],
                v[half * tq:(half + 3) * tq], ya_ref, half * tq, bias_sc, kpos >= 0)
    _attend(qm_ref[0], mk_ref[0].astype(BF16), mv_ref[0].astype(BF16), ym_ref)


def _attn_sample_kernel(q_ref, kn_ref, vn_ref, qm_ref, ck_ref, cv_ref, mk_ref, mv_ref, bias_ref,
                        ya_ref, ym_ref, nk_ref, nv_ref):
    n = kn_ref.shape[1]
    kk = jnp.concatenate([ck_ref[0], kn_ref[0]], axis=0)
    vv = jnp.concatenate([cv_ref[0], vn_ref[0]], axis=0)
    nk_ref[0] = kk[n:]
    nv_ref[0] = vv[n:]
    _attend(q_ref[0], kk.astype(BF16), vv.astype(BF16), ya_ref, 0, bias_ref)
    _attend(qm_ref[0], mk_ref[0].astype(BF16), mv_ref[0].astype(BF16), ym_ref)


def _rel_bias(table, n_q, n_k):
    period = n_q + n_k
    m = jnp.arange(period)
    offset = jnp.where(m < n_k, m, m - period)
    idx = jnp.clip(BAND - offset, -REL_CLIP, REL_CLIP) + REL_CLIP
    f = table.astype(F32)[:, idx]
    flat = jnp.tile(f, (1, n_q))[:, :n_q * (period - 1)]
    return flat.reshape(N_HEADS, n_q, period - 1)[:, :, :n_k]


def _attn_prompt(zb, mk, mv, table):
    b, s, _ = zb.shape
    tq = ATT_TQ
    bias = _rel_bias(table, tq, 3 * tq)
    col = lambda c: (lambda i, j: (i, j, c))
    prev = lambda c: (lambda i, j: (i, jnp.maximum(j - 1, 0), c))
    blk = (1, 2 * tq, D_ATT)
    return pl.pallas_call(
        _attn_prompt_kernel,
        grid=(b, s // (2 * tq)),
        in_specs=[pl.BlockSpec(blk, col(0)),
                  pl.BlockSpec(blk, prev(1)), pl.BlockSpec(blk, col(1)),
                  pl.BlockSpec(blk, prev(2)), pl.BlockSpec(blk, col(2)),
                  pl.BlockSpec(blk, col(3)),
                  pl.BlockSpec((1, N_MEM, D_MEM), lambda i, j: (i, 0, 0)),
                  pl.BlockSpec((1, N_MEM, D_MEM), lambda i, j: (i, 0, 0)),
                  pl.BlockSpec((N_HEADS, tq, 3 * tq), lambda i, j: (0, 0, 0))],
        out_specs=[pl.BlockSpec(blk, col(0)), pl.BlockSpec(blk, col(0))],
        out_shape=[jax.ShapeDtypeStruct((b, s, D_ATT), F32),
                   jax.ShapeDtypeStruct((b, s, D_MEM), F32)],
        scratch_shapes=[pltpu.VMEM((N_HEADS, tq, 3 * tq), F32)],
        compiler_params=_params(2),
        name="attn_prompt",
    )(zb, zb, zb, zb, zb, zb, mk, mv, bias)


def _attn_sample(kv, zb, cache_k, cache_v, mk, mv, table):
    b, n, _ = kv.shape
    w = cache_k.shape[1]
    bias = _rel_bias(table, n, w + n)
    col = lambda c: (lambda i: (i, 0, c))
    blk = (1, n, D_ATT)
    cblk = (1, w, D_ATT)
    mblk = (1, N_MEM, D_MEM)
    row = lambda i: (i, 0, 0)
    return pl.pallas_call(
        _attn_sample_kernel,
        grid=(b,),
        in_specs=[pl.BlockSpec(blk, col(0)), pl.BlockSpec(blk, col(0)), pl.BlockSpec(blk, col(1)),
                  pl.BlockSpec(blk, col(3)),
                  pl.BlockSpec(cblk, row), pl.BlockSpec(cblk, row),
                  pl.BlockSpec(mblk, row), pl.BlockSpec(mblk, row),
                  pl.BlockSpec((N_HEADS, n, w + n), lambda i: (0, 0, 0))],
        out_specs=[pl.BlockSpec(blk, row), pl.BlockSpec(blk, row),
                   pl.BlockSpec(cblk, row), pl.BlockSpec(cblk, row)],
        out_shape=[jax.ShapeDtypeStruct((b, n, D_ATT), F32),
                   jax.ShapeDtypeStruct((b, n, D_MEM), F32),
                   jax.ShapeDtypeStruct((b, w, D_ATT), F32),
                   jax.ShapeDtypeStruct((b, w, D_ATT), F32)],
        compiler_params=_params(1),
        name="attn_sample",
    )(zb, kv, kv, zb, cache_k, cache_v, mk, mv, bias)


def _rms(x, g):
    return x * lax.rsqrt(jnp.mean(jnp.square(x), axis=-1, keepdims=True) + LN_EPS) * g


def _layer_norm(x, g, b):
    mu = jnp.mean(x, axis=-1, keepdims=True)
    xc = x - mu
    var = jnp.mean(jnp.square(xc), axis=-1, keepdims=True)
    return xc * lax.rsqrt(var + LN_EPS) * g + b


def _split_bf16(a):
    hi = a.astype(BF16)
    lo = (a - hi.astype(F32)).astype(BF16)
    return hi, lo


def _merge_kernel(*refs, nb, n_carried):
    (x_ref, ys_ref, ya_ref, ym_ref, gs_ref, ga_ref, gm_ref, wo_ref, l1g_ref, l1b_ref,
     wrt_ref, brt_ref) = refs[:12]
    h_ref, pos_ref, gate_ref, cnt_ref, wo_sc = refs[12 + n_carried:]
    st = x_ref.shape[1]
    tm = nb * st
    n_batch = ys_ref.shape[1] // st

    @pl.when((pl.program_id(0) == 0) & (pl.program_id(1) == 0))
    def _():
        wo_sc[...] = wo_ref[...].astype(BF16)

    x = x_ref[...].reshape(tm, D_MODEL)
    first = pl.program_id(1) * nb
    ys = jnp.concatenate(
        [jnp.concatenate([ys_ref[c, pl.ds(first + i, st, stride=n_batch), :] for c in range(D_SSM // LANES)],
                         axis=1) for i in range(nb)], axis=0)
    ya = ya_ref[...].reshape(tm, D_ATT)
    ym = ym_ref[...].reshape(tm, D_MEM)
    a = _rms(ys, gs_ref[...]).astype(BF16)
    b = _rms(ya, ga_ref[...]).astype(BF16)
    c = _rms(ym, gm_ref[...]).astype(BF16)
    mix = (jnp.dot(a, wo_sc[0:D_SSM, :], preferred_element_type=F32)
           + jnp.dot(b, wo_sc[D_SSM:D_SSM + D_ATT, :], preferred_element_type=F32)
           + jnp.dot(c, wo_sc[D_SSM + D_ATT:, :], preferred_element_type=F32))
    h = _layer_norm(DEEPNORM_ALPHA * x + mix, l1g_ref[...], l1b_ref[...])
    h_ref[...] = h

    h_hi, h_lo = _split_bf16(h)
    w_hi, w_lo = _split_bf16(wrt_ref[...])
    logits = (lax.dot_general(w_hi, h_hi, _NT, preferred_element_type=F32)
              + lax.dot_general(w_hi, h_lo, _NT, preferred_element_type=F32)
              + lax.dot_general(w_lo, h_hi, _NT, preferred_element_type=F32)
              + brt_ref[...])
    erow = lax.broadcasted_iota(I32, (N_EXPERTS, tm), 0).astype(F32)
    tops, picks = [], []
    l = logits
    for k in range(TOP_K):
        m = jnp.max(l, axis=0, keepdims=True)
        e = jnp.min(jnp.where(l == m, erow, float(N_EXPERTS)), axis=0, keepdims=True)
        pick = erow == e
        tops.append(m)
        picks.append(jnp.where(pick, 1.0, 0.0))
        l = jnp.where(pick, -jnp.inf, l)
    ex = [jnp.exp(t - tops[0]) for t in tops]
    den = ex[0] + ex[1] + ex[2] + ex[3]
    for k in range(TOP_K):
        gate_ref[k:k + 1, :] = ex[k] / den

    chosen = picks[0] + picks[1] + picks[2] + picks[3]
    chosen_b = chosen.astype(BF16)
    earlier_tok = (lax.broadcasted_iota(I32, (tm, tm), 0) < lax.broadcasted_iota(I32, (tm, tm), 1))
    within = jnp.dot(chosen_b, jnp.where(earlier_tok, 1.0, 0.0).astype(BF16),
                     preferred_element_type=F32)
    lower_exp = (lax.broadcasted_iota(I32, (N_EXPERTS, N_EXPERTS), 1)
                 < lax.broadcasted_iota(I32, (N_EXPERTS, N_EXPERTS), 0))
    below = jnp.dot(jnp.where(lower_exp, 1.0, 0.0).astype(BF16), chosen_b,
                    preferred_element_type=F32)
    slot = within + jnp.sum(below, axis=1, keepdims=True)
    for k in range(TOP_K):
        pos_ref[k:k + 1, :] = jnp.sum(picks[k] * slot, axis=0, keepdims=True).astype(I32)
    cnt_ref[0] = jnp.sum(chosen, axis=1, keepdims=True)


def _merge(x, ys_tm, ya, ym, lp, nb, st, t_all, tile0, carried=None):
    b, s, _ = x.shape
    tm = nb * st
    assert tm == TOKEN_TM
    n_s = s // st
    tile = lambda j, i: (tile0 + i * n_s + j)
    c2 = lambda j, i: (0, 0)
    row3 = lambda j, i: (i, j, 0)
    vec = lambda a: a.reshape(1, -1)
    carried = () if carried is None else tuple(carried)
    return pl.pallas_call(
        functools.partial(_merge_kernel, nb=nb, n_carried=len(carried)),
        grid=(n_s, b // nb),
        in_specs=[pl.BlockSpec((nb, st, D_MODEL), row3),
                  pl.BlockSpec((D_SSM // LANES, st * b, LANES), lambda j, i: (0, j, 0)),
                  pl.BlockSpec((nb, st, D_ATT), row3),
                  pl.BlockSpec((nb, st, D_MEM), row3),
                  pl.BlockSpec((1, D_SSM), c2), pl.BlockSpec((1, D_ATT), c2),
                  pl.BlockSpec((1, D_MEM), c2),
                  pl.BlockSpec((D_MODEL, D_MODEL), c2),
                  pl.BlockSpec((1, D_MODEL), c2), pl.BlockSpec((1, D_MODEL), c2),
                  pl.BlockSpec((N_EXPERTS, D_MODEL), c2), pl.BlockSpec((N_EXPERTS, 1), c2)]
                 + [pl.BlockSpec(memory_space=pl.ANY)] * len(carried),
        out_specs=[pl.BlockSpec((tm, D_MODEL), lambda j, i: (tile(j, i), 0)),
                   pl.BlockSpec((TOP_K, tm), lambda j, i: (0, tile(j, i))),
                   pl.BlockSpec((TOP_K, tm), lambda j, i: (0, tile(j, i))),
                   pl.BlockSpec((1, N_EXPERTS, 1), lambda j, i: (tile(j, i), 0, 0))],
        out_shape=[jax.ShapeDtypeStruct((t_all, D_MODEL), F32),
                   jax.ShapeDtypeStruct((TOP_K, t_all), I32),
                   jax.ShapeDtypeStruct((TOP_K, t_all), F32),
                   jax.ShapeDtypeStruct((t_all // tm, N_EXPERTS, 1), F32)],
        scratch_shapes=[pltpu.VMEM((D_MODEL, D_MODEL), BF16)],
        input_output_aliases={12 + k: k for k in range(len(carried))},
        compiler_params=_params(2),
        name="merge_router",
    )(x, ys_tm, ya, ym, vec(lp['g_ssm']), vec(lp['g_att']), vec(lp['g_mem']), lp['w_out'],
      vec(lp['ln1_g']), vec(lp['ln1_b']), lp['w_router'].T, lp['b_router'].reshape(N_EXPERTS, 1),
      *carried)


def _rows(start, size):
    return pl.ds(pl.multiple_of(start * ROW_SUBLANES, ROW_SUBLANES), size * ROW_SUBLANES)


def _store_rows(ref, value, row0=0):
    n = value.shape[0]
    for j in range(ROW_SUBLANES):
        ref[pl.ds(row0 * ROW_SUBLANES + j, n, stride=ROW_SUBLANES), :] = value[:, LANES * j:LANES * (j + 1)]


def _load_rows(ref, dtype=F32):
    n = ref.shape[0] // ROW_SUBLANES
    return jnp.concatenate([ref[pl.ds(j, n, stride=ROW_SUBLANES), :].astype(dtype)
                            for j in range(ROW_SUBLANES)], axis=1)


def _for_each_run_piece(n, max_rows, fn):
    for bit in reversed(range(max_rows.bit_length())):
        size = 1 << bit
        start = (n >> (bit + 1)) << (bit + 1)

        @pl.when((n & size) != 0)
        def _(start=start, size=size):
            fn(start, size)


def _dispatch_kernel(n_ref, off_ref, dst_ref, padlo_ref, padn_ref, used_ref,
                     pos_ref, h_ref, xs_hbm, sorted_sc, zero_sc, sem, zsem):
    i = pl.program_id(0)
    tm = h_ref.shape[0]
    n_slots = TOP_K * tm
    n_blocks = xs_hbm.shape[0] // (EXPERT_TM * ROW_SUBLANES)

    @pl.when(i == 0)
    def _():
        zero_sc[...] = jnp.zeros_like(zero_sc)

        def pad_copy(e, start, size):
            return pltpu.make_async_copy(zero_sc.at[_rows(0, size)],
                                         xs_hbm.at[_rows(padlo_ref[e] + start, size)], zsem)

        def tail_copy(blk):
            return pltpu.make_async_copy(zero_sc, xs_hbm.at[_rows(blk * EXPERT_TM, EXPERT_TM)], zsem)

        for e in range(N_EXPERTS):
            _for_each_run_piece(padn_ref[e], EXPERT_TM - 1,
                                lambda start, size, e=e: pad_copy(e, start, size).start())

        def tail_start(blk, carry):
            tail_copy(blk).start()
            return carry

        lax.fori_loop(used_ref[0], n_blocks, tail_start, 0)
        for e in range(N_EXPERTS):
            _for_each_run_piece(padn_ref[e], EXPERT_TM - 1,
                                lambda start, size, e=e: pad_copy(e, start, size).wait())

        def tail_wait(blk, carry):
            tail_copy(blk).wait()
            return carry

        lax.fori_loop(used_ref[0], n_blocks, tail_wait, 0)

    n_tiles = pl.num_programs(0) - 1
    slot = lax.rem(i, N_SORT_BUFS)
    prev_slot = lax.rem(i + N_SORT_BUFS - 1, N_SORT_BUFS)
    buf = sorted_sc.at[slot]
    prev = sorted_sc.at[prev_slot]

    def wait_tile(sl):
        pltpu.make_async_copy(sorted_sc.at[sl], xs_hbm.at[_rows(0, n_slots)], sem.at[sl]).wait()

    @pl.when(i >= N_SORT_BUFS)
    def _():
        wait_tile(slot)

    pos = pos_ref[...]
    hb = h_ref[...].astype(BF16)
    base = jnp.maximum(i - 1, 0) * N_EXPERTS
    rows_c = n_slots // SORT_CHUNKS
    experts_c = N_EXPERTS // SORT_CHUNKS
    for c in range(SORT_CHUNKS):
        for e in range(c * experts_c, (c + 1) * experts_c):
            off = off_ref[base + e]
            dst = dst_ref[base + e]

            def run_start(start, size, off=off, dst=dst):
                pltpu.make_async_copy(prev.at[_rows(off + start, size)],
                                      xs_hbm.at[_rows(dst + start, size)], sem.at[prev_slot]).start()

            _for_each_run_piece(jnp.where(i >= 1, n_ref[base + e], 0), tm, run_start)

        srow = lax.broadcasted_iota(I32, (rows_c, tm), 0) + c * rows_c
        perm = jnp.where(srow == pos[0:1], 1.0,
                         jnp.where(srow == pos[1:2], 1.0,
                                   jnp.where(srow == pos[2:3], 1.0,
                                             jnp.where(srow == pos[3:4], 1.0, 0.0)))).astype(BF16)
        _store_rows(buf, jnp.dot(perm, hb, preferred_element_type=F32), c * rows_c)

    @pl.when(i == n_tiles)
    def _():
        wait_tile(prev_slot)

        @pl.when(i >= 2)
        def _():
            wait_tile(lax.rem(i + N_SORT_BUFS - 2, N_SORT_BUFS))


def _dispatch(run_n, run_off, run_dst, pad_lo, pad_n, n_used, pos, h, cap):
    tm = TOKEN_TM
    n_tiles = h.shape[0] // tm
    grid_spec = pltpu.PrefetchScalarGridSpec(
        num_scalar_prefetch=6,
        grid=(n_tiles + 1,),
        in_specs=[pl.BlockSpec((TOP_K, tm), lambda i, *_: (0, jnp.minimum(i, n_tiles - 1))),
                  pl.BlockSpec((tm, D_MODEL), lambda i, *_: (jnp.minimum(i, n_tiles - 1), 0))],
        out_specs=pl.BlockSpec(memory_space=pl.ANY),
        scratch_shapes=[pltpu.VMEM((N_SORT_BUFS, TOP_K * tm * ROW_SUBLANES, LANES), F32),
                        pltpu.VMEM((EXPERT_TM * ROW_SUBLANES, LANES), F32),
                        pltpu.SemaphoreType.DMA((N_SORT_BUFS,)), pltpu.SemaphoreType.DMA],
    )
    return pl.pallas_call(
        _dispatch_kernel,
        grid_spec=grid_spec,
        out_shape=jax.ShapeDtypeStruct((cap * ROW_SUBLANES, LANES), F32),
        compiler_params=_params(1),
        name="moe_dispatch",
    )(run_n, run_off, run_dst, pad_lo, pad_n, n_used, pos, h)


def _expert_kernel(be_ref, first_ref, ord_ref, seq_ref, used_ref,
                   x_ref, bgu_ref, bd_ref, wgu_hbm, wd_hbm, o_ref,
                   wgu_st, wd_st, wgu_sc, wd_sc, sem):
    i = pl.program_id(0)

    def weight_copies(e):
        return (pltpu.make_async_copy(wgu_hbm.at[e], wgu_st, sem.at[0]),
                pltpu.make_async_copy(wd_hbm.at[e], wd_st, sem.at[1]))

    @pl.when(i == 0)
    def _():
        for c in weight_copies(seq_ref[0]):
            c.start()

    @pl.when(i < used_ref[0])
    def _():
        @pl.when(first_ref[i] == 1)
        def _():
            k = ord_ref[i]
            for c in weight_copies(seq_ref[k]):
                c.wait()
            wgu_sc[...] = wgu_st[...].astype(BF16)
            wd_sc[...] = wd_st[...].astype(BF16)

            @pl.when(k + 1 < used_ref[1])
            def _():
                for c in weight_copies(seq_ref[k + 1]):
                    c.start()

        gu = jnp.dot(_load_rows(x_ref, BF16), wgu_sc[...], preferred_element_type=F32) + bgu_ref[0]
        gate = jnp.minimum(gu[:, :D_FF], SWIGLU_LIMIT)
        lin = jnp.clip(gu[:, D_FF:], -SWIGLU_LIMIT, SWIGLU_LIMIT)
        act = gate * jax.nn.sigmoid(SWIGLU_ALPHA * gate) * (lin + 1.0)
        _store_rows(o_ref, jnp.dot(act.astype(BF16), wd_sc[...], preferred_element_type=F32) + bd_ref[0])

    @pl.when(i >= used_ref[0])
    def _():
        o_ref[...] = jnp.zeros_like(o_ref)


def _experts(block_expert, block_first, block_ord, expert_seq, n_used, xs, lp):
    tm = EXPERT_TM * ROW_SUBLANES
    grid_spec = pltpu.PrefetchScalarGridSpec(
        num_scalar_prefetch=5,
        grid=(xs.shape[0] // tm,),
        in_specs=[pl.BlockSpec((tm, LANES), lambda i, be, *_: (i, 0)),
                  pl.BlockSpec((1, 1, 2 * D_FF), lambda i, be, *_: (be[i], 0, 0)),
                  pl.BlockSpec((1, 1, D_MODEL), lambda i, be, *_: (be[i], 0, 0)),
                  pl.BlockSpec(memory_space=pl.ANY),
                  pl.BlockSpec(memory_space=pl.ANY)],
        out_specs=pl.BlockSpec((tm, LANES), lambda i, be, *_: (i, 0)),
        scratch_shapes=[pltpu.VMEM((D_MODEL, 2 * D_FF), F32), pltpu.VMEM((D_FF, D_MODEL), F32),
                        pltpu.VMEM((D_MODEL, 2 * D_FF), BF16), pltpu.VMEM((D_FF, D_MODEL), BF16),
                        pltpu.SemaphoreType.DMA((2,))],
    )
    return pl.pallas_call(
        _expert_kernel,
        grid_spec=grid_spec,
        out_shape=jax.ShapeDtypeStruct(xs.shape, F32),
        compiler_params=_params(1),
        name="moe_experts",
    )(block_expert, block_first, block_ord, expert_seq, n_used, xs,
      lp['b_gu'].reshape(N_EXPERTS, 1, 2 * D_FF), lp['b_down'].reshape(N_EXPERTS, 1, D_MODEL),
      lp['w_gu'], lp['w_down'])


def _combine_kernel(n_ref, off_ref, dst_ref, pos_ref, gate_ref, h_ref, ys_hbm, g_ref, b_ref,
                    y1_ref, y2_ref, sorted_sc, w_sc, sem, *, n_first):
    i = pl.program_id(0)
    n_tiles = pl.num_programs(0) - 1
    tm = h_ref.shape[0]
    n_slots = TOP_K * tm
    slot = lax.rem(i, 2)
    buf = sorted_sc.at[slot]
    pos = pos_ref[...]
    gates = gate_ref[...]
    rows_per = tm // N_EXPERTS
    base = jnp.minimum(i, n_tiles - 1) * N_EXPERTS
    for e in range(N_EXPERTS):
        off = off_ref[base + e]
        dst = dst_ref[base + e]

        def run_start(start, size, off=off, dst=dst):
            pltpu.make_async_copy(ys_hbm.at[_rows(dst + start, size)],
                                  buf.at[_rows(off + start, size)], sem.at[slot]).start()

        _for_each_run_piece(jnp.where(i < n_tiles, n_ref[base + e], 0), tm, run_start)

        r = slice(e * rows_per, (e + 1) * rows_per)
        scol = lax.broadcasted_iota(I32, (rows_per, n_slots), 1)
        w_sc[r, :] = jnp.where(
            scol == pos[r, 0:1], gates[r, 0:1],
            jnp.where(scol == pos[r, 1:2], gates[r, 1:2],
                      jnp.where(scol == pos[r, 2:3], gates[r, 2:3],
                                jnp.where(scol == pos[r, 3:4], gates[r, 3:4], 0.0)))).astype(BF16)

    @pl.when(i >= 1)
    def _():
        done = sorted_sc.at[1 - slot]
        pltpu.make_async_copy(ys_hbm.at[_rows(0, n_slots)], done, sem.at[1 - slot]).wait()

        f = jnp.dot(w_sc[...], _load_rows(done, BF16), preferred_element_type=F32)
        y = _layer_norm(DEEPNORM_ALPHA * h_ref[...] + f, g_ref[...], b_ref[...])

        @pl.when(i - 1 < n_first)
        def _():
            y1_ref[...] = y

        @pl.when(i - 1 >= n_first)
        def _():
            y2_ref[...] = y


def _combine(run_n, run_off, run_dst, pos_t, gates_t, h, ys, lp, t_first):
    t = h.shape[0]
    tm = TOKEN_TM
    n_first = t_first // tm
    n_rest = (t - t_first) // tm
    c2 = lambda i, *_: (0, 0)
    done = lambda i, *_: (jnp.maximum(i - 1, 0), 0)
    done1 = lambda i, *_: (jnp.clip(i - 1, 0, n_first - 1), 0)
    done2 = lambda i, *_: (jnp.clip(i - 1 - n_first, 0, n_rest - 1), 0)
    grid_spec = pltpu.PrefetchScalarGridSpec(
        num_scalar_prefetch=3,
        grid=(t // tm + 1,),
        in_specs=[pl.BlockSpec((tm, TOP_K), done),
                  pl.BlockSpec((tm, TOP_K), done),
                  pl.BlockSpec((tm, D_MODEL), done),
                  pl.BlockSpec(memory_space=pl.ANY),
                  pl.BlockSpec((1, D_MODEL), c2), pl.BlockSpec((1, D_MODEL), c2)],
        out_specs=[pl.BlockSpec((tm, D_MODEL), done1), pl.BlockSpec((tm, D_MODEL), done2)],
        scratch_shapes=[pltpu.VMEM((2, TOP_K * tm * ROW_SUBLANES, LANES), F32),
                        pltpu.VMEM((tm, TOP_K * tm), BF16),
                        pltpu.SemaphoreType.DMA((2,))],
    )
    return pl.pallas_call(
        functools.partial(_combine_kernel, n_first=n_first),
        grid_spec=grid_spec,
        out_shape=[jax.ShapeDtypeStruct((t_first, D_MODEL), F32),
                   jax.ShapeDtypeStruct((t - t_first, D_MODEL), F32)],
        compiler_params=_params(1),
        name="moe_combine",
    )(run_n, run_off, run_dst, pos_t, gates_t, h, ys,
      lp['ln2_g'].reshape(1, D_MODEL), lp['ln2_b'].reshape(1, D_MODEL))


def _moe_and_norm(h, pos, gates, tile_counts, lp, t_first):
    t = h.shape[0]
    te = EXPERT_TM
    n_tiles = t // TOKEN_TM
    n_blocks = (t * TOP_K) // te + N_EXPERTS
    cap = n_blocks * te
    cnt = tile_counts.reshape(n_tiles, N_EXPERTS).astype(I32)
    counts = jnp.sum(cnt, axis=0)
    padded = (counts + te - 1) // te * te
    pad_ends = jnp.cumsum(padded)
    pad_starts = pad_ends - padded
    run_dst = pad_starts[None, :] + jnp.cumsum(cnt, axis=0) - cnt
    run_off = jnp.cumsum(cnt, axis=1) - cnt
    blk_start = jnp.arange(n_blocks, dtype=I32) * te
    expert_of = lambda slot_idx: jnp.minimum(jnp.sum(slot_idx[..., None] >= pad_ends, axis=-1), N_EXPERTS - 1)
    total = pad_ends[-1]
    be = jnp.where(blk_start < total, expert_of(blk_start), expert_of(jnp.maximum(total - 1, 0))).astype(I32)
    is_e = be[:, None] == jnp.arange(N_EXPERTS, dtype=I32)[None, :]
    pick = lambda table: jnp.sum(jnp.where(is_e, table[None, :], 0), axis=1)
    in_use = counts > 0
    ordinal = jnp.cumsum(in_use.astype(I32)) - 1
    rank = jnp.arange(N_EXPERTS, dtype=I32)
    expert_seq = jnp.sum(jnp.where(in_use[None, :] & (ordinal[None, :] == rank[:, None]), rank[None, :], 0),
                         axis=1)
    block_ord = pick(ordinal)
    block_first = (blk_start == pick(pad_starts)) & (blk_start < total)
    used = jnp.stack([total // te, jnp.sum(in_use.astype(I32))]).astype(I32)
    flat = lambda a: a.reshape(-1).astype(I32)
    xs = _dispatch(flat(cnt), flat(run_off), flat(run_dst), flat(pad_starts + counts),
                   flat(padded - counts), used, pos, h, cap)
    ys = _experts(be, flat(block_first), flat(block_ord), expert_seq, used, xs, lp)
    return _combine(flat(cnt), flat(run_off), flat(run_dst), pos.T, gates.T, h, ys, lp, t_first)


def kernel(x_prompt, x_sample, cache_attn_k, cache_attn_v, cache_mem_k, cache_mem_v, state_ssm_re, state_ssm_im, mem_prompt, w_in, lam_re, lam_im, log_dt, ssm_b_re, ssm_b_im, ssm_c_re, ssm_c_im, ssm_d, w_glu, b_glu, rel_bias, w_mem_kv, g_ssm, g_att, g_mem, w_out, ln1_g, ln1_b, w_router, b_router, w_gu, b_gu, w_down, b_down, ln2_g, ln2_b):
    assert w_in.shape[0] == 1, "single-layer step"
    lp = dict(w_in=w_in[0], lam_re=lam_re[0], lam_im=lam_im[0], log_dt=log_dt[0],
              ssm_b_re=ssm_b_re[0], ssm_b_im=ssm_b_im[0], ssm_c_re=ssm_c_re[0], ssm_c_im=ssm_c_im[0],
              ssm_d=ssm_d[0], w_glu=w_glu[0], b_glu=b_glu[0], rel_bias=rel_bias[0],
              w_mem_kv=w_mem_kv[0], g_ssm=g_ssm[0], g_att=g_att[0], g_mem=g_mem[0], w_out=w_out[0],
              ln1_g=ln1_g[0], ln1_b=ln1_b[0], w_router=w_router[0], b_router=b_router[0],
              w_gu=w_gu[0], b_gu=b_gu[0], w_down=w_down[0], b_down=b_down[0],
              ln2_g=ln2_g[0], ln2_b=ln2_b[0])

    bp, sp, _ = x_prompt.shape
    bs, ss, _ = x_sample.shape
    t_p = bp * sp
    t_all = t_p + bs * ss
    heads = lambda a: a.reshape(a.shape[0], a.shape[1], N_HEADS, HEAD_DIM)
    state = lambda a: a.reshape(a.shape[0], N_GROUPS, SSM_STATE)

    w = min(BAND, sp)
    u_p, kv_p, zb = _in_proj(x_prompt, lp['w_in'], min(128, sp), w)
    mk, mv = _mem_kv(mem_prompt, lp['w_mem_kv'])
    ya, ym = _attn_prompt(zb, mk, mv, lp['rel_bias'])
    zeros = jnp.zeros((bp, D_STATE), F32)
    ys_p, sr_p, si_p = _ssm(u_p, zeros, zeros, lp, bp)
    merged = _merge(x_prompt, ys_p, ya, ym, lp, 1, TOKEN_TM, t_all, 0)
    k_p = heads(kv_p[:, :, :D_ATT])
    v_p = heads(kv_p[:, :, D_ATT:])

    wc = cache_attn_k.shape[2]
    u_s, kv_s, zb_s = _in_proj(x_sample, lp['w_in'], ss, ss)
    ya_s, ym_s, nk, nv = _attn_sample(
        kv_s, zb_s, cache_attn_k[0].reshape(bs, wc, D_ATT), cache_attn_v[0].reshape(bs, wc, D_ATT),
        cache_mem_k[0].reshape(bs, N_MEM, D_MEM), cache_mem_v[0].reshape(bs, N_MEM, D_MEM),
        lp['rel_bias'])
    ys_s, sr_s, si_s = _ssm(u_s, state_ssm_re[0], state_ssm_im[0], lp, bs)
    merged = _merge(x_sample, ys_s, ya_s, ym_s, lp, TOKEN_TM // ss, ss, t_all, t_p // TOKEN_TM,
                    carried=merged)

    y_p, y_s = _moe_and_norm(*merged, lp, t_p)

    return (y_p.reshape(bp, sp, D_MODEL), y_s.reshape(bs, ss, D_MODEL),
            k_p[None], v_p[None], heads(mk)[None], heads(mv)[None], state(sr_p)[None], state(si_p)[None],
            heads(nk)[None], heads(nv)[None], state(sr_s)[None], state(si_s)[None])
```

```python
import functools

import jax
import jax.numpy as jnp
from jax import lax
from jax.experimental import pallas as pl
from jax.experimental.pallas import tpu as pltpu

F32 = jnp.float32
BF16 = jnp.bfloat16
I32 = jnp.int32

D_MODEL = 1024
D_SSM = 512
D_ATT = 256
D_MEM = 256
D_IN = D_SSM + 3 * D_ATT + D_MEM
D_REST = D_IN - D_SSM
HEAD_DIM = 64
N_HEADS = 4
N_GROUPS = 32
SSM_GROUP = 16
SSM_STATE = 64
D_STATE = N_GROUPS * SSM_STATE
CHUNK_GROUPS = 8
SSM_CHUNKS = N_GROUPS // CHUNK_GROUPS
CHUNK_IN = CHUNK_GROUPS * SSM_GROUP
CHUNK_STATE = CHUNK_GROUPS * SSM_STATE
CHUNK = 64
N_PREV_CHUNKS = 8
BAND = N_PREV_CHUNKS * CHUNK
REL_CLIP = 128
N_MEM = 256
N_EXPERTS = 32
TOP_K = 4
D_FF = D_MODEL
SWIGLU_LIMIT = 7.0
SWIGLU_ALPHA = 1.702
LN_EPS = 1e-5
NEG_INF = -1e30
ATT_SCALE = HEAD_DIM ** -0.5
DEEPNORM_ALPHA = 2.0 ** 0.25

V7X_VMEM_LIMIT = 56 * 1024 * 1024
IN_PROJ_TS = 128
ATT_TQ = 4 * CHUNK
SAMPLE_SEQS = 4
SCAN_LANES = 1024
SCAN_ROWS = 1024
LANES = 128
ROW_SUBLANES = D_MODEL // LANES
EXPERT_TM = 512
TOKEN_TM = 512
N_SORT_BUFS = 3
SORT_CHUNKS = 16

_NT = (((1,), (1,)), ((), ()))


def _params(n_axes, vmem=V7X_VMEM_LIMIT):
    return pltpu.CompilerParams(dimension_semantics=("arbitrary",) * n_axes,
                                vmem_limit_bytes=vmem)


def _in_proj_kernel(x_ref, w_ref, u_ref, kv_ref, zb_ref, wb_ref):
    @pl.when(pl.program_id(0) == 0)
    def _():
        wb_ref[...] = w_ref[...].astype(BF16)

    nb, ts, _ = x_ref.shape
    x = x_ref[...].reshape(nb * ts, D_MODEL).astype(BF16)
    z = jnp.dot(x, wb_ref[...], preferred_element_type=F32)
    for b in range(nb):
        for c in range(D_SSM // LANES):
            u_ref[c, pl.ds(b, ts, stride=nb), :] = z[b * ts:(b + 1) * ts, LANES * c:LANES * (c + 1)]
    zr = z[:, D_SSM:]
    kv_ref[...] = zr[:, D_ATT:3 * D_ATT].reshape(nb, ts, 2 * D_ATT)
    zb = jnp.concatenate([zr[:, :D_ATT] * ATT_SCALE, zr[:, D_ATT:3 * D_ATT], zr[:, 3 * D_ATT:] * ATT_SCALE],
                         axis=1)
    zb_ref[...] = zb.astype(BF16).reshape(nb, ts, D_REST)


def _in_proj(x, w_in, ts, tail):
    b, s, _ = x.shape
    skipped = (s - tail) // ts
    return pl.pallas_call(
        _in_proj_kernel,
        grid=(s // ts,),
        in_specs=[pl.BlockSpec((b, ts, D_MODEL), lambda j: (0, j, 0)),
                  pl.BlockSpec((D_MODEL, D_IN), lambda j: (0, 0))],
        out_specs=[pl.BlockSpec((D_SSM // LANES, ts * b, LANES), lambda j: (0, j, 0)),
                   pl.BlockSpec((b, ts, 2 * D_ATT), lambda j: (0, jnp.maximum(j - skipped, 0), 0)),
                   pl.BlockSpec((b, ts, D_REST), lambda j: (0, j, 0))],
        out_shape=[jax.ShapeDtypeStruct((D_SSM // LANES, s * b, LANES), F32),
                   jax.ShapeDtypeStruct((b, tail, 2 * D_ATT), F32),
                   jax.ShapeDtypeStruct((b, s, D_REST), BF16)],
        scratch_shapes=[pltpu.VMEM((D_MODEL, D_IN), BF16)],
        compiler_params=_params(1),
        name="in_proj",
    )(x, w_in)


def _mem_kv_kernel(m_ref, w_ref, mk_ref, mv_ref):
    kv = jnp.dot(m_ref[0].astype(BF16), w_ref[...].astype(BF16), preferred_element_type=F32)
    mk_ref[0] = kv[:, :D_MEM]
    mv_ref[0] = kv[:, D_MEM:]


def _mem_kv(mem, w_mem_kv):
    b = mem.shape[0]
    return pl.pallas_call(
        _mem_kv_kernel,
        grid=(b,),
        in_specs=[pl.BlockSpec((1, N_MEM, D_MODEL), lambda i: (i, 0, 0)),
                  pl.BlockSpec((D_MODEL, 2 * D_MEM), lambda i: (0, 0))],
        out_specs=[pl.BlockSpec((1, N_MEM, D_MEM), lambda i: (i, 0, 0)),
                   pl.BlockSpec((1, N_MEM, D_MEM), lambda i: (i, 0, 0))],
        out_shape=[jax.ShapeDtypeStruct((b, N_MEM, D_MEM), F32)] * 2,
        compiler_params=_params(1),
        name="mem_kv",
    )(mem, w_mem_kv)


def _ssm_kernel(u_ref, h0r_ref, h0i_ref, lr_ref, li_ref, ldt_ref, bre_ref, bim_ref,
                cre_ref, cim_ref, d_ref, wg_ref, bg_ref,
                y_ref, sr_ref, si_ref,
                a_sc, bbr_sc, bbi_sc, cr_sc, ci_sc, wg_sc, str_sc, sti_sc, xr_sc, xi_sc,
                *, n_batch):
    n_rows = u_ref.shape[1]
    n_steps = n_rows // n_batch

    @pl.when(pl.program_id(0) == 0)
    def _():
        lr = lr_ref[...]
        li = li_ref[...]
        dt = jnp.exp(ldt_ref[...])
        mag = jnp.exp(lr * dt)
        ar = mag * jnp.cos(li * dt)
        ai = mag * jnp.sin(li * dt)
        den = lr * lr + li * li
        fr = ((ar - 1.0) * lr + ai * li) / den
        fi = (ai * lr - (ar - 1.0) * li) / den
        a_sc[0:1, :] = ar
        a_sc[1:2, :] = ai
        for j in range(SSM_CHUNKS):
            frj = fr[:, CHUNK_STATE * j:CHUNK_STATE * (j + 1)]
            fij = fi[:, CHUNK_STATE * j:CHUNK_STATE * (j + 1)]
            bbr_sc[j] = (frj * bre_ref[j] - fij * bim_ref[j]).astype(BF16)
            bbi_sc[j] = (frj * bim_ref[j] + fij * bre_ref[j]).astype(BF16)
            cr_sc[j] = cre_ref[j].astype(BF16)
            ci_sc[j] = cim_ref[j].astype(BF16)
        wg_sc[...] = wg_ref[...].astype(BF16)
        str_sc[...] = h0r_ref[...]
        sti_sc[...] = h0i_ref[...]

    for j in range(SSM_CHUNKS):
        uc = u_ref[j].astype(BF16)
        xr_sc[:, CHUNK_STATE * j:CHUNK_STATE * (j + 1)] = jnp.dot(uc, bbr_sc[j], preferred_element_type=F32)
        xi_sc[:, CHUNK_STATE * j:CHUNK_STATE * (j + 1)] = jnp.dot(uc, bbi_sc[j], preferred_element_type=F32)

    for c in range(D_STATE // SCAN_LANES):
        lo = c * SCAN_LANES
        ar = jnp.broadcast_to(a_sc[0:1, lo:lo + SCAN_LANES], (n_batch, SCAN_LANES))
        ai = jnp.broadcast_to(a_sc[1:2, lo:lo + SCAN_LANES], (n_batch, SCAN_LANES))

        def step(t, carry, lo=lo, ar=ar, ai=ai):
            sr, si = carry
            r0 = pl.multiple_of(t * n_batch, n_batch)
            nr = ar * sr - ai * si + xr_sc[pl.ds(r0, n_batch), lo:lo + SCAN_LANES]
            ni = ar * si + ai * sr + xi_sc[pl.ds(r0, n_batch), lo:lo + SCAN_LANES]
            xr_sc[pl.ds(r0, n_batch), lo:lo + SCAN_LANES] = nr
            xi_sc[pl.ds(r0, n_batch), lo:lo + SCAN_LANES] = ni
            return nr, ni

        sr, si = lax.fori_loop(0, n_steps, step,
                               (str_sc[:, lo:lo + SCAN_LANES], sti_sc[:, lo:lo + SCAN_LANES]),
                               unroll=True)
        str_sc[:, lo:lo + SCAN_LANES] = sr
        sti_sc[:, lo:lo + SCAN_LANES] = si

    pieces = []
    for j in range(SSM_CHUNKS):
        xr = xr_sc[:, CHUNK_STATE * j:CHUNK_STATE * (j + 1)].astype(BF16)
        xi = xi_sc[:, CHUNK_STATE * j:CHUNK_STATE * (j + 1)].astype(BF16)
        pieces.append(jnp.dot(xr, cr_sc[j], preferred_element_type=F32)
                      - jnp.dot(xi, ci_sc[j], preferred_element_type=F32))
    u = jnp.concatenate([u_ref[j] for j in range(SSM_CHUNKS)], axis=1)
    y = jnp.concatenate(pieces, axis=1) + d_ref[...] * u
    y = jax.nn.gelu(y)
    z = jnp.dot(y.astype(BF16), wg_sc[...], preferred_element_type=F32) + bg_ref[...]
    out = z[:, :D_SSM] * jax.nn.sigmoid(z[:, D_SSM:])
    for j in range(D_SSM // LANES):
        y_ref[j] = out[:, LANES * j:LANES * (j + 1)]
    sr_ref[...] = str_sc[...]
    si_ref[...] = sti_sc[...]


def _block_diag_b(b):
    bt = b.transpose(0, 2, 1).reshape(SSM_CHUNKS, CHUNK_GROUPS, SSM_GROUP, SSM_STATE)
    same = jnp.eye(CHUNK_GROUPS, dtype=bool)[None, :, None, :, None]
    t = jnp.where(same, bt[:, :, :, None, :], 0.0)
    return t.reshape(SSM_CHUNKS, CHUNK_IN, CHUNK_STATE)


def _block_diag_c(c):
    ct = c.transpose(0, 2, 1).reshape(SSM_CHUNKS, CHUNK_GROUPS, SSM_STATE, SSM_GROUP)
    same = jnp.eye(CHUNK_GROUPS, dtype=bool)[None, :, None, :, None]
    t = jnp.where(same, ct[:, :, :, None, :], 0.0)
    return t.reshape(SSM_CHUNKS, CHUNK_STATE, CHUNK_IN)


def _ssm(u_rows, h0_re, h0_im, lp, n_batch):
    rows = u_rows.shape[1]
    planes = D_SSM // LANES
    tr = min(SCAN_ROWS, rows)
    flat = lambda a: a.reshape(1, D_STATE)
    ldt = jnp.repeat(lp['log_dt'], SSM_STATE).reshape(1, D_STATE)
    const2 = lambda i: (0, 0)
    const3 = lambda i: (0, 0, 0)
    y, sr, si = pl.pallas_call(
        functools.partial(_ssm_kernel, n_batch=n_batch),
        grid=(rows // tr,),
        in_specs=[pl.BlockSpec((planes, tr, LANES), lambda i: (0, i, 0)),
                  pl.BlockSpec((n_batch, D_STATE), const2),
                  pl.BlockSpec((n_batch, D_STATE), const2),
                  pl.BlockSpec((1, D_STATE), const2),
                  pl.BlockSpec((1, D_STATE), const2),
                  pl.BlockSpec((1, D_STATE), const2),
                  pl.BlockSpec((SSM_CHUNKS, CHUNK_IN, CHUNK_STATE), const3),
                  pl.BlockSpec((SSM_CHUNKS, CHUNK_IN, CHUNK_STATE), const3),
                  pl.BlockSpec((SSM_CHUNKS, CHUNK_STATE, CHUNK_IN), const3),
                  pl.BlockSpec((SSM_CHUNKS, CHUNK_STATE, CHUNK_IN), const3),
                  pl.BlockSpec((1, D_SSM), const2),
                  pl.BlockSpec((D_SSM, 2 * D_SSM), const2),
                  pl.BlockSpec((1, 2 * D_SSM), const2)],
        out_specs=[pl.BlockSpec((planes, tr, LANES), lambda i: (0, i, 0)),
                   pl.BlockSpec((n_batch, D_STATE), const2),
                   pl.BlockSpec((n_batch, D_STATE), const2)],
        out_shape=[jax.ShapeDtypeStruct((planes, rows, LANES), F32),
                   jax.ShapeDtypeStruct((n_batch, D_STATE), F32),
                   jax.ShapeDtypeStruct((n_batch, D_STATE), F32)],
        scratch_shapes=[pltpu.VMEM((2, D_STATE), F32),
                        pltpu.VMEM((SSM_CHUNKS, CHUNK_IN, CHUNK_STATE), BF16), pltpu.VMEM((SSM_CHUNKS, CHUNK_IN, CHUNK_STATE), BF16),
                        pltpu.VMEM((SSM_CHUNKS, CHUNK_STATE, CHUNK_IN), BF16), pltpu.VMEM((SSM_CHUNKS, CHUNK_STATE, CHUNK_IN), BF16),
                        pltpu.VMEM((D_SSM, 2 * D_SSM), BF16),
                        pltpu.VMEM((n_batch, D_STATE), F32), pltpu.VMEM((n_batch, D_STATE), F32),
                        pltpu.VMEM((tr, D_STATE), F32), pltpu.VMEM((tr, D_STATE), F32)],
        compiler_params=_params(1),
        name="ssm",
    )(u_rows, h0_re.reshape(n_batch, D_STATE), h0_im.reshape(n_batch, D_STATE),
      flat(lp['lam_re']), flat(lp['lam_im']), ldt,
      _block_diag_b(lp['ssm_b_re']), _block_diag_b(lp['ssm_b_im']),
      _block_diag_c(lp['ssm_c_re']), _block_diag_c(lp['ssm_c_im']),
      lp['ssm_d'].reshape(1, D_SSM), lp['w_glu'], lp['b_glu'].reshape(1, 2 * D_SSM))
    return y, sr, si


def _softmax_pv(s, v):
    m = jnp.max(s, axis=-1, keepdims=True)
    p = jnp.exp(s - m)
    l = jnp.sum(p, axis=-1, keepdims=True)
    return jnp.dot(p.astype(BF16), v, preferred_element_type=F32) / l


def _attend(qb, k, v, out_ref, row0=0, bias_ref=None, valid=None, seq=0):
    tq = qb.shape[0]
    for h in range(N_HEADS):
        sl = slice(HEAD_DIM * h, HEAD_DIM * (h + 1))
        s = lax.dot_general(qb[:, sl], k[:, sl], _NT, preferred_element_type=F32)
        if bias_ref is not None:
            s = s + bias_ref[h]
        if valid is not None:
            s = jnp.where(valid, s, NEG_INF)
        out_ref[seq, row0:row0 + tq, sl] = _softmax_pv(s, v[:, sl])


def _attn_prompt_kernel(q_ref, kp_ref, kc_ref, vp_ref, vc_ref, qm_ref, mk_ref, mv_ref, bias_ref,
                        ya_ref, ym_ref, bias_sc):
    tq = ATT_TQ

    @pl.when((pl.program_id(0) == 0) & (pl.program_id(1) == 0))
    def _():
        q_chunk = lax.broadcasted_iota(I32, (tq, 3 * tq), 0) // CHUNK
        k_chunk = lax.broadcasted_iota(I32, (tq, 3 * tq), 1) // CHUNK
        ahead = k_chunk - q_chunk
        for h in range(N_HEADS):
            bias_sc[h] = jnp.where(ahead >= 0, jnp.where(ahead <= N_PREV_CHUNKS, bias_ref[h], NEG_INF),
                                   NEG_INF)

    k = jnp.concatenate([kp_ref[0], kc_ref[0]], axis=0)
    v = jnp.concatenate([vp_ref[0], vc_ref[0]], axis=0)
    for half in range(2):
        first_key = (2 * pl.program_id(1) - 2 + half) * tq
        kpos = first_key + lax.broadcasted_iota(I32, (1, 3 * tq), 1)
        _attend(q_ref[0, half * tq:(half + 1) * tq, :], k[half * tq:(half + 3) * tq],
                v[half * tq:(half + 3) * tq], ya_ref, half * tq, bias_sc, kpos >= 0)
    _attend(qm_ref[0], mk_ref[0].astype(BF16), mv_ref[0].astype(BF16), ym_ref)


def _attn_sample_kernel(q_ref, kn_ref, vn_ref, qm_ref, ck_ref, cv_ref, mk_ref, mv_ref, bias_ref,
                        ya_ref, ym_ref, nk_ref, nv_ref):
    n = kn_ref.shape[1]
    for i in range(q_ref.shape[0]):
        kk = jnp.concatenate([ck_ref[i], kn_ref[i]], axis=0)
        vv = jnp.concatenate([cv_ref[i], vn_ref[i]], axis=0)
        nk_ref[i] = kk[n:]
        nv_ref[i] = vv[n:]
        _attend(q_ref[i], kk.astype(BF16), vv.astype(BF16), ya_ref, 0, bias_ref, seq=i)
        _attend(qm_ref[i], mk_ref[i].astype(BF16), mv_ref[i].astype(BF16), ym_ref, seq=i)


def _rel_bias(table, n_q, n_k):
    period = n_q + n_k
    m = jnp.arange(period)
    offset = jnp.where(m < n_k, m, m - period)
    idx = jnp.clip(BAND - offset, -REL_CLIP, REL_CLIP) + REL_CLIP
    f = table.astype(F32)[:, idx]
    flat = jnp.tile(f, (1, n_q))[:, :n_q * (period - 1)]
    return flat.reshape(N_HEADS, n_q, period - 1)[:, :, :n_k]


def _attn_prompt(zb, mk, mv, table):
    b, s, _ = zb.shape
    tq = ATT_TQ
    bias = _rel_bias(table, tq, 3 * tq)
    col = lambda c: (lambda i, j: (i, j, c))
    prev = lambda c: (lambda i, j: (i, jnp.maximum(j - 1, 0), c))
    blk = (1, 2 * tq, D_ATT)
    return pl.pallas_call(
        _attn_prompt_kernel,
        grid=(b, s // (2 * tq)),
        in_specs=[pl.BlockSpec(blk, col(0)),
                  pl.BlockSpec(blk, prev(1)), pl.BlockSpec(blk, col(1)),
                  pl.BlockSpec(blk, prev(2)), pl.BlockSpec(blk, col(2)),
                  pl.BlockSpec(blk, col(3)),
                  pl.BlockSpec((1, N_MEM, D_MEM), lambda i, j: (i, 0, 0)),
                  pl.BlockSpec((1, N_MEM, D_MEM), lambda i, j: (i, 0, 0)),
                  pl.BlockSpec((N_HEADS, tq, 3 * tq), lambda i, j: (0, 0, 0))],
        out_specs=[pl.BlockSpec(blk, col(0)), pl.BlockSpec(blk, col(0))],
        out_shape=[jax.ShapeDtypeStruct((b, s, D_ATT), F32),
                   jax.ShapeDtypeStruct((b, s, D_MEM), F32)],
        scratch_shapes=[pltpu.VMEM((N_HEADS, tq, 3 * tq), F32)],
        compiler_params=_params(2),
        name="attn_prompt",
    )(zb, zb, zb, zb, zb, zb, mk, mv, bias)


def _attn_sample(kv, zb, cache_k, cache_v, mk, mv, table):
    b, n, _ = kv.shape
    w = cache_k.shape[1]
    bias = _rel_bias(table, n, w + n)
    col = lambda c: (lambda i: (i, 0, c))
    per = SAMPLE_SEQS if b % SAMPLE_SEQS == 0 else 1
    blk = (per, n, D_ATT)
    cblk = (per, w, D_ATT)
    mblk = (per, N_MEM, D_MEM)
    row = lambda i: (i, 0, 0)
    return pl.pallas_call(
        _attn_sample_kernel,
        grid=(b // per,),
        in_specs=[pl.BlockSpec(blk, col(0)), pl.BlockSpec(blk, col(0)), pl.BlockSpec(blk, col(1)),
                  pl.BlockSpec(blk, col(3)),
                  pl.BlockSpec(cblk, row), pl.BlockSpec(cblk, row),
                  pl.BlockSpec(mblk, row), pl.BlockSpec(mblk, row),
                  pl.BlockSpec((N_HEADS, n, w + n), lambda i: (0, 0, 0))],
        out_specs=[pl.BlockSpec(blk, row), pl.BlockSpec(blk, row),
                   pl.BlockSpec(cblk, row), pl.BlockSpec(cblk, row)],
        out_shape=[jax.ShapeDtypeStruct((b, n, D_ATT), F32),
                   jax.ShapeDtypeStruct((b, n, D_MEM), F32),
                   jax.ShapeDtypeStruct((b, w, D_ATT), F32),
                   jax.ShapeDtypeStruct((b, w, D_ATT), F32)],
        compiler_params=_params(1),
        name="attn_sample",
    )(zb, kv, kv, zb, cache_k, cache_v, mk, mv, bias)


def _rms(x, g):
    return x * lax.rsqrt(jnp.mean(jnp.square(x), axis=-1, keepdims=True) + LN_EPS) * g


def _layer_norm(x, g, b):
    mu = jnp.mean(x, axis=-1, keepdims=True)
    xc = x - mu
    var = jnp.mean(jnp.square(xc), axis=-1, keepdims=True)
    return xc * lax.rsqrt(var + LN_EPS) * g + b


def _split_bf16(a):
    hi = a.astype(BF16)
    lo = (a - hi.astype(F32)).astype(BF16)
    return hi, lo


def _merge_kernel(*refs, nb, n_carried):
    (x_ref, ys_ref, ya_ref, ym_ref, gs_ref, ga_ref, gm_ref, wo_ref, l1g_ref, l1b_ref,
     wrt_ref, brt_ref) = refs[:12]
    h_ref, pos_ref, gate_ref, cnt_ref, wo_sc = refs[12 + n_carried:]
    st = x_ref.shape[1]
    tm = nb * st
    n_batch = ys_ref.shape[1] // st

    @pl.when((pl.program_id(0) == 0) & (pl.program_id(1) == 0))
    def _():
        wo_sc[...] = wo_ref[...].astype(BF16)

    x = x_ref[...].reshape(tm, D_MODEL)
    first = pl.program_id(1) * nb
    ys = jnp.concatenate(
        [jnp.concatenate([ys_ref[c, pl.ds(first + i, st, stride=n_batch), :] for c in range(D_SSM // LANES)],
                         axis=1) for i in range(nb)], axis=0)
    ya = ya_ref[...].reshape(tm, D_ATT)
    ym = ym_ref[...].reshape(tm, D_MEM)
    a = _rms(ys, gs_ref[...]).astype(BF16)
    b = _rms(ya, ga_ref[...]).astype(BF16)
    c = _rms(ym, gm_ref[...]).astype(BF16)
    mix = (jnp.dot(a, wo_sc[0:D_SSM, :], preferred_element_type=F32)
           + jnp.dot(b, wo_sc[D_SSM:D_SSM + D_ATT, :], preferred_element_type=F32)
           + jnp.dot(c, wo_sc[D_SSM + D_ATT:, :], preferred_element_type=F32))
    h = _layer_norm(DEEPNORM_ALPHA * x + mix, l1g_ref[...], l1b_ref[...])
    h_ref[...] = h

    h_hi, h_lo = _split_bf16(h)
    w_hi, w_lo = _split_bf16(wrt_ref[...])
    logits = (lax.dot_general(w_hi, h_hi, _NT, preferred_element_type=F32)
              + lax.dot_general(w_hi, h_lo, _NT, preferred_element_type=F32)
              + lax.dot_general(w_lo, h_hi, _NT, preferred_element_type=F32)
              + brt_ref[...])
    erow = lax.broadcasted_iota(I32, (N_EXPERTS, tm), 0).astype(F32)
    tops, picks = [], []
    l = logits
    for k in range(TOP_K):
        m = jnp.max(l, axis=0, keepdims=True)
        e = jnp.min(jnp.where(l == m, erow, float(N_EXPERTS)), axis=0, keepdims=True)
        pick = erow == e
        tops.append(m)
        picks.append(jnp.where(pick, 1.0, 0.0))
        l = jnp.where(pick, -jnp.inf, l)
    ex = [jnp.exp(t - tops[0]) for t in tops]
    den = ex[0] + ex[1] + ex[2] + ex[3]
    for k in range(TOP_K):
        gate_ref[k:k + 1, :] = ex[k] / den

    chosen = picks[0] + picks[1] + picks[2] + picks[3]
    chosen_b = chosen.astype(BF16)
    earlier_tok = (lax.broadcasted_iota(I32, (tm, tm), 0) < lax.broadcasted_iota(I32, (tm, tm), 1))
    within = jnp.dot(chosen_b, jnp.where(earlier_tok, 1.0, 0.0).astype(BF16),
                     preferred_element_type=F32)
    lower_exp = (lax.broadcasted_iota(I32, (N_EXPERTS, N_EXPERTS), 1)
                 < lax.broadcasted_iota(I32, (N_EXPERTS, N_EXPERTS), 0))
    below = jnp.dot(jnp.where(lower_exp, 1.0, 0.0).astype(BF16), chosen_b,
                    preferred_element_type=F32)
    slot = within + jnp.sum(below, axis=1, keepdims=True)
    for k in range(TOP_K):
        pos_ref[k:k + 1, :] = jnp.sum(picks[k] * slot, axis=0, keepdims=True).astype(I32)
    cnt_ref[0] = jnp.sum(chosen, axis=1, keepdims=True)


def _merge(x, ys_tm, ya, ym, lp, nb, st, t_all, tile0, carried=None):
    b, s, _ = x.shape
    tm = nb * st
    assert tm == TOKEN_TM
    n_s = s // st
    tile = lambda j, i: (tile0 + i * n_s + j)
    c2 = lambda j, i: (0, 0)
    row3 = lambda j, i: (i, j, 0)
    vec = lambda a: a.reshape(1, -1)
    carried = () if carried is None else tuple(carried)
    return pl.pallas_call(
        functools.partial(_merge_kernel, nb=nb, n_carried=len(carried)),
        grid=(n_s, b // nb),
        in_specs=[pl.BlockSpec((nb, st, D_MODEL), row3),
                  pl.BlockSpec((D_SSM // LANES, st * b, LANES), lambda j, i: (0, j, 0)),
                  pl.BlockSpec((nb, st, D_ATT), row3),
                  pl.BlockSpec((nb, st, D_MEM), row3),
                  pl.BlockSpec((1, D_SSM), c2), pl.BlockSpec((1, D_ATT), c2),
                  pl.BlockSpec((1, D_MEM), c2),
                  pl.BlockSpec((D_MODEL, D_MODEL), c2),
                  pl.BlockSpec((1, D_MODEL), c2), pl.BlockSpec((1, D_MODEL), c2),
                  pl.BlockSpec((N_EXPERTS, D_MODEL), c2), pl.BlockSpec((N_EXPERTS, 1), c2)]
                 + [pl.BlockSpec(memory_space=pl.ANY)] * len(carried),
        out_specs=[pl.BlockSpec((tm, D_MODEL), lambda j, i: (tile(j, i), 0)),
                   pl.BlockSpec((TOP_K, tm), lambda j, i: (0, tile(j, i))),
                   pl.BlockSpec((TOP_K, tm), lambda j, i: (0, tile(j, i))),
                   pl.BlockSpec((1, N_EXPERTS, 1), lambda j, i: (tile(j, i), 0, 0))],
        out_shape=[jax.ShapeDtypeStruct((t_all, D_MODEL), F32),
                   jax.ShapeDtypeStruct((TOP_K, t_all), I32),
                   jax.ShapeDtypeStruct((TOP_K, t_all), F32),
                   jax.ShapeDtypeStruct((t_all // tm, N_EXPERTS, 1), F32)],
        scratch_shapes=[pltpu.VMEM((D_MODEL, D_MODEL), BF16)],
        input_output_aliases={12 + k: k for k in range(len(carried))},
        compiler_params=_params(2),
        name="merge_router",
    )(x, ys_tm, ya, ym, vec(lp['g_ssm']), vec(lp['g_att']), vec(lp['g_mem']), lp['w_out'],
      vec(lp['ln1_g']), vec(lp['ln1_b']), lp['w_router'].T, lp['b_router'].reshape(N_EXPERTS, 1),
      *carried)


def _rows(start, size):
    return pl.ds(pl.multiple_of(start * ROW_SUBLANES, ROW_SUBLANES), size * ROW_SUBLANES)


def _store_rows(ref, value, row0=0):
    n = value.shape[0]
    for j in range(ROW_SUBLANES):
        ref[pl.ds(row0 * ROW_SUBLANES + j, n, stride=ROW_SUBLANES), :] = value[:, LANES * j:LANES * (j + 1)]


def _load_rows(ref, dtype=F32):
    n = ref.shape[0] // ROW_SUBLANES
    return jnp.concatenate([ref[pl.ds(j, n, stride=ROW_SUBLANES), :].astype(dtype)
                            for j in range(ROW_SUBLANES)], axis=1)


def _for_each_run_piece(n, max_rows, fn):
    for bit in reversed(range(max_rows.bit_length())):
        size = 1 << bit
        start = (n >> (bit + 1)) << (bit + 1)

        @pl.when((n & size) != 0)
        def _(start=start, size=size):
            fn(start, size)


def _dispatch_kernel(n_ref, off_ref, dst_ref, padlo_ref, padn_ref, used_ref,
                     pos_ref, h_ref, xs_hbm, sorted_sc, zero_sc, sem, zsem):
    i = pl.program_id(0)
    tm = h_ref.shape[0]
    n_slots = TOP_K * tm
    n_blocks = xs_hbm.shape[0] // (EXPERT_TM * ROW_SUBLANES)

    @pl.when(i == 0)
    def _():
        zero_sc[...] = jnp.zeros_like(zero_sc)

        def pad_copy(e, start, size):
            return pltpu.make_async_copy(zero_sc.at[_rows(0, size)],
                                         xs_hbm.at[_rows(padlo_ref[e] + start, size)], zsem)

        def tail_copy(blk):
            return pltpu.make_async_copy(zero_sc, xs_hbm.at[_rows(blk * EXPERT_TM, EXPERT_TM)], zsem)

        for e in range(N_EXPERTS):
            _for_each_run_piece(padn_ref[e], EXPERT_TM - 1,
                                lambda start, size, e=e: pad_copy(e, start, size).start())

        def tail_start(blk, carry):
            tail_copy(blk).start()
            return carry

        lax.fori_loop(used_ref[0], n_blocks, tail_start, 0)
        for e in range(N_EXPERTS):
            _for_each_run_piece(padn_ref[e], EXPERT_TM - 1,
                                lambda start, size, e=e: pad_copy(e, start, size).wait())

        def tail_wait(blk, carry):
            tail_copy(blk).wait()
            return carry

        lax.fori_loop(used_ref[0], n_blocks, tail_wait, 0)

    n_tiles = pl.num_programs(0) - 1
    slot = lax.rem(i, N_SORT_BUFS)
    prev_slot = lax.rem(i + N_SORT_BUFS - 1, N_SORT_BUFS)
    buf = sorted_sc.at[slot]
    prev = sorted_sc.at[prev_slot]

    def wait_tile(sl):
        pltpu.make_async_copy(sorted_sc.at[sl], xs_hbm.at[_rows(0, n_slots)], sem.at[sl]).wait()

    @pl.when(i >= N_SORT_BUFS)
    def _():
        wait_tile(slot)

    pos = pos_ref[...]
    hb = h_ref[...].astype(BF16)
    base = jnp.maximum(i - 1, 0) * N_EXPERTS
    rows_c = n_slots // SORT_CHUNKS
    experts_c = N_EXPERTS // SORT_CHUNKS
    for c in range(SORT_CHUNKS):
        for e in range(c * experts_c, (c + 1) * experts_c):
            off = off_ref[base + e]
            dst = dst_ref[base + e]

            def run_start(start, size, off=off, dst=dst):
                pltpu.make_async_copy(prev.at[_rows(off + start, size)],
                                      xs_hbm.at[_rows(dst + start, size)], sem.at[prev_slot]).start()

            _for_each_run_piece(jnp.where(i >= 1, n_ref[base + e], 0), tm, run_start)

        srow = lax.broadcasted_iota(I32, (rows_c, tm), 0) + c * rows_c
        perm = jnp.where(srow == pos[0:1], 1.0,
                         jnp.where(srow == pos[1:2], 1.0,
                                   jnp.where(srow == pos[2:3], 1.0,
                                             jnp.where(srow == pos[3:4], 1.0, 0.0)))).astype(BF16)
        _store_rows(buf, jnp.dot(perm, hb, preferred_element_type=F32), c * rows_c)

    @pl.when(i == n_tiles)
    def _():
        wait_tile(prev_slot)

        @pl.when(i >= 2)
        def _():
            wait_tile(lax.rem(i + N_SORT_BUFS - 2, N_SORT_BUFS))


def _dispatch(run_n, run_off, run_dst, pad_lo, pad_n, n_used, pos, h, cap):
    tm = TOKEN_TM
    n_tiles = h.shape[0] // tm
    grid_spec = pltpu.PrefetchScalarGridSpec(
        num_scalar_prefetch=6,
        grid=(n_tiles + 1,),
        in_specs=[pl.BlockSpec((TOP_K, tm), lambda i, *_: (0, jnp.minimum(i, n_tiles - 1))),
                  pl.BlockSpec((tm, D_MODEL), lambda i, *_: (jnp.minimum(i, n_tiles - 1), 0))],
        out_specs=pl.BlockSpec(memory_space=pl.ANY),
        scratch_shapes=[pltpu.VMEM((N_SORT_BUFS, TOP_K * tm * ROW_SUBLANES, LANES), F32),
                        pltpu.VMEM((EXPERT_TM * ROW_SUBLANES, LANES), F32),
                        pltpu.SemaphoreType.DMA((N_SORT_BUFS,)), pltpu.SemaphoreType.DMA],
    )
    return pl.pallas_call(
        _dispatch_kernel,
        grid_spec=grid_spec,
        out_shape=jax.ShapeDtypeStruct((cap * ROW_SUBLANES, LANES), F32),
        compiler_params=_params(1),
        name="moe_dispatch",
    )(run_n, run_off, run_dst, pad_lo, pad_n, n_used, pos, h)


def _expert_kernel(be_ref, first_ref, ord_ref, seq_ref, used_ref,
                   x_ref, bgu_ref, bd_ref, wgu_hbm, wd_hbm, o_ref,
                   wgu_st, wd_st, wgu_sc, wd_sc, sem):
    i = pl.program_id(0)

    def weight_copies(e):
        return (pltpu.make_async_copy(wgu_hbm.at[e], wgu_st, sem.at[0]),
                pltpu.make_async_copy(wd_hbm.at[e], wd_st, sem.at[1]))

    @pl.when(i == 0)
    def _():
        for c in weight_copies(seq_ref[0]):
            c.start()

    @pl.when(i < used_ref[0])
    def _():
        @pl.when(first_ref[i] == 1)
        def _():
            k = ord_ref[i]
            for c in weight_copies(seq_ref[k]):
                c.wait()
            wgu_sc[...] = wgu_st[...].astype(BF16)
            wd_sc[...] = wd_st[...].astype(BF16)

            @pl.when(k + 1 < used_ref[1])
            def _():
                for c in weight_copies(seq_ref[k + 1]):
                    c.start()

        gu = jnp.dot(_load_rows(x_ref, BF16), wgu_sc[...], preferred_element_type=F32) + bgu_ref[0]
        gate = jnp.minimum(gu[:, :D_FF], SWIGLU_LIMIT)
        lin = jnp.clip(gu[:, D_FF:], -SWIGLU_LIMIT, SWIGLU_LIMIT)
        act = gate * jax.nn.sigmoid(SWIGLU_ALPHA * gate) * (lin + 1.0)
        _store_rows(o_ref, jnp.dot(act.astype(BF16), wd_sc[...], preferred_element_type=F32) + bd_ref[0])

    @pl.when(i >= used_ref[0])
    def _():
        o_ref[...] = jnp.zeros_like(o_ref)


def _experts(block_expert, block_first, block_ord, expert_seq, n_used, xs, lp):
    tm = EXPERT_TM * ROW_SUBLANES
    grid_spec = pltpu.PrefetchScalarGridSpec(
        num_scalar_prefetch=5,
        grid=(xs.shape[0] // tm,),
        in_specs=[pl.BlockSpec((tm, LANES), lambda i, be, *_: (i, 0)),
                  pl.BlockSpec((1, 1, 2 * D_FF), lambda i, be, *_: (be[i], 0, 0)),
                  pl.BlockSpec((1, 1, D_MODEL), lambda i, be, *_: (be[i], 0, 0)),
                  pl.BlockSpec(memory_space=pl.ANY),
                  pl.BlockSpec(memory_space=pl.ANY)],
        out_specs=pl.BlockSpec((tm, LANES), lambda i, be, *_: (i, 0)),
        scratch_shapes=[pltpu.VMEM((D_MODEL, 2 * D_FF), F32), pltpu.VMEM((D_FF, D_MODEL), F32),
                        pltpu.VMEM((D_MODEL, 2 * D_FF), BF16), pltpu.VMEM((D_FF, D_MODEL), BF16),
                        pltpu.SemaphoreType.DMA((2,))],
    )
    return pl.pallas_call(
        _expert_kernel,
        grid_spec=grid_spec,
        out_shape=jax.ShapeDtypeStruct(xs.shape, F32),
        compiler_params=_params(1),
        name="moe_experts",
    )(block_expert, block_first, block_ord, expert_seq, n_used, xs,
      lp['b_gu'].reshape(N_EXPERTS, 1, 2 * D_FF), lp['b_down'].reshape(N_EXPERTS, 1, D_MODEL),
      lp['w_gu'], lp['w_down'])


def _combine_kernel(n_ref, off_ref, dst_ref, pos_ref, gate_ref, h_ref, ys_hbm, g_ref, b_ref,
                    y1_ref, y2_ref, sorted_sc, w_sc, sem, *, n_first):
    i = pl.program_id(0)
    n_tiles = pl.num_programs(0) - 1
    tm = h_ref.shape[0]
    n_slots = TOP_K * tm
    slot = lax.rem(i, 2)
    buf = sorted_sc.at[slot]
    pos = pos_ref[...]
    gates = gate_ref[...]
    rows_per = tm // N_EXPERTS
    base = jnp.minimum(i, n_tiles - 1) * N_EXPERTS
    for e in range(N_EXPERTS):
        off = off_ref[base + e]
        dst = dst_ref[base + e]

        def run_start(start, size, off=off, dst=dst):
            pltpu.make_async_copy(ys_hbm.at[_rows(dst + start, size)],
                                  buf.at[_rows(off + start, size)], sem.at[slot]).start()

        _for_each_run_piece(jnp.where(i < n_tiles, n_ref[base + e], 0), tm, run_start)

        r = slice(e * rows_per, (e + 1) * rows_per)
        scol = lax.broadcasted_iota(I32, (rows_per, n_slots), 1)
        w_sc[r, :] = jnp.where(
            scol == pos[r, 0:1], gates[r, 0:1],
            jnp.where(scol == pos[r, 1:2], gates[r, 1:2],
                      jnp.where(scol == pos[r, 2:3], gates[r, 2:3],
                                jnp.where(scol == pos[r, 3:4], gates[r, 3:4], 0.0)))).astype(BF16)

    @pl.when(i >= 1)
    def _():
        done = sorted_sc.at[1 - slot]
        pltpu.make_async_copy(ys_hbm.at[_rows(0, n_slots)], done, sem.at[1 - slot]).wait()

        f = jnp.dot(w_sc[...], _load_rows(done, BF16), preferred_element_type=F32)
        y = _layer_norm(DEEPNORM_ALPHA * h_ref[...] + f, g_ref[...], b_ref[...])

        @pl.when(i - 1 < n_first)
        def _():
            y1_ref[...] = y

        @pl.when(i - 1 >= n_first)
        def _():
            y2_ref[...] = y


def _combine(run_n, run_off, run_dst, pos_t, gates_t, h, ys, lp, t_first):
    t = h.shape[0]
    tm = TOKEN_TM
    n_first = t_first // tm
    n_rest = (t - t_first) // tm
    c2 = lambda i, *_: (0, 0)
    done = lambda i, *_: (jnp.maximum(i - 1, 0), 0)
    done1 = lambda i, *_: (jnp.clip(i - 1, 0, n_first - 1), 0)
    done2 = lambda i, *_: (jnp.clip(i - 1 - n_first, 0, n_rest - 1), 0)
    grid_spec = pltpu.PrefetchScalarGridSpec(
        num_scalar_prefetch=3,
        grid=(t // tm + 1,),
        in_specs=[pl.BlockSpec((tm, TOP_K), done),
                  pl.BlockSpec((tm, TOP_K), done),
                  pl.BlockSpec((tm, D_MODEL), done),
                  pl.BlockSpec(memory_space=pl.ANY),
                  pl.BlockSpec((1, D_MODEL), c2), pl.BlockSpec((1, D_MODEL), c2)],
        out_specs=[pl.BlockSpec((tm, D_MODEL), done1), pl.BlockSpec((tm, D_MODEL), done2)],
        scratch_shapes=[pltpu.VMEM((2, TOP_K * tm * ROW_SUBLANES, LANES), F32),
                        pltpu.VMEM((tm, TOP_K * tm), BF16),
                        pltpu.SemaphoreType.DMA((2,))],
    )
    return pl.pallas_call(
        functools.partial(_combine_kernel, n_first=n_first),
        grid_spec=grid_spec,
        out_shape=[jax.ShapeDtypeStruct((t_first, D_MODEL), F32),
                   jax.ShapeDtypeStruct((t - t_first, D_MODEL), F32)],
        compiler_params=_params(1),
        name="moe_combine",
    )(run_n, run_off, run_dst, pos_t, gates_t, h, ys,
      lp['ln2_g'].reshape(1, D_MODEL), lp['ln2_b'].reshape(1, D_MODEL))


def _moe_and_norm(h, pos, gates, tile_counts, lp, t_first):
    t = h.shape[0]
    te = EXPERT_TM
    n_tiles = t // TOKEN_TM
    n_blocks = (t * TOP_K) // te + N_EXPERTS
    cap = n_blocks * te
    cnt = tile_counts.reshape(n_tiles, N_EXPERTS).astype(I32)
    counts = jnp.sum(cnt, axis=0)
    padded = (counts + te - 1) // te * te
    pad_ends = jnp.cumsum(padded)
    pad_starts = pad_ends - padded
    run_dst = pad_starts[None, :] + jnp.cumsum(cnt, axis=0) - cnt
    run_off = jnp.cumsum(cnt, axis=1) - cnt
    blk_start = jnp.arange(n_blocks, dtype=I32) * te
    expert_of = lambda slot_idx: jnp.minimum(jnp.sum(slot_idx[..., None] >= pad_ends, axis=-1), N_EXPERTS - 1)
    total = pad_ends[-1]
    be = jnp.where(blk_start < total, expert_of(blk_start), expert_of(jnp.maximum(total - 1, 0))).astype(I32)
    is_e = be[:, None] == jnp.arange(N_EXPERTS, dtype=I32)[None, :]
    pick = lambda table: jnp.sum(jnp.where(is_e, table[None, :], 0), axis=1)
    in_use = counts > 0
    ordinal = jnp.cumsum(in_use.astype(I32)) - 1
    rank = jnp.arange(N_EXPERTS, dtype=I32)
    expert_seq = jnp.sum(jnp.where(in_use[None, :] & (ordinal[None, :] == rank[:, None]), rank[None, :], 0),
                         axis=1)
    block_ord = pick(ordinal)
    block_first = (blk_start == pick(pad_starts)) & (blk_start < total)
    used = jnp.stack([total // te, jnp.sum(in_use.astype(I32))]).astype(I32)
    flat = lambda a: a.reshape(-1).astype(I32)
    xs = _dispatch(flat(cnt), flat(run_off), flat(run_dst), flat(pad_starts + counts),
                   flat(padded - counts), used, pos, h, cap)
    ys = _experts(be, flat(block_first), flat(block_ord), expert_seq, used, xs, lp)
    return _combine(flat(cnt), flat(run_off), flat(run_dst), pos.T, gates.T, h, ys, lp, t_first)


def kernel(x_prompt, x_sample, cache_attn_k, cache_attn_v, cache_mem_k, cache_mem_v, state_ssm_re, state_ssm_im, mem_prompt, w_in, lam_re, lam_im, log_dt, ssm_b_re, ssm_b_im, ssm_c_re, ssm_c_im, ssm_d, w_glu, b_glu, rel_bias, w_mem_kv, g_ssm, g_att, g_mem, w_out, ln1_g, ln1_b, w_router, b_router, w_gu, b_gu, w_down, b_down, ln2_g, ln2_b):
    assert w_in.shape[0] == 1, "single-layer step"
    lp = dict(w_in=w_in[0], lam_re=lam_re[0], lam_im=lam_im[0], log_dt=log_dt[0],
              ssm_b_re=ssm_b_re[0], ssm_b_im=ssm_b_im[0], ssm_c_re=ssm_c_re[0], ssm_c_im=ssm_c_im[0],
              ssm_d=ssm_d[0], w_glu=w_glu[0], b_glu=b_glu[0], rel_bias=rel_bias[0],
              w_mem_kv=w_mem_kv[0], g_ssm=g_ssm[0], g_att=g_att[0], g_mem=g_mem[0], w_out=w_out[0],
              ln1_g=ln1_g[0], ln1_b=ln1_b[0], w_router=w_router[0], b_router=b_router[0],
              w_gu=w_gu[0], b_gu=b_gu[0], w_down=w_down[0], b_down=b_down[0],
              ln2_g=ln2_g[0], ln2_b=ln2_b[0])

    bp, sp, _ = x_prompt.shape
    bs, ss, _ = x_sample.shape
    t_p = bp * sp
    t_all = t_p + bs * ss
    heads = lambda a: a.reshape(a.shape[0], a.shape[1], N_HEADS, HEAD_DIM)
    state = lambda a: a.reshape(a.shape[0], N_GROUPS, SSM_STATE)

    w = min(BAND, sp)
    u_p, kv_p, zb = _in_proj(x_prompt, lp['w_in'], min(IN_PROJ_TS, sp), w)
    mk, mv = _mem_kv(mem_prompt, lp['w_mem_kv'])
    ya, ym = _attn_prompt(zb, mk, mv, lp['rel_bias'])
    zeros = jnp.zeros((bp, D_STATE), F32)
    ys_p, sr_p, si_p = _ssm(u_p, zeros, zeros, lp, bp)
    merged = _merge(x_prompt, ys_p, ya, ym, lp, 1, TOKEN_TM, t_all, 0)
    k_p = heads(kv_p[:, :, :D_ATT])
    v_p = heads(kv_p[:, :, D_ATT:])

    wc = cache_attn_k.shape[2]
    u_s, kv_s, zb_s = _in_proj(x_sample, lp['w_in'], ss, ss)
    ya_s, ym_s, nk, nv = _attn_sample(
        kv_s, zb_s, cache_attn_k[0].reshape(bs, wc, D_ATT), cache_attn_v[0].reshape(bs, wc, D_ATT),
        cache_mem_k[0].reshape(bs, N_MEM, D_MEM), cache_mem_v[0].reshape(bs, N_MEM, D_MEM),
        lp['rel_bias'])
    ys_s, sr_s, si_s = _ssm(u_s, state_ssm_re[0], state_ssm_im[0], lp, bs)
    merged = _merge(x_sample, ys_s, ya_s, ym_s, lp, TOKEN_TM // ss, ss, t_all, t_p // TOKEN_TM,
                    carried=merged)

    y_p, y_s = _moe_and_norm(*merged, lp, t_p)

    return (y_p.reshape(bp, sp, D_MODEL), y_s.reshape(bs, ss, D_MODEL),
            k_p[None], v_p[None], heads(mk)[None], heads(mv)[None], state(sr_p)[None], state(si_p)[None],
            heads(nk)[None], heads(nv)[None], state(sr_s)[None], state(si_s)[None])
```

```python
import functools

import jax
import jax.numpy as jnp
from jax import lax
from jax.experimental import pallas as pl
from jax.experimental.pallas import tpu as pltpu

F32 = jnp.float32
BF16 = jnp.bfloat16
I32 = jnp.int32

D_MODEL = 1024
D_SSM = 512
D_ATT = 256
D_MEM = 256
D_IN = D_SSM + 3 * D_ATT + D_MEM
D_REST = D_IN - D_SSM
HEAD_DIM = 64
N_HEADS = 4
N_GROUPS = 32
SSM_GROUP = 16
SSM_STATE = 64
D_STATE = N_GROUPS * SSM_STATE
CHUNK_GROUPS = 8
SSM_CHUNKS = N_GROUPS // CHUNK_GROUPS
CHUNK_IN = CHUNK_GROUPS * SSM_GROUP
CHUNK_STATE = CHUNK_GROUPS * SSM_STATE
CHUNK = 64
N_PREV_CHUNKS = 8
BAND = N_PREV_CHUNKS * CHUNK
REL_CLIP = 128
N_MEM = 256
N_EXPERTS = 32
TOP_K = 4
D_FF = D_MODEL
SWIGLU_LIMIT = 7.0
SWIGLU_ALPHA = 1.702
LN_EPS = 1e-5
NEG_INF = -1e30
ATT_SCALE = HEAD_DIM ** -0.5
DEEPNORM_ALPHA = 2.0 ** 0.25

V7X_VMEM_LIMIT = 56 * 1024 * 1024
IN_PROJ_TS = 128
ATT_TQ = 4 * CHUNK
SAMPLE_SEQS = 4
SCAN_LANES = 1024
SCAN_ROWS = 1024
LANES = 128
ROW_SUBLANES = D_MODEL // LANES
EXPERT_TM = 512
TOKEN_TM = 512
N_SORT_BUFS = 3
SORT_CHUNKS = 16

_NT = (((1,), (1,)), ((), ()))


def _params(n_axes, vmem=V7X_VMEM_LIMIT):
    return pltpu.CompilerParams(dimension_semantics=("arbitrary",) * n_axes,
                                vmem_limit_bytes=vmem)


def _in_proj_kernel(x_ref, w_ref, u_ref, kv_ref, zb_ref, wb_ref):
    @pl.when(pl.program_id(0) == 0)
    def _():
        wb_ref[...] = w_ref[...].astype(BF16)

    nb, ts, _ = x_ref.shape
    x = x_ref[...].reshape(nb * ts, D_MODEL).astype(BF16)
    z = jnp.dot(x, wb_ref[...], preferred_element_type=F32)
    for b in range(nb):
        for c in range(D_SSM // LANES):
            u_ref[c, pl.ds(b, ts, stride=nb), :] = z[b * ts:(b + 1) * ts, LANES * c:LANES * (c + 1)]
    zr = z[:, D_SSM:]
    kv_ref[...] = zr[:, D_ATT:3 * D_ATT].reshape(nb, ts, 2 * D_ATT)
    zb = jnp.concatenate([zr[:, :D_ATT] * ATT_SCALE, zr[:, D_ATT:3 * D_ATT], zr[:, 3 * D_ATT:] * ATT_SCALE],
                         axis=1)
    zb_ref[...] = zb.astype(BF16).reshape(nb, ts, D_REST)


def _in_proj(x, w_in, ts, tail):
    b, s, _ = x.shape
    skipped = (s - tail) // ts
    return pl.pallas_call(
        _in_proj_kernel,
        grid=(s // ts,),
        in_specs=[pl.BlockSpec((b, ts, D_MODEL), lambda j: (0, j, 0)),
                  pl.BlockSpec((D_MODEL, D_IN), lambda j: (0, 0))],
        out_specs=[pl.BlockSpec((D_SSM // LANES, ts * b, LANES), lambda j: (0, j, 0)),
                   pl.BlockSpec((b, ts, 2 * D_ATT), lambda j: (0, jnp.maximum(j - skipped, 0), 0)),
                   pl.BlockSpec((b, ts, D_REST), lambda j: (0, j, 0))],
        out_shape=[jax.ShapeDtypeStruct((D_SSM // LANES, s * b, LANES), F32),
                   jax.ShapeDtypeStruct((b, tail, 2 * D_ATT), F32),
                   jax.ShapeDtypeStruct((b, s, D_REST), BF16)],
        scratch_shapes=[pltpu.VMEM((D_MODEL, D_IN), BF16)],
        compiler_params=_params(1),
        name="in_proj",
    )(x, w_in)


def _mem_kv_kernel(m_ref, w_ref, mk_ref, mv_ref):
    kv = jnp.dot(m_ref[0].astype(BF16), w_ref[...].astype(BF16), preferred_element_type=F32)
    mk_ref[0] = kv[:, :D_MEM]
    mv_ref[0] = kv[:, D_MEM:]


def _mem_kv(mem, w_mem_kv):
    b = mem.shape[0]
    return pl.pallas_call(
        _mem_kv_kernel,
        grid=(b,),
        in_specs=[pl.BlockSpec((1, N_MEM, D_MODEL), lambda i: (i, 0, 0)),
                  pl.BlockSpec((D_MODEL, 2 * D_MEM), lambda i: (0, 0))],
        out_specs=[pl.BlockSpec((1, N_MEM, D_MEM), lambda i: (i, 0, 0)),
                   pl.BlockSpec((1, N_MEM, D_MEM), lambda i: (i, 0, 0))],
        out_shape=[jax.ShapeDtypeStruct((b, N_MEM, D_MEM), F32)] * 2,
        compiler_params=_params(1),
        name="mem_kv",
    )(mem, w_mem_kv)


def _ssm_kernel(u_ref, h0r_ref, h0i_ref, lr_ref, li_ref, ldt_ref, bre_ref, bim_ref,
                cre_ref, cim_ref, d_ref, wg_ref, bg_ref,
                y_ref, sr_ref, si_ref,
                a_sc, bbr_sc, bbi_sc, cr_sc, ci_sc, wg_sc, str_sc, sti_sc, xr_sc, xi_sc,
                *, n_batch):
    n_rows = u_ref.shape[1]
    n_steps = n_rows // n_batch

    @pl.when(pl.program_id(0) == 0)
    def _():
        lr = lr_ref[...]
        li = li_ref[...]
        dt = jnp.exp(ldt_ref[...])
        mag = jnp.exp(lr * dt)
        ar = mag * jnp.cos(li * dt)
        ai = mag * jnp.sin(li * dt)
        den = lr * lr + li * li
        fr = ((ar - 1.0) * lr + ai * li) / den
        fi = (ai * lr - (ar - 1.0) * li) / den
        a_sc[0:1, :] = ar
        a_sc[1:2, :] = ai
        for j in range(SSM_CHUNKS):
            frj = fr[:, CHUNK_STATE * j:CHUNK_STATE * (j + 1)]
            fij = fi[:, CHUNK_STATE * j:CHUNK_STATE * (j + 1)]
            bbr_sc[j] = (frj * bre_ref[j] - fij * bim_ref[j]).astype(BF16)
            bbi_sc[j] = (frj * bim_ref[j] + fij * bre_ref[j]).astype(BF16)
            cr_sc[j] = cre_ref[j].astype(BF16)
            ci_sc[j] = cim_ref[j].astype(BF16)
        wg_sc[...] = wg_ref[...].astype(BF16)
        str_sc[...] = h0r_ref[...]
        sti_sc[...] = h0i_ref[...]

    for j in range(SSM_CHUNKS):
        uc = u_ref[j].astype(BF16)
        xr_sc[:, CHUNK_STATE * j:CHUNK_STATE * (j + 1)] = jnp.dot(uc, bbr_sc[j], preferred_element_type=F32)
        xi_sc[:, CHUNK_STATE * j:CHUNK_STATE * (j + 1)] = jnp.dot(uc, bbi_sc[j], preferred_element_type=F32)

    for c in range(D_STATE // SCAN_LANES):
        lo = c * SCAN_LANES
        ar = jnp.broadcast_to(a_sc[0:1, lo:lo + SCAN_LANES], (n_batch, SCAN_LANES))
        ai = jnp.broadcast_to(a_sc[1:2, lo:lo + SCAN_LANES], (n_batch, SCAN_LANES))

        def step(t, carry, lo=lo, ar=ar, ai=ai):
            sr, si = carry
            r0 = pl.multiple_of(t * n_batch, n_batch)
            nr = ar * sr - ai * si + xr_sc[pl.ds(r0, n_batch), lo:lo + SCAN_LANES]
            ni = ar * si + ai * sr + xi_sc[pl.ds(r0, n_batch), lo:lo + SCAN_LANES]
            xr_sc[pl.ds(r0, n_batch), lo:lo + SCAN_LANES] = nr
            xi_sc[pl.ds(r0, n_batch), lo:lo + SCAN_LANES] = ni
            return nr, ni

        sr, si = lax.fori_loop(0, n_steps, step,
                               (str_sc[:, lo:lo + SCAN_LANES], sti_sc[:, lo:lo + SCAN_LANES]),
                               unroll=True)
        str_sc[:, lo:lo + SCAN_LANES] = sr
        sti_sc[:, lo:lo + SCAN_LANES] = si

    pieces = []
    for j in range(SSM_CHUNKS):
        xr = xr_sc[:, CHUNK_STATE * j:CHUNK_STATE * (j + 1)].astype(BF16)
        xi = xi_sc[:, CHUNK_STATE * j:CHUNK_STATE * (j + 1)].astype(BF16)
        pieces.append(jnp.dot(xr, cr_sc[j], preferred_element_type=F32)
                      - jnp.dot(xi, ci_sc[j], preferred_element_type=F32))
    u = jnp.concatenate([u_ref[j] for j in range(SSM_CHUNKS)], axis=1)
    y = jnp.concatenate(pieces, axis=1) + d_ref[...] * u
    y = jax.nn.gelu(y)
    z = jnp.dot(y.astype(BF16), wg_sc[...], preferred_element_type=F32) + bg_ref[...]
    out = z[:, :D_SSM] * jax.nn.sigmoid(z[:, D_SSM:])
    for j in range(D_SSM // LANES):
        y_ref[j] = out[:, LANES * j:LANES * (j + 1)]
    sr_ref[...] = str_sc[...]
    si_ref[...] = sti_sc[...]


def _block_diag_b(b):
    bt = b.transpose(0, 2, 1).reshape(SSM_CHUNKS, CHUNK_GROUPS, SSM_GROUP, SSM_STATE)
    same = jnp.eye(CHUNK_GROUPS, dtype=bool)[None, :, None, :, None]
    t = jnp.where(same, bt[:, :, :, None, :], 0.0)
    return t.reshape(SSM_CHUNKS, CHUNK_IN, CHUNK_STATE)


def _block_diag_c(c):
    ct = c.transpose(0, 2, 1).reshape(SSM_CHUNKS, CHUNK_GROUPS, SSM_STATE, SSM_GROUP)
    same = jnp.eye(CHUNK_GROUPS, dtype=bool)[None, :, None, :, None]
    t = jnp.where(same, ct[:, :, :, None, :], 0.0)
    return t.reshape(SSM_CHUNKS, CHUNK_STATE, CHUNK_IN)


def _ssm(u_rows, h0_re, h0_im, lp, n_batch):
    rows = u_rows.shape[1]
    planes = D_SSM // LANES
    tr = min(SCAN_ROWS, rows)
    flat = lambda a: a.reshape(1, D_STATE)
    ldt = jnp.repeat(lp['log_dt'], SSM_STATE).reshape(1, D_STATE)
    const2 = lambda i: (0, 0)
    const3 = lambda i: (0, 0, 0)
    y, sr, si = pl.pallas_call(
        functools.partial(_ssm_kernel, n_batch=n_batch),
        grid=(rows // tr,),
        in_specs=[pl.BlockSpec((planes, tr, LANES), lambda i: (0, i, 0)),
                  pl.BlockSpec((n_batch, D_STATE), const2),
                  pl.BlockSpec((n_batch, D_STATE), const2),
                  pl.BlockSpec((1, D_STATE), const2),
                  pl.BlockSpec((1, D_STATE), const2),
                  pl.BlockSpec((1, D_STATE), const2),
                  pl.BlockSpec((SSM_CHUNKS, CHUNK_IN, CHUNK_STATE), const3),
                  pl.BlockSpec((SSM_CHUNKS, CHUNK_IN, CHUNK_STATE), const3),
                  pl.BlockSpec((SSM_CHUNKS, CHUNK_STATE, CHUNK_IN), const3),
                  pl.BlockSpec((SSM_CHUNKS, CHUNK_STATE, CHUNK_IN), const3),
                  pl.BlockSpec((1, D_SSM), const2),
                  pl.BlockSpec((D_SSM, 2 * D_SSM), const2),
                  pl.BlockSpec((1, 2 * D_SSM), const2)],
        out_specs=[pl.BlockSpec((planes, tr, LANES), lambda i: (0, i, 0)),
                   pl.BlockSpec((n_batch, D_STATE), const2),
                   pl.BlockSpec((n_batch, D_STATE), const2)],
        out_shape=[jax.ShapeDtypeStruct((planes, rows, LANES), F32),
                   jax.ShapeDtypeStruct((n_batch, D_STATE), F32),
                   jax.ShapeDtypeStruct((n_batch, D_STATE), F32)],
        scratch_shapes=[pltpu.VMEM((2, D_STATE), F32),
                        pltpu.VMEM((SSM_CHUNKS, CHUNK_IN, CHUNK_STATE), BF16), pltpu.VMEM((SSM_CHUNKS, CHUNK_IN, CHUNK_STATE), BF16),
                        pltpu.VMEM((SSM_CHUNKS, CHUNK_STATE, CHUNK_IN), BF16), pltpu.VMEM((SSM_CHUNKS, CHUNK_STATE, CHUNK_IN), BF16),
                        pltpu.VMEM((D_SSM, 2 * D_SSM), BF16),
                        pltpu.VMEM((n_batch, D_STATE), F32), pltpu.VMEM((n_batch, D_STATE), F32),
                        pltpu.VMEM((tr, D_STATE), F32), pltpu.VMEM((tr, D_STATE), F32)],
        compiler_params=_params(1),
        name="ssm",
    )(u_rows, h0_re.reshape(n_batch, D_STATE), h0_im.reshape(n_batch, D_STATE),
      flat(lp['lam_re']), flat(lp['lam_im']), ldt,
      _block_diag_b(lp['ssm_b_re']), _block_diag_b(lp['ssm_b_im']),
      _block_diag_c(lp['ssm_c_re']), _block_diag_c(lp['ssm_c_im']),
      lp['ssm_d'].reshape(1, D_SSM), lp['w_glu'], lp['b_glu'].reshape(1, 2 * D_SSM))
    return y, sr, si


def _softmax_pv(s, v):
    m = jnp.max(s, axis=-1, keepdims=True)
    p = jnp.exp(s - m)
    l = jnp.sum(p, axis=-1, keepdims=True)
    return jnp.dot(p.astype(BF16), v, preferred_element_type=F32) / l


def _attend(qb, k, v, out_ref, row0=0, bias_ref=None, valid=None, seq=0):
    tq = qb.shape[0]
    for h in range(N_HEADS):
        sl = slice(HEAD_DIM * h, HEAD_DIM * (h + 1))
        s = lax.dot_general(qb[:, sl], k[:, sl], _NT, preferred_element_type=F32)
        if bias_ref is not None:
            s = s + bias_ref[h]
        if valid is not None:
            s = jnp.where(valid, s, NEG_INF)
        out_ref[seq, row0:row0 + tq, sl] = _softmax_pv(s, v[:, sl])


def _attn_prompt_kernel(q_ref, kp_ref, kc_ref, vp_ref, vc_ref, qm_ref, mk_ref, mv_ref, bias_ref,
                        ya_ref, ym_ref, bias_sc):
    tq = ATT_TQ

    @pl.when((pl.program_id(0) == 0) & (pl.program_id(1) == 0))
    def _():
        q_chunk = lax.broadcasted_iota(I32, (tq, 3 * tq), 0) // CHUNK
        k_chunk = lax.broadcasted_iota(I32, (tq, 3 * tq), 1) // CHUNK
        ahead = k_chunk - q_chunk
        for h in range(N_HEADS):
            bias_sc[h] = jnp.where(ahead >= 0, jnp.where(ahead <= N_PREV_CHUNKS, bias_ref[h], NEG_INF),
                                   NEG_INF)

    k = jnp.concatenate([kp_ref[0], kc_ref[0]], axis=0)
    v = jnp.concatenate([vp_ref[0], vc_ref[0]], axis=0)
    for half in range(2):
        first_key = (2 * pl.program_id(1) - 2 + half) * tq
        kpos = first_key + lax.broadcasted_iota(I32, (1, 3 * tq), 1)
        _attend(q_ref[0, half * tq:(half + 1) * tq, :], k[half * tq:(half + 3) * tq],
                v[half * tq:(half + 3) * tq], ya_ref, half * tq, bias_sc, kpos >= 0)
    _attend(qm_ref[0], mk_ref[0].astype(BF16), mv_ref[0].astype(BF16), ym_ref)


def _attn_sample_kernel(q_ref, kn_ref, vn_ref, qm_ref, ck_ref, cv_ref, mk_ref, mv_ref, bias_ref,
                        ya_ref, ym_ref, nk_ref, nv_ref):
    n = kn_ref.shape[1]
    for i in range(q_ref.shape[0]):
        kk = jnp.concatenate([ck_ref[i], kn_ref[i]], axis=0)
        vv = jnp.concatenate([cv_ref[i], vn_ref[i]], axis=0)
        nk_ref[i] = kk[n:]
        nv_ref[i] = vv[n:]
        _attend(q_ref[i], kk.astype(BF16), vv.astype(BF16), ya_ref, 0, bias_ref, seq=i)
        _attend(qm_ref[i], mk_ref[i].astype(BF16), mv_ref[i].astype(BF16), ym_ref, seq=i)


def _rel_bias(table, n_q, n_k):
    period = n_q + n_k
    m = jnp.arange(period)
    offset = jnp.where(m < n_k, m, m - period)
    idx = jnp.clip(BAND - offset, -REL_CLIP, REL_CLIP) + REL_CLIP
    f = table.astype(F32)[:, idx]
    flat = jnp.tile(f, (1, n_q))[:, :n_q * (period - 1)]
    return flat.reshape(N_HEADS, n_q, period - 1)[:, :, :n_k]


def _attn_prompt(zb, mk, mv, table):
    b, s, _ = zb.shape
    tq = ATT_TQ
    bias = _rel_bias(table, tq, 3 * tq)
    col = lambda c: (lambda i, j: (i, j, c))
    prev = lambda c: (lambda i, j: (i, jnp.maximum(j - 1, 0), c))
    blk = (1, 2 * tq, D_ATT)
    return pl.pallas_call(
        _attn_prompt_kernel,
        grid=(b, s // (2 * tq)),
        in_specs=[pl.BlockSpec(blk, col(0)),
                  pl.BlockSpec(blk, prev(1)), pl.BlockSpec(blk, col(1)),
                  pl.BlockSpec(blk, prev(2)), pl.BlockSpec(blk, col(2)),
                  pl.BlockSpec(blk, col(3)),
                  pl.BlockSpec((1, N_MEM, D_MEM), lambda i, j: (i, 0, 0)),
                  pl.BlockSpec((1, N_MEM, D_MEM), lambda i, j: (i, 0, 0)),
                  pl.BlockSpec((N_HEADS, tq, 3 * tq), lambda i, j: (0, 0, 0))],
        out_specs=[pl.BlockSpec(blk, col(0)), pl.BlockSpec(blk, col(0))],
        out_shape=[jax.ShapeDtypeStruct((b, s, D_ATT), F32),
                   jax.ShapeDtypeStruct((b, s, D_MEM), F32)],
        scratch_shapes=[pltpu.VMEM((N_HEADS, tq, 3 * tq), F32)],
        compiler_params=_params(2),
        name="attn_prompt",
    )(zb, zb, zb, zb, zb, zb, mk, mv, bias)


def _attn_sample(kv, zb, cache_k, cache_v, mk, mv, table):
    b, n, _ = kv.shape
    w = cache_k.shape[1]
    bias = _rel_bias(table, n, w + n)
    col = lambda c: (lambda i: (i, 0, c))
    per = SAMPLE_SEQS if b % SAMPLE_SEQS == 0 else 1
    blk = (per, n, D_ATT)
    cblk = (per, w, D_ATT)
    mblk = (per, N_MEM, D_MEM)
    row = lambda i: (i, 0, 0)
    return pl.pallas_call(
        _attn_sample_kernel,
        grid=(b // per,),
        in_specs=[pl.BlockSpec(blk, col(0)), pl.BlockSpec(blk, col(0)), pl.BlockSpec(blk, col(1)),
                  pl.BlockSpec(blk, col(3)),
                  pl.BlockSpec(cblk, row), pl.BlockSpec(cblk, row),
                  pl.BlockSpec(mblk, row), pl.BlockSpec(mblk, row),
                  pl.BlockSpec((N_HEADS, n, w + n), lambda i: (0, 0, 0))],
        out_specs=[pl.BlockSpec(blk, row), pl.BlockSpec(blk, row),
                   pl.BlockSpec(cblk, row), pl.BlockSpec(cblk, row)],
        out_shape=[jax.ShapeDtypeStruct((b, n, D_ATT), F32),
                   jax.ShapeDtypeStruct((b, n, D_MEM), F32),
                   jax.ShapeDtypeStruct((b, w, D_ATT), F32),
                   jax.ShapeDtypeStruct((b, w, D_ATT), F32)],
        compiler_params=_params(1),
        name="attn_sample",
    )(zb, kv, kv, zb, cache_k, cache_v, mk, mv, bias)


def _rms(x, g):
    return x * lax.rsqrt(jnp.mean(jnp.square(x), axis=-1, keepdims=True) + LN_EPS) * g


def _layer_norm(x, g, b):
    mu = jnp.mean(x, axis=-1, keepdims=True)
    xc = x - mu
    var = jnp.mean(jnp.square(xc), axis=-1, keepdims=True)
    return xc * lax.rsqrt(var + LN_EPS) * g + b


def _split_bf16(a):
    hi = a.astype(BF16)
    lo = (a - hi.astype(F32)).astype(BF16)
    return hi, lo


def _merge_kernel(x_ref, ys_ref, ya_ref, ym_ref, gs_ref, ga_ref, gm_ref, wo_ref, l1g_ref, l1b_ref,
                  wrt_ref, brt_ref,
                  h_ref, pos_ref, gate_ref, cnt_ref,
                  wo_sc, *, nb):
    st = x_ref.shape[1]
    tm = nb * st
    n_batch = ys_ref.shape[1] // st

    @pl.when((pl.program_id(0) == 0) & (pl.program_id(1) == 0))
    def _():
        wo_sc[...] = wo_ref[...].astype(BF16)

    x = x_ref[...].reshape(tm, D_MODEL)
    first = pl.program_id(1) * nb
    ys = jnp.concatenate(
        [jnp.concatenate([ys_ref[c, pl.ds(first + i, st, stride=n_batch), :] for c in range(D_SSM // LANES)],
                         axis=1) for i in range(nb)], axis=0)
    ya = ya_ref[...].reshape(tm, D_ATT)
    ym = ym_ref[...].reshape(tm, D_MEM)
    a = _rms(ys, gs_ref[...]).astype(BF16)
    b = _rms(ya, ga_ref[...]).astype(BF16)
    c = _rms(ym, gm_ref[...]).astype(BF16)
    mix = (jnp.dot(a, wo_sc[0:D_SSM, :], preferred_element_type=F32)
           + jnp.dot(b, wo_sc[D_SSM:D_SSM + D_ATT, :], preferred_element_type=F32)
           + jnp.dot(c, wo_sc[D_SSM + D_ATT:, :], preferred_element_type=F32))
    h = _layer_norm(DEEPNORM_ALPHA * x + mix, l1g_ref[...], l1b_ref[...])
    h_ref[...] = h

    h_hi, h_lo = _split_bf16(h)
    w_hi, w_lo = _split_bf16(wrt_ref[...])
    logits = (lax.dot_general(w_hi, h_hi, _NT, preferred_element_type=F32)
              + lax.dot_general(w_hi, h_lo, _NT, preferred_element_type=F32)
              + lax.dot_general(w_lo, h_hi, _NT, preferred_element_type=F32)
              + brt_ref[...])
    erow = lax.broadcasted_iota(I32, (N_EXPERTS, tm), 0).astype(F32)
    tops, picks = [], []
    l = logits
    for k in range(TOP_K):
        m = jnp.max(l, axis=0, keepdims=True)
        e = jnp.min(jnp.where(l == m, erow, float(N_EXPERTS)), axis=0, keepdims=True)
        pick = erow == e
        tops.append(m)
        picks.append(jnp.where(pick, 1.0, 0.0))
        l = jnp.where(pick, -jnp.inf, l)
    ex = [jnp.exp(t - tops[0]) for t in tops]
    den = ex[0] + ex[1] + ex[2] + ex[3]
    for k in range(TOP_K):
        gate_ref[k:k + 1, :] = ex[k] / den

    chosen = picks[0] + picks[1] + picks[2] + picks[3]
    chosen_b = chosen.astype(BF16)
    earlier_tok = (lax.broadcasted_iota(I32, (tm, tm), 0) < lax.broadcasted_iota(I32, (tm, tm), 1))
    within = jnp.dot(chosen_b, jnp.where(earlier_tok, 1.0, 0.0).astype(BF16),
                     preferred_element_type=F32)
    lower_exp = (lax.broadcasted_iota(I32, (N_EXPERTS, N_EXPERTS), 1)
                 < lax.broadcasted_iota(I32, (N_EXPERTS, N_EXPERTS), 0))
    below = jnp.dot(jnp.where(lower_exp, 1.0, 0.0).astype(BF16), chosen_b,
                    preferred_element_type=F32)
    slot = within + jnp.sum(below, axis=1, keepdims=True)
    for k in range(TOP_K):
        pos_ref[k:k + 1, :] = jnp.sum(picks[k] * slot, axis=0, keepdims=True).astype(I32)
    cnt_ref[0] = jnp.sum(chosen, axis=1, keepdims=True)


def _merge(x, ys_tm, ya, ym, lp, nb, st):
    b, s, _ = x.shape
    tm = nb * st
    assert tm == TOKEN_TM
    n_s = s // st
    t_all = b * s
    tile = lambda j, i: (i * n_s + j)
    c2 = lambda j, i: (0, 0)
    row3 = lambda j, i: (i, j, 0)
    vec = lambda a: a.reshape(1, -1)
    return pl.pallas_call(
        functools.partial(_merge_kernel, nb=nb),
        grid=(n_s, b // nb),
        in_specs=[pl.BlockSpec((nb, st, D_MODEL), row3),
                  pl.BlockSpec((D_SSM // LANES, st * b, LANES), lambda j, i: (0, j, 0)),
                  pl.BlockSpec((nb, st, D_ATT), row3),
                  pl.BlockSpec((nb, st, D_MEM), row3),
                  pl.BlockSpec((1, D_SSM), c2), pl.BlockSpec((1, D_ATT), c2),
                  pl.BlockSpec((1, D_MEM), c2),
                  pl.BlockSpec((D_MODEL, D_MODEL), c2),
                  pl.BlockSpec((1, D_MODEL), c2), pl.BlockSpec((1, D_MODEL), c2),
                  pl.BlockSpec((N_EXPERTS, D_MODEL), c2), pl.BlockSpec((N_EXPERTS, 1), c2)],
        out_specs=[pl.BlockSpec((tm, D_MODEL), lambda j, i: (tile(j, i), 0)),
                   pl.BlockSpec((TOP_K, tm), lambda j, i: (0, tile(j, i))),
                   pl.BlockSpec((TOP_K, tm), lambda j, i: (0, tile(j, i))),
                   pl.BlockSpec((1, N_EXPERTS, 1), lambda j, i: (tile(j, i), 0, 0))],
        out_shape=[jax.ShapeDtypeStruct((t_all, D_MODEL), F32),
                   jax.ShapeDtypeStruct((TOP_K, t_all), I32),
                   jax.ShapeDtypeStruct((TOP_K, t_all), F32),
                   jax.ShapeDtypeStruct((t_all // tm, N_EXPERTS, 1), F32)],
        scratch_shapes=[pltpu.VMEM((D_MODEL, D_MODEL), BF16)],
        compiler_params=_params(2),
        name="merge_router",
    )(x, ys_tm, ya, ym, vec(lp['g_ssm']), vec(lp['g_att']), vec(lp['g_mem']), lp['w_out'],
      vec(lp['ln1_g']), vec(lp['ln1_b']), lp['w_router'].T, lp['b_router'].reshape(N_EXPERTS, 1))


def _rows(start, size):
    return pl.ds(pl.multiple_of(start * ROW_SUBLANES, ROW_SUBLANES), size * ROW_SUBLANES)


def _store_rows(ref, value, row0=0):
    n = value.shape[0]
    for j in range(ROW_SUBLANES):
        ref[pl.ds(row0 * ROW_SUBLANES + j, n, stride=ROW_SUBLANES), :] = value[:, LANES * j:LANES * (j + 1)]


def _load_rows(ref, dtype=F32):
    n = ref.shape[0] // ROW_SUBLANES
    return jnp.concatenate([ref[pl.ds(j, n, stride=ROW_SUBLANES), :].astype(dtype)
                            for j in range(ROW_SUBLANES)], axis=1)


def _for_each_run_piece(n, max_rows, fn):
    for bit in reversed(range(max_rows.bit_length())):
        size = 1 << bit
        start = (n >> (bit + 1)) << (bit + 1)

        @pl.when((n & size) != 0)
        def _(start=start, size=size):
            fn(start, size)


def _dispatch_kernel(n_ref, off_ref, dst_ref, padlo_ref, padn_ref, used_ref,
                     pos_ref, h1_ref, h2_ref, xs_hbm, sorted_sc, zero_sc, sem, zsem, *, n_first):
    i = pl.program_id(0)
    tm = h1_ref.shape[0]
    n_slots = TOP_K * tm
    n_blocks = xs_hbm.shape[0] // (EXPERT_TM * ROW_SUBLANES)

    @pl.when(i == 0)
    def _():
        zero_sc[...] = jnp.zeros_like(zero_sc)

        def pad_copy(e, start, size):
            return pltpu.make_async_copy(zero_sc.at[_rows(0, size)],
                                         xs_hbm.at[_rows(padlo_ref[e] + start, size)], zsem)

        def tail_copy(blk):
            return pltpu.make_async_copy(zero_sc, xs_hbm.at[_rows(blk * EXPERT_TM, EXPERT_TM)], zsem)

        for e in range(N_EXPERTS):
            _for_each_run_piece(padn_ref[e], EXPERT_TM - 1,
                                lambda start, size, e=e: pad_copy(e, start, size).start())

        def tail_start(blk, carry):
            tail_copy(blk).start()
            return carry

        lax.fori_loop(used_ref[0], n_blocks, tail_start, 0)
        for e in range(N_EXPERTS):
            _for_each_run_piece(padn_ref[e], EXPERT_TM - 1,
                                lambda start, size, e=e: pad_copy(e, start, size).wait())

        def tail_wait(blk, carry):
            tail_copy(blk).wait()
            return carry

        lax.fori_loop(used_ref[0], n_blocks, tail_wait, 0)

    n_tiles = pl.num_programs(0) - 1
    slot = lax.rem(i, N_SORT_BUFS)
    prev_slot = lax.rem(i + N_SORT_BUFS - 1, N_SORT_BUFS)
    buf = sorted_sc.at[slot]
    prev = sorted_sc.at[prev_slot]

    def wait_tile(sl):
        pltpu.make_async_copy(sorted_sc.at[sl], xs_hbm.at[_rows(0, n_slots)], sem.at[sl]).wait()

    @pl.when(i >= N_SORT_BUFS)
    def _():
        wait_tile(slot)

    pos = pos_ref[...]
    hb = jnp.where(i < n_first, h1_ref[...], h2_ref[...]).astype(BF16)
    base = jnp.maximum(i - 1, 0) * N_EXPERTS
    rows_c = n_slots // SORT_CHUNKS
    experts_c = N_EXPERTS // SORT_CHUNKS
    for c in range(SORT_CHUNKS):
        for e in range(c * experts_c, (c + 1) * experts_c):
            off = off_ref[base + e]
            dst = dst_ref[base + e]

            def run_start(start, size, off=off, dst=dst):
                pltpu.make_async_copy(prev.at[_rows(off + start, size)],
                                      xs_hbm.at[_rows(dst + start, size)], sem.at[prev_slot]).start()

            _for_each_run_piece(jnp.where(i >= 1, n_ref[base + e], 0), tm, run_start)

        srow = lax.broadcasted_iota(I32, (rows_c, tm), 0) + c * rows_c
        perm = jnp.where(srow == pos[0:1], 1.0,
                         jnp.where(srow == pos[1:2], 1.0,
                                   jnp.where(srow == pos[2:3], 1.0,
                                             jnp.where(srow == pos[3:4], 1.0, 0.0)))).astype(BF16)
        _store_rows(buf, jnp.dot(perm, hb, preferred_element_type=F32), c * rows_c)

    @pl.when(i == n_tiles)
    def _():
        wait_tile(prev_slot)

        @pl.when(i >= 2)
        def _():
            wait_tile(lax.rem(i + N_SORT_BUFS - 2, N_SORT_BUFS))


def _dispatch(run_n, run_off, run_dst, pad_lo, pad_n, n_used, pos, h1, h2, cap):
    tm = TOKEN_TM
    n_first = h1.shape[0] // tm
    n_tiles = n_first + h2.shape[0] // tm
    grid_spec = pltpu.PrefetchScalarGridSpec(
        num_scalar_prefetch=6,
        grid=(n_tiles + 1,),
        in_specs=[pl.BlockSpec((TOP_K, tm), lambda i, *_: (0, jnp.minimum(i, n_tiles - 1))),
                  pl.BlockSpec((tm, D_MODEL), lambda i, *_: (jnp.minimum(i, n_first - 1), 0)),
                  pl.BlockSpec((tm, D_MODEL), lambda i, *_: (jnp.clip(i - n_first, 0, n_tiles - n_first - 1), 0))],
        out_specs=pl.BlockSpec(memory_space=pl.ANY),
        scratch_shapes=[pltpu.VMEM((N_SORT_BUFS, TOP_K * tm * ROW_SUBLANES, LANES), F32),
                        pltpu.VMEM((EXPERT_TM * ROW_SUBLANES, LANES), F32),
                        pltpu.SemaphoreType.DMA((N_SORT_BUFS,)), pltpu.SemaphoreType.DMA],
    )
    return pl.pallas_call(
        functools.partial(_dispatch_kernel, n_first=n_first),
        grid_spec=grid_spec,
        out_shape=jax.ShapeDtypeStruct((cap * ROW_SUBLANES, LANES), F32),
        compiler_params=_params(1),
        name="moe_dispatch",
    )(run_n, run_off, run_dst, pad_lo, pad_n, n_used, pos, h1, h2)


def _expert_kernel(be_ref, first_ref, ord_ref, seq_ref, used_ref,
                   x_ref, bgu_ref, bd_ref, wgu_hbm, wd_hbm, o_ref,
                   wgu_st, wd_st, wgu_sc, wd_sc, sem):
    i = pl.program_id(0)

    def weight_copies(e):
        return (pltpu.make_async_copy(wgu_hbm.at[e], wgu_st, sem.at[0]),
                pltpu.make_async_copy(wd_hbm.at[e], wd_st, sem.at[1]))

    @pl.when(i == 0)
    def _():
        for c in weight_copies(seq_ref[0]):
            c.start()

    @pl.when(i < used_ref[0])
    def _():
        @pl.when(first_ref[i] == 1)
        def _():
            k = ord_ref[i]
            for c in weight_copies(seq_ref[k]):
                c.wait()
            wgu_sc[...] = wgu_st[...].astype(BF16)
            wd_sc[...] = wd_st[...].astype(BF16)

            @pl.when(k + 1 < used_ref[1])
            def _():
                for c in weight_copies(seq_ref[k + 1]):
                    c.start()

        gu = jnp.dot(_load_rows(x_ref, BF16), wgu_sc[...], preferred_element_type=F32) + bgu_ref[0]
        gate = jnp.minimum(gu[:, :D_FF], SWIGLU_LIMIT)
        lin = jnp.clip(gu[:, D_FF:], -SWIGLU_LIMIT, SWIGLU_LIMIT)
        act = gate * jax.nn.sigmoid(SWIGLU_ALPHA * gate) * (lin + 1.0)
        _store_rows(o_ref, jnp.dot(act.astype(BF16), wd_sc[...], preferred_element_type=F32) + bd_ref[0])

    @pl.when(i >= used_ref[0])
    def _():
        o_ref[...] = jnp.zeros_like(o_ref)


def _experts(block_expert, block_first, block_ord, expert_seq, n_used, xs, lp):
    tm = EXPERT_TM * ROW_SUBLANES
    grid_spec = pltpu.PrefetchScalarGridSpec(
        num_scalar_prefetch=5,
        grid=(xs.shape[0] // tm,),
        in_specs=[pl.BlockSpec((tm, LANES), lambda i, be, *_: (i, 0)),
                  pl.BlockSpec((1, 1, 2 * D_FF), lambda i, be, *_: (be[i], 0, 0)),
                  pl.BlockSpec((1, 1, D_MODEL), lambda i, be, *_: (be[i], 0, 0)),
                  pl.BlockSpec(memory_space=pl.ANY),
                  pl.BlockSpec(memory_space=pl.ANY)],
        out_specs=pl.BlockSpec((tm, LANES), lambda i, be, *_: (i, 0)),
        scratch_shapes=[pltpu.VMEM((D_MODEL, 2 * D_FF), F32), pltpu.VMEM((D_FF, D_MODEL), F32),
                        pltpu.VMEM((D_MODEL, 2 * D_FF), BF16), pltpu.VMEM((D_FF, D_MODEL), BF16),
                        pltpu.SemaphoreType.DMA((2,))],
    )
    return pl.pallas_call(
        _expert_kernel,
        grid_spec=grid_spec,
        out_shape=jax.ShapeDtypeStruct(xs.shape, F32),
        compiler_params=_params(1),
        name="moe_experts",
    )(block_expert, block_first, block_ord, expert_seq, n_used, xs,
      lp['b_gu'].reshape(N_EXPERTS, 1, 2 * D_FF), lp['b_down'].reshape(N_EXPERTS, 1, D_MODEL),
      lp['w_gu'], lp['w_down'])


def _combine_kernel(n_ref, off_ref, dst_ref, pos_ref, gate_ref, h1_ref, h2_ref, ys_hbm, g_ref, b_ref,
                    y1_ref, y2_ref, sorted_sc, w_sc, sem, *, n_first):
    i = pl.program_id(0)
    n_tiles = pl.num_programs(0) - 1
    tm = h1_ref.shape[0]
    n_slots = TOP_K * tm
    slot = lax.rem(i, 2)
    buf = sorted_sc.at[slot]
    pos = pos_ref[...]
    gates = gate_ref[...]
    rows_per = tm // N_EXPERTS
    base = jnp.minimum(i, n_tiles - 1) * N_EXPERTS
    for e in range(N_EXPERTS):
        off = off_ref[base + e]
        dst = dst_ref[base + e]

        def run_start(start, size, off=off, dst=dst):
            pltpu.make_async_copy(ys_hbm.at[_rows(dst + start, size)],
                                  buf.at[_rows(off + start, size)], sem.at[slot]).start()

        _for_each_run_piece(jnp.where(i < n_tiles, n_ref[base + e], 0), tm, run_start)

        r = slice(e * rows_per, (e + 1) * rows_per)
        scol = lax.broadcasted_iota(I32, (rows_per, n_slots), 1)
        w_sc[r, :] = jnp.where(
            scol == pos[r, 0:1], gates[r, 0:1],
            jnp.where(scol == pos[r, 1:2], gates[r, 1:2],
                      jnp.where(scol == pos[r, 2:3], gates[r, 2:3],
                                jnp.where(scol == pos[r, 3:4], gates[r, 3:4], 0.0)))).astype(BF16)

    @pl.when(i >= 1)
    def _():
        done = sorted_sc.at[1 - slot]
        pltpu.make_async_copy(ys_hbm.at[_rows(0, n_slots)], done, sem.at[1 - slot]).wait()

        f = jnp.dot(w_sc[...], _load_rows(done, BF16), preferred_element_type=F32)

        @pl.when(i - 1 < n_first)
        def _():
            y1_ref[...] = _layer_norm(DEEPNORM_ALPHA * h1_ref[...] + f, g_ref[...], b_ref[...])

        @pl.when(i - 1 >= n_first)
        def _():
            y2_ref[...] = _layer_norm(DEEPNORM_ALPHA * h2_ref[...] + f, g_ref[...], b_ref[...])


def _combine(run_n, run_off, run_dst, pos_t, gates_t, h1, h2, ys, lp):
    tm = TOKEN_TM
    t_first = h1.shape[0]
    t = t_first + h2.shape[0]
    n_first = t_first // tm
    n_rest = (t - t_first) // tm
    c2 = lambda i, *_: (0, 0)
    done = lambda i, *_: (jnp.maximum(i - 1, 0), 0)
    done1 = lambda i, *_: (jnp.clip(i - 1, 0, n_first - 1), 0)
    done2 = lambda i, *_: (jnp.clip(i - 1 - n_first, 0, n_rest - 1), 0)
    grid_spec = pltpu.PrefetchScalarGridSpec(
        num_scalar_prefetch=3,
        grid=(t // tm + 1,),
        in_specs=[pl.BlockSpec((tm, TOP_K), done),
                  pl.BlockSpec((tm, TOP_K), done),
                  pl.BlockSpec((tm, D_MODEL), done1), pl.BlockSpec((tm, D_MODEL), done2),
                  pl.BlockSpec(memory_space=pl.ANY),
                  pl.BlockSpec((1, D_MODEL), c2), pl.BlockSpec((1, D_MODEL), c2)],
        out_specs=[pl.BlockSpec((tm, D_MODEL), done1), pl.BlockSpec((tm, D_MODEL), done2)],
        scratch_shapes=[pltpu.VMEM((2, TOP_K * tm * ROW_SUBLANES, LANES), F32),
                        pltpu.VMEM((tm, TOP_K * tm), BF16),
                        pltpu.SemaphoreType.DMA((2,))],
    )
    return pl.pallas_call(
        functools.partial(_combine_kernel, n_first=n_first),
        grid_spec=grid_spec,
        out_shape=[jax.ShapeDtypeStruct((t_first, D_MODEL), F32),
                   jax.ShapeDtypeStruct((t - t_first, D_MODEL), F32)],
        compiler_params=_params(1),
        name="moe_combine",
    )(run_n, run_off, run_dst, pos_t, gates_t, h1, h2, ys,
      lp['ln2_g'].reshape(1, D_MODEL), lp['ln2_b'].reshape(1, D_MODEL))


def _moe_and_norm(h1, h2, pos, gates, tile_counts, lp):
    t = h1.shape[0] + h2.shape[0]
    te = EXPERT_TM
    n_tiles = t // TOKEN_TM
    n_blocks = (t * TOP_K) // te + N_EXPERTS
    cap = n_blocks * te
    cnt = tile_counts.reshape(n_tiles, N_EXPERTS).astype(I32)
    counts = jnp.sum(cnt, axis=0)
    padded = (counts + te - 1) // te * te
    pad_ends = jnp.cumsum(padded)
    pad_starts = pad_ends - padded
    run_dst = pad_starts[None, :] + jnp.cumsum(cnt, axis=0) - cnt
    run_off = jnp.cumsum(cnt, axis=1) - cnt
    blk_start = jnp.arange(n_blocks, dtype=I32) * te
    expert_of = lambda slot_idx: jnp.minimum(jnp.sum(slot_idx[..., None] >= pad_ends, axis=-1), N_EXPERTS - 1)
    total = pad_ends[-1]
    be = jnp.where(blk_start < total, expert_of(blk_start), expert_of(jnp.maximum(total - 1, 0))).astype(I32)
    is_e = be[:, None] == jnp.arange(N_EXPERTS, dtype=I32)[None, :]
    pick = lambda table: jnp.sum(jnp.where(is_e, table[None, :], 0), axis=1)
    in_use = counts > 0
    ordinal = jnp.cumsum(in_use.astype(I32)) - 1
    rank = jnp.arange(N_EXPERTS, dtype=I32)
    expert_seq = jnp.sum(jnp.where(in_use[None, :] & (ordinal[None, :] == rank[:, None]), rank[None, :], 0),
                         axis=1)
    block_ord = pick(ordinal)
    block_first = (blk_start == pick(pad_starts)) & (blk_start < total)
    used = jnp.stack([total // te, jnp.sum(in_use.astype(I32))]).astype(I32)
    flat = lambda a: a.reshape(-1).astype(I32)
    xs = _dispatch(flat(cnt), flat(run_off), flat(run_dst), flat(pad_starts + counts),
                   flat(padded - counts), used, pos, h1, h2, cap)
    ys = _experts(be, flat(block_first), flat(block_ord), expert_seq, used, xs, lp)
    return _combine(flat(cnt), flat(run_off), flat(run_dst), pos.T, gates.T, h1, h2, ys, lp)


def kernel(x_prompt, x_sample, cache_attn_k, cache_attn_v, cache_mem_k, cache_mem_v, state_ssm_re, state_ssm_im, mem_prompt, w_in, lam_re, lam_im, log_dt, ssm_b_re, ssm_b_im, ssm_c_re, ssm_c_im, ssm_d, w_glu, b_glu, rel_bias, w_mem_kv, g_ssm, g_att, g_mem, w_out, ln1_g, ln1_b, w_router, b_router, w_gu, b_gu, w_down, b_down, ln2_g, ln2_b):
    assert w_in.shape[0] == 1, "single-layer step"
    lp = dict(w_in=w_in[0], lam_re=lam_re[0], lam_im=lam_im[0], log_dt=log_dt[0],
              ssm_b_re=ssm_b_re[0], ssm_b_im=ssm_b_im[0], ssm_c_re=ssm_c_re[0], ssm_c_im=ssm_c_im[0],
              ssm_d=ssm_d[0], w_glu=w_glu[0], b_glu=b_glu[0], rel_bias=rel_bias[0],
              w_mem_kv=w_mem_kv[0], g_ssm=g_ssm[0], g_att=g_att[0], g_mem=g_mem[0], w_out=w_out[0],
              ln1_g=ln1_g[0], ln1_b=ln1_b[0], w_router=w_router[0], b_router=b_router[0],
              w_gu=w_gu[0], b_gu=b_gu[0], w_down=w_down[0], b_down=b_down[0],
              ln2_g=ln2_g[0], ln2_b=ln2_b[0])

    bp, sp, _ = x_prompt.shape
    bs, ss, _ = x_sample.shape
    heads = lambda a: a.reshape(a.shape[0], a.shape[1], N_HEADS, HEAD_DIM)
    state = lambda a: a.reshape(a.shape[0], N_GROUPS, SSM_STATE)

    w = min(BAND, sp)
    u_p, kv_p, zb = _in_proj(x_prompt, lp['w_in'], min(IN_PROJ_TS, sp), w)
    mk, mv = _mem_kv(mem_prompt, lp['w_mem_kv'])
    ya, ym = _attn_prompt(zb, mk, mv, lp['rel_bias'])
    zeros = jnp.zeros((bp, D_STATE), F32)
    ys_p, sr_p, si_p = _ssm(u_p, zeros, zeros, lp, bp)
    h_p, pos_p, gates_p, cnt_p = _merge(x_prompt, ys_p, ya, ym, lp, 1, TOKEN_TM)
    k_p = heads(kv_p[:, :, :D_ATT])
    v_p = heads(kv_p[:, :, D_ATT:])

    wc = cache_attn_k.shape[2]
    u_s, kv_s, zb_s = _in_proj(x_sample, lp['w_in'], ss, ss)
    ya_s, ym_s, nk, nv = _attn_sample(
        kv_s, zb_s, cache_attn_k[0].reshape(bs, wc, D_ATT), cache_attn_v[0].reshape(bs, wc, D_ATT),
        cache_mem_k[0].reshape(bs, N_MEM, D_MEM), cache_mem_v[0].reshape(bs, N_MEM, D_MEM),
        lp['rel_bias'])
    ys_s, sr_s, si_s = _ssm(u_s, state_ssm_re[0], state_ssm_im[0], lp, bs)
    h_s, pos_s, gates_s, cnt_s = _merge(x_sample, ys_s, ya_s, ym_s, lp, TOKEN_TM // ss, ss)

    y_p, y_s = _moe_and_norm(h_p, h_s, jnp.concatenate([pos_p, pos_s], axis=1),
                             jnp.concatenate([gates_p, gates_s], axis=1),
                             jnp.concatenate([cnt_p, cnt_s], axis=0), lp)

    return (y_p.reshape(bp, sp, D_MODEL), y_s.reshape(bs, ss, D_MODEL),
            k_p[None], v_p[None], heads(mk)[None], heads(mv)[None], state(sr_p)[None], state(si_p)[None],
            heads(nk)[None], heads(nv)[None], state(sr_s)[None], state(si_s)[None])
```

```python
import functools

import jax
import jax.numpy as jnp
from jax import lax
from jax.experimental import pallas as pl
from jax.experimental.pallas import tpu as pltpu

F32 = jnp.float32
BF16 = jnp.bfloat16
I32 = jnp.int32

D_MODEL = 1024
D_SSM = 512
D_ATT = 256
D_MEM = 256
D_IN = D_SSM + 3 * D_ATT + D_MEM
D_REST = D_IN - D_SSM
HEAD_DIM = 64
N_HEADS = 4
N_GROUPS = 32
SSM_GROUP = 16
SSM_STATE = 64
D_STATE = N_GROUPS * SSM_STATE
CHUNK_GROUPS = 8
SSM_CHUNKS = N_GROUPS // CHUNK_GROUPS
CHUNK_IN = CHUNK_GROUPS * SSM_GROUP
CHUNK_STATE = CHUNK_GROUPS * SSM_STATE
CHUNK = 64
N_PREV_CHUNKS = 8
BAND = N_PREV_CHUNKS * CHUNK
REL_CLIP = 128
N_MEM = 256
N_EXPERTS = 32
TOP_K = 4
D_FF = D_MODEL
SWIGLU_LIMIT = 7.0
SWIGLU_ALPHA = 1.702
LN_EPS = 1e-5
NEG_INF = -1e30
ATT_SCALE = HEAD_DIM ** -0.5
DEEPNORM_ALPHA = 2.0 ** 0.25

V7X_VMEM_LIMIT = 56 * 1024 * 1024
IN_PROJ_TS = 128
ATT_TQ = 4 * CHUNK
SAMPLE_SEQS = 4
SCAN_LANES = 1024
SCAN_ROWS = 1024
LANES = 128
ROW_SUBLANES = D_MODEL // LANES
EXPERT_TM = 512
TOKEN_TM = 512
N_SORT_BUFS = 3
SORT_CHUNKS = 16

_NT = (((1,), (1,)), ((), ()))


def _params(n_axes, vmem=V7X_VMEM_LIMIT):
    return pltpu.CompilerParams(dimension_semantics=("arbitrary",) * n_axes,
                                vmem_limit_bytes=vmem)


def _in_proj_kernel(x_ref, w_ref, u_ref, kv_ref, zb_ref, wb_ref):
    @pl.when(pl.program_id(0) == 0)
    def _():
        wb_ref[...] = w_ref[...].astype(BF16)

    nb, ts, _ = x_ref.shape
    x = x_ref[...].reshape(nb * ts, D_MODEL).astype(BF16)
    z = jnp.dot(x, wb_ref[...], preferred_element_type=F32)
    for b in range(nb):
        for c in range(D_SSM // LANES):
            u_ref[c, pl.ds(b, ts, stride=nb), :] = z[b * ts:(b + 1) * ts, LANES * c:LANES * (c + 1)]
    zr = z[:, D_SSM:]
    kv_ref[...] = zr[:, D_ATT:3 * D_ATT].reshape(nb, ts, 2 * D_ATT)
    zb = jnp.concatenate([zr[:, :D_ATT] * ATT_SCALE, zr[:, D_ATT:3 * D_ATT], zr[:, 3 * D_ATT:] * ATT_SCALE],
                         axis=1)
    zb_ref[...] = zb.astype(BF16).reshape(nb, ts, D_REST)


def _in_proj(x, w_in, ts, tail):
    b, s, _ = x.shape
    skipped = (s - tail) // ts
    return pl.pallas_call(
        _in_proj_kernel,
        grid=(s // ts,),
        in_specs=[pl.BlockSpec((b, ts, D_MODEL), lambda j: (0, j, 0)),
                  pl.BlockSpec((D_MODEL, D_IN), lambda j: (0, 0))],
        out_specs=[pl.BlockSpec((D_SSM // LANES, ts * b, LANES), lambda j: (0, j, 0)),
                   pl.BlockSpec((b, ts, 2 * D_ATT), lambda j: (0, jnp.maximum(j - skipped, 0), 0)),
                   pl.BlockSpec((b, ts, D_REST), lambda j: (0, j, 0))],
        out_shape=[jax.ShapeDtypeStruct((D_SSM // LANES, s * b, LANES), F32),
                   jax.ShapeDtypeStruct((b, tail, 2 * D_ATT), F32),
                   jax.ShapeDtypeStruct((b, s, D_REST), BF16)],
        scratch_shapes=[pltpu.VMEM((D_MODEL, D_IN), BF16)],
        compiler_params=_params(1),
        name="in_proj",
    )(x, w_in)


def _mem_kv_kernel(m_ref, w_ref, mk_ref, mv_ref):
    kv = jnp.dot(m_ref[0].astype(BF16), w_ref[...].astype(BF16), preferred_element_type=F32)
    mk_ref[0] = kv[:, :D_MEM]
    mv_ref[0] = kv[:, D_MEM:]


def _mem_kv(mem, w_mem_kv):
    b = mem.shape[0]
    return pl.pallas_call(
        _mem_kv_kernel,
        grid=(b,),
        in_specs=[pl.BlockSpec((1, N_MEM, D_MODEL), lambda i: (i, 0, 0)),
                  pl.BlockSpec((D_MODEL, 2 * D_MEM), lambda i: (0, 0))],
        out_specs=[pl.BlockSpec((1, N_MEM, D_MEM), lambda i: (i, 0, 0)),
                   pl.BlockSpec((1, N_MEM, D_MEM), lambda i: (i, 0, 0))],
        out_shape=[jax.ShapeDtypeStruct((b, N_MEM, D_MEM), F32)] * 2,
        compiler_params=_params(1),
        name="mem_kv",
    )(mem, w_mem_kv)


def _ssm_kernel(u_ref, h0r_ref, h0i_ref, lr_ref, li_ref, ldt_ref, bre_ref, bim_ref,
                cre_ref, cim_ref, d_ref, wg_ref, bg_ref,
                y_ref, sr_ref, si_ref,
                a_sc, bbr_sc, bbi_sc, cr_sc, ci_sc, wg_sc, str_sc, sti_sc, xr_sc, xi_sc,
                *, n_batch):
    n_rows = u_ref.shape[1]
    n_steps = n_rows // n_batch

    @pl.when(pl.program_id(0) == 0)
    def _():
        lr = lr_ref[...]
        li = li_ref[...]
        dt = jnp.exp(ldt_ref[...])
        mag = jnp.exp(lr * dt)
        ar = mag * jnp.cos(li * dt)
        ai = mag * jnp.sin(li * dt)
        den = lr * lr + li * li
        fr = ((ar - 1.0) * lr + ai * li) / den
        fi = (ai * lr - (ar - 1.0) * li) / den
        a_sc[0:1, :] = ar
        a_sc[1:2, :] = ai
        for j in range(SSM_CHUNKS):
            frj = fr[:, CHUNK_STATE * j:CHUNK_STATE * (j + 1)]
            fij = fi[:, CHUNK_STATE * j:CHUNK_STATE * (j + 1)]
            bbr_sc[j] = (frj * bre_ref[j] - fij * bim_ref[j]).astype(BF16)
            bbi_sc[j] = (frj * bim_ref[j] + fij * bre_ref[j]).astype(BF16)
            cr_sc[j] = cre_ref[j].astype(BF16)
            ci_sc[j] = cim_ref[j].astype(BF16)
        wg_sc[...] = wg_ref[...].astype(BF16)
        str_sc[...] = h0r_ref[...]
        sti_sc[...] = h0i_ref[...]

    for j in range(SSM_CHUNKS):
        uc = u_ref[j].astype(BF16)
        xr_sc[:, CHUNK_STATE * j:CHUNK_STATE * (j + 1)] = jnp.dot(uc, bbr_sc[j], preferred_element_type=F32)
        xi_sc[:, CHUNK_STATE * j:CHUNK_STATE * (j + 1)] = jnp.dot(uc, bbi_sc[j], preferred_element_type=F32)

    for c in range(D_STATE // SCAN_LANES):
        lo = c * SCAN_LANES
        ar = jnp.broadcast_to(a_sc[0:1, lo:lo + SCAN_LANES], (n_batch, SCAN_LANES))
        ai = jnp.broadcast_to(a_sc[1:2, lo:lo + SCAN_LANES], (n_batch, SCAN_LANES))

        def step(t, carry, lo=lo, ar=ar, ai=ai):
            sr, si = carry
            r0 = pl.multiple_of(t * n_batch, n_batch)
            nr = ar * sr - ai * si + xr_sc[pl.ds(r0, n_batch), lo:lo + SCAN_LANES]
            ni = ar * si + ai * sr + xi_sc[pl.ds(r0, n_batch), lo:lo + SCAN_LANES]
            xr_sc[pl.ds(r0, n_batch), lo:lo + SCAN_LANES] = nr
            xi_sc[pl.ds(r0, n_batch), lo:lo + SCAN_LANES] = ni
            return nr, ni

        sr, si = lax.fori_loop(0, n_steps, step,
                               (str_sc[:, lo:lo + SCAN_LANES], sti_sc[:, lo:lo + SCAN_LANES]),
                               unroll=True)
        str_sc[:, lo:lo + SCAN_LANES] = sr
        sti_sc[:, lo:lo + SCAN_LANES] = si

    pieces = []
    for j in range(SSM_CHUNKS):
        xr = xr_sc[:, CHUNK_STATE * j:CHUNK_STATE * (j + 1)].astype(BF16)
        xi = xi_sc[:, CHUNK_STATE * j:CHUNK_STATE * (j + 1)].astype(BF16)
        pieces.append(jnp.dot(xr, cr_sc[j], preferred_element_type=F32)
                      - jnp.dot(xi, ci_sc[j], preferred_element_type=F32))
    u = jnp.concatenate([u_ref[j] for j in range(SSM_CHUNKS)], axis=1)
    y = jnp.concatenate(pieces, axis=1) + d_ref[...] * u
    y = jax.nn.gelu(y)
    z = jnp.dot(y.astype(BF16), wg_sc[...], preferred_element_type=F32) + bg_ref[...]
    out = z[:, :D_SSM] * jax.nn.sigmoid(z[:, D_SSM:])
    for j in range(D_SSM // LANES):
        y_ref[j] = out[:, LANES * j:LANES * (j + 1)]
    sr_ref[...] = str_sc[...]
    si_ref[...] = sti_sc[...]


def _block_diag_b(b):
    bt = b.transpose(0, 2, 1).reshape(SSM_CHUNKS, CHUNK_GROUPS, SSM_GROUP, SSM_STATE)
    same = jnp.eye(CHUNK_GROUPS, dtype=bool)[None, :, None, :, None]
    t = jnp.where(same, bt[:, :, :, None, :], 0.0)
    return t.reshape(SSM_CHUNKS, CHUNK_IN, CHUNK_STATE)


def _block_diag_c(c):
    ct = c.transpose(0, 2, 1).reshape(SSM_CHUNKS, CHUNK_GROUPS, SSM_STATE, SSM_GROUP)
    same = jnp.eye(CHUNK_GROUPS, dtype=bool)[None, :, None, :, None]
    t = jnp.where(same, ct[:, :, :, None, :], 0.0)
    return t.reshape(SSM_CHUNKS, CHUNK_STATE, CHUNK_IN)


def _ssm(u_rows, h0_re, h0_im, lp, n_batch):
    rows = u_rows.shape[1]
    planes = D_SSM // LANES
    tr = min(SCAN_ROWS, rows)
    flat = lambda a: a.reshape(1, D_STATE)
    ldt = jnp.repeat(lp['log_dt'], SSM_STATE).reshape(1, D_STATE)
    const2 = lambda i: (0, 0)
    const3 = lambda i: (0, 0, 0)
    y, sr, si = pl.pallas_call(
        functools.partial(_ssm_kernel, n_batch=n_batch),
        grid=(rows // tr,),
        in_specs=[pl.BlockSpec((planes, tr, LANES), lambda i: (0, i, 0)),
                  pl.BlockSpec((n_batch, D_STATE), const2),
                  pl.BlockSpec((n_batch, D_STATE), const2),
                  pl.BlockSpec((1, D_STATE), const2),
                  pl.BlockSpec((1, D_STATE), const2),
                  pl.BlockSpec((1, D_STATE), const2),
                  pl.BlockSpec((SSM_CHUNKS, CHUNK_IN, CHUNK_STATE), const3),
                  pl.BlockSpec((SSM_CHUNKS, CHUNK_IN, CHUNK_STATE), const3),
                  pl.BlockSpec((SSM_CHUNKS, CHUNK_STATE, CHUNK_IN), const3),
                  pl.BlockSpec((SSM_CHUNKS, CHUNK_STATE, CHUNK_IN), const3),
                  pl.BlockSpec((1, D_SSM), const2),
                  pl.BlockSpec((D_SSM, 2 * D_SSM), const2),
                  pl.BlockSpec((1, 2 * D_SSM), const2)],
        out_specs=[pl.BlockSpec((planes, tr, LANES), lambda i: (0, i, 0)),
                   pl.BlockSpec((n_batch, D_STATE), const2),
                   pl.BlockSpec((n_batch, D_STATE), const2)],
        out_shape=[jax.ShapeDtypeStruct((planes, rows, LANES), F32),
                   jax.ShapeDtypeStruct((n_batch, D_STATE), F32),
                   jax.ShapeDtypeStruct((n_batch, D_STATE), F32)],
        scratch_shapes=[pltpu.VMEM((2, D_STATE), F32),
                        pltpu.VMEM((SSM_CHUNKS, CHUNK_IN, CHUNK_STATE), BF16), pltpu.VMEM((SSM_CHUNKS, CHUNK_IN, CHUNK_STATE), BF16),
                        pltpu.VMEM((SSM_CHUNKS, CHUNK_STATE, CHUNK_IN), BF16), pltpu.VMEM((SSM_CHUNKS, CHUNK_STATE, CHUNK_IN), BF16),
                        pltpu.VMEM((D_SSM, 2 * D_SSM), BF16),
                        pltpu.VMEM((n_batch, D_STATE), F32), pltpu.VMEM((n_batch, D_STATE), F32),
                        pltpu.VMEM((tr, D_STATE), F32), pltpu.VMEM((tr, D_STATE), F32)],
        compiler_params=_params(1),
        name="ssm",
    )(u_rows, h0_re.reshape(n_batch, D_STATE), h0_im.reshape(n_batch, D_STATE),
      flat(lp['lam_re']), flat(lp['lam_im']), ldt,
      _block_diag_b(lp['ssm_b_re']), _block_diag_b(lp['ssm_b_im']),
      _block_diag_c(lp['ssm_c_re']), _block_diag_c(lp['ssm_c_im']),
      lp['ssm_d'].reshape(1, D_SSM), lp['w_glu'], lp['b_glu'].reshape(1, 2 * D_SSM))
    return y, sr, si


def _softmax_pv(s, v):
    m = jnp.max(s, axis=-1, keepdims=True)
    p = jnp.exp(s - m)
    l = jnp.sum(p, axis=-1, keepdims=True)
    return jnp.dot(p.astype(BF16), v, preferred_element_type=F32) / l


def _attend(qb, k, v, out_ref, row0=0, bias_ref=None, valid=None, seq=0):
    tq = qb.shape[0]
    for h in range(N_HEADS):
        sl = slice(HEAD_DIM * h, HEAD_DIM * (h + 1))
        s = lax.dot_general(qb[:, sl], k[:, sl], _NT, preferred_element_type=F32)
        if bias_ref is not None:
            s = s + bias_ref[h]
        if valid is not None:
            s = jnp.where(valid, s, NEG_INF)
        out_ref[seq, row0:row0 + tq, sl] = _softmax_pv(s, v[:, sl])


def _attn_prompt_kernel(q_ref, kp_ref, kc_ref, vp_ref, vc_ref, qm_ref, mk_ref, mv_ref, bias_ref,
                        ya_ref, ym_ref, bias_sc):
    tq = ATT_TQ

    @pl.when((pl.program_id(0) == 0) & (pl.program_id(1) == 0))
    def _():
        q_chunk = lax.broadcasted_iota(I32, (tq, 3 * tq), 0) // CHUNK
        k_chunk = lax.broadcasted_iota(I32, (tq, 3 * tq), 1) // CHUNK
        ahead = k_chunk - q_chunk
        for h in range(N_HEADS):
            bias_sc[h] = jnp.where(ahead >= 0, jnp.where(ahead <= N_PREV_CHUNKS, bias_ref[h], NEG_INF),
                                   NEG_INF)

    k = jnp.concatenate([kp_ref[0], kc_ref[0]], axis=0)
    v = jnp.concatenate([vp_ref[0], vc_ref[0]], axis=0)
    for half in range(2):
        first_key = (2 * pl.program_id(1) - 2 + half) * tq
        kpos = first_key + lax.broadcasted_iota(I32, (1, 3 * tq), 1)
        _attend(q_ref[0, half * tq:(half + 1) * tq, :], k[half * tq:(half + 3) * tq],
                v[half * tq:(half + 3) * tq], ya_ref, half * tq, bias_sc, kpos >= 0)
    _attend(qm_ref[0], mk_ref[0].astype(BF16), mv_ref[0].astype(BF16), ym_ref)


def _attn_sample_kernel(q_ref, kn_ref, vn_ref, qm_ref, ck_ref, cv_ref, mk_ref, mv_ref, bias_ref,
                        ya_ref, ym_ref, nk_ref, nv_ref):
    n = kn_ref.shape[1]
    for i in range(q_ref.shape[0]):
        kk = jnp.concatenate([ck_ref[i], kn_ref[i]], axis=0)
        vv = jnp.concatenate([cv_ref[i], vn_ref[i]], axis=0)
        nk_ref[i] = kk[n:]
        nv_ref[i] = vv[n:]
        _attend(q_ref[i], kk.astype(BF16), vv.astype(BF16), ya_ref, 0, bias_ref, seq=i)
        _attend(qm_ref[i], mk_ref[i].astype(BF16), mv_ref[i].astype(BF16), ym_ref, seq=i)


def _rel_bias(table, n_q, n_k):
    period = n_q + n_k
    m = jnp.arange(period)
    offset = jnp.where(m < n_k, m, m - period)
    idx = jnp.clip(BAND - offset, -REL_CLIP, REL_CLIP) + REL_CLIP
    f = table.astype(F32)[:, idx]
    flat = jnp.tile(f, (1, n_q))[:, :n_q * (period - 1)]
    return flat.reshape(N_HEADS, n_q, period - 1)[:, :, :n_k]


def _attn_prompt(zb, mk, mv, table):
    b, s, _ = zb.shape
    tq = ATT_TQ
    bias = _rel_bias(table, tq, 3 * tq)
    col = lambda c: (lambda i, j: (i, j, c))
    prev = lambda c: (lambda i, j: (i, jnp.maximum(j - 1, 0), c))
    blk = (1, 2 * tq, D_ATT)
    return pl.pallas_call(
        _attn_prompt_kernel,
        grid=(b, s // (2 * tq)),
        in_specs=[pl.BlockSpec(blk, col(0)),
                  pl.BlockSpec(blk, prev(1)), pl.BlockSpec(blk, col(1)),
                  pl.BlockSpec(blk, prev(2)), pl.BlockSpec(blk, col(2)),
                  pl.BlockSpec(blk, col(3)),
                  pl.BlockSpec((1, N_MEM, D_MEM), lambda i, j: (i, 0, 0)),
                  pl.BlockSpec((1, N_MEM, D_MEM), lambda i, j: (i, 0, 0)),
                  pl.BlockSpec((N_HEADS, tq, 3 * tq), lambda i, j: (0, 0, 0))],
        out_specs=[pl.BlockSpec(blk, col(0)), pl.BlockSpec(blk, col(0))],
        out_shape=[jax.ShapeDtypeStruct((b, s, D_ATT), F32),
                   jax.ShapeDtypeStruct((b, s, D_MEM), F32)],
        scratch_shapes=[pltpu.VMEM((N_HEADS, tq, 3 * tq), F32)],
        compiler_params=_params(2),
        name="attn_prompt",
    )(zb, zb, zb, zb, zb, zb, mk, mv, bias)


def _attn_sample(kv, zb, cache_k, cache_v, mk, mv, table):
    b, n, _ = kv.shape
    w = cache_k.shape[1]
    bias = _rel_bias(table, n, w + n)
    col = lambda c: (lambda i: (i, 0, c))
    per = SAMPLE_SEQS if b % SAMPLE_SEQS == 0 else 1
    blk = (per, n, D_ATT)
    cblk = (per, w, D_ATT)
    mblk = (per, N_MEM, D_MEM)
    row = lambda i: (i, 0, 0)
    return pl.pallas_call(
        _attn_sample_kernel,
        grid=(b // per,),
        in_specs=[pl.BlockSpec(blk, col(0)), pl.BlockSpec(blk, col(0)), pl.BlockSpec(blk, col(1)),
                  pl.BlockSpec(blk, col(3)),
                  pl.BlockSpec(cblk, row), pl.BlockSpec(cblk, row),
                  pl.BlockSpec(mblk, row), pl.BlockSpec(mblk, row),
                  pl.BlockSpec((N_HEADS, n, w + n), lambda i: (0, 0, 0))],
        out_specs=[pl.BlockSpec(blk, row), pl.BlockSpec(blk, row),
                   pl.BlockSpec(cblk, row), pl.BlockSpec(cblk, row)],
        out_shape=[jax.ShapeDtypeStruct((b, n, D_ATT), F32),
                   jax.ShapeDtypeStruct((b, n, D_MEM), F32),
                   jax.ShapeDtypeStruct((b, w, D_ATT), F32),
                   jax.ShapeDtypeStruct((b, w, D_ATT), F32)],
        compiler_params=_params(1),
        name="attn_sample",
    )(zb, kv, kv, zb, cache_k, cache_v, mk, mv, bias)


def _rms(x, g):
    return x * lax.rsqrt(jnp.mean(jnp.square(x), axis=-1, keepdims=True) + LN_EPS) * g


def _layer_norm(x, g, b):
    mu = jnp.mean(x, axis=-1, keepdims=True)
    xc = x - mu
    var = jnp.mean(jnp.square(xc), axis=-1, keepdims=True)
    return xc * lax.rsqrt(var + LN_EPS) * g + b


def _split_bf16(a):
    hi = a.astype(BF16)
    lo = (a - hi.astype(F32)).astype(BF16)
    return hi, lo


def _merge_kernel(x_ref, ys_ref, ya_ref, ym_ref, gs_ref, ga_ref, gm_ref, wo_ref, l1g_ref, l1b_ref,
                  wrt_ref, brt_ref,
                  h_ref, hb_ref, pos_ref, gate_ref, cnt_ref,
                  wo_sc, *, nb):
    st = x_ref.shape[1]
    tm = nb * st
    n_batch = ys_ref.shape[1] // st

    @pl.when((pl.program_id(0) == 0) & (pl.program_id(1) == 0))
    def _():
        wo_sc[...] = wo_ref[...].astype(BF16)

    x = x_ref[...].reshape(tm, D_MODEL)
    first = pl.program_id(1) * nb
    ys = jnp.concatenate(
        [jnp.concatenate([ys_ref[c, pl.ds(first + i, st, stride=n_batch), :] for c in range(D_SSM // LANES)],
                         axis=1) for i in range(nb)], axis=0)
    ya = ya_ref[...].reshape(tm, D_ATT)
    ym = ym_ref[...].reshape(tm, D_MEM)
    a = _rms(ys, gs_ref[...]).astype(BF16)
    b = _rms(ya, ga_ref[...]).astype(BF16)
    c = _rms(ym, gm_ref[...]).astype(BF16)
    mix = (jnp.dot(a, wo_sc[0:D_SSM, :], preferred_element_type=F32)
           + jnp.dot(b, wo_sc[D_SSM:D_SSM + D_ATT, :], preferred_element_type=F32)
           + jnp.dot(c, wo_sc[D_SSM + D_ATT:, :], preferred_element_type=F32))
    h = _layer_norm(DEEPNORM_ALPHA * x + mix, l1g_ref[...], l1b_ref[...])
    h_ref[...] = h

    h_hi, h_lo = _split_bf16(h)
    hb_ref[...] = h_hi
    w_hi, w_lo = _split_bf16(wrt_ref[...])
    logits = (lax.dot_general(w_hi, h_hi, _NT, preferred_element_type=F32)
              + lax.dot_general(w_hi, h_lo, _NT, preferred_element_type=F32)
              + lax.dot_general(w_lo, h_hi, _NT, preferred_element_type=F32)
              + brt_ref[...])
    erow = lax.broadcasted_iota(I32, (N_EXPERTS, tm), 0).astype(F32)
    tops, picks = [], []
    l = logits
    for k in range(TOP_K):
        m = jnp.max(l, axis=0, keepdims=True)
        e = jnp.min(jnp.where(l == m, erow, float(N_EXPERTS)), axis=0, keepdims=True)
        pick = erow == e
        tops.append(m)
        picks.append(jnp.where(pick, 1.0, 0.0))
        l = jnp.where(pick, -jnp.inf, l)
    ex = [jnp.exp(t - tops[0]) for t in tops]
    den = ex[0] + ex[1] + ex[2] + ex[3]
    for k in range(TOP_K):
        gate_ref[k:k + 1, :] = ex[k] / den

    chosen = picks[0] + picks[1] + picks[2] + picks[3]
    chosen_b = chosen.astype(BF16)
    earlier_tok = (lax.broadcasted_iota(I32, (tm, tm), 0) < lax.broadcasted_iota(I32, (tm, tm), 1))
    within = jnp.dot(chosen_b, jnp.where(earlier_tok, 1.0, 0.0).astype(BF16),
                     preferred_element_type=F32)
    lower_exp = (lax.broadcasted_iota(I32, (N_EXPERTS, N_EXPERTS), 1)
                 < lax.broadcasted_iota(I32, (N_EXPERTS, N_EXPERTS), 0))
    below = jnp.dot(jnp.where(lower_exp, 1.0, 0.0).astype(BF16), chosen_b,
                    preferred_element_type=F32)
    slot = within + jnp.sum(below, axis=1, keepdims=True)
    for k in range(TOP_K):
        pos_ref[k:k + 1, :] = jnp.sum(picks[k] * slot, axis=0, keepdims=True).astype(I32)
    cnt_ref[0] = jnp.sum(chosen, axis=1, keepdims=True)


def _merge(x, ys_tm, ya, ym, lp, nb, st):
    b, s, _ = x.shape
    tm = nb * st
    assert tm == TOKEN_TM
    n_s = s // st
    t_all = b * s
    tile = lambda j, i: (i * n_s + j)
    c2 = lambda j, i: (0, 0)
    row3 = lambda j, i: (i, j, 0)
    vec = lambda a: a.reshape(1, -1)
    return pl.pallas_call(
        functools.partial(_merge_kernel, nb=nb),
        grid=(n_s, b // nb),
        in_specs=[pl.BlockSpec((nb, st, D_MODEL), row3),
                  pl.BlockSpec((D_SSM // LANES, st * b, LANES), lambda j, i: (0, j, 0)),
                  pl.BlockSpec((nb, st, D_ATT), row3),
                  pl.BlockSpec((nb, st, D_MEM), row3),
                  pl.BlockSpec((1, D_SSM), c2), pl.BlockSpec((1, D_ATT), c2),
                  pl.BlockSpec((1, D_MEM), c2),
                  pl.BlockSpec((D_MODEL, D_MODEL), c2),
                  pl.BlockSpec((1, D_MODEL), c2), pl.BlockSpec((1, D_MODEL), c2),
                  pl.BlockSpec((N_EXPERTS, D_MODEL), c2), pl.BlockSpec((N_EXPERTS, 1), c2)],
        out_specs=[pl.BlockSpec((tm, D_MODEL), lambda j, i: (tile(j, i), 0)),
                   pl.BlockSpec((tm, D_MODEL), lambda j, i: (tile(j, i), 0)),
                   pl.BlockSpec((TOP_K, tm), lambda j, i: (0, tile(j, i))),
                   pl.BlockSpec((TOP_K, tm), lambda j, i: (0, tile(j, i))),
                   pl.BlockSpec((1, N_EXPERTS, 1), lambda j, i: (tile(j, i), 0, 0))],
        out_shape=[jax.ShapeDtypeStruct((t_all, D_MODEL), F32),
                   jax.ShapeDtypeStruct((t_all, D_MODEL), BF16),
                   jax.ShapeDtypeStruct((TOP_K, t_all), I32),
                   jax.ShapeDtypeStruct((TOP_K, t_all), F32),
                   jax.ShapeDtypeStruct((t_all // tm, N_EXPERTS, 1), F32)],
        scratch_shapes=[pltpu.VMEM((D_MODEL, D_MODEL), BF16)],
        compiler_params=_params(2),
        name="merge_router",
    )(x, ys_tm, ya, ym, vec(lp['g_ssm']), vec(lp['g_att']), vec(lp['g_mem']), lp['w_out'],
      vec(lp['ln1_g']), vec(lp['ln1_b']), lp['w_router'].T, lp['b_router'].reshape(N_EXPERTS, 1))


def _rows(start, size):
    return pl.ds(pl.multiple_of(start * ROW_SUBLANES, ROW_SUBLANES), size * ROW_SUBLANES)


def _store_rows(ref, value, row0=0):
    n = value.shape[0]
    for j in range(ROW_SUBLANES):
        ref[pl.ds(row0 * ROW_SUBLANES + j, n, stride=ROW_SUBLANES), :] = value[:, LANES * j:LANES * (j + 1)]


def _load_rows(ref, dtype=F32):
    n = ref.shape[0] // ROW_SUBLANES
    return jnp.concatenate([ref[pl.ds(j, n, stride=ROW_SUBLANES), :].astype(dtype)
                            for j in range(ROW_SUBLANES)], axis=1)


def _for_each_run_piece(n, max_rows, fn):
    for bit in reversed(range(max_rows.bit_length())):
        size = 1 << bit
        start = (n >> (bit + 1)) << (bit + 1)

        @pl.when((n & size) != 0)
        def _(start=start, size=size):
            fn(start, size)


def _dispatch_kernel(n_ref, off_ref, dst_ref, padlo_ref, padn_ref, used_ref,
                     pos_ref, h1_ref, h2_ref, xs_hbm, sorted_sc, zero_sc, sem, zsem, *, n_first):
    i = pl.program_id(0)
    tm = h1_ref.shape[0]
    n_slots = TOP_K * tm
    n_blocks = xs_hbm.shape[0] // (EXPERT_TM * ROW_SUBLANES)

    @pl.when(i == 0)
    def _():
        zero_sc[...] = jnp.zeros_like(zero_sc)

        def pad_copy(e, start, size):
            return pltpu.make_async_copy(zero_sc.at[_rows(0, size)],
                                         xs_hbm.at[_rows(padlo_ref[e] + start, size)], zsem)

        def tail_copy(blk):
            return pltpu.make_async_copy(zero_sc, xs_hbm.at[_rows(blk * EXPERT_TM, EXPERT_TM)], zsem)

        for e in range(N_EXPERTS):
            _for_each_run_piece(padn_ref[e], EXPERT_TM - 1,
                                lambda start, size, e=e: pad_copy(e, start, size).start())

        def tail_start(blk, carry):
            tail_copy(blk).start()
            return carry

        lax.fori_loop(used_ref[0], n_blocks, tail_start, 0)
        for e in range(N_EXPERTS):
            _for_each_run_piece(padn_ref[e], EXPERT_TM - 1,
                                lambda start, size, e=e: pad_copy(e, start, size).wait())

        def tail_wait(blk, carry):
            tail_copy(blk).wait()
            return carry

        lax.fori_loop(used_ref[0], n_blocks, tail_wait, 0)

    n_tiles = pl.num_programs(0) - 1
    slot = lax.rem(i, N_SORT_BUFS)
    prev_slot = lax.rem(i + N_SORT_BUFS - 1, N_SORT_BUFS)
    buf = sorted_sc.at[slot]
    prev = sorted_sc.at[prev_slot]

    def wait_tile(sl):
        pltpu.make_async_copy(sorted_sc.at[sl], xs_hbm.at[_rows(0, n_slots)], sem.at[sl]).wait()

    @pl.when(i >= N_SORT_BUFS)
    def _():
        wait_tile(slot)

    pos = pos_ref[...]
    hb = jnp.where(i < n_first, h1_ref[...], h2_ref[...])
    base = jnp.maximum(i - 1, 0) * N_EXPERTS
    rows_c = n_slots // SORT_CHUNKS
    experts_c = N_EXPERTS // SORT_CHUNKS
    for c in range(SORT_CHUNKS):
        for e in range(c * experts_c, (c + 1) * experts_c):
            off = off_ref[base + e]
            dst = dst_ref[base + e]

            def run_start(start, size, off=off, dst=dst):
                pltpu.make_async_copy(prev.at[_rows(off + start, size)],
                                      xs_hbm.at[_rows(dst + start, size)], sem.at[prev_slot]).start()

            _for_each_run_piece(jnp.where(i >= 1, n_ref[base + e], 0), tm, run_start)

        srow = lax.broadcasted_iota(I32, (rows_c, tm), 0) + c * rows_c
        perm = jnp.where(srow == pos[0:1], 1.0,
                         jnp.where(srow == pos[1:2], 1.0,
                                   jnp.where(srow == pos[2:3], 1.0,
                                             jnp.where(srow == pos[3:4], 1.0, 0.0)))).astype(BF16)
        _store_rows(buf, jnp.dot(perm, hb, preferred_element_type=F32), c * rows_c)

    @pl.when(i == n_tiles)
    def _():
        wait_tile(prev_slot)

        @pl.when(i >= 2)
        def _():
            wait_tile(lax.rem(i + N_SORT_BUFS - 2, N_SORT_BUFS))


def _dispatch(run_n, run_off, run_dst, pad_lo, pad_n, n_used, pos, h1, h2, cap):
    tm = TOKEN_TM
    n_first = h1.shape[0] // tm
    n_tiles = n_first + h2.shape[0] // tm
    grid_spec = pltpu.PrefetchScalarGridSpec(
        num_scalar_prefetch=6,
        grid=(n_tiles + 1,),
        in_specs=[pl.BlockSpec((TOP_K, tm), lambda i, *_: (0, jnp.minimum(i, n_tiles - 1))),
                  pl.BlockSpec((tm, D_MODEL), lambda i, *_: (jnp.minimum(i, n_first - 1), 0)),
                  pl.BlockSpec((tm, D_MODEL), lambda i, *_: (jnp.clip(i - n_first, 0, n_tiles - n_first - 1), 0))],
        out_specs=pl.BlockSpec(memory_space=pl.ANY),
        scratch_shapes=[pltpu.VMEM((N_SORT_BUFS, TOP_K * tm * ROW_SUBLANES, LANES), F32),
                        pltpu.VMEM((EXPERT_TM * ROW_SUBLANES, LANES), F32),
                        pltpu.SemaphoreType.DMA((N_SORT_BUFS,)), pltpu.SemaphoreType.DMA],
    )
    return pl.pallas_call(
        functools.partial(_dispatch_kernel, n_first=n_first),
        grid_spec=grid_spec,
        out_shape=jax.ShapeDtypeStruct((cap * ROW_SUBLANES, LANES), F32),
        compiler_params=_params(1),
        name="moe_dispatch",
    )(run_n, run_off, run_dst, pad_lo, pad_n, n_used, pos, h1, h2)


def _expert_kernel(be_ref, first_ref, ord_ref, seq_ref, used_ref,
                   x_ref, bgu_ref, bd_ref, wgu_hbm, wd_hbm, o_ref,
                   wgu_st, wd_st, wgu_sc, wd_sc, sem):
    i = pl.program_id(0)

    def weight_copies(e):
        return (pltpu.make_async_copy(wgu_hbm.at[e], wgu_st, sem.at[0]),
                pltpu.make_async_copy(wd_hbm.at[e], wd_st, sem.at[1]))

    @pl.when(i == 0)
    def _():
        for c in weight_copies(seq_ref[0]):
            c.start()

    @pl.when(i < used_ref[0])
    def _():
        @pl.when(first_ref[i] == 1)
        def _():
            k = ord_ref[i]
            for c in weight_copies(seq_ref[k]):
                c.wait()
            wgu_sc[...] = wgu_st[...].astype(BF16)
            wd_sc[...] = wd_st[...].astype(BF16)

            @pl.when(k + 1 < used_ref[1])
            def _():
                for c in weight_copies(seq_ref[k + 1]):
                    c.start()

        gu = jnp.dot(_load_rows(x_ref, BF16), wgu_sc[...], preferred_element_type=F32) + bgu_ref[0]
        gate = jnp.minimum(gu[:, :D_FF], SWIGLU_LIMIT)
        lin = jnp.clip(gu[:, D_FF:], -SWIGLU_LIMIT, SWIGLU_LIMIT)
        act = gate * jax.nn.sigmoid(SWIGLU_ALPHA * gate) * (lin + 1.0)
        _store_rows(o_ref, jnp.dot(act.astype(BF16), wd_sc[...], preferred_element_type=F32) + bd_ref[0])

    @pl.when(i >= used_ref[0])
    def _():
        o_ref[...] = jnp.zeros_like(o_ref)


def _experts(block_expert, block_first, block_ord, expert_seq, n_used, xs, lp):
    tm = EXPERT_TM * ROW_SUBLANES
    grid_spec = pltpu.PrefetchScalarGridSpec(
        num_scalar_prefetch=5,
        grid=(xs.shape[0] // tm,),
        in_specs=[pl.BlockSpec((tm, LANES), lambda i, be, *_: (i, 0)),
                  pl.BlockSpec((1, 1, 2 * D_FF), lambda i, be, *_: (be[i], 0, 0)),
                  pl.BlockSpec((1, 1, D_MODEL), lambda i, be, *_: (be[i], 0, 0)),
                  pl.BlockSpec(memory_space=pl.ANY),
                  pl.BlockSpec(memory_space=pl.ANY)],
        out_specs=pl.BlockSpec((tm, LANES), lambda i, be, *_: (i, 0)),
        scratch_shapes=[pltpu.VMEM((D_MODEL, 2 * D_FF), F32), pltpu.VMEM((D_FF, D_MODEL), F32),
                        pltpu.VMEM((D_MODEL, 2 * D_FF), BF16), pltpu.VMEM((D_FF, D_MODEL), BF16),
                        pltpu.SemaphoreType.DMA((2,))],
    )
    return pl.pallas_call(
        _expert_kernel,
        grid_spec=grid_spec,
        out_shape=jax.ShapeDtypeStruct(xs.shape, F32),
        compiler_params=_params(1),
        name="moe_experts",
    )(block_expert, block_first, block_ord, expert_seq, n_used, xs,
      lp['b_gu'].reshape(N_EXPERTS, 1, 2 * D_FF), lp['b_down'].reshape(N_EXPERTS, 1, D_MODEL),
      lp['w_gu'], lp['w_down'])


def _combine_kernel(n_ref, off_ref, dst_ref, pos_ref, gate_ref, h1_ref, h2_ref, ys_hbm, g_ref, b_ref,
                    y1_ref, y2_ref, sorted_sc, w_sc, sem, *, n_first):
    i = pl.program_id(0)
    n_tiles = pl.num_programs(0) - 1
    tm = h1_ref.shape[0]
    n_slots = TOP_K * tm
    slot = lax.rem(i, 2)
    buf = sorted_sc.at[slot]
    pos = pos_ref[...]
    gates = gate_ref[...]
    rows_per = tm // N_EXPERTS
    base = jnp.minimum(i, n_tiles - 1) * N_EXPERTS
    for e in range(N_EXPERTS):
        off = off_ref[base + e]
        dst = dst_ref[base + e]

        def run_start(start, size, off=off, dst=dst):
            pltpu.make_async_copy(ys_hbm.at[_rows(dst + start, size)],
                                  buf.at[_rows(off + start, size)], sem.at[slot]).start()

        _for_each_run_piece(jnp.where(i < n_tiles, n_ref[base + e], 0), tm, run_start)

        r = slice(e * rows_per, (e + 1) * rows_per)
        scol = lax.broadcasted_iota(I32, (rows_per, n_slots), 1)
        w_sc[r, :] = jnp.where(
            scol == pos[r, 0:1], gates[r, 0:1],
            jnp.where(scol == pos[r, 1:2], gates[r, 1:2],
                      jnp.where(scol == pos[r, 2:3], gates[r, 2:3],
                                jnp.where(scol == pos[r, 3:4], gates[r, 3:4], 0.0)))).astype(BF16)

    @pl.when(i >= 1)
    def _():
        done = sorted_sc.at[1 - slot]
        pltpu.make_async_copy(ys_hbm.at[_rows(0, n_slots)], done, sem.at[1 - slot]).wait()

        f = jnp.dot(w_sc[...], _load_rows(done, BF16), preferred_element_type=F32)

        @pl.when(i - 1 < n_first)
        def _():
            y1_ref[...] = _layer_norm(DEEPNORM_ALPHA * h1_ref[...] + f, g_ref[...], b_ref[...])

        @pl.when(i - 1 >= n_first)
        def _():
            y2_ref[...] = _layer_norm(DEEPNORM_ALPHA * h2_ref[...] + f, g_ref[...], b_ref[...])


def _combine(run_n, run_off, run_dst, pos_t, gates_t, h1, h2, ys, lp):
    tm = TOKEN_TM
    t_first = h1.shape[0]
    t = t_first + h2.shape[0]
    n_first = t_first // tm
    n_rest = (t - t_first) // tm
    c2 = lambda i, *_: (0, 0)
    done = lambda i, *_: (jnp.maximum(i - 1, 0), 0)
    done1 = lambda i, *_: (jnp.clip(i - 1, 0, n_first - 1), 0)
    done2 = lambda i, *_: (jnp.clip(i - 1 - n_first, 0, n_rest - 1), 0)
    grid_spec = pltpu.PrefetchScalarGridSpec(
        num_scalar_prefetch=3,
        grid=(t // tm + 1,),
        in_specs=[pl.BlockSpec((tm, TOP_K), done),
                  pl.BlockSpec((tm, TOP_K), done),
                  pl.BlockSpec((tm, D_MODEL), done1), pl.BlockSpec((tm, D_MODEL), done2),
                  pl.BlockSpec(memory_space=pl.ANY),
                  pl.BlockSpec((1, D_MODEL), c2), pl.BlockSpec((1, D_MODEL), c2)],
        out_specs=[pl.BlockSpec((tm, D_MODEL), done1), pl.BlockSpec((tm, D_MODEL), done2)],
        scratch_shapes=[pltpu.VMEM((2, TOP_K * tm * ROW_SUBLANES, LANES), F32),
                        pltpu.VMEM((tm, TOP_K * tm), BF16),
                        pltpu.SemaphoreType.DMA((2,))],
    )
    return pl.pallas_call(
        functools.partial(_combine_kernel, n_first=n_first),
        grid_spec=grid_spec,
        out_shape=[jax.ShapeDtypeStruct((t_first, D_MODEL), F32),
                   jax.ShapeDtypeStruct((t - t_first, D_MODEL), F32)],
        compiler_params=_params(1),
        name="moe_combine",
    )(run_n, run_off, run_dst, pos_t, gates_t, h1, h2, ys,
      lp['ln2_g'].reshape(1, D_MODEL), lp['ln2_b'].reshape(1, D_MODEL))


def _moe_and_norm(h1, h2, hb1, hb2, pos, gates, tile_counts, lp):
    t = h1.shape[0] + h2.shape[0]
    te = EXPERT_TM
    n_tiles = t // TOKEN_TM
    n_blocks = (t * TOP_K) // te + N_EXPERTS
    cap = n_blocks * te
    cnt = tile_counts.reshape(n_tiles, N_EXPERTS).astype(I32)
    counts = jnp.sum(cnt, axis=0)
    padded = (counts + te - 1) // te * te
    pad_ends = jnp.cumsum(padded)
    pad_starts = pad_ends - padded
    run_dst = pad_starts[None, :] + jnp.cumsum(cnt, axis=0) - cnt
    run_off = jnp.cumsum(cnt, axis=1) - cnt
    blk_start = jnp.arange(n_blocks, dtype=I32) * te
    expert_of = lambda slot_idx: jnp.minimum(jnp.sum(slot_idx[..., None] >= pad_ends, axis=-1), N_EXPERTS - 1)
    total = pad_ends[-1]
    be = jnp.where(blk_start < total, expert_of(blk_start), expert_of(jnp.maximum(total - 1, 0))).astype(I32)
    is_e = be[:, None] == jnp.arange(N_EXPERTS, dtype=I32)[None, :]
    pick = lambda table: jnp.sum(jnp.where(is_e, table[None, :], 0), axis=1)
    in_use = counts > 0
    ordinal = jnp.cumsum(in_use.astype(I32)) - 1
    rank = jnp.arange(N_EXPERTS, dtype=I32)
    expert_seq = jnp.sum(jnp.where(in_use[None, :] & (ordinal[None, :] == rank[:, None]), rank[None, :], 0),
                         axis=1)
    block_ord = pick(ordinal)
    block_first = (blk_start == pick(pad_starts)) & (blk_start < total)
    used = jnp.stack([total // te, jnp.sum(in_use.astype(I32))]).astype(I32)
    flat = lambda a: a.reshape(-1).astype(I32)
    xs = _dispatch(flat(cnt), flat(run_off), flat(run_dst), flat(pad_starts + counts),
                   flat(padded - counts), used, pos, hb1, hb2, cap)
    ys = _experts(be, flat(block_first), flat(block_ord), expert_seq, used, xs, lp)
    return _combine(flat(cnt), flat(run_off), flat(run_dst), pos.T, gates.T, h1, h2, ys, lp)


def kernel(x_prompt, x_sample, cache_attn_k, cache_attn_v, cache_mem_k, cache_mem_v, state_ssm_re, state_ssm_im, mem_prompt, w_in, lam_re, lam_im, log_dt, ssm_b_re, ssm_b_im, ssm_c_re, ssm_c_im, ssm_d, w_glu, b_glu, rel_bias, w_mem_kv, g_ssm, g_att, g_mem, w_out, ln1_g, ln1_b, w_router, b_router, w_gu, b_gu, w_down, b_down, ln2_g, ln2_b):
    assert w_in.shape[0] == 1, "single-layer step"
    lp = dict(w_in=w_in[0], lam_re=lam_re[0], lam_im=lam_im[0], log_dt=log_dt[0],
              ssm_b_re=ssm_b_re[0], ssm_b_im=ssm_b_im[0], ssm_c_re=ssm_c_re[0], ssm_c_im=ssm_c_im[0],
              ssm_d=ssm_d[0], w_glu=w_glu[0], b_glu=b_glu[0], rel_bias=rel_bias[0],
              w_mem_kv=w_mem_kv[0], g_ssm=g_ssm[0], g_att=g_att[0], g_mem=g_mem[0], w_out=w_out[0],
              ln1_g=ln1_g[0], ln1_b=ln1_b[0], w_router=w_router[0], b_router=b_router[0],
              w_gu=w_gu[0], b_gu=b_gu[0], w_down=w_down[0], b_down=b_down[0],
              ln2_g=ln2_g[0], ln2_b=ln2_b[0])

    bp, sp, _ = x_prompt.shape
    bs, ss, _ = x_sample.shape
    heads = lambda a: a.reshape(a.shape[0], a.shape[1], N_HEADS, HEAD_DIM)
    state = lambda a: a.reshape(a.shape[0], N_GROUPS, SSM_STATE)

    w = min(BAND, sp)
    u_p, kv_p, zb = _in_proj(x_prompt, lp['w_in'], min(IN_PROJ_TS, sp), w)
    mk, mv = _mem_kv(mem_prompt, lp['w_mem_kv'])
    ya, ym = _attn_prompt(zb, mk, mv, lp['rel_bias'])
    zeros = jnp.zeros((bp, D_STATE), F32)
    ys_p, sr_p, si_p = _ssm(u_p, zeros, zeros, lp, bp)
    h_p, hb_p, pos_p, gates_p, cnt_p = _merge(x_prompt, ys_p, ya, ym, lp, 1, TOKEN_TM)
    k_p = heads(kv_p[:, :, :D_ATT])
    v_p = heads(kv_p[:, :, D_ATT:])

    wc = cache_attn_k.shape[2]
    u_s, kv_s, zb_s = _in_proj(x_sample, lp['w_in'], ss, ss)
    ya_s, ym_s, nk, nv = _attn_sample(
        kv_s, zb_s, cache_attn_k[0].reshape(bs, wc, D_ATT), cache_attn_v[0].reshape(bs, wc, D_ATT),
        cache_mem_k[0].reshape(bs, N_MEM, D_MEM), cache_mem_v[0].reshape(bs, N_MEM, D_MEM),
        lp['rel_bias'])
    ys_s, sr_s, si_s = _ssm(u_s, state_ssm_re[0], state_ssm_im[0], lp, bs)
    h_s, hb_s, pos_s, gates_s, cnt_s = _merge(x_sample, ys_s, ya_s, ym_s, lp, TOKEN_TM // ss, ss)

    y_p, y_s = _moe_and_norm(h_p, h_s, hb_p, hb_s, jnp.concatenate([pos_p, pos_s], axis=1),
                             jnp.concatenate([gates_p, gates_s], axis=1),
                             jnp.concatenate([cnt_p, cnt_s], axis=0), lp)

    return (y_p.reshape(bp, sp, D_MODEL), y_s.reshape(bs, ss, D_MODEL),
            k_p[None], v_p[None], heads(mk)[None], heads(mv)[None], state(sr_p)[None], state(si_p)[None],
            heads(nk)[None], heads(nv)[None], state(sr_s)[None], state(si_s)[None])
```

```python
import functools

import jax
import jax.numpy as jnp
from jax import lax
from jax.experimental import pallas as pl
from jax.experimental.pallas import tpu as pltpu

F32 = jnp.float32
BF16 = jnp.bfloat16
I32 = jnp.int32

D_MODEL = 1024
D_SSM = 512
D_ATT = 256
D_MEM = 256
D_IN = D_SSM + 3 * D_ATT + D_MEM
D_REST = D_IN - D_SSM
HEAD_DIM = 64
N_HEADS = 4
N_GROUPS = 32
SSM_GROUP = 16
SSM_STATE = 64
D_STATE = N_GROUPS * SSM_STATE
CHUNK_GROUPS = 8
SSM_CHUNKS = N_GROUPS // CHUNK_GROUPS
CHUNK_IN = CHUNK_GROUPS * SSM_GROUP
CHUNK_STATE = CHUNK_GROUPS * SSM_STATE
CHUNK = 64
N_PREV_CHUNKS = 8
BAND = N_PREV_CHUNKS * CHUNK
REL_CLIP = 128
N_MEM = 256
N_EXPERTS = 32
TOP_K = 4
D_FF = D_MODEL
SWIGLU_LIMIT = 7.0
SWIGLU_ALPHA = 1.702
LN_EPS = 1e-5
NEG_INF = -1e30
ATT_SCALE = HEAD_DIM ** -0.5
DEEPNORM_ALPHA = 2.0 ** 0.25

V7X_VMEM_LIMIT = 56 * 1024 * 1024
IN_PROJ_TS = 128
ATT_TQ = 4 * CHUNK
SAMPLE_SEQS = 4
SCAN_LANES = 1024
SCAN_ROWS = 1024
LANES = 128
ROW_SUBLANES = D_MODEL // LANES
EXPERT_TM = 512
TOKEN_TM = 512
N_SORT_BUFS = 3
SORT_CHUNKS = 16

_NT = (((1,), (1,)), ((), ()))


def _params(n_axes, vmem=V7X_VMEM_LIMIT):
    return pltpu.CompilerParams(dimension_semantics=("arbitrary",) * n_axes,
                                vmem_limit_bytes=vmem)


def _in_proj_kernel(x_ref, w_ref, u_ref, kv_ref, zb_ref, wb_ref):
    @pl.when(pl.program_id(0) == 0)
    def _():
        wb_ref[...] = w_ref[...].astype(BF16)

    nb, ts, _ = x_ref.shape
    x = x_ref[...].reshape(nb * ts, D_MODEL).astype(BF16)
    z = jnp.dot(x, wb_ref[...], preferred_element_type=F32)
    for b in range(nb):
        for c in range(D_SSM // LANES):
            u_ref[c, pl.ds(b, ts, stride=nb), :] = z[b * ts:(b + 1) * ts, LANES * c:LANES * (c + 1)]
    zr = z[:, D_SSM:]
    kv_ref[...] = zr[:, D_ATT:3 * D_ATT].reshape(nb, ts, 2 * D_ATT)
    zb = jnp.concatenate([zr[:, :D_ATT] * ATT_SCALE, zr[:, D_ATT:3 * D_ATT], zr[:, 3 * D_ATT:] * ATT_SCALE],
                         axis=1)
    zb_ref[...] = zb.astype(BF16).reshape(nb, ts, D_REST)


def _in_proj(x, w_in, ts, tail):
    b, s, _ = x.shape
    skipped = (s - tail) // ts
    return pl.pallas_call(
        _in_proj_kernel,
        grid=(s // ts,),
        in_specs=[pl.BlockSpec((b, ts, D_MODEL), lambda j: (0, j, 0)),
                  pl.BlockSpec((D_MODEL, D_IN), lambda j: (0, 0))],
        out_specs=[pl.BlockSpec((D_SSM // LANES, ts * b, LANES), lambda j: (0, j, 0)),
                   pl.BlockSpec((b, ts, 2 * D_ATT), lambda j: (0, jnp.maximum(j - skipped, 0), 0)),
                   pl.BlockSpec((b, ts, D_REST), lambda j: (0, j, 0))],
        out_shape=[jax.ShapeDtypeStruct((D_SSM // LANES, s * b, LANES), F32),
                   jax.ShapeDtypeStruct((b, tail, 2 * D_ATT), F32),
                   jax.ShapeDtypeStruct((b, s, D_REST), BF16)],
        scratch_shapes=[pltpu.VMEM((D_MODEL, D_IN), BF16)],
        compiler_params=_params(1),
        name="in_proj",
    )(x, w_in)


def _mem_kv_kernel(m_ref, w_ref, mk_ref, mv_ref):
    kv = jnp.dot(m_ref[0].astype(BF16), w_ref[...].astype(BF16), preferred_element_type=F32)
    mk_ref[0] = kv[:, :D_MEM]
    mv_ref[0] = kv[:, D_MEM:]


def _mem_kv(mem, w_mem_kv):
    b = mem.shape[0]
    return pl.pallas_call(
        _mem_kv_kernel,
        grid=(b,),
        in_specs=[pl.BlockSpec((1, N_MEM, D_MODEL), lambda i: (i, 0, 0)),
                  pl.BlockSpec((D_MODEL, 2 * D_MEM), lambda i: (0, 0))],
        out_specs=[pl.BlockSpec((1, N_MEM, D_MEM), lambda i: (i, 0, 0)),
                   pl.BlockSpec((1, N_MEM, D_MEM), lambda i: (i, 0, 0))],
        out_shape=[jax.ShapeDtypeStruct((b, N_MEM, D_MEM), F32)] * 2,
        compiler_params=_params(1),
        name="mem_kv",
    )(mem, w_mem_kv)


def _ssm_kernel(u_ref, h0r_ref, h0i_ref, lr_ref, li_ref, ldt_ref, bre_ref, bim_ref,
                cre_ref, cim_ref, d_ref, wg_ref, bg_ref,
                y_ref, sr_ref, si_ref,
                a_sc, bbr_sc, bbi_sc, cr_sc, ci_sc, wg_sc, str_sc, sti_sc, xr_sc, xi_sc,
                *, n_batch):
    n_rows = u_ref.shape[1]
    n_steps = n_rows // n_batch

    @pl.when(pl.program_id(0) == 0)
    def _():
        lr = lr_ref[...]
        li = li_ref[...]
        dt = jnp.exp(ldt_ref[...])
        mag = jnp.exp(lr * dt)
        ar = mag * jnp.cos(li * dt)
        ai = mag * jnp.sin(li * dt)
        den = lr * lr + li * li
        fr = ((ar - 1.0) * lr + ai * li) / den
        fi = (ai * lr - (ar - 1.0) * li) / den
        a_sc[0:1, :] = ar
        a_sc[1:2, :] = ai
        for j in range(SSM_CHUNKS):
            frj = fr[:, CHUNK_STATE * j:CHUNK_STATE * (j + 1)]
            fij = fi[:, CHUNK_STATE * j:CHUNK_STATE * (j + 1)]
            bbr_sc[j] = (frj * bre_ref[j] - fij * bim_ref[j]).astype(BF16)
            bbi_sc[j] = (frj * bim_ref[j] + fij * bre_ref[j]).astype(BF16)
            cr_sc[j] = cre_ref[j].astype(BF16)
            ci_sc[j] = cim_ref[j].astype(BF16)
        wg_sc[...] = wg_ref[...].astype(BF16)
        str_sc[...] = h0r_ref[...]
        sti_sc[...] = h0i_ref[...]

    for j in range(SSM_CHUNKS):
        uc = u_ref[j].astype(BF16)
        xr_sc[:, CHUNK_STATE * j:CHUNK_STATE * (j + 1)] = jnp.dot(uc, bbr_sc[j], preferred_element_type=F32)
        xi_sc[:, CHUNK_STATE * j:CHUNK_STATE * (j + 1)] = jnp.dot(uc, bbi_sc[j], preferred_element_type=F32)

    for c in range(D_STATE // SCAN_LANES):
        lo = c * SCAN_LANES
        ar = jnp.broadcast_to(a_sc[0:1, lo:lo + SCAN_LANES], (n_batch, SCAN_LANES))
        ai = jnp.broadcast_to(a_sc[1:2, lo:lo + SCAN_LANES], (n_batch, SCAN_LANES))

        def step(t, carry, lo=lo, ar=ar, ai=ai):
            sr, si = carry
            r0 = pl.multiple_of(t * n_batch, n_batch)
            nr = ar * sr - ai * si + xr_sc[pl.ds(r0, n_batch), lo:lo + SCAN_LANES]
            ni = ar * si + ai * sr + xi_sc[pl.ds(r0, n_batch), lo:lo + SCAN_LANES]
            xr_sc[pl.ds(r0, n_batch), lo:lo + SCAN_LANES] = nr
            xi_sc[pl.ds(r0, n_batch), lo:lo + SCAN_LANES] = ni
            return nr, ni

        sr, si = lax.fori_loop(0, n_steps, step,
                               (str_sc[:, lo:lo + SCAN_LANES], sti_sc[:, lo:lo + SCAN_LANES]),
                               unroll=True)
        str_sc[:, lo:lo + SCAN_LANES] = sr
        sti_sc[:, lo:lo + SCAN_LANES] = si

    pieces = []
    for j in range(SSM_CHUNKS):
        xr = xr_sc[:, CHUNK_STATE * j:CHUNK_STATE * (j + 1)].astype(BF16)
        xi = xi_sc[:, CHUNK_STATE * j:CHUNK_STATE * (j + 1)].astype(BF16)
        pieces.append(jnp.dot(xr, cr_sc[j], preferred_element_type=F32)
                      - jnp.dot(xi, ci_sc[j], preferred_element_type=F32))
    u = jnp.concatenate([u_ref[j] for j in range(SSM_CHUNKS)], axis=1)
    y = jnp.concatenate(pieces, axis=1) + d_ref[...] * u
    y = jax.nn.gelu(y)
    z = jnp.dot(y.astype(BF16), wg_sc[...], preferred_element_type=F32) + bg_ref[...]
    out = z[:, :D_SSM] * jax.nn.sigmoid(z[:, D_SSM:])
    for j in range(D_SSM // LANES):
        y_ref[j] = out[:, LANES * j:LANES * (j + 1)]
    sr_ref[...] = str_sc[...]
    si_ref[...] = sti_sc[...]


def _block_diag_b(b):
    bt = b.transpose(0, 2, 1).reshape(SSM_CHUNKS, CHUNK_GROUPS, SSM_GROUP, SSM_STATE)
    same = jnp.eye(CHUNK_GROUPS, dtype=bool)[None, :, None, :, None]
    t = jnp.where(same, bt[:, :, :, None, :], 0.0)
    return t.reshape(SSM_CHUNKS, CHUNK_IN, CHUNK_STATE)


def _block_diag_c(c):
    ct = c.transpose(0, 2, 1).reshape(SSM_CHUNKS, CHUNK_GROUPS, SSM_STATE, SSM_GROUP)
    same = jnp.eye(CHUNK_GROUPS, dtype=bool)[None, :, None, :, None]
    t = jnp.where(same, ct[:, :, :, None, :], 0.0)
    return t.reshape(SSM_CHUNKS, CHUNK_STATE, CHUNK_IN)


def _ssm(u_rows, h0_re, h0_im, lp, n_batch):
    rows = u_rows.shape[1]
    planes = D_SSM // LANES
    tr = min(SCAN_ROWS, rows)
    flat = lambda a: a.reshape(1, D_STATE)
    ldt = jnp.repeat(lp['log_dt'], SSM_STATE).reshape(1, D_STATE)
    const2 = lambda i: (0, 0)
    const3 = lambda i: (0, 0, 0)
    y, sr, si = pl.pallas_call(
        functools.partial(_ssm_kernel, n_batch=n_batch),
        grid=(rows // tr,),
        in_specs=[pl.BlockSpec((planes, tr, LANES), lambda i: (0, i, 0)),
                  pl.BlockSpec((n_batch, D_STATE), const2),
                  pl.BlockSpec((n_batch, D_STATE), const2),
                  pl.BlockSpec((1, D_STATE), const2),
                  pl.BlockSpec((1, D_STATE), const2),
                  pl.BlockSpec((1, D_STATE), const2),
                  pl.BlockSpec((SSM_CHUNKS, CHUNK_IN, CHUNK_STATE), const3),
                  pl.BlockSpec((SSM_CHUNKS, CHUNK_IN, CHUNK_STATE), const3),
                  pl.BlockSpec((SSM_CHUNKS, CHUNK_STATE, CHUNK_IN), const3),
                  pl.BlockSpec((SSM_CHUNKS, CHUNK_STATE, CHUNK_IN), const3),
                  pl.BlockSpec((1, D_SSM), const2),
                  pl.BlockSpec((D_SSM, 2 * D_SSM), const2),
                  pl.BlockSpec((1, 2 * D_SSM), const2)],
        out_specs=[pl.BlockSpec((planes, tr, LANES), lambda i: (0, i, 0)),
                   pl.BlockSpec((n_batch, D_STATE), const2),
                   pl.BlockSpec((n_batch, D_STATE), const2)],
        out_shape=[jax.ShapeDtypeStruct((planes, rows, LANES), F32),
                   jax.ShapeDtypeStruct((n_batch, D_STATE), F32),
                   jax.ShapeDtypeStruct((n_batch, D_STATE), F32)],
        scratch_shapes=[pltpu.VMEM((2, D_STATE), F32),
                        pltpu.VMEM((SSM_CHUNKS, CHUNK_IN, CHUNK_STATE), BF16), pltpu.VMEM((SSM_CHUNKS, CHUNK_IN, CHUNK_STATE), BF16),
                        pltpu.VMEM((SSM_CHUNKS, CHUNK_STATE, CHUNK_IN), BF16), pltpu.VMEM((SSM_CHUNKS, CHUNK_STATE, CHUNK_IN), BF16),
                        pltpu.VMEM((D_SSM, 2 * D_SSM), BF16),
                        pltpu.VMEM((n_batch, D_STATE), F32), pltpu.VMEM((n_batch, D_STATE), F32),
                        pltpu.VMEM((tr, D_STATE), F32), pltpu.VMEM((tr, D_STATE), F32)],
        compiler_params=_params(1),
        name="ssm",
    )(u_rows, h0_re.reshape(n_batch, D_STATE), h0_im.reshape(n_batch, D_STATE),
      flat(lp['lam_re']), flat(lp['lam_im']), ldt,
      _block_diag_b(lp['ssm_b_re']), _block_diag_b(lp['ssm_b_im']),
      _block_diag_c(lp['ssm_c_re']), _block_diag_c(lp['ssm_c_im']),
      lp['ssm_d'].reshape(1, D_SSM), lp['w_glu'], lp['b_glu'].reshape(1, 2 * D_SSM))
    return y, sr, si


def _softmax_pv(s, v):
    m = jnp.max(s, axis=-1, keepdims=True)
    p = jnp.exp(s - m)
    l = jnp.sum(p, axis=-1, keepdims=True)
    return jnp.dot(p.astype(BF16), v, preferred_element_type=F32) / l


def _attend(qb, k, v, out_ref, row0=0, bias_ref=None, valid=None, seq=0):
    tq = qb.shape[0]
    for h in range(N_HEADS):
        sl = slice(HEAD_DIM * h, HEAD_DIM * (h + 1))
        s = lax.dot_general(qb[:, sl], k[:, sl], _NT, preferred_element_type=F32)
        if bias_ref is not None:
            s = s + bias_ref[h]
        if valid is not None:
            s = jnp.where(valid, s, NEG_INF)
        out_ref[seq, row0:row0 + tq, sl] = _softmax_pv(s, v[:, sl])


def _attn_prompt_kernel(q_ref, kp_ref, kc_ref, vp_ref, vc_ref, qm_ref, mk_ref, mv_ref, bias_ref,
                        ya_ref, ym_ref, bias_sc):
    tq = ATT_TQ

    @pl.when((pl.program_id(0) == 0) & (pl.program_id(1) == 0))
    def _():
        q_chunk = lax.broadcasted_iota(I32, (tq, 3 * tq), 0) // CHUNK
        k_chunk = lax.broadcasted_iota(I32, (tq, 3 * tq), 1) // CHUNK
        ahead = k_chunk - q_chunk
        for h in range(N_HEADS):
            bias_sc[h] = jnp.where(ahead >= 0, jnp.where(ahead <= N_PREV_CHUNKS, bias_ref[h], NEG_INF),
                                   NEG_INF)

    k = jnp.concatenate([kp_ref[0], kc_ref[0]], axis=0)
    v = jnp.concatenate([vp_ref[0], vc_ref[0]], axis=0)
    for half in range(2):
        first_key = (2 * pl.program_id(1) - 2 + half) * tq
        kpos = first_key + lax.broadcasted_iota(I32, (1, 3 * tq), 1)
        _attend(q_ref[0, half * tq:(half + 1) * tq, :], k[half * tq:(half + 3) * tq],
                v[half * tq:(half + 3) * tq], ya_ref, half * tq, bias_sc, kpos >= 0)
    _attend(qm_ref[0], mk_ref[0].astype(BF16), mv_ref[0].astype(BF16), ym_ref)


def _attn_sample_kernel(q_ref, kn_ref, vn_ref, qm_ref, ck_ref, cv_ref, mk_ref, mv_ref, bias_ref,
                        ya_ref, ym_ref, nk_ref, nv_ref):
    n = kn_ref.shape[1]
    for i in range(q_ref.shape[0]):
        kk = jnp.concatenate([ck_ref[i], kn_ref[i]], axis=0)
        vv = jnp.concatenate([cv_ref[i], vn_ref[i]], axis=0)
        nk_ref[i] = kk[n:]
        nv_ref[i] = vv[n:]
        _attend(q_ref[i], kk.astype(BF16), vv.astype(BF16), ya_ref, 0, bias_ref, seq=i)
        _attend(qm_ref[i], mk_ref[i].astype(BF16), mv_ref[i].astype(BF16), ym_ref, seq=i)


def _rel_bias(table, n_q, n_k):
    period = n_q + n_k
    m = jnp.arange(period)
    offset = jnp.where(m < n_k, m, m - period)
    idx = jnp.clip(BAND - offset, -REL_CLIP, REL_CLIP) + REL_CLIP
    f = table.astype(F32)[:, idx]
    flat = jnp.tile(f, (1, n_q))[:, :n_q * (period - 1)]
    return flat.reshape(N_HEADS, n_q, period - 1)[:, :, :n_k]


def _attn_prompt(zb, mk, mv, table):
    b, s, _ = zb.shape
    tq = ATT_TQ
    bias = _rel_bias(table, tq, 3 * tq)
    col = lambda c: (lambda i, j: (i, j, c))
    prev = lambda c: (lambda i, j: (i, jnp.maximum(j - 1, 0), c))
    blk = (1, 2 * tq, D_ATT)
    return pl.pallas_call(
        _attn_prompt_kernel,
        grid=(b, s // (2 * tq)),
        in_specs=[pl.BlockSpec(blk, col(0)),
                  pl.BlockSpec(blk, prev(1)), pl.BlockSpec(blk, col(1)),
                  pl.BlockSpec(blk, prev(2)), pl.BlockSpec(blk, col(2)),
                  pl.BlockSpec(blk, col(3)),
                  pl.BlockSpec((1, N_MEM, D_MEM), lambda i, j: (i, 0, 0)),
                  pl.BlockSpec((1, N_MEM, D_MEM), lambda i, j: (i, 0, 0)),
                  pl.BlockSpec((N_HEADS, tq, 3 * tq), lambda i, j: (0, 0, 0))],
        out_specs=[pl.BlockSpec(blk, col(0)), pl.BlockSpec(blk, col(0))],
        out_shape=[jax.ShapeDtypeStruct((b, s, D_ATT), F32),
                   jax.ShapeDtypeStruct((b, s, D_MEM), F32)],
        scratch_shapes=[pltpu.VMEM((N_HEADS, tq, 3 * tq), F32)],
        compiler_params=_params(2),
        name="attn_prompt",
    )(zb, zb, zb, zb, zb, zb, mk, mv, bias)


def _attn_sample(kv, zb, cache_k, cache_v, mk, mv, table):
    b, n, _ = kv.shape
    w = cache_k.shape[1]
    bias = _rel_bias(table, n, w + n)
    col = lambda c: (lambda i: (i, 0, c))
    per = SAMPLE_SEQS if b % SAMPLE_SEQS == 0 else 1
    blk = (per, n, D_ATT)
    cblk = (per, w, D_ATT)
    mblk = (per, N_MEM, D_MEM)
    row = lambda i: (i, 0, 0)
    return pl.pallas_call(
        _attn_sample_kernel,
        grid=(b // per,),
        in_specs=[pl.BlockSpec(blk, col(0)), pl.BlockSpec(blk, col(0)), pl.BlockSpec(blk, col(1)),
                  pl.BlockSpec(blk, col(3)),
                  pl.BlockSpec(cblk, row), pl.BlockSpec(cblk, row),
                  pl.BlockSpec(mblk, row), pl.BlockSpec(mblk, row),
                  pl.BlockSpec((N_HEADS, n, w + n), lambda i: (0, 0, 0))],
        out_specs=[pl.BlockSpec(blk, row), pl.BlockSpec(blk, row),
                   pl.BlockSpec(cblk, row), pl.BlockSpec(cblk, row)],
        out_shape=[jax.ShapeDtypeStruct((b, n, D_ATT), F32),
                   jax.ShapeDtypeStruct((b, n, D_MEM), F32),
                   jax.ShapeDtypeStruct((b, w, D_ATT), F32),
                   jax.ShapeDtypeStruct((b, w, D_ATT), F32)],
        compiler_params=_params(1),
        name="attn_sample",
    )(zb, kv, kv, zb, cache_k, cache_v, mk, mv, bias)


def _rms(x, g):
    return x * lax.rsqrt(jnp.mean(jnp.square(x), axis=-1, keepdims=True) + LN_EPS) * g


def _layer_norm(x, g, b):
    mu = jnp.mean(x, axis=-1, keepdims=True)
    xc = x - mu
    var = jnp.mean(jnp.square(xc), axis=-1, keepdims=True)
    return xc * lax.rsqrt(var + LN_EPS) * g + b


def _split_bf16(a):
    hi = a.astype(BF16)
    lo = (a - hi.astype(F32)).astype(BF16)
    return hi, lo


def _merge_kernel(x_ref, ys_ref, ya_ref, ym_ref, gs_ref, ga_ref, gm_ref, wo_ref, l1g_ref, l1b_ref,
                  wrt_ref, brt_ref,
                  h_ref, hb_ref, pos_ref, gate_ref, cnt_ref,
                  wo_sc, *, nb):
    st = x_ref.shape[1]
    tm = nb * st
    n_batch = ys_ref.shape[1] // st

    @pl.when((pl.program_id(0) == 0) & (pl.program_id(1) == 0))
    def _():
        wo_sc[...] = wo_ref[...].astype(BF16)

    x = x_ref[...].reshape(tm, D_MODEL)
    first = pl.program_id(1) * nb
    ys = jnp.concatenate(
        [jnp.concatenate([ys_ref[c, pl.ds(first + i, st, stride=n_batch), :] for c in range(D_SSM // LANES)],
                         axis=1) for i in range(nb)], axis=0)
    ya = ya_ref[...].reshape(tm, D_ATT)
    ym = ym_ref[...].reshape(tm, D_MEM)
    a = _rms(ys, gs_ref[...]).astype(BF16)
    b = _rms(ya, ga_ref[...]).astype(BF16)
    c = _rms(ym, gm_ref[...]).astype(BF16)
    mix = (jnp.dot(a, wo_sc[0:D_SSM, :], preferred_element_type=F32)
           + jnp.dot(b, wo_sc[D_SSM:D_SSM + D_ATT, :], preferred_element_type=F32)
           + jnp.dot(c, wo_sc[D_SSM + D_ATT:, :], preferred_element_type=F32))
    h = _layer_norm(DEEPNORM_ALPHA * x + mix, l1g_ref[...], l1b_ref[...])
    h_ref[...] = h

    h_hi, h_lo = _split_bf16(h)
    hb_ref[...] = h_hi
    w_hi, w_lo = _split_bf16(wrt_ref[...])
    logits = (lax.dot_general(w_hi, h_hi, _NT, preferred_element_type=F32)
              + lax.dot_general(w_hi, h_lo, _NT, preferred_element_type=F32)
              + lax.dot_general(w_lo, h_hi, _NT, preferred_element_type=F32)
              + brt_ref[...])
    erow = lax.broadcasted_iota(I32, (N_EXPERTS, tm), 0).astype(F32)
    tops, picks = [], []
    l = logits
    for k in range(TOP_K):
        m = jnp.max(l, axis=0, keepdims=True)
        e = jnp.min(jnp.where(l == m, erow, float(N_EXPERTS)), axis=0, keepdims=True)
        pick = erow == e
        tops.append(m)
        picks.append(jnp.where(pick, 1.0, 0.0))
        l = jnp.where(pick, -jnp.inf, l)
    ex = [jnp.exp(t - tops[0]) for t in tops]
    den = ex[0] + ex[1] + ex[2] + ex[3]
    for k in range(TOP_K):
        gate_ref[k:k + 1, :] = ex[k] / den

    chosen = picks[0] + picks[1] + picks[2] + picks[3]
    chosen_b = chosen.astype(BF16)
    earlier_tok = (lax.broadcasted_iota(I32, (tm, tm), 0) < lax.broadcasted_iota(I32, (tm, tm), 1))
    within = jnp.dot(chosen_b, jnp.where(earlier_tok, 1.0, 0.0).astype(BF16),
                     preferred_element_type=F32)
    lower_exp = (lax.broadcasted_iota(I32, (N_EXPERTS, N_EXPERTS), 1)
                 < lax.broadcasted_iota(I32, (N_EXPERTS, N_EXPERTS), 0))
    below = jnp.dot(jnp.where(lower_exp, 1.0, 0.0).astype(BF16), chosen_b,
                    preferred_element_type=F32)
    slot = within + jnp.sum(below, axis=1, keepdims=True)
    for k in range(TOP_K):
        pos_ref[k:k + 1, :] = jnp.sum(picks[k] * slot, axis=0, keepdims=True).astype(I32)
    cnt_ref[0] = jnp.sum(chosen, axis=1, keepdims=True)


def _merge(x, ys_tm, ya, ym, lp, nb, st):
    b, s, _ = x.shape
    tm = nb * st
    assert tm == TOKEN_TM
    n_s = s // st
    t_all = b * s
    tile = lambda j, i: (i * n_s + j)
    c2 = lambda j, i: (0, 0)
    row3 = lambda j, i: (i, j, 0)
    vec = lambda a: a.reshape(1, -1)
    return pl.pallas_call(
        functools.partial(_merge_kernel, nb=nb),
        grid=(n_s, b // nb),
        in_specs=[pl.BlockSpec((nb, st, D_MODEL), row3),
                  pl.BlockSpec((D_SSM // LANES, st * b, LANES), lambda j, i: (0, j, 0)),
                  pl.BlockSpec((nb, st, D_ATT), row3),
                  pl.BlockSpec((nb, st, D_MEM), row3),
                  pl.BlockSpec((1, D_SSM), c2), pl.BlockSpec((1, D_ATT), c2),
                  pl.BlockSpec((1, D_MEM), c2),
                  pl.BlockSpec((D_MODEL, D_MODEL), c2),
                  pl.BlockSpec((1, D_MODEL), c2), pl.BlockSpec((1, D_MODEL), c2),
                  pl.BlockSpec((N_EXPERTS, D_MODEL), c2), pl.BlockSpec((N_EXPERTS, 1), c2)],
        out_specs=[pl.BlockSpec((tm, D_MODEL), lambda j, i: (tile(j, i), 0)),
                   pl.BlockSpec((tm, D_MODEL), lambda j, i: (tile(j, i), 0)),
                   pl.BlockSpec((TOP_K, tm), lambda j, i: (0, tile(j, i))),
                   pl.BlockSpec((TOP_K, tm), lambda j, i: (0, tile(j, i))),
                   pl.BlockSpec((1, N_EXPERTS, 1), lambda j, i: (tile(j, i), 0, 0))],
        out_shape=[jax.ShapeDtypeStruct((t_all, D_MODEL), F32),
                   jax.ShapeDtypeStruct((t_all, D_MODEL), BF16),
                   jax.ShapeDtypeStruct((TOP_K, t_all), I32),
                   jax.ShapeDtypeStruct((TOP_K, t_all), F32),
                   jax.ShapeDtypeStruct((t_all // tm, N_EXPERTS, 1), F32)],
        scratch_shapes=[pltpu.VMEM((D_MODEL, D_MODEL), BF16)],
        compiler_params=_params(2),
        name="merge_router",
    )(x, ys_tm, ya, ym, vec(lp['g_ssm']), vec(lp['g_att']), vec(lp['g_mem']), lp['w_out'],
      vec(lp['ln1_g']), vec(lp['ln1_b']), lp['w_router'].T, lp['b_router'].reshape(N_EXPERTS, 1))


def _rows(start, size):
    return pl.ds(pl.multiple_of(start * ROW_SUBLANES, ROW_SUBLANES), size * ROW_SUBLANES)


def _store_rows(ref, value, row0=0):
    n = value.shape[0]
    for j in range(ROW_SUBLANES):
        ref[pl.ds(row0 * ROW_SUBLANES + j, n, stride=ROW_SUBLANES), :] = value[:, LANES * j:LANES * (j + 1)]


def _load_rows(ref, dtype=F32):
    n = ref.shape[0] // ROW_SUBLANES
    return jnp.concatenate([ref[pl.ds(j, n, stride=ROW_SUBLANES), :].astype(dtype)
                            for j in range(ROW_SUBLANES)], axis=1)


def _for_each_run_piece(n, max_rows, fn):
    for bit in reversed(range(max_rows.bit_length())):
        size = 1 << bit
        start = (n >> (bit + 1)) << (bit + 1)

        @pl.when((n & size) != 0)
        def _(start=start, size=size):
            fn(start, size)


def _dispatch_kernel(n_ref, off_ref, dst_ref, padlo_ref, padn_ref, used_ref,
                     pos_ref, h1_ref, h2_ref, xs_hbm, sorted_sc, zero_sc, sem, zsem, *, n_first):
    i = pl.program_id(0)
    tm = h1_ref.shape[0]
    n_slots = TOP_K * tm
    n_blocks = xs_hbm.shape[0] // (EXPERT_TM * ROW_SUBLANES)

    def pad_copy(e, start, size):
        return pltpu.make_async_copy(zero_sc.at[_rows(0, size)],
                                     xs_hbm.at[_rows(padlo_ref[e] + start, size)], zsem)

    def tail_copy(blk):
        return pltpu.make_async_copy(zero_sc, xs_hbm.at[_rows(blk * EXPERT_TM, EXPERT_TM)], zsem)

    @pl.when(i == 0)
    def _():
        zero_sc[...] = jnp.zeros_like(zero_sc)
        for e in range(N_EXPERTS):
            _for_each_run_piece(padn_ref[e], EXPERT_TM - 1,
                                lambda start, size, e=e: pad_copy(e, start, size).start())

        def tail_start(blk, carry):
            tail_copy(blk).start()
            return carry

        lax.fori_loop(used_ref[0], n_blocks, tail_start, 0)

    @pl.when(i == pl.num_programs(0) - 1)
    def _():
        for e in range(N_EXPERTS):
            _for_each_run_piece(padn_ref[e], EXPERT_TM - 1,
                                lambda start, size, e=e: pad_copy(e, start, size).wait())

        def tail_wait(blk, carry):
            tail_copy(blk).wait()
            return carry

        lax.fori_loop(used_ref[0], n_blocks, tail_wait, 0)

    n_tiles = pl.num_programs(0) - 1
    slot = lax.rem(i, N_SORT_BUFS)
    prev_slot = lax.rem(i + N_SORT_BUFS - 1, N_SORT_BUFS)
    buf = sorted_sc.at[slot]
    prev = sorted_sc.at[prev_slot]

    def wait_tile(sl):
        pltpu.make_async_copy(sorted_sc.at[sl], xs_hbm.at[_rows(0, n_slots)], sem.at[sl]).wait()

    @pl.when(i >= N_SORT_BUFS)
    def _():
        wait_tile(slot)

    pos = pos_ref[...]
    hb = jnp.where(i < n_first, h1_ref[...], h2_ref[...])
    base = jnp.maximum(i - 1, 0) * N_EXPERTS
    rows_c = n_slots // SORT_CHUNKS
    experts_c = N_EXPERTS // SORT_CHUNKS
    for c in range(SORT_CHUNKS):
        for e in range(c * experts_c, (c + 1) * experts_c):
            off = off_ref[base + e]
            dst = dst_ref[base + e]

            def run_start(start, size, off=off, dst=dst):
                pltpu.make_async_copy(prev.at[_rows(off + start, size)],
                                      xs_hbm.at[_rows(dst + start, size)], sem.at[prev_slot]).start()

            _for_each_run_piece(jnp.where(i >= 1, n_ref[base + e], 0), tm, run_start)

        srow = lax.broadcasted_iota(I32, (rows_c, tm), 0) + c * rows_c
        perm = jnp.where(srow == pos[0:1], 1.0,
                         jnp.where(srow == pos[1:2], 1.0,
                                   jnp.where(srow == pos[2:3], 1.0,
                                             jnp.where(srow == pos[3:4], 1.0, 0.0)))).astype(BF16)
        _store_rows(buf, jnp.dot(perm, hb, preferred_element_type=F32), c * rows_c)

    @pl.when(i == n_tiles)
    def _():
        wait_tile(prev_slot)

        @pl.when(i >= 2)
        def _():
            wait_tile(lax.rem(i + N_SORT_BUFS - 2, N_SORT_BUFS))


def _dispatch(run_n, run_off, run_dst, pad_lo, pad_n, n_used, pos, h1, h2, cap):
    tm = TOKEN_TM
    n_first = h1.shape[0] // tm
    n_tiles = n_first + h2.shape[0] // tm
    grid_spec = pltpu.PrefetchScalarGridSpec(
        num_scalar_prefetch=6,
        grid=(n_tiles + 1,),
        in_specs=[pl.BlockSpec((TOP_K, tm), lambda i, *_: (0, jnp.minimum(i, n_tiles - 1))),
                  pl.BlockSpec((tm, D_MODEL), lambda i, *_: (jnp.minimum(i, n_first - 1), 0)),
                  pl.BlockSpec((tm, D_MODEL), lambda i, *_: (jnp.clip(i - n_first, 0, n_tiles - n_first - 1), 0))],
        out_specs=pl.BlockSpec(memory_space=pl.ANY),
        scratch_shapes=[pltpu.VMEM((N_SORT_BUFS, TOP_K * tm * ROW_SUBLANES, LANES), F32),
                        pltpu.VMEM((EXPERT_TM * ROW_SUBLANES, LANES), F32),
                        pltpu.SemaphoreType.DMA((N_SORT_BUFS,)), pltpu.SemaphoreType.DMA],
    )
    return pl.pallas_call(
        functools.partial(_dispatch_kernel, n_first=n_first),
        grid_spec=grid_spec,
        out_shape=jax.ShapeDtypeStruct((cap * ROW_SUBLANES, LANES), F32),
        compiler_params=_params(1),
        name="moe_dispatch",
    )(run_n, run_off, run_dst, pad_lo, pad_n, n_used, pos, h1, h2)


def _expert_kernel(be_ref, first_ref, ord_ref, seq_ref, used_ref,
                   x_ref, bgu_ref, bd_ref, wgu_hbm, wd_hbm, o_ref,
                   wgu_st, wd_st, wgu_sc, wd_sc, sem):
    i = pl.program_id(0)

    def weight_copies(e):
        return (pltpu.make_async_copy(wgu_hbm.at[e], wgu_st, sem.at[0]),
                pltpu.make_async_copy(wd_hbm.at[e], wd_st, sem.at[1]))

    @pl.when(i == 0)
    def _():
        for c in weight_copies(seq_ref[0]):
            c.start()

    @pl.when(i < used_ref[0])
    def _():
        @pl.when(first_ref[i] == 1)
        def _():
            k = ord_ref[i]
            for c in weight_copies(seq_ref[k]):
                c.wait()
            wgu_sc[...] = wgu_st[...].astype(BF16)
            wd_sc[...] = wd_st[...].astype(BF16)

            @pl.when(k + 1 < used_ref[1])
            def _():
                for c in weight_copies(seq_ref[k + 1]):
                    c.start()

        gu = jnp.dot(_load_rows(x_ref, BF16), wgu_sc[...], preferred_element_type=F32) + bgu_ref[0]
        gate = jnp.minimum(gu[:, :D_FF], SWIGLU_LIMIT)
        lin = jnp.clip(gu[:, D_FF:], -SWIGLU_LIMIT, SWIGLU_LIMIT)
        act = gate * jax.nn.sigmoid(SWIGLU_ALPHA * gate) * (lin + 1.0)
        _store_rows(o_ref, jnp.dot(act.astype(BF16), wd_sc[...], preferred_element_type=F32) + bd_ref[0])

    @pl.when(i >= used_ref[0])
    def _():
        o_ref[...] = jnp.zeros_like(o_ref)


def _experts(block_expert, block_first, block_ord, expert_seq, n_used, xs, lp):
    tm = EXPERT_TM * ROW_SUBLANES
    grid_spec = pltpu.PrefetchScalarGridSpec(
        num_scalar_prefetch=5,
        grid=(xs.shape[0] // tm,),
        in_specs=[pl.BlockSpec((tm, LANES), lambda i, be, *_: (i, 0)),
                  pl.BlockSpec((1, 1, 2 * D_FF), lambda i, be, *_: (be[i], 0, 0)),
                  pl.BlockSpec((1, 1, D_MODEL), lambda i, be, *_: (be[i], 0, 0)),
                  pl.BlockSpec(memory_space=pl.ANY),
                  pl.BlockSpec(memory_space=pl.ANY)],
        out_specs=pl.BlockSpec((tm, LANES), lambda i, be, *_: (i, 0)),
        scratch_shapes=[pltpu.VMEM((D_MODEL, 2 * D_FF), F32), pltpu.VMEM((D_FF, D_MODEL), F32),
                        pltpu.VMEM((D_MODEL, 2 * D_FF), BF16), pltpu.VMEM((D_FF, D_MODEL), BF16),
                        pltpu.SemaphoreType.DMA((2,))],
    )
    return pl.pallas_call(
        _expert_kernel,
        grid_spec=grid_spec,
        out_shape=jax.ShapeDtypeStruct(xs.shape, F32),
        compiler_params=_params(1),
        name="moe_experts",
    )(block_expert, block_first, block_ord, expert_seq, n_used, xs,
      lp['b_gu'].reshape(N_EXPERTS, 1, 2 * D_FF), lp['b_down'].reshape(N_EXPERTS, 1, D_MODEL),
      lp['w_gu'], lp['w_down'])


def _combine_kernel(n_ref, off_ref, dst_ref, pos_ref, gate_ref, h1_ref, h2_ref, ys_hbm, g_ref, b_ref,
                    y1_ref, y2_ref, sorted_sc, w_sc, sem, *, n_first):
    i = pl.program_id(0)
    n_tiles = pl.num_programs(0) - 1
    tm = h1_ref.shape[0]
    n_slots = TOP_K * tm
    slot = lax.rem(i, 2)
    buf = sorted_sc.at[slot]
    pos = pos_ref[...]
    gates = gate_ref[...]
    rows_per = tm // N_EXPERTS
    base = jnp.minimum(i, n_tiles - 1) * N_EXPERTS
    for e in range(N_EXPERTS):
        off = off_ref[base + e]
        dst = dst_ref[base + e]

        def run_start(start, size, off=off, dst=dst):
            pltpu.make_async_copy(ys_hbm.at[_rows(dst + start, size)],
                                  buf.at[_rows(off + start, size)], sem.at[slot]).start()

        _for_each_run_piece(jnp.where(i < n_tiles, n_ref[base + e], 0), tm, run_start)

        r = slice(e * rows_per, (e + 1) * rows_per)
        scol = lax.broadcasted_iota(I32, (rows_per, n_slots), 1)
        w_sc[r, :] = jnp.where(
            scol == pos[r, 0:1], gates[r, 0:1],
            jnp.where(scol == pos[r, 1:2], gates[r, 1:2],
                      jnp.where(scol == pos[r, 2:3], gates[r, 2:3],
                                jnp.where(scol == pos[r, 3:4], gates[r, 3:4], 0.0)))).astype(BF16)

    @pl.when(i >= 1)
    def _():
        done = sorted_sc.at[1 - slot]
        pltpu.make_async_copy(ys_hbm.at[_rows(0, n_slots)], done, sem.at[1 - slot]).wait()

        f = jnp.dot(w_sc[...], _load_rows(done, BF16), preferred_element_type=F32)

        @pl.when(i - 1 < n_first)
        def _():
            y1_ref[...] = _layer_norm(DEEPNORM_ALPHA * h1_ref[...] + f, g_ref[...], b_ref[...])

        @pl.when(i - 1 >= n_first)
        def _():
            y2_ref[...] = _layer_norm(DEEPNORM_ALPHA * h2_ref[...] + f, g_ref[...], b_ref[...])


def _combine(run_n, run_off, run_dst, pos_t, gates_t, h1, h2, ys, lp):
    tm = TOKEN_TM
    t_first = h1.shape[0]
    t = t_first + h2.shape[0]
    n_first = t_first // tm
    n_rest = (t - t_first) // tm
    c2 = lambda i, *_: (0, 0)
    done = lambda i, *_: (jnp.maximum(i - 1, 0), 0)
    done1 = lambda i, *_: (jnp.clip(i - 1, 0, n_first - 1), 0)
    done2 = lambda i, *_: (jnp.clip(i - 1 - n_first, 0, n_rest - 1), 0)
    grid_spec = pltpu.PrefetchScalarGridSpec(
        num_scalar_prefetch=3,
        grid=(t // tm + 1,),
        in_specs=[pl.BlockSpec((tm, TOP_K), done),
                  pl.BlockSpec((tm, TOP_K), done),
                  pl.BlockSpec((tm, D_MODEL), done1), pl.BlockSpec((tm, D_MODEL), done2),
                  pl.BlockSpec(memory_space=pl.ANY),
                  pl.BlockSpec((1, D_MODEL), c2), pl.BlockSpec((1, D_MODEL), c2)],
        out_specs=[pl.BlockSpec((tm, D_MODEL), done1), pl.BlockSpec((tm, D_MODEL), done2)],
        scratch_shapes=[pltpu.VMEM((2, TOP_K * tm * ROW_SUBLANES, LANES), F32),
                        pltpu.VMEM((tm, TOP_K * tm), BF16),
                        pltpu.SemaphoreType.DMA((2,))],
    )
    return pl.pallas_call(
        functools.partial(_combine_kernel, n_first=n_first),
        grid_spec=grid_spec,
        out_shape=[jax.ShapeDtypeStruct((t_first, D_MODEL), F32),
                   jax.ShapeDtypeStruct((t - t_first, D_MODEL), F32)],
        compiler_params=_params(1),
        name="moe_combine",
    )(run_n, run_off, run_dst, pos_t, gates_t, h1, h2, ys,
      lp['ln2_g'].reshape(1, D_MODEL), lp['ln2_b'].reshape(1, D_MODEL))


def _moe_and_norm(h1, h2, hb1, hb2, pos, gates, tile_counts, lp):
    t = h1.shape[0] + h2.shape[0]
    te = EXPERT_TM
    n_tiles = t // TOKEN_TM
    n_blocks = (t * TOP_K) // te + N_EXPERTS
    cap = n_blocks * te
    cnt = tile_counts.reshape(n_tiles, N_EXPERTS).astype(I32)
    counts = jnp.sum(cnt, axis=0)
    padded = (counts + te - 1) // te * te
    pad_ends = jnp.cumsum(padded)
    pad_starts = pad_ends - padded
    run_dst = pad_starts[None, :] + jnp.cumsum(cnt, axis=0) - cnt
    run_off = jnp.cumsum(cnt, axis=1) - cnt
    blk_start = jnp.arange(n_blocks, dtype=I32) * te
    expert_of = lambda slot_idx: jnp.minimum(jnp.sum(slot_idx[..., None] >= pad_ends, axis=-1), N_EXPERTS - 1)
    total = pad_ends[-1]
    be = jnp.where(blk_start < total, expert_of(blk_start), expert_of(jnp.maximum(total - 1, 0))).astype(I32)
    is_e = be[:, None] == jnp.arange(N_EXPERTS, dtype=I32)[None, :]
    pick = lambda table: jnp.sum(jnp.where(is_e, table[None, :], 0), axis=1)
    in_use = counts > 0
    ordinal = jnp.cumsum(in_use.astype(I32)) - 1
    rank = jnp.arange(N_EXPERTS, dtype=I32)
    expert_seq = jnp.sum(jnp.where(in_use[None, :] & (ordinal[None, :] == rank[:, None]), rank[None, :], 0),
                         axis=1)
    block_ord = pick(ordinal)
    block_first = (blk_start == pick(pad_starts)) & (blk_start < total)
    used = jnp.stack([total // te, jnp.sum(in_use.astype(I32))]).astype(I32)
    flat = lambda a: a.reshape(-1).astype(I32)
    xs = _dispatch(flat(cnt), flat(run_off), flat(run_dst), flat(pad_starts + counts),
                   flat(padded - counts), used, pos, hb1, hb2, cap)
    ys = _experts(be, flat(block_first), flat(block_ord), expert_seq, used, xs, lp)
    return _combine(flat(cnt), flat(run_off), flat(run_dst), pos.T, gates.T, h1, h2, ys, lp)


def kernel(x_prompt, x_sample, cache_attn_k, cache_attn_v, cache_mem_k, cache_mem_v, state_ssm_re, state_ssm_im, mem_prompt, w_in, lam_re, lam_im, log_dt, ssm_b_re, ssm_b_im, ssm_c_re, ssm_c_im, ssm_d, w_glu, b_glu, rel_bias, w_mem_kv, g_ssm, g_att, g_mem, w_out, ln1_g, ln1_b, w_router, b_router, w_gu, b_gu, w_down, b_down, ln2_g, ln2_b):
    assert w_in.shape[0] == 1, "single-layer step"
    lp = dict(w_in=w_in[0], lam_re=lam_re[0], lam_im=lam_im[0], log_dt=log_dt[0],
              ssm_b_re=ssm_b_re[0], ssm_b_im=ssm_b_im[0], ssm_c_re=ssm_c_re[0], ssm_c_im=ssm_c_im[0],
              ssm_d=ssm_d[0], w_glu=w_glu[0], b_glu=b_glu[0], rel_bias=rel_bias[0],
              w_mem_kv=w_mem_kv[0], g_ssm=g_ssm[0], g_att=g_att[0], g_mem=g_mem[0], w_out=w_out[0],
              ln1_g=ln1_g[0], ln1_b=ln1_b[0], w_router=w_router[0], b_router=b_router[0],
              w_gu=w_gu[0], b_gu=b_gu[0], w_down=w_down[0], b_down=b_down[0],
              ln2_g=ln2_g[0], ln2_b=ln2_b[0])

    bp, sp, _ = x_prompt.shape
    bs, ss, _ = x_sample.shape
    heads = lambda a: a.reshape(a.shape[0], a.shape[1], N_HEADS, HEAD_DIM)
    state = lambda a: a.reshape(a.shape[0], N_GROUPS, SSM_STATE)

    w = min(BAND, sp)
    u_p, kv_p, zb = _in_proj(x_prompt, lp['w_in'], min(IN_PROJ_TS, sp), w)
    mk, mv = _mem_kv(mem_prompt, lp['w_mem_kv'])
    ya, ym = _attn_prompt(zb, mk, mv, lp['rel_bias'])
    zeros = jnp.zeros((bp, D_STATE), F32)
    ys_p, sr_p, si_p = _ssm(u_p, zeros, zeros, lp, bp)
    h_p, hb_p, pos_p, gates_p, cnt_p = _merge(x_prompt, ys_p, ya, ym, lp, 1, TOKEN_TM)
    k_p = heads(kv_p[:, :, :D_ATT])
    v_p = heads(kv_p[:, :, D_ATT:])

    wc = cache_attn_k.shape[2]
    u_s, kv_s, zb_s = _in_proj(x_sample, lp['w_in'], ss, ss)
    ya_s, ym_s, nk, nv = _attn_sample(
        kv_s, zb_s, cache_attn_k[0].reshape(bs, wc, D_ATT), cache_attn_v[0].reshape(bs, wc, D_ATT),
        cache_mem_k[0].reshape(bs, N_MEM, D_MEM), cache_mem_v[0].reshape(bs, N_MEM, D_MEM),
        lp['rel_bias'])
    ys_s, sr_s, si_s = _ssm(u_s, state_ssm_re[0], state_ssm_im[0], lp, bs)
    h_s, hb_s, pos_s, gates_s, cnt_s = _merge(x_sample, ys_s, ya_s, ym_s, lp, TOKEN_TM // ss, ss)

    y_p, y_s = _moe_and_norm(h_p, h_s, hb_p, hb_s, jnp.concatenate([pos_p, pos_s], axis=1),
                             jnp.concatenate([gates_p, gates_s], axis=1),
                             jnp.concatenate([cnt_p, cnt_s], axis=0), lp)

    return (y_p.reshape(bp, sp, D_MODEL), y_s.reshape(bs, ss, D_MODEL),
            k_p[None], v_p[None], heads(mk)[None], heads(mv)[None], state(sr_p)[None], state(si_p)[None],
            heads(nk)[None], heads(nv)[None], state(sr_s)[None], state(si_s)[None])
```

```python
import functools

import jax
import jax.numpy as jnp
from jax import lax
from jax.experimental import pallas as pl
from jax.experimental.pallas import tpu as pltpu

F32 = jnp.float32
BF16 = jnp.bfloat16
I32 = jnp.int32

D_MODEL = 1024
D_SSM = 512
D_ATT = 256
D_MEM = 256
D_IN = D_SSM + 3 * D_ATT + D_MEM
D_REST = D_IN - D_SSM
HEAD_DIM = 64
N_HEADS = 4
N_GROUPS = 32
SSM_GROUP = 16
SSM_STATE = 64
D_STATE = N_GROUPS * SSM_STATE
CHUNK_GROUPS = 8
SSM_CHUNKS = N_GROUPS // CHUNK_GROUPS
CHUNK_IN = CHUNK_GROUPS * SSM_GROUP
CHUNK_STATE = CHUNK_GROUPS * SSM_STATE
CHUNK = 64
N_PREV_CHUNKS = 8
BAND = N_PREV_CHUNKS * CHUNK
REL_CLIP = 128
N_MEM = 256
N_EXPERTS = 32
TOP_K = 4
D_FF = D_MODEL
SWIGLU_LIMIT = 7.0
SWIGLU_ALPHA = 1.702
LN_EPS = 1e-5
NEG_INF = -1e30
ATT_SCALE = HEAD_DIM ** -0.5
DEEPNORM_ALPHA = 2.0 ** 0.25

V7X_VMEM_LIMIT = 56 * 1024 * 1024
IN_PROJ_TS = 128
ATT_TQ = 4 * CHUNK
SAMPLE_SEQS = 4
SCAN_LANES = 1024
SCAN_ROWS = 1024
LANES = 128
ROW_SUBLANES = D_MODEL // LANES
EXPERT_TM = 512
TOKEN_TM = 512
N_SORT_BUFS = 3
SORT_CHUNKS = 16

_NT = (((1,), (1,)), ((), ()))


def _params(n_axes, vmem=V7X_VMEM_LIMIT):
    return pltpu.CompilerParams(dimension_semantics=("arbitrary",) * n_axes,
                                vmem_limit_bytes=vmem)


def _in_proj_kernel(x_ref, w_ref, u_ref, kv_ref, zb_ref, wb_ref):
    @pl.when(pl.program_id(0) == 0)
    def _():
        wb_ref[...] = w_ref[...].astype(BF16)

    nb, ts, _ = x_ref.shape
    x = x_ref[...].reshape(nb * ts, D_MODEL).astype(BF16)
    z = jnp.dot(x, wb_ref[...], preferred_element_type=F32)
    for b in range(nb):
        for c in range(D_SSM // LANES):
            u_ref[c, pl.ds(b, ts, stride=nb), :] = z[b * ts:(b + 1) * ts, LANES * c:LANES * (c + 1)]
    zr = z[:, D_SSM:]
    kv_ref[...] = zr[:, D_ATT:3 * D_ATT].reshape(nb, ts, 2 * D_ATT)
    zb = jnp.concatenate([zr[:, :D_ATT] * ATT_SCALE, zr[:, D_ATT:3 * D_ATT], zr[:, 3 * D_ATT:] * ATT_SCALE],
                         axis=1)
    zb_ref[...] = zb.astype(BF16).reshape(nb, ts, D_REST)


def _in_proj(x, w_in, ts, tail):
    b, s, _ = x.shape
    skipped = (s - tail) // ts
    return pl.pallas_call(
        _in_proj_kernel,
        grid=(s // ts,),
        in_specs=[pl.BlockSpec((b, ts, D_MODEL), lambda j: (0, j, 0)),
                  pl.BlockSpec((D_MODEL, D_IN), lambda j: (0, 0))],
        out_specs=[pl.BlockSpec((D_SSM // LANES, ts * b, LANES), lambda j: (0, j, 0)),
                   pl.BlockSpec((b, ts, 2 * D_ATT), lambda j: (0, jnp.maximum(j - skipped, 0), 0)),
                   pl.BlockSpec((b, ts, D_REST), lambda j: (0, j, 0))],
        out_shape=[jax.ShapeDtypeStruct((D_SSM // LANES, s * b, LANES), F32),
                   jax.ShapeDtypeStruct((b, tail, 2 * D_ATT), F32),
                   jax.ShapeDtypeStruct((b, s, D_REST), BF16)],
        scratch_shapes=[pltpu.VMEM((D_MODEL, D_IN), BF16)],
        compiler_params=_params(1),
        name="in_proj",
    )(x, w_in)


def _mem_kv_kernel(m_ref, w_ref, mk_ref, mv_ref):
    kv = jnp.dot(m_ref[0].astype(BF16), w_ref[...].astype(BF16), preferred_element_type=F32)
    mk_ref[0] = kv[:, :D_MEM]
    mv_ref[0] = kv[:, D_MEM:]


def _mem_kv(mem, w_mem_kv):
    b = mem.shape[0]
    return pl.pallas_call(
        _mem_kv_kernel,
        grid=(b,),
        in_specs=[pl.BlockSpec((1, N_MEM, D_MODEL), lambda i: (i, 0, 0)),
                  pl.BlockSpec((D_MODEL, 2 * D_MEM), lambda i: (0, 0))],
        out_specs=[pl.BlockSpec((1, N_MEM, D_MEM), lambda i: (i, 0, 0)),
                   pl.BlockSpec((1, N_MEM, D_MEM), lambda i: (i, 0, 0))],
        out_shape=[jax.ShapeDtypeStruct((b, N_MEM, D_MEM), F32)] * 2,
        compiler_params=_params(1),
        name="mem_kv",
    )(mem, w_mem_kv)


def _ssm_kernel(u_ref, h0r_ref, h0i_ref, lr_ref, li_ref, ldt_ref, bre_ref, bim_ref,
                cre_ref, cim_ref, d_ref, wg_ref, bg_ref,
                y_ref, sr_ref, si_ref,
                a_sc, bbr_sc, bbi_sc, cr_sc, ci_sc, wg_sc, str_sc, sti_sc, xr_sc, xi_sc,
                *, n_batch):
    n_rows = u_ref.shape[1]
    n_steps = n_rows // n_batch

    @pl.when(pl.program_id(0) == 0)
    def _():
        lr = lr_ref[...]
        li = li_ref[...]
        dt = jnp.exp(ldt_ref[...])
        mag = jnp.exp(lr * dt)
        ar = mag * jnp.cos(li * dt)
        ai = mag * jnp.sin(li * dt)
        den = lr * lr + li * li
        fr = ((ar - 1.0) * lr + ai * li) / den
        fi = (ai * lr - (ar - 1.0) * li) / den
        a_sc[0:1, :] = ar
        a_sc[1:2, :] = ai
        for j in range(SSM_CHUNKS):
            frj = fr[:, CHUNK_STATE * j:CHUNK_STATE * (j + 1)]
            fij = fi[:, CHUNK_STATE * j:CHUNK_STATE * (j + 1)]
            bbr_sc[j] = (frj * bre_ref[j] - fij * bim_ref[j]).astype(BF16)
            bbi_sc[j] = (frj * bim_ref[j] + fij * bre_ref[j]).astype(BF16)
            cr_sc[j] = cre_ref[j].astype(BF16)
            ci_sc[j] = cim_ref[j].astype(BF16)
        wg_sc[...] = wg_ref[...].astype(BF16)
        str_sc[...] = h0r_ref[...]
        sti_sc[...] = h0i_ref[...]

    for j in range(SSM_CHUNKS):
        uc = u_ref[j].astype(BF16)
        xr_sc[:, CHUNK_STATE * j:CHUNK_STATE * (j + 1)] = jnp.dot(uc, bbr_sc[j], preferred_element_type=F32)
        xi_sc[:, CHUNK_STATE * j:CHUNK_STATE * (j + 1)] = jnp.dot(uc, bbi_sc[j], preferred_element_type=F32)

    for c in range(D_STATE // SCAN_LANES):
        lo = c * SCAN_LANES
        ar = jnp.broadcast_to(a_sc[0:1, lo:lo + SCAN_LANES], (n_batch, SCAN_LANES))
        ai = jnp.broadcast_to(a_sc[1:2, lo:lo + SCAN_LANES], (n_batch, SCAN_LANES))

        def step(t, carry, lo=lo, ar=ar, ai=ai):
            sr, si = carry
            r0 = pl.multiple_of(t * n_batch, n_batch)
            nr = ar * sr - ai * si + xr_sc[pl.ds(r0, n_batch), lo:lo + SCAN_LANES]
            ni = ar * si + ai * sr + xi_sc[pl.ds(r0, n_batch), lo:lo + SCAN_LANES]
            xr_sc[pl.ds(r0, n_batch), lo:lo + SCAN_LANES] = nr
            xi_sc[pl.ds(r0, n_batch), lo:lo + SCAN_LANES] = ni
            return nr, ni

        sr, si = lax.fori_loop(0, n_steps, step,
                               (str_sc[:, lo:lo + SCAN_LANES], sti_sc[:, lo:lo + SCAN_LANES]),
                               unroll=True)
        str_sc[:, lo:lo + SCAN_LANES] = sr
        sti_sc[:, lo:lo + SCAN_LANES] = si

    pieces = []
    for j in range(SSM_CHUNKS):
        xr = xr_sc[:, CHUNK_STATE * j:CHUNK_STATE * (j + 1)].astype(BF16)
        xi = xi_sc[:, CHUNK_STATE * j:CHUNK_STATE * (j + 1)].astype(BF16)
        pieces.append(jnp.dot(xr, cr_sc[j], preferred_element_type=F32)
                      - jnp.dot(xi, ci_sc[j], preferred_element_type=F32))
    u = jnp.concatenate([u_ref[j] for j in range(SSM_CHUNKS)], axis=1)
    y = jnp.concatenate(pieces, axis=1) + d_ref[...] * u
    y = jax.nn.gelu(y)
    z = jnp.dot(y.astype(BF16), wg_sc[...], preferred_element_type=F32) + bg_ref[...]
    out = z[:, :D_SSM] * jax.nn.sigmoid(z[:, D_SSM:])
    for j in range(D_SSM // LANES):
        y_ref[j] = out[:, LANES * j:LANES * (j + 1)]
    sr_ref[...] = str_sc[...]
    si_ref[...] = sti_sc[...]


def _block_diag_b(b):
    bt = b.transpose(0, 2, 1).reshape(SSM_CHUNKS, CHUNK_GROUPS, SSM_GROUP, SSM_STATE)
    same = jnp.eye(CHUNK_GROUPS, dtype=bool)[None, :, None, :, None]
    t = jnp.where(same, bt[:, :, :, None, :], 0.0)
    return t.reshape(SSM_CHUNKS, CHUNK_IN, CHUNK_STATE)


def _block_diag_c(c):
    ct = c.transpose(0, 2, 1).reshape(SSM_CHUNKS, CHUNK_GROUPS, SSM_STATE, SSM_GROUP)
    same = jnp.eye(CHUNK_GROUPS, dtype=bool)[None, :, None, :, None]
    t = jnp.where(same, ct[:, :, :, None, :], 0.0)
    return t.reshape(SSM_CHUNKS, CHUNK_STATE, CHUNK_IN)


def _ssm(u_rows, h0_re, h0_im, lp, n_batch):
    rows = u_rows.shape[1]
    planes = D_SSM // LANES
    tr = min(SCAN_ROWS, rows)
    flat = lambda a: a.reshape(1, D_STATE)
    ldt = jnp.repeat(lp['log_dt'], SSM_STATE).reshape(1, D_STATE)
    const2 = lambda i: (0, 0)
    const3 = lambda i: (0, 0, 0)
    y, sr, si = pl.pallas_call(
        functools.partial(_ssm_kernel, n_batch=n_batch),
        grid=(rows // tr,),
        in_specs=[pl.BlockSpec((planes, tr, LANES), lambda i: (0, i, 0)),
                  pl.BlockSpec((n_batch, D_STATE), const2),
                  pl.BlockSpec((n_batch, D_STATE), const2),
                  pl.BlockSpec((1, D_STATE), const2),
                  pl.BlockSpec((1, D_STATE), const2),
                  pl.BlockSpec((1, D_STATE), const2),
                  pl.BlockSpec((SSM_CHUNKS, CHUNK_IN, CHUNK_STATE), const3),
                  pl.BlockSpec((SSM_CHUNKS, CHUNK_IN, CHUNK_STATE), const3),
                  pl.BlockSpec((SSM_CHUNKS, CHUNK_STATE, CHUNK_IN), const3),
                  pl.BlockSpec((SSM_CHUNKS, CHUNK_STATE, CHUNK_IN), const3),
                  pl.BlockSpec((1, D_SSM), const2),
                  pl.BlockSpec((D_SSM, 2 * D_SSM), const2),
                  pl.BlockSpec((1, 2 * D_SSM), const2)],
        out_specs=[pl.BlockSpec((planes, tr, LANES), lambda i: (0, i, 0)),
                   pl.BlockSpec((n_batch, D_STATE), const2),
                   pl.BlockSpec((n_batch, D_STATE), const2)],
        out_shape=[jax.ShapeDtypeStruct((planes, rows, LANES), F32),
                   jax.ShapeDtypeStruct((n_batch, D_STATE), F32),
                   jax.ShapeDtypeStruct((n_batch, D_STATE), F32)],
        scratch_shapes=[pltpu.VMEM((2, D_STATE), F32),
                        pltpu.VMEM((SSM_CHUNKS, CHUNK_IN, CHUNK_STATE), BF16), pltpu.VMEM((SSM_CHUNKS, CHUNK_IN, CHUNK_STATE), BF16),
                        pltpu.VMEM((SSM_CHUNKS, CHUNK_STATE, CHUNK_IN), BF16), pltpu.VMEM((SSM_CHUNKS, CHUNK_STATE, CHUNK_IN), BF16),
                        pltpu.VMEM((D_SSM, 2 * D_SSM), BF16),
                        pltpu.VMEM((n_batch, D_STATE), F32), pltpu.VMEM((n_batch, D_STATE), F32),
                        pltpu.VMEM((tr, D_STATE), F32), pltpu.VMEM((tr, D_STATE), F32)],
        compiler_params=_params(1),
        name="ssm",
    )(u_rows, h0_re.reshape(n_batch, D_STATE), h0_im.reshape(n_batch, D_STATE),
      flat(lp['lam_re']), flat(lp['lam_im']), ldt,
      _block_diag_b(lp['ssm_b_re']), _block_diag_b(lp['ssm_b_im']),
      _block_diag_c(lp['ssm_c_re']), _block_diag_c(lp['ssm_c_im']),
      lp['ssm_d'].reshape(1, D_SSM), lp['w_glu'], lp['b_glu'].reshape(1, 2 * D_SSM))
    return y, sr, si


def _softmax_pv(s, v):
    m = jnp.max(s, axis=-1, keepdims=True)
    p = jnp.exp(s - m)
    l = jnp.sum(p, axis=-1, keepdims=True)
    return jnp.dot(p.astype(BF16), v, preferred_element_type=F32) / l


def _attend(qb, k, v, out_ref, row0=0, bias_ref=None, valid=None, seq=0):
    tq = qb.shape[0]
    for h in range(N_HEADS):
        sl = slice(HEAD_DIM * h, HEAD_DIM * (h + 1))
        s = lax.dot_general(qb[:, sl], k[:, sl], _NT, preferred_element_type=F32)
        if bias_ref is not None:
            s = s + bias_ref[h]
        if valid is not None:
            s = jnp.where(valid, s, NEG_INF)
        out_ref[seq, row0:row0 + tq, sl] = _softmax_pv(s, v[:, sl])


def _attend_keys_major(qb, k, v, out_ref):
    kb = k.astype(BF16)
    vt = v.T.astype(BF16)
    heads = []
    for h in range(N_HEADS):
        sl = slice(HEAD_DIM * h, HEAD_DIM * (h + 1))
        st = lax.dot_general(kb[:, sl], qb[:, sl], _NT, preferred_element_type=F32)
        m = jnp.max(st, axis=0, keepdims=True)
        p = jnp.exp(st - m)
        l = jnp.sum(p, axis=0, keepdims=True)
        heads.append(jnp.dot(vt[sl, :], p.astype(BF16), preferred_element_type=F32) / l)
    out_ref[0] = jnp.concatenate(heads, axis=0).T


def _attn_prompt_kernel(q_ref, kp_ref, kc_ref, vp_ref, vc_ref, qm_ref, mk_ref, mv_ref, bias_ref,
                        ya_ref, ym_ref, bias_sc):
    tq = ATT_TQ

    @pl.when((pl.program_id(0) == 0) & (pl.program_id(1) == 0))
    def _():
        q_chunk = lax.broadcasted_iota(I32, (tq, 3 * tq), 0) // CHUNK
        k_chunk = lax.broadcasted_iota(I32, (tq, 3 * tq), 1) // CHUNK
        ahead = k_chunk - q_chunk
        for h in range(N_HEADS):
            bias_sc[h] = jnp.where(ahead >= 0, jnp.where(ahead <= N_PREV_CHUNKS, bias_ref[h], NEG_INF),
                                   NEG_INF)

    k = jnp.concatenate([kp_ref[0], kc_ref[0]], axis=0)
    v = jnp.concatenate([vp_ref[0], vc_ref[0]], axis=0)
    for half in range(2):
        first_key = (2 * pl.program_id(1) - 2 + half) * tq
        kpos = first_key + lax.broadcasted_iota(I32, (1, 3 * tq), 1)
        _attend(q_ref[0, half * tq:(half + 1) * tq, :], k[half * tq:(half + 3) * tq],
                v[half * tq:(half + 3) * tq], ya_ref, half * tq, bias_sc, kpos >= 0)
    _attend_keys_major(qm_ref[0], mk_ref[0], mv_ref[0], ym_ref)


def _attn_sample_kernel(q_ref, kn_ref, vn_ref, qm_ref, ck_ref, cv_ref, mk_ref, mv_ref, bias_ref,
                        ya_ref, ym_ref, nk_ref, nv_ref):
    n = kn_ref.shape[1]
    for i in range(q_ref.shape[0]):
        kk = jnp.concatenate([ck_ref[i], kn_ref[i]], axis=0)
        vv = jnp.concatenate([cv_ref[i], vn_ref[i]], axis=0)
        nk_ref[i] = kk[n:]
        nv_ref[i] = vv[n:]
        _attend(q_ref[i], kk.astype(BF16), vv.astype(BF16), ya_ref, 0, bias_ref, seq=i)
        _attend(qm_ref[i], mk_ref[i].astype(BF16), mv_ref[i].astype(BF16), ym_ref, seq=i)


def _rel_bias(table, n_q, n_k):
    period = n_q + n_k
    m = jnp.arange(period)
    offset = jnp.where(m < n_k, m, m - period)
    idx = jnp.clip(BAND - offset, -REL_CLIP, REL_CLIP) + REL_CLIP
    f = table.astype(F32)[:, idx]
    flat = jnp.tile(f, (1, n_q))[:, :n_q * (period - 1)]
    return flat.reshape(N_HEADS, n_q, period - 1)[:, :, :n_k]


def _attn_prompt(zb, mk, mv, table):
    b, s, _ = zb.shape
    tq = ATT_TQ
    bias = _rel_bias(table, tq, 3 * tq)
    col = lambda c: (lambda i, j: (i, j, c))
    prev = lambda c: (lambda i, j: (i, jnp.maximum(j - 1, 0), c))
    blk = (1, 2 * tq, D_ATT)
    return pl.pallas_call(
        _attn_prompt_kernel,
        grid=(b, s // (2 * tq)),
        in_specs=[pl.BlockSpec(blk, col(0)),
                  pl.BlockSpec(blk, prev(1)), pl.BlockSpec(blk, col(1)),
                  pl.BlockSpec(blk, prev(2)), pl.BlockSpec(blk, col(2)),
                  pl.BlockSpec(blk, col(3)),
                  pl.BlockSpec((1, N_MEM, D_MEM), lambda i, j: (i, 0, 0)),
                  pl.BlockSpec((1, N_MEM, D_MEM), lambda i, j: (i, 0, 0)),
                  pl.BlockSpec((N_HEADS, tq, 3 * tq), lambda i, j: (0, 0, 0))],
        out_specs=[pl.BlockSpec(blk, col(0)), pl.BlockSpec(blk, col(0))],
        out_shape=[jax.ShapeDtypeStruct((b, s, D_ATT), F32),
                   jax.ShapeDtypeStruct((b, s, D_MEM), F32)],
        scratch_shapes=[pltpu.VMEM((N_HEADS, tq, 3 * tq), F32)],
        compiler_params=_params(2),
        name="attn_prompt",
    )(zb, zb, zb, zb, zb, zb, mk, mv, bias)


def _attn_sample(kv, zb, cache_k, cache_v, mk, mv, table):
    b, n, _ = kv.shape
    w = cache_k.shape[1]
    bias = _rel_bias(table, n, w + n)
    col = lambda c: (lambda i: (i, 0, c))
    per = SAMPLE_SEQS if b % SAMPLE_SEQS == 0 else 1
    blk = (per, n, D_ATT)
    cblk = (per, w, D_ATT)
    mblk = (per, N_MEM, D_MEM)
    row = lambda i: (i, 0, 0)
    return pl.pallas_call(
        _attn_sample_kernel,
        grid=(b // per,),
        in_specs=[pl.BlockSpec(blk, col(0)), pl.BlockSpec(blk, col(0)), pl.BlockSpec(blk, col(1)),
                  pl.BlockSpec(blk, col(3)),
                  pl.BlockSpec(cblk, row), pl.BlockSpec(cblk, row),
                  pl.BlockSpec(mblk, row), pl.BlockSpec(mblk, row),
                  pl.BlockSpec((N_HEADS, n, w + n), lambda i: (0, 0, 0))],
        out_specs=[pl.BlockSpec(blk, row), pl.BlockSpec(blk, row),
                   pl.BlockSpec(cblk, row), pl.BlockSpec(cblk, row)],
        out_shape=[jax.ShapeDtypeStruct((b, n, D_ATT), F32),
                   jax.ShapeDtypeStruct((b, n, D_MEM), F32),
                   jax.ShapeDtypeStruct((b, w, D_ATT), F32),
                   jax.ShapeDtypeStruct((b, w, D_ATT), F32)],
        compiler_params=_params(1),
        name="attn_sample",
    )(zb, kv, kv, zb, cache_k, cache_v, mk, mv, bias)


def _rms(x, g):
    return x * lax.rsqrt(jnp.mean(jnp.square(x), axis=-1, keepdims=True) + LN_EPS) * g


def _layer_norm(x, g, b):
    mu = jnp.mean(x, axis=-1, keepdims=True)
    xc = x - mu
    var = jnp.mean(jnp.square(xc), axis=-1, keepdims=True)
    return xc * lax.rsqrt(var + LN_EPS) * g + b


def _split_bf16(a):
    hi = a.astype(BF16)
    lo = (a - hi.astype(F32)).astype(BF16)
    return hi, lo


def _merge_kernel(x_ref, ys_ref, ya_ref, ym_ref, gs_ref, ga_ref, gm_ref, wo_ref, l1g_ref, l1b_ref,
                  wrt_ref, brt_ref,
                  h_ref, hb_ref, pos_ref, gate_ref, cnt_ref,
                  wo_sc, *, nb):
    st = x_ref.shape[1]
    tm = nb * st
    n_batch = ys_ref.shape[1] // st

    @pl.when((pl.program_id(0) == 0) & (pl.program_id(1) == 0))
    def _():
        wo_sc[...] = wo_ref[...].astype(BF16)

    x = x_ref[...].reshape(tm, D_MODEL)
    first = pl.program_id(1) * nb
    ys = jnp.concatenate(
        [jnp.concatenate([ys_ref[c, pl.ds(first + i, st, stride=n_batch), :] for c in range(D_SSM // LANES)],
                         axis=1) for i in range(nb)], axis=0)
    ya = ya_ref[...].reshape(tm, D_ATT)
    ym = ym_ref[...].reshape(tm, D_MEM)
    a = _rms(ys, gs_ref[...]).astype(BF16)
    b = _rms(ya, ga_ref[...]).astype(BF16)
    c = _rms(ym, gm_ref[...]).astype(BF16)
    mix = (jnp.dot(a, wo_sc[0:D_SSM, :], preferred_element_type=F32)
           + jnp.dot(b, wo_sc[D_SSM:D_SSM + D_ATT, :], preferred_element_type=F32)
           + jnp.dot(c, wo_sc[D_SSM + D_ATT:, :], preferred_element_type=F32))
    h = _layer_norm(DEEPNORM_ALPHA * x + mix, l1g_ref[...], l1b_ref[...])
    h_ref[...] = h

    h_hi, h_lo = _split_bf16(h)
    hb_ref[...] = h_hi
    w_hi, w_lo = _split_bf16(wrt_ref[...])
    logits = (lax.dot_general(w_hi, h_hi, _NT, preferred_element_type=F32)
              + lax.dot_general(w_hi, h_lo, _NT, preferred_element_type=F32)
              + lax.dot_general(w_lo, h_hi, _NT, preferred_element_type=F32)
              + brt_ref[...])
    erow = lax.broadcasted_iota(I32, (N_EXPERTS, tm), 0).astype(F32)
    tops, picks = [], []
    l = logits
    for k in range(TOP_K):
        m = jnp.max(l, axis=0, keepdims=True)
        e = jnp.min(jnp.where(l == m, erow, float(N_EXPERTS)), axis=0, keepdims=True)
        pick = erow == e
        tops.append(m)
        picks.append(jnp.where(pick, 1.0, 0.0))
        l = jnp.where(pick, -jnp.inf, l)
    ex = [jnp.exp(t - tops[0]) for t in tops]
    den = ex[0] + ex[1] + ex[2] + ex[3]
    for k in range(TOP_K):
        gate_ref[k:k + 1, :] = ex[k] / den

    chosen = picks[0] + picks[1] + picks[2] + picks[3]
    chosen_b = chosen.astype(BF16)
    earlier_tok = (lax.broadcasted_iota(I32, (tm, tm), 0) < lax.broadcasted_iota(I32, (tm, tm), 1))
    within = jnp.dot(chosen_b, jnp.where(earlier_tok, 1.0, 0.0).astype(BF16),
                     preferred_element_type=F32)
    lower_exp = (lax.broadcasted_iota(I32, (N_EXPERTS, N_EXPERTS), 1)
                 < lax.broadcasted_iota(I32, (N_EXPERTS, N_EXPERTS), 0))
    below = jnp.dot(jnp.where(lower_exp, 1.0, 0.0).astype(BF16), chosen_b,
                    preferred_element_type=F32)
    slot = within + jnp.sum(below, axis=1, keepdims=True)
    for k in range(TOP_K):
        pos_ref[k:k + 1, :] = jnp.sum(picks[k] * slot, axis=0, keepdims=True).astype(I32)
    cnt_ref[0] = jnp.sum(chosen, axis=1, keepdims=True)


def _merge(x, ys_tm, ya, ym, lp, nb, st):
    b, s, _ = x.shape
    tm = nb * st
    assert tm == TOKEN_TM
    n_s = s // st
    t_all = b * s
    tile = lambda j, i: (i * n_s + j)
    c2 = lambda j, i: (0, 0)
    row3 = lambda j, i: (i, j, 0)
    vec = lambda a: a.reshape(1, -1)
    return pl.pallas_call(
        functools.partial(_merge_kernel, nb=nb),
        grid=(n_s, b // nb),
        in_specs=[pl.BlockSpec((nb, st, D_MODEL), row3),
                  pl.BlockSpec((D_SSM // LANES, st * b, LANES), lambda j, i: (0, j, 0)),
                  pl.BlockSpec((nb, st, D_ATT), row3),
                  pl.BlockSpec((nb, st, D_MEM), row3),
                  pl.BlockSpec((1, D_SSM), c2), pl.BlockSpec((1, D_ATT), c2),
                  pl.BlockSpec((1, D_MEM), c2),
                  pl.BlockSpec((D_MODEL, D_MODEL), c2),
                  pl.BlockSpec((1, D_MODEL), c2), pl.BlockSpec((1, D_MODEL), c2),
                  pl.BlockSpec((N_EXPERTS, D_MODEL), c2), pl.BlockSpec((N_EXPERTS, 1), c2)],
        out_specs=[pl.BlockSpec((tm, D_MODEL), lambda j, i: (tile(j, i), 0)),
                   pl.BlockSpec((tm, D_MODEL), lambda j, i: (tile(j, i), 0)),
                   pl.BlockSpec((TOP_K, tm), lambda j, i: (0, tile(j, i))),
                   pl.BlockSpec((TOP_K, tm), lambda j, i: (0, tile(j, i))),
                   pl.BlockSpec((1, N_EXPERTS, 1), lambda j, i: (tile(j, i), 0, 0))],
        out_shape=[jax.ShapeDtypeStruct((t_all, D_MODEL), F32),
                   jax.ShapeDtypeStruct((t_all, D_MODEL), BF16),
                   jax.ShapeDtypeStruct((TOP_K, t_all), I32),
                   jax.ShapeDtypeStruct((TOP_K, t_all), F32),
                   jax.ShapeDtypeStruct((t_all // tm, N_EXPERTS, 1), F32)],
        scratch_shapes=[pltpu.VMEM((D_MODEL, D_MODEL), BF16)],
        compiler_params=_params(2),
        name="merge_router",
    )(x, ys_tm, ya, ym, vec(lp['g_ssm']), vec(lp['g_att']), vec(lp['g_mem']), lp['w_out'],
      vec(lp['ln1_g']), vec(lp['ln1_b']), lp['w_router'].T, lp['b_router'].reshape(N_EXPERTS, 1))


def _rows(start, size):
    return pl.ds(pl.multiple_of(start * ROW_SUBLANES, ROW_SUBLANES), size * ROW_SUBLANES)


def _store_rows(ref, value, row0=0):
    n = value.shape[0]
    for j in range(ROW_SUBLANES):
        ref[pl.ds(row0 * ROW_SUBLANES + j, n, stride=ROW_SUBLANES), :] = value[:, LANES * j:LANES * (j + 1)]


def _load_rows(ref, dtype=F32):
    n = ref.shape[0] // ROW_SUBLANES
    return jnp.concatenate([ref[pl.ds(j, n, stride=ROW_SUBLANES), :].astype(dtype)
                            for j in range(ROW_SUBLANES)], axis=1)


def _for_each_run_piece(n, max_rows, fn):
    for bit in reversed(range(max_rows.bit_length())):
        size = 1 << bit
        start = (n >> (bit + 1)) << (bit + 1)

        @pl.when((n & size) != 0)
        def _(start=start, size=size):
            fn(start, size)


def _dispatch_kernel(n_ref, off_ref, dst_ref, padlo_ref, padn_ref, used_ref,
                     pos_ref, h1_ref, h2_ref, xs_hbm, sorted_sc, zero_sc, sem, zsem, *, n_first):
    i = pl.program_id(0)
    tm = h1_ref.shape[0]
    n_slots = TOP_K * tm
    n_blocks = xs_hbm.shape[0] // (EXPERT_TM * ROW_SUBLANES)

    @pl.when(i == 0)
    def _():
        zero_sc[...] = jnp.zeros_like(zero_sc)

        def pad_copy(e, start, size):
            return pltpu.make_async_copy(zero_sc.at[_rows(0, size)],
                                         xs_hbm.at[_rows(padlo_ref[e] + start, size)], zsem)

        def tail_copy(blk):
            return pltpu.make_async_copy(zero_sc, xs_hbm.at[_rows(blk * EXPERT_TM, EXPERT_TM)], zsem)

        for e in range(N_EXPERTS):
            _for_each_run_piece(padn_ref[e], EXPERT_TM - 1,
                                lambda start, size, e=e: pad_copy(e, start, size).start())

        def tail_start(blk, carry):
            tail_copy(blk).start()
            return carry

        lax.fori_loop(used_ref[0], n_blocks, tail_start, 0)
        for e in range(N_EXPERTS):
            _for_each_run_piece(padn_ref[e], EXPERT_TM - 1,
                                lambda start, size, e=e: pad_copy(e, start, size).wait())

        def tail_wait(blk, carry):
            tail_copy(blk).wait()
            return carry

        lax.fori_loop(used_ref[0], n_blocks, tail_wait, 0)

    n_tiles = pl.num_programs(0) - 1
    slot = lax.rem(i, N_SORT_BUFS)
    prev_slot = lax.rem(i + N_SORT_BUFS - 1, N_SORT_BUFS)
    buf = sorted_sc.at[slot]
    prev = sorted_sc.at[prev_slot]

    def wait_tile(sl):
        pltpu.make_async_copy(sorted_sc.at[sl], xs_hbm.at[_rows(0, n_slots)], sem.at[sl]).wait()

    @pl.when(i >= N_SORT_BUFS)
    def _():
        wait_tile(slot)

    pos = pos_ref[...]
    hb = jnp.where(i < n_first, h1_ref[...], h2_ref[...])
    base = jnp.maximum(i - 1, 0) * N_EXPERTS
    rows_c = n_slots // SORT_CHUNKS
    experts_c = N_EXPERTS // SORT_CHUNKS
    for c in range(SORT_CHUNKS):
        for e in range(c * experts_c, (c + 1) * experts_c):
            off = off_ref[base + e]
            dst = dst_ref[base + e]

            def run_start(start, size, off=off, dst=dst):
                pltpu.make_async_copy(prev.at[_rows(off + start, size)],
                                      xs_hbm.at[_rows(dst + start, size)], sem.at[prev_slot]).start()

            _for_each_run_piece(jnp.where(i >= 1, n_ref[base + e], 0), tm, run_start)

        srow = lax.broadcasted_iota(I32, (rows_c, tm), 0) + c * rows_c
        perm = jnp.where(srow == pos[0:1], 1.0,
                         jnp.where(srow == pos[1:2], 1.0,
                                   jnp.where(srow == pos[2:3], 1.0,
                                             jnp.where(srow == pos[3:4], 1.0, 0.0)))).astype(BF16)
        _store_rows(buf, jnp.dot(perm, hb, preferred_element_type=F32), c * rows_c)

    @pl.when(i == n_tiles)
    def _():
        wait_tile(prev_slot)

        @pl.when(i >= 2)
        def _():
            wait_tile(lax.rem(i + N_SORT_BUFS - 2, N_SORT_BUFS))


def _dispatch(run_n, run_off, run_dst, pad_lo, pad_n, n_used, pos, h1, h2, cap):
    tm = TOKEN_TM
    n_first = h1.shape[0] // tm
    n_tiles = n_first + h2.shape[0] // tm
    grid_spec = pltpu.PrefetchScalarGridSpec(
        num_scalar_prefetch=6,
        grid=(n_tiles + 1,),
        in_specs=[pl.BlockSpec((TOP_K, tm), lambda i, *_: (0, jnp.minimum(i, n_tiles - 1))),
                  pl.BlockSpec((tm, D_MODEL), lambda i, *_: (jnp.minimum(i, n_first - 1), 0)),
                  pl.BlockSpec((tm, D_MODEL), lambda i, *_: (jnp.clip(i - n_first, 0, n_tiles - n_first - 1), 0))],
        out_specs=pl.BlockSpec(memory_space=pl.ANY),
        scratch_shapes=[pltpu.VMEM((N_SORT_BUFS, TOP_K * tm * ROW_SUBLANES, LANES), F32),
                        pltpu.VMEM((EXPERT_TM * ROW_SUBLANES, LANES), F32),
                        pltpu.SemaphoreType.DMA((N_SORT_BUFS,)), pltpu.SemaphoreType.DMA],
    )
    return pl.pallas_call(
        functools.partial(_dispatch_kernel, n_first=n_first),
        grid_spec=grid_spec,
        out_shape=jax.ShapeDtypeStruct((cap * ROW_SUBLANES, LANES), F32),
        compiler_params=_params(1),
        name="moe_dispatch",
    )(run_n, run_off, run_dst, pad_lo, pad_n, n_used, pos, h1, h2)


def _expert_kernel(be_ref, first_ref, ord_ref, seq_ref, used_ref,
                   x_ref, bgu_ref, bd_ref, wgu_hbm, wd_hbm, o_ref,
                   wgu_st, wd_st, wgu_sc, wd_sc, sem):
    i = pl.program_id(0)

    def weight_copies(e):
        return (pltpu.make_async_copy(wgu_hbm.at[e], wgu_st, sem.at[0]),
                pltpu.make_async_copy(wd_hbm.at[e], wd_st, sem.at[1]))

    @pl.when(i == 0)
    def _():
        for c in weight_copies(seq_ref[0]):
            c.start()

    @pl.when(i < used_ref[0])
    def _():
        @pl.when(first_ref[i] == 1)
        def _():
            k = ord_ref[i]
            for c in weight_copies(seq_ref[k]):
                c.wait()
            wgu_sc[...] = wgu_st[...].astype(BF16)
            wd_sc[...] = wd_st[...].astype(BF16)

            @pl.when(k + 1 < used_ref[1])
            def _():
                for c in weight_copies(seq_ref[k + 1]):
                    c.start()

        gu = jnp.dot(_load_rows(x_ref, BF16), wgu_sc[...], preferred_element_type=F32) + bgu_ref[0]
        gate = jnp.minimum(gu[:, :D_FF], SWIGLU_LIMIT)
        lin = jnp.clip(gu[:, D_FF:], -SWIGLU_LIMIT, SWIGLU_LIMIT)
        act = gate * jax.nn.sigmoid(SWIGLU_ALPHA * gate) * (lin + 1.0)
        _store_rows(o_ref, jnp.dot(act.astype(BF16), wd_sc[...], preferred_element_type=F32) + bd_ref[0])

    @pl.when(i >= used_ref[0])
    def _():
        o_ref[...] = jnp.zeros_like(o_ref)


def _experts(block_expert, block_first, block_ord, expert_seq, n_used, xs, lp):
    tm = EXPERT_TM * ROW_SUBLANES
    grid_spec = pltpu.PrefetchScalarGridSpec(
        num_scalar_prefetch=5,
        grid=(xs.shape[0] // tm,),
        in_specs=[pl.BlockSpec((tm, LANES), lambda i, be, *_: (i, 0)),
                  pl.BlockSpec((1, 1, 2 * D_FF), lambda i, be, *_: (be[i], 0, 0)),
                  pl.BlockSpec((1, 1, D_MODEL), lambda i, be, *_: (be[i], 0, 0)),
                  pl.BlockSpec(memory_space=pl.ANY),
                  pl.BlockSpec(memory_space=pl.ANY)],
        out_specs=pl.BlockSpec((tm, LANES), lambda i, be, *_: (i, 0)),
        scratch_shapes=[pltpu.VMEM((D_MODEL, 2 * D_FF), F32), pltpu.VMEM((D_FF, D_MODEL), F32),
                        pltpu.VMEM((D_MODEL, 2 * D_FF), BF16), pltpu.VMEM((D_FF, D_MODEL), BF16),
                        pltpu.SemaphoreType.DMA((2,))],
    )
    return pl.pallas_call(
        _expert_kernel,
        grid_spec=grid_spec,
        out_shape=jax.ShapeDtypeStruct(xs.shape, F32),
        compiler_params=_params(1),
        name="moe_experts",
    )(block_expert, block_first, block_ord, expert_seq, n_used, xs,
      lp['b_gu'].reshape(N_EXPERTS, 1, 2 * D_FF), lp['b_down'].reshape(N_EXPERTS, 1, D_MODEL),
      lp['w_gu'], lp['w_down'])


def _combine_kernel(n_ref, off_ref, dst_ref, pos_ref, gate_ref, h1_ref, h2_ref, ys_hbm, g_ref, b_ref,
                    y1_ref, y2_ref, sorted_sc, w_sc, sem, *, n_first):
    i = pl.program_id(0)
    n_tiles = pl.num_programs(0) - 1
    tm = h1_ref.shape[0]
    n_slots = TOP_K * tm
    slot = lax.rem(i, 2)
    buf = sorted_sc.at[slot]
    pos = pos_ref[...]
    gates = gate_ref[...]
    rows_per = tm // N_EXPERTS
    base = jnp.minimum(i, n_tiles - 1) * N_EXPERTS
    for e in range(N_EXPERTS):
        off = off_ref[base + e]
        dst = dst_ref[base + e]

        def run_start(start, size, off=off, dst=dst):
            pltpu.make_async_copy(ys_hbm.at[_rows(dst + start, size)],
                                  buf.at[_rows(off + start, size)], sem.at[slot]).start()

        _for_each_run_piece(jnp.where(i < n_tiles, n_ref[base + e], 0), tm, run_start)

        r = slice(e * rows_per, (e + 1) * rows_per)
        scol = lax.broadcasted_iota(I32, (rows_per, n_slots), 1)
        w_sc[r, :] = jnp.where(
            scol == pos[r, 0:1], gates[r, 0:1],
            jnp.where(scol == pos[r, 1:2], gates[r, 1:2],
                      jnp.where(scol == pos[r, 2:3], gates[r, 2:3],
                                jnp.where(scol == pos[r, 3:4], gates[r, 3:4], 0.0)))).astype(BF16)

    @pl.when(i >= 1)
    def _():
        done = sorted_sc.at[1 - slot]
        pltpu.make_async_copy(ys_hbm.at[_rows(0, n_slots)], done, sem.at[1 - slot]).wait()

        f = jnp.dot(w_sc[...], _load_rows(done, BF16), preferred_element_type=F32)

        @pl.when(i - 1 < n_first)
        def _():
            y1_ref[...] = _layer_norm(DEEPNORM_ALPHA * h1_ref[...] + f, g_ref[...], b_ref[...])

        @pl.when(i - 1 >= n_first)
        def _():
            y2_ref[...] = _layer_norm(DEEPNORM_ALPHA * h2_ref[...] + f, g_ref[...], b_ref[...])


def _combine(run_n, run_off, run_dst, pos_t, gates_t, h1, h2, ys, lp):
    tm = TOKEN_TM
    t_first = h1.shape[0]
    t = t_first + h2.shape[0]
    n_first = t_first // tm
    n_rest = (t - t_first) // tm
    c2 = lambda i, *_: (0, 0)
    done = lambda i, *_: (jnp.maximum(i - 1, 0), 0)
    done1 = lambda i, *_: (jnp.clip(i - 1, 0, n_first - 1), 0)
    done2 = lambda i, *_: (jnp.clip(i - 1 - n_first, 0, n_rest - 1), 0)
    grid_spec = pltpu.PrefetchScalarGridSpec(
        num_scalar_prefetch=3,
        grid=(t // tm + 1,),
        in_specs=[pl.BlockSpec((tm, TOP_K), done),
                  pl.BlockSpec((tm, TOP_K), done),
                  pl.BlockSpec((tm, D_MODEL), done1), pl.BlockSpec((tm, D_MODEL), done2),
                  pl.BlockSpec(memory_space=pl.ANY),
                  pl.BlockSpec((1, D_MODEL), c2), pl.BlockSpec((1, D_MODEL), c2)],
        out_specs=[pl.BlockSpec((tm, D_MODEL), done1), pl.BlockSpec((tm, D_MODEL), done2)],
        scratch_shapes=[pltpu.VMEM((2, TOP_K * tm * ROW_SUBLANES, LANES), F32),
                        pltpu.VMEM((tm, TOP_K * tm), BF16),
                        pltpu.SemaphoreType.DMA((2,))],
    )
    return pl.pallas_call(
        functools.partial(_combine_kernel, n_first=n_first),
        grid_spec=grid_spec,
        out_shape=[jax.ShapeDtypeStruct((t_first, D_MODEL), F32),
                   jax.ShapeDtypeStruct((t - t_first, D_MODEL), F32)],
        compiler_params=_params(1),
        name="moe_combine",
    )(run_n, run_off, run_dst, pos_t, gates_t, h1, h2, ys,
      lp['ln2_g'].reshape(1, D_MODEL), lp['ln2_b'].reshape(1, D_MODEL))


def _moe_and_norm(h1, h2, hb1, hb2, pos, gates, tile_counts, lp):
    t = h1.shape[0] + h2.shape[0]
    te = EXPERT_TM
    n_tiles = t // TOKEN_TM
    n_blocks = (t * TOP_K) // te + N_EXPERTS
    cap = n_blocks * te
    cnt = tile_counts.reshape(n_tiles, N_EXPERTS).astype(I32)
    counts = jnp.sum(cnt, axis=0)
    padded = (counts + te - 1) // te * te
    pad_ends = jnp.cumsum(padded)
    pad_starts = pad_ends - padded
    run_dst = pad_starts[None, :] + jnp.cumsum(cnt, axis=0) - cnt
    run_off = jnp.cumsum(cnt, axis=1) - cnt
    blk_start = jnp.arange(n_blocks, dtype=I32) * te
    expert_of = lambda slot_idx: jnp.minimum(jnp.sum(slot_idx[..., None] >= pad_ends, axis=-1), N_EXPERTS - 1)
    total = pad_ends[-1]
    be = jnp.where(blk_start < total, expert_of(blk_start), expert_of(jnp.maximum(total - 1, 0))).astype(I32)
    is_e = be[:, None] == jnp.arange(N_EXPERTS, dtype=I32)[None, :]
    pick = lambda table: jnp.sum(jnp.where(is_e, table[None, :], 0), axis=1)
    in_use = counts > 0
    ordinal = jnp.cumsum(in_use.astype(I32)) - 1
    rank = jnp.arange(N_EXPERTS, dtype=I32)
    expert_seq = jnp.sum(jnp.where(in_use[None, :] & (ordinal[None, :] == rank[:, None]), rank[None, :], 0),
                         axis=1)
    block_ord = pick(ordinal)
    block_first = (blk_start == pick(pad_starts)) & (blk_start < total)
    used = jnp.stack([total // te, jnp.sum(in_use.astype(I32))]).astype(I32)
    flat = lambda a: a.reshape(-1).astype(I32)
    xs = _dispatch(flat(cnt), flat(run_off), flat(run_dst), flat(pad_starts + counts),
                   flat(padded - counts), used, pos, hb1, hb2, cap)
    ys = _experts(be, flat(block_first), flat(block_ord), expert_seq, used, xs, lp)
    return _combine(flat(cnt), flat(run_off), flat(run_dst), pos.T, gates.T, h1, h2, ys, lp)


def kernel(x_prompt, x_sample, cache_attn_k, cache_attn_v, cache_mem_k, cache_mem_v, state_ssm_re, state_ssm_im, mem_prompt, w_in, lam_re, lam_im, log_dt, ssm_b_re, ssm_b_im, ssm_c_re, ssm_c_im, ssm_d, w_glu, b_glu, rel_bias, w_mem_kv, g_ssm, g_att, g_mem, w_out, ln1_g, ln1_b, w_router, b_router, w_gu, b_gu, w_down, b_down, ln2_g, ln2_b):
    assert w_in.shape[0] == 1, "single-layer step"
    lp = dict(w_in=w_in[0], lam_re=lam_re[0], lam_im=lam_im[0], log_dt=log_dt[0],
              ssm_b_re=ssm_b_re[0], ssm_b_im=ssm_b_im[0], ssm_c_re=ssm_c_re[0], ssm_c_im=ssm_c_im[0],
              ssm_d=ssm_d[0], w_glu=w_glu[0], b_glu=b_glu[0], rel_bias=rel_bias[0],
              w_mem_kv=w_mem_kv[0], g_ssm=g_ssm[0], g_att=g_att[0], g_mem=g_mem[0], w_out=w_out[0],
              ln1_g=ln1_g[0], ln1_b=ln1_b[0], w_router=w_router[0], b_router=b_router[0],
              w_gu=w_gu[0], b_gu=b_gu[0], w_down=w_down[0], b_down=b_down[0],
              ln2_g=ln2_g[0], ln2_b=ln2_b[0])

    bp, sp, _ = x_prompt.shape
    bs, ss, _ = x_sample.shape
    heads = lambda a: a.reshape(a.shape[0], a.shape[1], N_HEADS, HEAD_DIM)
    state = lambda a: a.reshape(a.shape[0], N_GROUPS, SSM_STATE)

    w = min(BAND, sp)
    u_p, kv_p, zb = _in_proj(x_prompt, lp['w_in'], min(IN_PROJ_TS, sp), w)
    mk, mv = _mem_kv(mem_prompt, lp['w_mem_kv'])
    ya, ym = _attn_prompt(zb, mk, mv, lp['rel_bias'])
    zeros = jnp.zeros((bp, D_STATE), F32)
    ys_p, sr_p, si_p = _ssm(u_p, zeros, zeros, lp, bp)
    h_p, hb_p, pos_p, gates_p, cnt_p = _merge(x_prompt, ys_p, ya, ym, lp, 1, TOKEN_TM)
    k_p = heads(kv_p[:, :, :D_ATT])
    v_p = heads(kv_p[:, :, D_ATT:])

    wc = cache_attn_k.shape[2]
    u_s, kv_s, zb_s = _in_proj(x_sample, lp['w_in'], ss, ss)
    ya_s, ym_s, nk, nv = _attn_sample(
        kv_s, zb_s, cache_attn_k[0].reshape(bs, wc, D_ATT), cache_attn_v[0].reshape(bs, wc, D_ATT),
        cache_mem_k[0].reshape(bs, N_MEM, D_MEM), cache_mem_v[0].reshape(bs, N_MEM, D_MEM),
        lp['rel_bias'])
    ys_s, sr_s, si_s = _ssm(u_s, state_ssm_re[0], state_ssm_im[0], lp, bs)
    h_s, hb_s, pos_s, gates_s, cnt_s = _merge(x_sample, ys_s, ya_s, ym_s, lp, TOKEN_TM // ss, ss)

    y_p, y_s = _moe_and_norm(h_p, h_s, hb_p, hb_s, jnp.concatenate([pos_p, pos_s], axis=1),
                             jnp.concatenate([gates_p, gates_s], axis=1),
                             jnp.concatenate([cnt_p, cnt_s], axis=0), lp)

    return (y_p.reshape(bp, sp, D_MODEL), y_s.reshape(bs, ss, D_MODEL),
            k_p[None], v_p[None], heads(mk)[None], heads(mv)[None], state(sr_p)[None], state(si_p)[None],
            heads(nk)[None], heads(nv)[None], state(sr_s)[None], state(si_s)[None])
```

```python
import functools

import jax
import jax.numpy as jnp
from jax import lax
from jax.experimental import pallas as pl
from jax.experimental.pallas import tpu as pltpu

F32 = jnp.float32
BF16 = jnp.bfloat16
I32 = jnp.int32

D_MODEL = 1024
D_SSM = 512
D_ATT = 256
D_MEM = 256
D_IN = D_SSM + 3 * D_ATT + D_MEM
D_REST = D_IN - D_SSM
HEAD_DIM = 64
N_HEADS = 4
N_GROUPS = 32
SSM_GROUP = 16
SSM_STATE = 64
D_STATE = N_GROUPS * SSM_STATE
CHUNK_GROUPS = 8
SSM_CHUNKS = N_GROUPS // CHUNK_GROUPS
CHUNK_IN = CHUNK_GROUPS * SSM_GROUP
CHUNK_STATE = CHUNK_GROUPS * SSM_STATE
CHUNK = 64
N_PREV_CHUNKS = 8
BAND = N_PREV_CHUNKS * CHUNK
REL_CLIP = 128
N_MEM = 256
N_EXPERTS = 32
TOP_K = 4
D_FF = D_MODEL
SWIGLU_LIMIT = 7.0
SWIGLU_ALPHA = 1.702
LN_EPS = 1e-5
NEG_INF = -1e30
ATT_SCALE = HEAD_DIM ** -0.5
DEEPNORM_ALPHA = 2.0 ** 0.25

V7X_VMEM_LIMIT = 56 * 1024 * 1024
IN_PROJ_TS = 128
ATT_TQ = 4 * CHUNK
SAMPLE_SEQS = 4
SCAN_LANES = 1024
SCAN_ROWS = 1024
LANES = 128
ROW_SUBLANES = D_MODEL // LANES
EXPERT_TM = 512
TOKEN_TM = 512
N_SORT_BUFS = 3
SORT_CHUNKS = 16

_NT = (((1,), (1,)), ((), ()))


def _params(n_axes, vmem=V7X_VMEM_LIMIT):
    return pltpu.CompilerParams(dimension_semantics=("arbitrary",) * n_axes,
                                vmem_limit_bytes=vmem)


def _in_proj_kernel(x_ref, w_ref, u_ref, kv_ref, zb_ref, wb_ref):
    @pl.when(pl.program_id(0) == 0)
    def _():
        wb_ref[...] = w_ref[...].astype(BF16)

    nb, ts, _ = x_ref.shape
    x = x_ref[...].reshape(nb * ts, D_MODEL).astype(BF16)
    z = jnp.dot(x, wb_ref[...], preferred_element_type=F32)
    for b in range(nb):
        for c in range(D_SSM // LANES):
            u_ref[c, pl.ds(b, ts, stride=nb), :] = z[b * ts:(b + 1) * ts, LANES * c:LANES * (c + 1)]
    zr = z[:, D_SSM:]
    kv_ref[...] = zr[:, D_ATT:3 * D_ATT].reshape(nb, ts, 2 * D_ATT)
    zb = jnp.concatenate([zr[:, :D_ATT] * ATT_SCALE, zr[:, D_ATT:3 * D_ATT], zr[:, 3 * D_ATT:] * ATT_SCALE],
                         axis=1)
    zb_ref[...] = zb.astype(BF16).reshape(nb, ts, D_REST)


def _in_proj(x, w_in, ts, tail):
    b, s, _ = x.shape
    skipped = (s - tail) // ts
    return pl.pallas_call(
        _in_proj_kernel,
        grid=(s // ts,),
        in_specs=[pl.BlockSpec((b, ts, D_MODEL), lambda j: (0, j, 0)),
                  pl.BlockSpec((D_MODEL, D_IN), lambda j: (0, 0))],
        out_specs=[pl.BlockSpec((D_SSM // LANES, ts * b, LANES), lambda j: (0, j, 0)),
                   pl.BlockSpec((b, ts, 2 * D_ATT), lambda j: (0, jnp.maximum(j - skipped, 0), 0)),
                   pl.BlockSpec((b, ts, D_REST), lambda j: (0, j, 0))],
        out_shape=[jax.ShapeDtypeStruct((D_SSM // LANES, s * b, LANES), F32),
                   jax.ShapeDtypeStruct((b, tail, 2 * D_ATT), F32),
                   jax.ShapeDtypeStruct((b, s, D_REST), BF16)],
        scratch_shapes=[pltpu.VMEM((D_MODEL, D_IN), BF16)],
        compiler_params=_params(1),
        name="in_proj",
    )(x, w_in)


def _mem_kv_kernel(m_ref, w_ref, mk_ref, mv_ref):
    kv = jnp.dot(m_ref[0].astype(BF16), w_ref[...].astype(BF16), preferred_element_type=F32)
    mk_ref[0] = kv[:, :D_MEM]
    mv_ref[0] = kv[:, D_MEM:]


def _mem_kv(mem, w_mem_kv):
    b = mem.shape[0]
    return pl.pallas_call(
        _mem_kv_kernel,
        grid=(b,),
        in_specs=[pl.BlockSpec((1, N_MEM, D_MODEL), lambda i: (i, 0, 0)),
                  pl.BlockSpec((D_MODEL, 2 * D_MEM), lambda i: (0, 0))],
        out_specs=[pl.BlockSpec((1, N_MEM, D_MEM), lambda i: (i, 0, 0)),
                   pl.BlockSpec((1, N_MEM, D_MEM), lambda i: (i, 0, 0))],
        out_shape=[jax.ShapeDtypeStruct((b, N_MEM, D_MEM), F32)] * 2,
        compiler_params=_params(1),
        name="mem_kv",
    )(mem, w_mem_kv)


def _ssm_kernel(u_ref, h0r_ref, h0i_ref, lr_ref, li_ref, ldt_ref, bre_ref, bim_ref,
                cre_ref, cim_ref, d_ref, wg_ref, bg_ref,
                y_ref, sr_ref, si_ref,
                a_sc, bbr_sc, bbi_sc, cr_sc, ci_sc, wg_sc, str_sc, sti_sc, xr_sc, xi_sc,
                *, n_batch):
    n_rows = u_ref.shape[1]
    n_steps = n_rows // n_batch

    @pl.when(pl.program_id(0) == 0)
    def _():
        lr = lr_ref[...]
        li = li_ref[...]
        dt = jnp.exp(ldt_ref[...])
        mag = jnp.exp(lr * dt)
        ar = mag * jnp.cos(li * dt)
        ai = mag * jnp.sin(li * dt)
        den = lr * lr + li * li
        fr = ((ar - 1.0) * lr + ai * li) / den
        fi = (ai * lr - (ar - 1.0) * li) / den
        a_sc[0:1, :] = ar
        a_sc[1:2, :] = ai
        for j in range(SSM_CHUNKS):
            frj = fr[:, CHUNK_STATE * j:CHUNK_STATE * (j + 1)]
            fij = fi[:, CHUNK_STATE * j:CHUNK_STATE * (j + 1)]
            bbr_sc[j] = (frj * bre_ref[j] - fij * bim_ref[j]).astype(BF16)
            bbi_sc[j] = (frj * bim_ref[j] + fij * bre_ref[j]).astype(BF16)
            cr_sc[j] = cre_ref[j].astype(BF16)
            ci_sc[j] = cim_ref[j].astype(BF16)
        wg_sc[...] = wg_ref[...].astype(BF16)
        str_sc[...] = h0r_ref[...]
        sti_sc[...] = h0i_ref[...]

    for j in range(SSM_CHUNKS):
        uc = u_ref[j].astype(BF16)
        xr_sc[:, CHUNK_STATE * j:CHUNK_STATE * (j + 1)] = jnp.dot(uc, bbr_sc[j], preferred_element_type=F32)
        xi_sc[:, CHUNK_STATE * j:CHUNK_STATE * (j + 1)] = jnp.dot(uc, bbi_sc[j], preferred_element_type=F32)

    for c in range(D_STATE // SCAN_LANES):
        lo = c * SCAN_LANES
        ar = jnp.broadcast_to(a_sc[0:1, lo:lo + SCAN_LANES], (n_batch, SCAN_LANES))
        ai = jnp.broadcast_to(a_sc[1:2, lo:lo + SCAN_LANES], (n_batch, SCAN_LANES))

        def step(t, carry, lo=lo, ar=ar, ai=ai):
            sr, si = carry
            r0 = pl.multiple_of(t * n_batch, n_batch)
            nr = ar * sr - ai * si + xr_sc[pl.ds(r0, n_batch), lo:lo + SCAN_LANES]
            ni = ar * si + ai * sr + xi_sc[pl.ds(r0, n_batch), lo:lo + SCAN_LANES]
            xr_sc[pl.ds(r0, n_batch), lo:lo + SCAN_LANES] = nr
            xi_sc[pl.ds(r0, n_batch), lo:lo + SCAN_LANES] = ni
            return nr, ni

        sr, si = lax.fori_loop(0, n_steps, step,
                               (str_sc[:, lo:lo + SCAN_LANES], sti_sc[:, lo:lo + SCAN_LANES]),
                               unroll=True)
        str_sc[:, lo:lo + SCAN_LANES] = sr
        sti_sc[:, lo:lo + SCAN_LANES] = si

    pieces = []
    for j in range(SSM_CHUNKS):
        xr = xr_sc[:, CHUNK_STATE * j:CHUNK_STATE * (j + 1)].astype(BF16)
        xi = xi_sc[:, CHUNK_STATE * j:CHUNK_STATE * (j + 1)].astype(BF16)
        pieces.append(jnp.dot(xr, cr_sc[j], preferred_element_type=F32)
                      - jnp.dot(xi, ci_sc[j], preferred_element_type=F32))
    u = jnp.concatenate([u_ref[j] for j in range(SSM_CHUNKS)], axis=1)
    y = jnp.concatenate(pieces, axis=1) + d_ref[...] * u
    y = jax.nn.gelu(y)
    z = jnp.dot(y.astype(BF16), wg_sc[...], preferred_element_type=F32) + bg_ref[...]
    out = z[:, :D_SSM] * jax.nn.sigmoid(z[:, D_SSM:])
    for j in range(D_SSM // LANES):
        y_ref[j] = out[:, LANES * j:LANES * (j + 1)]
    sr_ref[...] = str_sc[...]
    si_ref[...] = sti_sc[...]


def _block_diag_b(b):
    bt = b.transpose(0, 2, 1).reshape(SSM_CHUNKS, CHUNK_GROUPS, SSM_GROUP, SSM_STATE)
    same = jnp.eye(CHUNK_GROUPS, dtype=bool)[None, :, None, :, None]
    t = jnp.where(same, bt[:, :, :, None, :], 0.0)
    return t.reshape(SSM_CHUNKS, CHUNK_IN, CHUNK_STATE)


def _block_diag_c(c):
    ct = c.transpose(0, 2, 1).reshape(SSM_CHUNKS, CHUNK_GROUPS, SSM_STATE, SSM_GROUP)
    same = jnp.eye(CHUNK_GROUPS, dtype=bool)[None, :, None, :, None]
    t = jnp.where(same, ct[:, :, :, None, :], 0.0)
    return t.reshape(SSM_CHUNKS, CHUNK_STATE, CHUNK_IN)


def _ssm(u_rows, h0_re, h0_im, lp, n_batch):
    rows = u_rows.shape[1]
    planes = D_SSM // LANES
    tr = min(SCAN_ROWS, rows)
    flat = lambda a: a.reshape(1, D_STATE)
    ldt = jnp.repeat(lp['log_dt'], SSM_STATE).reshape(1, D_STATE)
    const2 = lambda i: (0, 0)
    const3 = lambda i: (0, 0, 0)
    y, sr, si = pl.pallas_call(
        functools.partial(_ssm_kernel, n_batch=n_batch),
        grid=(rows // tr,),
        in_specs=[pl.BlockSpec((planes, tr, LANES), lambda i: (0, i, 0)),
                  pl.BlockSpec((n_batch, D_STATE), const2),
                  pl.BlockSpec((n_batch, D_STATE), const2),
                  pl.BlockSpec((1, D_STATE), const2),
                  pl.BlockSpec((1, D_STATE), const2),
                  pl.BlockSpec((1, D_STATE), const2),
                  pl.BlockSpec((SSM_CHUNKS, CHUNK_IN, CHUNK_STATE), const3),
                  pl.BlockSpec((SSM_CHUNKS, CHUNK_IN, CHUNK_STATE), const3),
                  pl.BlockSpec((SSM_CHUNKS, CHUNK_STATE, CHUNK_IN), const3),
                  pl.BlockSpec((SSM_CHUNKS, CHUNK_STATE, CHUNK_IN), const3),
                  pl.BlockSpec((1, D_SSM), const2),
                  pl.BlockSpec((D_SSM, 2 * D_SSM), const2),
                  pl.BlockSpec((1, 2 * D_SSM), const2)],
        out_specs=[pl.BlockSpec((planes, tr, LANES), lambda i: (0, i, 0)),
                   pl.BlockSpec((n_batch, D_STATE), const2),
                   pl.BlockSpec((n_batch, D_STATE), const2)],
        out_shape=[jax.ShapeDtypeStruct((planes, rows, LANES), F32),
                   jax.ShapeDtypeStruct((n_batch, D_STATE), F32),
                   jax.ShapeDtypeStruct((n_batch, D_STATE), F32)],
        scratch_shapes=[pltpu.VMEM((2, D_STATE), F32),
                        pltpu.VMEM((SSM_CHUNKS, CHUNK_IN, CHUNK_STATE), BF16), pltpu.VMEM((SSM_CHUNKS, CHUNK_IN, CHUNK_STATE), BF16),
                        pltpu.VMEM((SSM_CHUNKS, CHUNK_STATE, CHUNK_IN), BF16), pltpu.VMEM((SSM_CHUNKS, CHUNK_STATE, CHUNK_IN), BF16),
                        pltpu.VMEM((D_SSM, 2 * D_SSM), BF16),
                        pltpu.VMEM((n_batch, D_STATE), F32), pltpu.VMEM((n_batch, D_STATE), F32),
                        pltpu.VMEM((tr, D_STATE), F32), pltpu.VMEM((tr, D_STATE), F32)],
        compiler_params=_params(1),
        name="ssm",
    )(u_rows, h0_re.reshape(n_batch, D_STATE), h0_im.reshape(n_batch, D_STATE),
      flat(lp['lam_re']), flat(lp['lam_im']), ldt,
      _block_diag_b(lp['ssm_b_re']), _block_diag_b(lp['ssm_b_im']),
      _block_diag_c(lp['ssm_c_re']), _block_diag_c(lp['ssm_c_im']),
      lp['ssm_d'].reshape(1, D_SSM), lp['w_glu'], lp['b_glu'].reshape(1, 2 * D_SSM))
    return y, sr, si


def _softmax_pv(s, v):
    m = jnp.max(s, axis=-1, keepdims=True)
    p = jnp.exp(s - m)
    l = jnp.sum(p, axis=-1, keepdims=True)
    return jnp.dot(p.astype(BF16), v, preferred_element_type=F32) / l


def _attend(qb, k, v, out_ref, row0=0, bias_ref=None, valid=None, seq=0):
    tq = qb.shape[0]
    for h in range(N_HEADS):
        sl = slice(HEAD_DIM * h, HEAD_DIM * (h + 1))
        s = lax.dot_general(qb[:, sl], k[:, sl], _NT, preferred_element_type=F32)
        if bias_ref is not None:
            s = s + bias_ref[h]
        if valid is not None:
            s = jnp.where(valid, s, NEG_INF)
        out_ref[seq, row0:row0 + tq, sl] = _softmax_pv(s, v[:, sl])


def _attn_prompt_kernel(q_ref, kp_ref, kc_ref, vp_ref, vc_ref, qm_ref, mk_ref, mv_ref, bias_ref,
                        ya_ref, ym_ref, bias_sc):
    tq = ATT_TQ

    @pl.when((pl.program_id(0) == 0) & (pl.program_id(1) == 0))
    def _():
        q_chunk = lax.broadcasted_iota(I32, (tq, 3 * tq), 0) // CHUNK
        k_chunk = lax.broadcasted_iota(I32, (tq, 3 * tq), 1) // CHUNK
        ahead = k_chunk - q_chunk
        for h in range(N_HEADS):
            bias_sc[h] = jnp.where(ahead >= 0, jnp.where(ahead <= N_PREV_CHUNKS, bias_ref[h], NEG_INF),
                                   NEG_INF)

    k = jnp.concatenate([kp_ref[0], kc_ref[0]], axis=0)
    v = jnp.concatenate([vp_ref[0], vc_ref[0]], axis=0)
    for half in range(2):
        first_key = (2 * pl.program_id(1) - 2 + half) * tq
        kpos = first_key + lax.broadcasted_iota(I32, (1, 3 * tq), 1)
        _attend(q_ref[0, half * tq:(half + 1) * tq, :], k[half * tq:(half + 3) * tq],
                v[half * tq:(half + 3) * tq], ya_ref, half * tq, bias_sc, kpos >= 0)
    _attend(qm_ref[0], mk_ref[0].astype(BF16), mv_ref[0].astype(BF16), ym_ref)


def _attn_sample_kernel(q_ref, kn_ref, vn_ref, qm_ref, ck_ref, cv_ref, mk_ref, mv_ref, bias_ref,
                        ya_ref, ym_ref, nk_ref, nv_ref):
    n = kn_ref.shape[1]
    for i in range(q_ref.shape[0]):
        kk = jnp.concatenate([ck_ref[i], kn_ref[i]], axis=0)
        vv = jnp.concatenate([cv_ref[i], vn_ref[i]], axis=0)
        nk_ref[i] = kk[n:]
        nv_ref[i] = vv[n:]
        _attend(q_ref[i], kk.astype(BF16), vv.astype(BF16), ya_ref, 0, bias_ref, seq=i)
        _attend(qm_ref[i], mk_ref[i].astype(BF16), mv_ref[i].astype(BF16), ym_ref, seq=i)


def _rel_bias(table, n_q, n_k):
    period = n_q + n_k
    m = jnp.arange(period)
    offset = jnp.where(m < n_k, m, m - period)
    idx = jnp.clip(BAND - offset, -REL_CLIP, REL_CLIP) + REL_CLIP
    f = table.astype(F32)[:, idx]
    flat = jnp.tile(f, (1, n_q))[:, :n_q * (period - 1)]
    return flat.reshape(N_HEADS, n_q, period - 1)[:, :, :n_k]


def _attn_prompt(zb, mk, mv, table):
    b, s, _ = zb.shape
    tq = ATT_TQ
    bias = _rel_bias(table, tq, 3 * tq)
    col = lambda c: (lambda i, j: (i, j, c))
    prev = lambda c: (lambda i, j: (i, jnp.maximum(j - 1, 0), c))
    blk = (1, 2 * tq, D_ATT)
    return pl.pallas_call(
        _attn_prompt_kernel,
        grid=(b, s // (2 * tq)),
        in_specs=[pl.BlockSpec(blk, col(0)),
                  pl.BlockSpec(blk, prev(1)), pl.BlockSpec(blk, col(1)),
                  pl.BlockSpec(blk, prev(2)), pl.BlockSpec(blk, col(2)),
                  pl.BlockSpec(blk, col(3)),
                  pl.BlockSpec((1, N_MEM, D_MEM), lambda i, j: (i, 0, 0)),
                  pl.BlockSpec((1, N_MEM, D_MEM), lambda i, j: (i, 0, 0)),
                  pl.BlockSpec((N_HEADS, tq, 3 * tq), lambda i, j: (0, 0, 0))],
        out_specs=[pl.BlockSpec(blk, col(0)), pl.BlockSpec(blk, col(0))],
        out_shape=[jax.ShapeDtypeStruct((b, s, D_ATT), F32),
                   jax.ShapeDtypeStruct((b, s, D_MEM), F32)],
        scratch_shapes=[pltpu.VMEM((N_HEADS, tq, 3 * tq), F32)],
        compiler_params=_params(2),
        name="attn_prompt",
    )(zb, zb, zb, zb, zb, zb, mk, mv, bias)


def _attn_sample(kv, zb, cache_k, cache_v, mk, mv, table):
    b, n, _ = kv.shape
    w = cache_k.shape[1]
    bias = _rel_bias(table, n, w + n)
    col = lambda c: (lambda i: (i, 0, c))
    per = SAMPLE_SEQS if b % SAMPLE_SEQS == 0 else 1
    blk = (per, n, D_ATT)
    cblk = (per, w, D_ATT)
    mblk = (per, N_MEM, D_MEM)
    row = lambda i: (i, 0, 0)
    return pl.pallas_call(
        _attn_sample_kernel,
        grid=(b // per,),
        in_specs=[pl.BlockSpec(blk, col(0)), pl.BlockSpec(blk, col(0)), pl.BlockSpec(blk, col(1)),
                  pl.BlockSpec(blk, col(3)),
                  pl.BlockSpec(cblk, row), pl.BlockSpec(cblk, row),
                  pl.BlockSpec(mblk, row), pl.BlockSpec(mblk, row),
                  pl.BlockSpec((N_HEADS, n, w + n), lambda i: (0, 0, 0))],
        out_specs=[pl.BlockSpec(blk, row), pl.BlockSpec(blk, row),
                   pl.BlockSpec(cblk, row), pl.BlockSpec(cblk, row)],
        out_shape=[jax.ShapeDtypeStruct((b, n, D_ATT), F32),
                   jax.ShapeDtypeStruct((b, n, D_MEM), F32),
                   jax.ShapeDtypeStruct((b, w, D_ATT), F32),
                   jax.ShapeDtypeStruct((b, w, D_ATT), F32)],
        compiler_params=_params(1),
        name="attn_sample",
    )(zb, kv, kv, zb, cache_k, cache_v, mk, mv, bias)


def _rms(x, g):
    return x * lax.rsqrt(jnp.mean(jnp.square(x), axis=-1, keepdims=True) + LN_EPS) * g


def _layer_norm(x, g, b):
    mu = jnp.mean(x, axis=-1, keepdims=True)
    xc = x - mu
    var = jnp.mean(jnp.square(xc), axis=-1, keepdims=True)
    return xc * lax.rsqrt(var + LN_EPS) * g + b


def _split_bf16(a):
    hi = a.astype(BF16)
    lo = (a - hi.astype(F32)).astype(BF16)
    return hi, lo


def _merge_kernel(x_ref, ys_ref, ya_ref, ym_ref, gs_ref, ga_ref, gm_ref, wo_ref, l1g_ref, l1b_ref,
                  wrt_ref, brt_ref,
                  h_ref, hb_ref, pos_ref, gate_ref, cnt_ref,
                  wo_sc, *, nb):
    st = x_ref.shape[1]
    tm = nb * st
    n_batch = ys_ref.shape[1] // st

    @pl.when((pl.program_id(0) == 0) & (pl.program_id(1) == 0))
    def _():
        wo_sc[...] = wo_ref[...].astype(BF16)

    x = x_ref[...].reshape(tm, D_MODEL)
    first = pl.program_id(1) * nb
    ys = jnp.concatenate(
        [jnp.concatenate([ys_ref[c, pl.ds(first + i, st, stride=n_batch), :] for c in range(D_SSM // LANES)],
                         axis=1) for i in range(nb)], axis=0)
    ya = ya_ref[...].reshape(tm, D_ATT)
    ym = ym_ref[...].reshape(tm, D_MEM)
    a = _rms(ys, gs_ref[...]).astype(BF16)
    b = _rms(ya, ga_ref[...]).astype(BF16)
    c = _rms(ym, gm_ref[...]).astype(BF16)
    mix = (jnp.dot(a, wo_sc[0:D_SSM, :], preferred_element_type=F32)
           + jnp.dot(b, wo_sc[D_SSM:D_SSM + D_ATT, :], preferred_element_type=F32)
           + jnp.dot(c, wo_sc[D_SSM + D_ATT:, :], preferred_element_type=F32))
    h = _layer_norm(DEEPNORM_ALPHA * x + mix, l1g_ref[...], l1b_ref[...])
    h_ref[...] = h

    h_hi, h_lo = _split_bf16(h)
    hb_ref[...] = h_hi
    w_hi, w_lo = _split_bf16(wrt_ref[...])
    logits = (lax.dot_general(w_hi, h_hi, _NT, preferred_element_type=F32)
              + lax.dot_general(w_hi, h_lo, _NT, preferred_element_type=F32)
              + lax.dot_general(w_lo, h_hi, _NT, preferred_element_type=F32)
              + brt_ref[...])
    erow = lax.broadcasted_iota(I32, (N_EXPERTS, tm), 0).astype(F32)
    tops, picks = [], []
    l = logits
    for k in range(TOP_K):
        m = jnp.max(l, axis=0, keepdims=True)
        e = jnp.min(jnp.where(l == m, erow, float(N_EXPERTS)), axis=0, keepdims=True)
        pick = erow == e
        tops.append(m)
        picks.append(jnp.where(pick, 1.0, 0.0))
        l = jnp.where(pick, -jnp.inf, l)
    ex = [jnp.exp(t - tops[0]) for t in tops]
    den = ex[0] + ex[1] + ex[2] + ex[3]
    for k in range(TOP_K):
        gate_ref[k:k + 1, :] = ex[k] / den

    chosen = picks[0] + picks[1] + picks[2] + picks[3]
    chosen_b = chosen.astype(BF16)
    earlier_tok = (lax.broadcasted_iota(I32, (tm, tm), 0) < lax.broadcasted_iota(I32, (tm, tm), 1))
    within = jnp.dot(chosen_b, jnp.where(earlier_tok, 1.0, 0.0).astype(BF16),
                     preferred_element_type=F32)
    lower_exp = (lax.broadcasted_iota(I32, (N_EXPERTS, N_EXPERTS), 1)
                 < lax.broadcasted_iota(I32, (N_EXPERTS, N_EXPERTS), 0))
    below = jnp.dot(jnp.where(lower_exp, 1.0, 0.0).astype(BF16), chosen_b,
                    preferred_element_type=F32)
    slot = within + jnp.sum(below, axis=1, keepdims=True)
    for k in range(TOP_K):
        pos_ref[k:k + 1, :] = jnp.sum(picks[k] * slot, axis=0, keepdims=True).astype(I32)
    cnt_ref[0] = jnp.sum(chosen, axis=1, keepdims=True)


def _merge(x, ys_tm, ya, ym, lp, nb, st):
    b, s, _ = x.shape
    tm = nb * st
    assert tm == TOKEN_TM
    n_s = s // st
    t_all = b * s
    tile = lambda j, i: (i * n_s + j)
    c2 = lambda j, i: (0, 0)
    row3 = lambda j, i: (i, j, 0)
    vec = lambda a: a.reshape(1, -1)
    return pl.pallas_call(
        functools.partial(_merge_kernel, nb=nb),
        grid=(n_s, b // nb),
        in_specs=[pl.BlockSpec((nb, st, D_MODEL), row3),
                  pl.BlockSpec((D_SSM // LANES, st * b, LANES), lambda j, i: (0, j, 0)),
                  pl.BlockSpec((nb, st, D_ATT), row3),
                  pl.BlockSpec((nb, st, D_MEM), row3),
                  pl.BlockSpec((1, D_SSM), c2), pl.BlockSpec((1, D_ATT), c2),
                  pl.BlockSpec((1, D_MEM), c2),
                  pl.BlockSpec((D_MODEL, D_MODEL), c2),
                  pl.BlockSpec((1, D_MODEL), c2), pl.BlockSpec((1, D_MODEL), c2),
                  pl.BlockSpec((N_EXPERTS, D_MODEL), c2), pl.BlockSpec((N_EXPERTS, 1), c2)],
        out_specs=[pl.BlockSpec((tm, D_MODEL), lambda j, i: (tile(j, i), 0)),
                   pl.BlockSpec((tm, D_MODEL), lambda j, i: (tile(j, i), 0)),
                   pl.BlockSpec((TOP_K, tm), lambda j, i: (0, tile(j, i))),
                   pl.BlockSpec((TOP_K, tm), lambda j, i: (0, tile(j, i))),
                   pl.BlockSpec((1, N_EXPERTS, 1), lambda j, i: (tile(j, i), 0, 0))],
        out_shape=[jax.ShapeDtypeStruct((t_all, D_MODEL), F32),
                   jax.ShapeDtypeStruct((t_all, D_MODEL), BF16),
                   jax.ShapeDtypeStruct((TOP_K, t_all), I32),
                   jax.ShapeDtypeStruct((TOP_K, t_all), F32),
                   jax.ShapeDtypeStruct((t_all // tm, N_EXPERTS, 1), F32)],
        scratch_shapes=[pltpu.VMEM((D_MODEL, D_MODEL), BF16)],
        compiler_params=_params(2),
        name="merge_router",
    )(x, ys_tm, ya, ym, vec(lp['g_ssm']), vec(lp['g_att']), vec(lp['g_mem']), lp['w_out'],
      vec(lp['ln1_g']), vec(lp['ln1_b']), lp['w_router'].T, lp['b_router'].reshape(N_EXPERTS, 1))


def _rows(start, size):
    return pl.ds(pl.multiple_of(start * ROW_SUBLANES, ROW_SUBLANES), size * ROW_SUBLANES)


def _store_rows(ref, value, row0=0):
    n = value.shape[0]
    for j in range(ROW_SUBLANES):
        ref[pl.ds(row0 * ROW_SUBLANES + j, n, stride=ROW_SUBLANES), :] = value[:, LANES * j:LANES * (j + 1)]


def _load_rows(ref, dtype=F32):
    n = ref.shape[0] // ROW_SUBLANES
    return jnp.concatenate([ref[pl.ds(j, n, stride=ROW_SUBLANES), :].astype(dtype)
                            for j in range(ROW_SUBLANES)], axis=1)


def _for_each_run_piece(n, max_rows, fn):
    for bit in reversed(range(max_rows.bit_length())):
        size = 1 << bit
        start = (n >> (bit + 1)) << (bit + 1)

        @pl.when((n & size) != 0)
        def _(start=start, size=size):
            fn(start, size)


def _dispatch_kernel(n_ref, off_ref, dst_ref, padlo_ref, padn_ref, used_ref,
                     pos_ref, h1_ref, h2_ref, xs_hbm, sorted_sc, zero_sc, sem, zsem, *, n_first):
    i = pl.program_id(0)
    tm = h1_ref.shape[0]
    n_slots = TOP_K * tm
    n_blocks = xs_hbm.shape[0] // (EXPERT_TM * ROW_SUBLANES)

    @pl.when(i == 0)
    def _():
        zero_sc[...] = jnp.zeros_like(zero_sc)

        def pad_copy(e, start, size):
            return pltpu.make_async_copy(zero_sc.at[_rows(0, size)],
                                         xs_hbm.at[_rows(padlo_ref[e] + start, size)], zsem)

        def tail_copy(blk):
            return pltpu.make_async_copy(zero_sc, xs_hbm.at[_rows(blk * EXPERT_TM, EXPERT_TM)], zsem)

        for e in range(N_EXPERTS):
            _for_each_run_piece(padn_ref[e], EXPERT_TM - 1,
                                lambda start, size, e=e: pad_copy(e, start, size).start())

        def tail_start(blk, carry):
            tail_copy(blk).start()
            return carry

        lax.fori_loop(used_ref[0], n_blocks, tail_start, 0)
        for e in range(N_EXPERTS):
            _for_each_run_piece(padn_ref[e], EXPERT_TM - 1,
                                lambda start, size, e=e: pad_copy(e, start, size).wait())

        def tail_wait(blk, carry):
            tail_copy(blk).wait()
            return carry

        lax.fori_loop(used_ref[0], n_blocks, tail_wait, 0)

    n_tiles = pl.num_programs(0) - 1
    slot = lax.rem(i, N_SORT_BUFS)
    prev_slot = lax.rem(i + N_SORT_BUFS - 1, N_SORT_BUFS)
    buf = sorted_sc.at[slot]
    prev = sorted_sc.at[prev_slot]

    def wait_tile(sl):
        pltpu.make_async_copy(sorted_sc.at[sl], xs_hbm.at[_rows(0, n_slots)], sem.at[sl]).wait()

    @pl.when(i >= N_SORT_BUFS)
    def _():
        wait_tile(slot)

    pos = pos_ref[...]
    hb = jnp.where(i < n_first, h1_ref[...], h2_ref[...])
    base = jnp.maximum(i - 1, 0) * N_EXPERTS
    rows_c = n_slots // SORT_CHUNKS
    experts_c = N_EXPERTS // SORT_CHUNKS
    for c in range(SORT_CHUNKS):
        for e in range(c * experts_c, (c + 1) * experts_c):
            off = off_ref[base + e]
            dst = dst_ref[base + e]

            def run_start(start, size, off=off, dst=dst, prio=e % 2):
                pltpu.make_async_copy(prev.at[_rows(off + start, size)],
                                      xs_hbm.at[_rows(dst + start, size)],
                                      sem.at[prev_slot]).start(priority=prio)

            _for_each_run_piece(jnp.where(i >= 1, n_ref[base + e], 0), tm, run_start)

        srow = lax.broadcasted_iota(I32, (rows_c, tm), 0) + c * rows_c
        perm = jnp.where(srow == pos[0:1], 1.0,
                         jnp.where(srow == pos[1:2], 1.0,
                                   jnp.where(srow == pos[2:3], 1.0,
                                             jnp.where(srow == pos[3:4], 1.0, 0.0)))).astype(BF16)
        _store_rows(buf, jnp.dot(perm, hb, preferred_element_type=F32), c * rows_c)

    @pl.when(i == n_tiles)
    def _():
        wait_tile(prev_slot)

        @pl.when(i >= 2)
        def _():
            wait_tile(lax.rem(i + N_SORT_BUFS - 2, N_SORT_BUFS))


def _dispatch(run_n, run_off, run_dst, pad_lo, pad_n, n_used, pos, h1, h2, cap):
    tm = TOKEN_TM
    n_first = h1.shape[0] // tm
    n_tiles = n_first + h2.shape[0] // tm
    grid_spec = pltpu.PrefetchScalarGridSpec(
        num_scalar_prefetch=6,
        grid=(n_tiles + 1,),
        in_specs=[pl.BlockSpec((TOP_K, tm), lambda i, *_: (0, jnp.minimum(i, n_tiles - 1))),
                  pl.BlockSpec((tm, D_MODEL), lambda i, *_: (jnp.minimum(i, n_first - 1), 0)),
                  pl.BlockSpec((tm, D_MODEL), lambda i, *_: (jnp.clip(i - n_first, 0, n_tiles - n_first - 1), 0))],
        out_specs=pl.BlockSpec(memory_space=pl.ANY),
        scratch_shapes=[pltpu.VMEM((N_SORT_BUFS, TOP_K * tm * ROW_SUBLANES, LANES), F32),
                        pltpu.VMEM((EXPERT_TM * ROW_SUBLANES, LANES), F32),
                        pltpu.SemaphoreType.DMA((N_SORT_BUFS,)), pltpu.SemaphoreType.DMA],
    )
    return pl.pallas_call(
        functools.partial(_dispatch_kernel, n_first=n_first),
        grid_spec=grid_spec,
        out_shape=jax.ShapeDtypeStruct((cap * ROW_SUBLANES, LANES), F32),
        compiler_params=_params(1),
        name="moe_dispatch",
    )(run_n, run_off, run_dst, pad_lo, pad_n, n_used, pos, h1, h2)


def _expert_kernel(be_ref, first_ref, ord_ref, seq_ref, used_ref,
                   x_ref, bgu_ref, bd_ref, wgu_hbm, wd_hbm, o_ref,
                   wgu_st, wd_st, wgu_sc, wd_sc, sem):
    i = pl.program_id(0)

    def weight_copies(e):
        return (pltpu.make_async_copy(wgu_hbm.at[e], wgu_st, sem.at[0]),
                pltpu.make_async_copy(wd_hbm.at[e], wd_st, sem.at[1]))

    @pl.when(i == 0)
    def _():
        for c in weight_copies(seq_ref[0]):
            c.start()

    @pl.when(i < used_ref[0])
    def _():
        @pl.when(first_ref[i] == 1)
        def _():
            k = ord_ref[i]
            for c in weight_copies(seq_ref[k]):
                c.wait()
            wgu_sc[...] = wgu_st[...].astype(BF16)
            wd_sc[...] = wd_st[...].astype(BF16)

            @pl.when(k + 1 < used_ref[1])
            def _():
                for c in weight_copies(seq_ref[k + 1]):
                    c.start()

        gu = jnp.dot(_load_rows(x_ref, BF16), wgu_sc[...], preferred_element_type=F32) + bgu_ref[0]
        gate = jnp.minimum(gu[:, :D_FF], SWIGLU_LIMIT)
        lin = jnp.clip(gu[:, D_FF:], -SWIGLU_LIMIT, SWIGLU_LIMIT)
        act = gate * jax.nn.sigmoid(SWIGLU_ALPHA * gate) * (lin + 1.0)
        _store_rows(o_ref, jnp.dot(act.astype(BF16), wd_sc[...], preferred_element_type=F32) + bd_ref[0])

    @pl.when(i >= used_ref[0])
    def _():
        o_ref[...] = jnp.zeros_like(o_ref)


def _experts(block_expert, block_first, block_ord, expert_seq, n_used, xs, lp):
    tm = EXPERT_TM * ROW_SUBLANES
    grid_spec = pltpu.PrefetchScalarGridSpec(
        num_scalar_prefetch=5,
        grid=(xs.shape[0] // tm,),
        in_specs=[pl.BlockSpec((tm, LANES), lambda i, be, *_: (i, 0)),
                  pl.BlockSpec((1, 1, 2 * D_FF), lambda i, be, *_: (be[i], 0, 0)),
                  pl.BlockSpec((1, 1, D_MODEL), lambda i, be, *_: (be[i], 0, 0)),
                  pl.BlockSpec(memory_space=pl.ANY),
                  pl.BlockSpec(memory_space=pl.ANY)],
        out_specs=pl.BlockSpec((tm, LANES), lambda i, be, *_: (i, 0)),
        scratch_shapes=[pltpu.VMEM((D_MODEL, 2 * D_FF), F32), pltpu.VMEM((D_FF, D_MODEL), F32),
                        pltpu.VMEM((D_MODEL, 2 * D_FF), BF16), pltpu.VMEM((D_FF, D_MODEL), BF16),
                        pltpu.SemaphoreType.DMA((2,))],
    )
    return pl.pallas_call(
        _expert_kernel,
        grid_spec=grid_spec,
        out_shape=jax.ShapeDtypeStruct(xs.shape, F32),
        compiler_params=_params(1),
        name="moe_experts",
    )(block_expert, block_first, block_ord, expert_seq, n_used, xs,
      lp['b_gu'].reshape(N_EXPERTS, 1, 2 * D_FF), lp['b_down'].reshape(N_EXPERTS, 1, D_MODEL),
      lp['w_gu'], lp['w_down'])


def _combine_kernel(n_ref, off_ref, dst_ref, pos_ref, gate_ref, h1_ref, h2_ref, ys_hbm, g_ref, b_ref,
                    y1_ref, y2_ref, sorted_sc, w_sc, sem, *, n_first):
    i = pl.program_id(0)
    n_tiles = pl.num_programs(0) - 1
    tm = h1_ref.shape[0]
    n_slots = TOP_K * tm
    slot = lax.rem(i, 2)
    buf = sorted_sc.at[slot]
    pos = pos_ref[...]
    gates = gate_ref[...]
    rows_per = tm // N_EXPERTS
    base = jnp.minimum(i, n_tiles - 1) * N_EXPERTS
    for e in range(N_EXPERTS):
        off = off_ref[base + e]
        dst = dst_ref[base + e]

        def run_start(start, size, off=off, dst=dst, prio=e % 2):
            pltpu.make_async_copy(ys_hbm.at[_rows(dst + start, size)],
                                  buf.at[_rows(off + start, size)], sem.at[slot]).start(priority=prio)

        _for_each_run_piece(jnp.where(i < n_tiles, n_ref[base + e], 0), tm, run_start)

        r = slice(e * rows_per, (e + 1) * rows_per)
        scol = lax.broadcasted_iota(I32, (rows_per, n_slots), 1)
        w_sc[r, :] = jnp.where(
            scol == pos[r, 0:1], gates[r, 0:1],
            jnp.where(scol == pos[r, 1:2], gates[r, 1:2],
                      jnp.where(scol == pos[r, 2:3], gates[r, 2:3],
                                jnp.where(scol == pos[r, 3:4], gates[r, 3:4], 0.0)))).astype(BF16)

    @pl.when(i >= 1)
    def _():
        done = sorted_sc.at[1 - slot]
        pltpu.make_async_copy(ys_hbm.at[_rows(0, n_slots)], done, sem.at[1 - slot]).wait()

        f = jnp.dot(w_sc[...], _load_rows(done, BF16), preferred_element_type=F32)

        @pl.when(i - 1 < n_first)
        def _():
            y1_ref[...] = _layer_norm(DEEPNORM_ALPHA * h1_ref[...] + f, g_ref[...], b_ref[...])

        @pl.when(i - 1 >= n_first)
        def _():
            y2_ref[...] = _layer_norm(DEEPNORM_ALPHA * h2_ref[...] + f, g_ref[...], b_ref[...])


def _combine(run_n, run_off, run_dst, pos_t, gates_t, h1, h2, ys, lp):
    tm = TOKEN_TM
    t_first = h1.shape[0]
    t = t_first + h2.shape[0]
    n_first = t_first // tm
    n_rest = (t - t_first) // tm
    c2 = lambda i, *_: (0, 0)
    done = lambda i, *_: (jnp.maximum(i - 1, 0), 0)
    done1 = lambda i, *_: (jnp.clip(i - 1, 0, n_first - 1), 0)
    done2 = lambda i, *_: (jnp.clip(i - 1 - n_first, 0, n_rest - 1), 0)
    grid_spec = pltpu.PrefetchScalarGridSpec(
        num_scalar_prefetch=3,
        grid=(t // tm + 1,),
        in_specs=[pl.BlockSpec((tm, TOP_K), done),
                  pl.BlockSpec((tm, TOP_K), done),
                  pl.BlockSpec((tm, D_MODEL), done1), pl.BlockSpec((tm, D_MODEL), done2),
                  pl.BlockSpec(memory_space=pl.ANY),
                  pl.BlockSpec((1, D_MODEL), c2), pl.BlockSpec((1, D_MODEL), c2)],
        out_specs=[pl.BlockSpec((tm, D_MODEL), done1), pl.BlockSpec((tm, D_MODEL), done2)],
        scratch_shapes=[pltpu.VMEM((2, TOP_K * tm * ROW_SUBLANES, LANES), F32),
                        pltpu.VMEM((tm, TOP_K * tm), BF16),
                        pltpu.SemaphoreType.DMA((2,))],
    )
    return pl.pallas_call(
        functools.partial(_combine_kernel, n_first=n_first),
        grid_spec=grid_spec,
        out_shape=[jax.ShapeDtypeStruct((t_first, D_MODEL), F32),
                   jax.ShapeDtypeStruct((t - t_first, D_MODEL), F32)],
        compiler_params=_params(1),
        name="moe_combine",
    )(run_n, run_off, run_dst, pos_t, gates_t, h1, h2, ys,
      lp['ln2_g'].reshape(1, D_MODEL), lp['ln2_b'].reshape(1, D_MODEL))


def _moe_and_norm(h1, h2, hb1, hb2, pos, gates, tile_counts, lp):
    t = h1.shape[0] + h2.shape[0]
    te = EXPERT_TM
    n_tiles = t // TOKEN_TM
    n_blocks = (t * TOP_K) // te + N_EXPERTS
    cap = n_blocks * te
    cnt = tile_counts.reshape(n_tiles, N_EXPERTS).astype(I32)
    counts = jnp.sum(cnt, axis=0)
    padded = (counts + te - 1) // te * te
    pad_ends = jnp.cumsum(padded)
    pad_starts = pad_ends - padded
    run_dst = pad_starts[None, :] + jnp.cumsum(cnt, axis=0) - cnt
    run_off = jnp.cumsum(cnt, axis=1) - cnt
    blk_start = jnp.arange(n_blocks, dtype=I32) * te
    expert_of = lambda slot_idx: jnp.minimum(jnp.sum(slot_idx[..., None] >= pad_ends, axis=-1), N_EXPERTS - 1)
    total = pad_ends[-1]
    be = jnp.where(blk_start < total, expert_of(blk_start), expert_of(jnp.maximum(total - 1, 0))).astype(I32)
    is_e = be[:, None] == jnp.arange(N_EXPERTS, dtype=I32)[None, :]
    pick = lambda table: jnp.sum(jnp.where(is_e, table[None, :], 0), axis=1)
    in_use = counts > 0
    ordinal = jnp.cumsum(in_use.astype(I32)) - 1
    rank = jnp.arange(N_EXPERTS, dtype=I32)
    expert_seq = jnp.sum(jnp.where(in_use[None, :] & (ordinal[None, :] == rank[:, None]), rank[None, :], 0),
                         axis=1)
    block_ord = pick(ordinal)
    block_first = (blk_start == pick(pad_starts)) & (blk_start < total)
    used = jnp.stack([total // te, jnp.sum(in_use.astype(I32))]).astype(I32)
    flat = lambda a: a.reshape(-1).astype(I32)
    xs = _dispatch(flat(cnt), flat(run_off), flat(run_dst), flat(pad_starts + counts),
                   flat(padded - counts), used, pos, hb1, hb2, cap)
    ys = _experts(be, flat(block_first), flat(block_ord), expert_seq, used, xs, lp)
    return _combine(flat(cnt), flat(run_off), flat(run_dst), pos.T, gates.T, h1, h2, ys, lp)


def kernel(x_prompt, x_sample, cache_attn_k, cache_attn_v, cache_mem_k, cache_mem_v, state_ssm_re, state_ssm_im, mem_prompt, w_in, lam_re, lam_im, log_dt, ssm_b_re, ssm_b_im, ssm_c_re, ssm_c_im, ssm_d, w_glu, b_glu, rel_bias, w_mem_kv, g_ssm, g_att, g_mem, w_out, ln1_g, ln1_b, w_router, b_router, w_gu, b_gu, w_down, b_down, ln2_g, ln2_b):
    assert w_in.shape[0] == 1, "single-layer step"
    lp = dict(w_in=w_in[0], lam_re=lam_re[0], lam_im=lam_im[0], log_dt=log_dt[0],
              ssm_b_re=ssm_b_re[0], ssm_b_im=ssm_b_im[0], ssm_c_re=ssm_c_re[0], ssm_c_im=ssm_c_im[0],
              ssm_d=ssm_d[0], w_glu=w_glu[0], b_glu=b_glu[0], rel_bias=rel_bias[0],
              w_mem_kv=w_mem_kv[0], g_ssm=g_ssm[0], g_att=g_att[0], g_mem=g_mem[0], w_out=w_out[0],
              ln1_g=ln1_g[0], ln1_b=ln1_b[0], w_router=w_router[0], b_router=b_router[0],
              w_gu=w_gu[0], b_gu=b_gu[0], w_down=w_down[0], b_down=b_down[0],
              ln2_g=ln2_g[0], ln2_b=ln2_b[0])

    bp, sp, _ = x_prompt.shape
    bs, ss, _ = x_sample.shape
    heads = lambda a: a.reshape(a.shape[0], a.shape[1], N_HEADS, HEAD_DIM)
    state = lambda a: a.reshape(a.shape[0], N_GROUPS, SSM_STATE)

    w = min(BAND, sp)
    u_p, kv_p, zb = _in_proj(x_prompt, lp['w_in'], min(IN_PROJ_TS, sp), w)
    mk, mv = _mem_kv(mem_prompt, lp['w_mem_kv'])
    ya, ym = _attn_prompt(zb, mk, mv, lp['rel_bias'])
    zeros = jnp.zeros((bp, D_STATE), F32)
    ys_p, sr_p, si_p = _ssm(u_p, zeros, zeros, lp, bp)
    h_p, hb_p, pos_p, gates_p, cnt_p = _merge(x_prompt, ys_p, ya, ym, lp, 1, TOKEN_TM)
    k_p = heads(kv_p[:, :, :D_ATT])
    v_p = heads(kv_p[:, :, D_ATT:])

    wc = cache_attn_k.shape[2]
    u_s, kv_s, zb_s = _in_proj(x_sample, lp['w_in'], ss, ss)
    ya_s, ym_s, nk, nv = _attn_sample(
        kv_s, zb_s, cache_attn_k[0].reshape(bs, wc, D_ATT), cache_attn_v[0].reshape(bs, wc, D_ATT),
        cache_mem_k[0].reshape(bs, N_MEM, D_MEM), cache_mem_v[0].reshape(bs, N_MEM, D_MEM),
        lp['rel_bias'])
    ys_s, sr_s, si_s = _ssm(u_s, state_ssm_re[0], state_ssm_im[0], lp, bs)
    h_s, hb_s, pos_s, gates_s, cnt_s = _merge(x_sample, ys_s, ya_s, ym_s, lp, TOKEN_TM // ss, ss)

    y_p, y_s = _moe_and_norm(h_p, h_s, hb_p, hb_s, jnp.concatenate([pos_p, pos_s], axis=1),
                             jnp.concatenate([gates_p, gates_s], axis=1),
                             jnp.concatenate([cnt_p, cnt_s], axis=0), lp)

    return (y_p.reshape(bp, sp, D_MODEL), y_s.reshape(bs, ss, D_MODEL),
            k_p[None], v_p[None], heads(mk)[None], heads(mv)[None], state(sr_p)[None], state(si_p)[None],
            heads(nk)[None], heads(nv)[None], state(sr_s)[None], state(si_s)[None])
```
